```python
import jax, jax.numpy as jnp
from jax import lax
import numpy as np

D_MODEL = 1024
BATCH = 16
SEQ = 256
DEPTH = 2
DEC_BATCH = 8
DEC_SEQ = 2048
PAST_LEN = 256

GRID_W = 64
N_MIXERS = 2
N_SSM_LAYERS = (DEPTH + N_MIXERS - 1) // N_MIXERS
N_FNET_LAYERS = DEPTH // N_MIXERS
SSM_WIDTH = D_MODEL
SSM_GROUP = 16
SSM_GROUPS = SSM_WIDTH // SSM_GROUP
STATE_DIM = 64
FNET_GROUPS = 8
FNET_GROUP = D_MODEL // FNET_GROUPS
N_EXPERTS = 16
N_EXPERT_GROUPS = 4
EXPERTS_PER_GROUP = N_EXPERTS // N_EXPERT_GROUPS
TOP_K = 2
D_EXPERT = 1024
N_MOD = 6
EPS = 1e-6
DT_MIN = 1e-3
DT_MAX = 1e-1

kernel_name = "hybrid_s5_fnet_moe_diffusion_step"


def rms_norm(x, g):
    xf = x.astype(jnp.float32)
    y = xf * lax.rsqrt(jnp.mean(xf * xf, axis=-1, keepdims=True) + EPS)
    return (y * g.astype(jnp.float32)).astype(x.dtype)


def _linrec_combine(left, right):
    a1r, a1i, b1r, b1i = left
    a2r, a2i, b2r, b2i = right
    return (a2r * a1r - a2i * a1i,
            a2r * a1i + a2i * a1r,
            a2r * b1r - a2i * b1i + b2r,
            a2r * b1i + a2i * b1r + b2i)


def zoh_discretize(lam_re, lam_im, log_dt, b_re, b_im):
    f32 = jnp.float32
    lr = lam_re.astype(f32)
    li = lam_im.astype(f32)
    dt = jnp.exp(log_dt.astype(f32))[:, None]
    mag = jnp.exp(lr * dt)
    a_re = mag * jnp.cos(li * dt)
    a_im = mag * jnp.sin(li * dt)
    den = lr * lr + li * li
    nr = a_re - 1.0
    f_re = ((nr * lr + a_im * li) / den)[..., None]
    f_im = ((a_im * lr - nr * li) / den)[..., None]
    br = b_re.astype(f32)
    bi = b_im.astype(f32)
    return a_re, a_im, f_re * br - f_im * bi, f_re * bi + f_im * br


def ssm_scan(u_g, a_re, a_im, bb_re, bb_im, c_re, c_im, h0_re, h0_im, reverse):
    f32 = jnp.float32
    seq = u_g.shape[1]
    x_re = jnp.einsum("blgh,gph->blgp", u_g, bb_re)
    x_im = jnp.einsum("blgh,gph->blgp", u_g, bb_im)
    first, last = (-1, 0) if reverse else (0, -1)
    h0r = h0_re.astype(f32)
    h0i = h0_im.astype(f32)
    x_re = x_re.at[:, first].add(a_re * h0r - a_im * h0i)
    x_im = x_im.at[:, first].add(a_re * h0i + a_im * h0r)
    shape = (1, seq) + a_re.shape
    _, _, h_re, h_im = lax.associative_scan(
        _linrec_combine,
        (jnp.broadcast_to(a_re, shape), jnp.broadcast_to(a_im, shape), x_re, x_im),
        reverse=reverse, axis=1)
    y = (jnp.einsum("blgp,ghp->blgh", h_re, c_re.astype(f32))
         - jnp.einsum("blgp,ghp->blgh", h_im, c_im.astype(f32)))
    return y, h_re[:, last], h_im[:, last]


def ssm_mixer(h, w_in, lam_re, lam_im, log_dt, b_re, b_im, c_re, c_im, d_skip, w_glu,
              h0_re, h0_im):
    bsz, seq, _ = h.shape
    u = (h @ w_in).astype(jnp.float32)
    u_g = u.reshape(bsz, seq, SSM_GROUPS, SSM_GROUP)
    y = u * d_skip.astype(jnp.float32)
    fin_re, fin_im = [], []
    for d in range(2):
        a_re, a_im, bb_re, bb_im = zoh_discretize(lam_re[d], lam_im[d], log_dt[d],
                                                  b_re[d], b_im[d])
        y_d, f_re, f_im = ssm_scan(u_g, a_re, a_im, bb_re, bb_im, c_re[d], c_im[d],
                                   h0_re[:, d], h0_im[:, d], reverse=(d == 1))
        y = y + y_d.reshape(bsz, seq, SSM_WIDTH)
        fin_re.append(f_re)
        fin_im.append(f_im)
    val, gate = jnp.split(jax.nn.gelu(y).astype(h.dtype) @ w_glu, 2, axis=-1)
    return val * jax.nn.sigmoid(gate), jnp.stack(fin_re, axis=1), jnp.stack(fin_im, axis=1)


def fourier_mixer(h, w_out):
    bsz, seq, _ = h.shape
    u = h.astype(jnp.float32).reshape(bsz, seq, FNET_GROUPS, FNET_GROUP)
    f = jnp.fft.fft2(u, axes=(1, 3), norm="ortho")
    y = jnp.real(f).reshape(bsz, seq, D_MODEL).astype(h.dtype)
    return y @ w_out


def moe(h, router_w, router_b, w_gate, w_up, w_down):
    bsz, seq, dm = h.shape
    t = h.reshape(-1, dm)
    scores = jax.nn.sigmoid((t @ router_w).astype(jnp.float32))
    biased = scores + router_b.astype(jnp.float32)
    grp = biased.reshape(-1, N_EXPERT_GROUPS, EXPERTS_PER_GROUP)
    grp_score = lax.top_k(grp, TOP_K)[0].sum(-1)
    best = jnp.argmax(grp_score, axis=-1)
    mask = jnp.repeat(jax.nn.one_hot(best, N_EXPERT_GROUPS, dtype=jnp.bool_),
                      EXPERTS_PER_GROUP, axis=-1)
    _, top_idx = lax.top_k(jnp.where(mask, biased, -jnp.inf), TOP_K)
    top_w = jnp.take_along_axis(scores, top_idx, axis=-1)
    top_w = top_w / jnp.sum(top_w, axis=-1, keepdims=True)
    combine = jnp.einsum("nk,nke->ne", top_w,
                         jax.nn.one_hot(top_idx, N_EXPERTS, dtype=jnp.float32))
    g = jnp.einsum("nd,edf->nef", t, w_gate)
    u = jnp.einsum("nd,edf->nef", t, w_up)
    a = jax.nn.silu(g) * u * combine[:, :, None].astype(t.dtype)
    y = jnp.einsum("nef,efd->nd", a, w_down)
    return y.reshape(bsz, seq, dm)


def run_stream(x, cond, h0_re, h0_im, p):
    fin_re, fin_im = [], []
    for i in range(DEPTH):
        mod = (jax.nn.silu(cond) @ p["w_ada"][i] + p["b_ada"][i])[:, None, :]
        sh1, sc1, g1, sh2, sc2, g2 = jnp.split(mod, N_MOD, axis=-1)
        h = rms_norm(x, p["norm_mix"][i]) * (1 + sc1) + sh1
        j = i // N_MIXERS
        if i % N_MIXERS == 0:
            m, f_re, f_im = ssm_mixer(
                h, p["ssm_w_in"][j], p["ssm_lam_re"][j], p["ssm_lam_im"][j],
                p["ssm_log_dt"][j], p["ssm_b_re"][j], p["ssm_b_im"][j],
                p["ssm_c_re"][j], p["ssm_c_im"][j], p["ssm_d"][j], p["ssm_w_glu"][j],
                h0_re[:, j], h0_im[:, j])
            fin_re.append(f_re)
            fin_im.append(f_im)
        else:
            m = fourier_mixer(h, p["fnet_w_out"][j])
        x = x + g1 * m
        h = rms_norm(x, p["norm_ffn"][i]) * (1 + sc2) + sh2
        x = x + g2 * moe(h, p["router_w"], p["router_b"], p["moe_w_gate"][i],
                         p["moe_w_up"][i], p["moe_w_down"][i])
    return rms_norm(x, p["norm_final"]), jnp.stack(fin_re, axis=1), jnp.stack(fin_im, axis=1)


def setup_inputs(seed: int = 0) -> dict:
    key = jax.random.key(seed)
    ks = iter(jax.random.split(key, 40))
    f32 = jnp.float32

    def nrm(shape, scale):
        return scale * jax.random.normal(next(ks), shape, f32)

    st_shape = (DEC_BATCH, N_SSM_LAYERS, 2, SSM_GROUPS, STATE_DIM)
    n_idx = jnp.arange(STATE_DIM, dtype=f32)
    lam_shape = (N_SSM_LAYERS, 2, SSM_GROUPS, STATE_DIM)
    return {
        "x_prompt": nrm((BATCH, SEQ, D_MODEL), 1.0),
        "x_sample": nrm((DEC_BATCH, DEC_SEQ, D_MODEL), 1.0),
        "c": nrm((DEC_BATCH, D_MODEL), 1.0),
        "state_ssm_re": nrm(st_shape, 0.3),
        "state_ssm_im": nrm(st_shape, 0.3),
        "c_ctx": nrm((D_MODEL,), 1.0),
        "norm_mix": 1.0 + nrm((DEPTH, D_MODEL), 0.02),
        "norm_ffn": 1.0 + nrm((DEPTH, D_MODEL), 0.02),
        "w_ada": nrm((DEPTH, D_MODEL, N_MOD * D_MODEL), 0.5 * D_MODEL ** -0.5),
        "b_ada": nrm((DEPTH, N_MOD * D_MODEL), 0.02),
        "ssm_w_in": nrm((N_SSM_LAYERS, D_MODEL, SSM_WIDTH), D_MODEL ** -0.5),
        "ssm_lam_re": -0.5 + nrm(lam_shape, 0.01),
        "ssm_lam_im": jnp.pi * n_idx + nrm(lam_shape, 0.01),
        "ssm_log_dt": jax.random.uniform(next(ks), (N_SSM_LAYERS, 2, SSM_GROUPS), f32,
                                         np.log(DT_MIN), np.log(DT_MAX)),
        "ssm_b_re": nrm((N_SSM_LAYERS, 2, SSM_GROUPS, STATE_DIM, SSM_GROUP), (2 * SSM_GROUP) ** -0.5),
        "ssm_b_im": nrm((N_SSM_LAYERS, 2, SSM_GROUPS, STATE_DIM, SSM_GROUP), (2 * SSM_GROUP) ** -0.5),
        "ssm_c_re": nrm((N_SSM_LAYERS, 2, SSM_GROUPS, SSM_GROUP, STATE_DIM), STATE_DIM ** -0.5),
        "ssm_c_im": nrm((N_SSM_LAYERS, 2, SSM_GROUPS, SSM_GROUP, STATE_DIM), STATE_DIM ** -0.5),
        "ssm_d": nrm((N_SSM_LAYERS, SSM_WIDTH), 1.0),
        "ssm_w_glu": nrm((N_SSM_LAYERS, SSM_WIDTH, 2 * D_MODEL), SSM_WIDTH ** -0.5),
        "fnet_w_out": nrm((N_FNET_LAYERS, D_MODEL, D_MODEL), D_MODEL ** -0.5),
        "router_w": nrm((D_MODEL, N_EXPERTS), D_MODEL ** -0.5),
        "router_b": nrm((N_EXPERTS,), 0.01),
        "moe_w_gate": nrm((DEPTH, N_EXPERTS, D_MODEL, D_EXPERT), D_MODEL ** -0.5),
        "moe_w_up": nrm((DEPTH, N_EXPERTS, D_MODEL, D_EXPERT), D_MODEL ** -0.5),
        "moe_w_down": nrm((DEPTH, N_EXPERTS, D_EXPERT, D_MODEL), D_EXPERT ** -0.5),
        "norm_final": 1.0 + nrm((D_MODEL,), 0.02),
    }


def reference(x_prompt, x_sample, c, state_ssm_re, state_ssm_im, c_ctx,
              norm_mix, norm_ffn, w_ada, b_ada,
              ssm_w_in, ssm_lam_re, ssm_lam_im, ssm_log_dt, ssm_b_re, ssm_b_im,
              ssm_c_re, ssm_c_im, ssm_d, ssm_w_glu,
              fnet_w_out, router_w, router_b, moe_w_gate, moe_w_up, moe_w_down,
              norm_final):
    p = dict(norm_mix=norm_mix, norm_ffn=norm_ffn, w_ada=w_ada, b_ada=b_ada,
             ssm_w_in=ssm_w_in, ssm_lam_re=ssm_lam_re, ssm_lam_im=ssm_lam_im,
             ssm_log_dt=ssm_log_dt, ssm_b_re=ssm_b_re, ssm_b_im=ssm_b_im,
             ssm_c_re=ssm_c_re, ssm_c_im=ssm_c_im, ssm_d=ssm_d, ssm_w_glu=ssm_w_glu,
             fnet_w_out=fnet_w_out, router_w=router_w, router_b=router_b,
             moe_w_gate=moe_w_gate, moe_w_up=moe_w_up, moe_w_down=moe_w_down,
             norm_final=norm_final)
    zeros = jnp.zeros((x_prompt.shape[0], N_SSM_LAYERS, 2, SSM_GROUPS, STATE_DIM), jnp.float32)
    y_prompt, new_state_re, new_state_im = run_stream(x_prompt, c_ctx[None, :], zeros, zeros, p)
    y_sample, _, _ = run_stream(x_sample, c, state_ssm_re, state_ssm_im, p)
    return (y_prompt, y_sample, new_state_re, new_state_im)
```

```python
import functools

import jax
import jax.numpy as jnp
from jax import lax
from jax.experimental import pallas as pl
from jax.experimental.pallas import tpu as pltpu

F32 = jnp.float32
BF16 = jnp.bfloat16
I32 = jnp.int32
HIGHEST = lax.Precision.HIGHEST

D_MODEL = 1024
DEPTH = 2
SSM_GROUP = 16
SSM_GROUPS = 64
STATE_DIM = 64
N_STATE = SSM_GROUPS * STATE_DIM
FNET_GROUP = 128
N_EXPERTS = 16
N_EXPERT_GROUPS = 4
EXPERTS_PER_GROUP = 4
D_EXPERT = 1024
N_MOD = 6
EPS = 1e-6

LANES = 128
SUBLANES = 8
VMEM_LIMIT = 56 * 1024 * 1024

TOKEN_TILE = 512
FFN_TILE = 256
N_COND = 16


def _cparams(sem, vmem=VMEM_LIMIT):
    return pltpu.CompilerParams(dimension_semantics=sem, vmem_limit_bytes=vmem)


def _sigmoid(x):
    return 1.0 / (1.0 + jnp.exp(-x))


def _gelu_tanh(x):
    c = 0.7978845608028654
    return x * (0.5 * (1.0 + jnp.tanh(c * (x + 0.044715 * (x * x * x)))))


def _per_row(v, pat, fn):
    tm, d = v.shape
    p = pat.shape[0]
    return fn(v.reshape(tm // p, p, d), pat[None]).reshape(tm, d)


def _norm_mod(x, g, sc, sh):
    ms = jnp.mean(x * x, axis=-1, keepdims=True)
    y = x * lax.rsqrt(ms + EPS) * g
    y = _per_row(y, sc, lambda a, b: a * (1.0 + b))
    return _per_row(y, sh, lambda a, b: a + b)


def _route(logits_t, rb, tri, carry):
    ne, tm = logits_t.shape
    s = _sigmoid(logits_t)
    bz = s + rb
    row = lambda a, r: a[r:r + 1, :]
    gs = []
    for g in range(N_EXPERT_GROUPS):
        v0, v1, v2, v3 = (row(bz, EXPERTS_PER_GROUP * g + j) for j in range(4))
        hi1, lo1 = jnp.maximum(v0, v1), jnp.minimum(v0, v1)
        hi2, lo2 = jnp.maximum(v2, v3), jnp.minimum(v2, v3)
        top1 = jnp.maximum(hi1, hi2)
        top2 = jnp.maximum(jnp.minimum(hi1, hi2), jnp.maximum(lo1, lo2))
        gs.append(top1 + top2)
    bg = jnp.zeros((1, tm), I32)
    bv = gs[0]
    for g in range(1, N_EXPERT_GROUPS):
        upd = gs[g] > bv
        bg = jnp.where(upd, g, bg)
        bv = jnp.where(upd, gs[g], bv)
    cb, cs = [], []
    for j in range(EXPERTS_PER_GROUP):
        vb, vs = row(bz, j), row(s, j)
        for g in range(1, N_EXPERT_GROUPS):
            sel = bg == g
            vb = jnp.where(sel, row(bz, EXPERTS_PER_GROUP * g + j), vb)
            vs = jnp.where(sel, row(s, EXPERTS_PER_GROUP * g + j), vs)
        cb.append(vb)
        cs.append(vs)
    i1 = jnp.zeros((1, tm), I32)
    b1, s1 = cb[0], cs[0]
    for j in range(1, EXPERTS_PER_GROUP):
        upd = cb[j] > b1
        i1 = jnp.where(upd, j, i1)
        b1 = jnp.where(upd, cb[j], b1)
        s1 = jnp.where(upd, cs[j], s1)
    i2 = jnp.zeros((1, tm), I32)
    b2 = jnp.full((1, tm), -jnp.inf, F32)
    s2 = jnp.zeros((1, tm), F32)
    for j in range(EXPERTS_PER_GROUP):
        cand = jnp.where(i1 == j, -jnp.inf, cb[j])
        upd = cand > b2
        i2 = jnp.where(upd, j, i2)
        b2 = jnp.where(upd, cand, b2)
        s2 = jnp.where(upd, cs[j], s2)
    e1 = EXPERTS_PER_GROUP * bg + i1
    e2 = EXPERTS_PER_GROUP * bg + i2
    den = s1 + s2
    w1 = s1 / den
    w2 = s2 / den
    eio = lax.broadcasted_iota(I32, (ne, tm), 0)
    oh1 = eio == e1
    oh2 = eio == e2
    ohf = jnp.where(oh1 | oh2, 1.0, 0.0)
    cum = jnp.dot(ohf.astype(BF16), tri, preferred_element_type=F32) + carry
    r1 = jnp.sum(jnp.where(oh1, cum, 0.0), axis=0, keepdims=True)
    r2 = jnp.sum(jnp.where(oh2, cum, 0.0), axis=0, keepdims=True)
    new_carry = carry + jnp.sum(ohf, axis=1, keepdims=True)
    return e1, e2, r1.astype(I32), r2.astype(I32), w1, w2, new_carry


def _post_mixer(m, x, g1, nf, sc2, sh2, rwt_ref, rb_ref, tri_ref, cnt_in_ref,
                x1_ref, h2_ref, ri_ref, rw_ref, cnt_ref, carry, first):
    @pl.when(first)
    def _():
        carry[...] = cnt_in_ref[...]

    x1 = x + _per_row(m, g1, lambda a, b: a * b)
    x1_ref[...] = x1
    h2 = _norm_mod(x1, nf, sc2, sh2)
    h2_ref[...] = h2
    logits_t = lax.dot_general(rwt_ref[...], h2, (((1,), (1,)), ((), ())),
                               precision=HIGHEST, preferred_element_type=F32)
    e1, e2, r1, r2, w1, w2, nc = _route(logits_t, rb_ref[...], tri_ref[...],
                                        carry[:, 0:1])
    tm = x.shape[0]
    ri_ref[0:1, :] = e1
    ri_ref[1:2, :] = e2
    ri_ref[2:3, :] = r1
    ri_ref[3:4, :] = r2
    ri_ref[4:8, :] = jnp.zeros((4, tm), I32)
    rw_ref[0:1, :] = w1
    rw_ref[1:2, :] = w2
    rw_ref[2:8, :] = jnp.zeros((6, tm), F32)
    carry[...] = jnp.broadcast_to(nc, carry.shape)
    cnt_ref[...] = carry[...]


def _ada_kernel(cond_ref, w_ref, b_ref, o_ref):
    c = cond_ref[...]
    s = c * _sigmoid(c)
    o_ref[...] = jnp.dot(s, w_ref[...], precision=HIGHEST,
                         preferred_element_type=F32) + b_ref[...]


def _ada_table(cond, w_ada, b_ada):
    tn = 1536
    n_out = N_MOD * D_MODEL
    return pl.pallas_call(
        _ada_kernel,
        out_shape=jax.ShapeDtypeStruct((DEPTH, N_COND, n_out), F32),
        grid=(DEPTH, n_out // tn),
        in_specs=[
            pl.BlockSpec((N_COND, D_MODEL), lambda l, j: (0, 0)),
            pl.BlockSpec((None, D_MODEL, tn), lambda l, j: (l, 0, j)),
            pl.BlockSpec((None, 1, tn), lambda l, j: (l, 0, j)),
        ],
        out_specs=pl.BlockSpec((None, N_COND, tn), lambda l, j: (l, 0, j)),
        compiler_params=_cparams(("arbitrary", "arbitrary")),
        name="ada_table",
    )(cond, w_ada, b_ada.reshape(DEPTH, 1, n_out))


def _zoh_kernel(lr_ref, li_ref, ldt_ref, br_ref, bi_ref,
                are_ref, aim_ref, bbre_ref, bbim_ref):
    lr = lr_ref[...]
    li = li_ref[...]
    dt = jnp.exp(ldt_ref[...])
    mag = jnp.exp(lr * dt)
    a_re = mag * jnp.cos(li * dt)
    a_im = mag * jnp.sin(li * dt)
    den = lr * lr + li * li
    nr = a_re - 1.0
    f_re = (nr * lr + a_im * li) / den
    f_im = (a_im * lr - nr * li) / den
    br = br_ref[...]
    bi = bi_ref[...]
    are_ref[...] = a_re
    aim_ref[...] = a_im
    bbre_ref[...] = f_re * br - f_im * bi
    bbim_ref[...] = f_re * bi + f_im * br


def _zoh(lam_re, lam_im, log_dt, b_re, b_im):
    shape = b_re.shape
    flat = (shape[0] * shape[1] * shape[2] * shape[3] // LANES, LANES)
    bc = lambda a: jnp.broadcast_to(a, shape).reshape(flat)
    args = (bc(lam_re[..., None]), bc(lam_im[..., None]),
            bc(log_dt[:, :, None, None]), b_re.reshape(flat), b_im.reshape(flat))
    outs = pl.pallas_call(
        _zoh_kernel,
        out_shape=[jax.ShapeDtypeStruct(flat, F32)] * 4,
        name="zoh_discretize",
    )(*args)
    a_re, a_im, bb_re, bb_im = (o.reshape(shape) for o in outs)
    return a_re[..., 0], a_im[..., 0], bb_re, bb_im


def _ssm_in_kernel(x_ref, g_ref, sc_ref, sh_ref, w_ref, u_ref):
    h = _norm_mod(x_ref[...], g_ref[...], sc_ref[...], sh_ref[...])
    u_ref[...] = jnp.dot(h.astype(BF16), w_ref[...], preferred_element_type=F32)


def _ssm_in(x, g, modpat, pat_of_tile, w_in_bf):
    n = x.shape[0]
    tm = TOKEN_TILE
    p = modpat.shape[1]
    mod = lambda k: pl.BlockSpec((None, p, D_MODEL),
                                 lambda i: (pat_of_tile(i), 0, k))
    return pl.pallas_call(
        _ssm_in_kernel,
        out_shape=jax.ShapeDtypeStruct((n, D_MODEL), F32),
        grid=(n // tm,),
        in_specs=[
            pl.BlockSpec((tm, D_MODEL), lambda i: (i, 0)),
            pl.BlockSpec((1, D_MODEL), lambda i: (0, 0)),
            mod(1), mod(0),
            pl.BlockSpec((D_MODEL, D_MODEL), lambda i: (0, 0)),
        ],
        out_specs=pl.BlockSpec((tm, D_MODEL), lambda i: (i, 0)),
        compiler_params=_cparams(("arbitrary",)),
        name="ssm_in",
    )(x, g, modpat, modpat, w_in_bf)


def _scan_kernel(u_ref, bt_ref, cre_ref, cim_ref, are_ref, aim_ref,
                 h0re_ref, h0im_ref, yprev_ref, y_ref, fre_ref, fim_ref,
                 xre, xim, hre, him, *, batch, steps, state_tile):
    del yprev_ref
    d = pl.program_id(0)
    c = pl.program_id(1)
    n_lane_tiles = D_MODEL // LANES
    cols = N_STATE // n_lane_tiles

    @pl.when(c == 0)
    def _():
        hre[...] = h0re_ref[...]
        him[...] = h0im_ref[...]

    u = u_ref[...].astype(BF16)
    for i in range(n_lane_tiles):
        xt = jnp.dot(u[:, LANES * i:LANES * (i + 1)], bt_ref[i],
                     preferred_element_type=F32)
        xre[:, cols * i:cols * (i + 1)] = xt[:, :cols]
        xim[:, cols * i:cols * (i + 1)] = xt[:, cols:]

    for s in range(N_STATE // state_tile):
        sl = slice(s * state_tile, (s + 1) * state_tile)
        ar = jnp.broadcast_to(are_ref[:, sl], (batch, state_tile))
        ai = jnp.broadcast_to(aim_ref[:, sl], (batch, state_tile))

        def body(t, carry, sl=sl, ar=ar, ai=ai):
            hr, hi = carry
            tt = jnp.where(d == 0, t, steps - 1 - t)
            rows = pl.ds(pl.multiple_of(tt * batch, batch), batch)
            nr = ar * hr - ai * hi + xre[rows, sl]
            ni = ar * hi + ai * hr + xim[rows, sl]
            xre[rows, sl] = nr
            xim[rows, sl] = ni
            return nr, ni

        hr, hi = lax.fori_loop(0, steps, body, (hre[:, sl], him[:, sl]))
        hre[:, sl] = hr
        him[:, sl] = hi

    for i in range(n_lane_tiles):
        hr = xre[:, cols * i:cols * (i + 1)].astype(BF16)
        hi = xim[:, cols * i:cols * (i + 1)].astype(BF16)
        y_ref[:, LANES * i:LANES * (i + 1)] = (
            jnp.dot(hr, cre_ref[i], preferred_element_type=F32)
            - jnp.dot(hi, cim_ref[i], preferred_element_type=F32))
    fre_ref[...] = hre[...]
    fim_ref[...] = him[...]


def _ssm_scan(u, y_prev, bt, ct_re, ct_im, a_re, a_im, h0_re, h0_im,
              *, batch, seq, row_off, steps):
    n_tot = u.shape[0]
    rows = steps * batch
    n_chunks = seq // steps
    blk_off = row_off // rows
    state_tile = 1024 if batch == SUBLANES else 512

    def chunk(d, c):
        return blk_off + jnp.where(d == 0, c, n_chunks - 1 - c)

    kern = functools.partial(_scan_kernel, batch=batch, steps=steps,
                             state_tile=state_tile)
    lt = D_MODEL // LANES
    cols = N_STATE // lt
    dspec = lambda shape: pl.BlockSpec((None,) + shape,
                                       lambda d, c: (d,) + (0,) * len(shape))
    return pl.pallas_call(
        kern,
        out_shape=[jax.ShapeDtypeStruct((2, n_tot, D_MODEL), F32),
                   jax.ShapeDtypeStruct((2, batch, N_STATE), F32),
                   jax.ShapeDtypeStruct((2, batch, N_STATE), F32)],
        grid=(2, n_chunks),
        in_specs=[
            pl.BlockSpec((rows, D_MODEL), lambda d, c: (chunk(d, c), 0)),
            dspec((lt, LANES, 2 * cols)),
            dspec((lt, cols, LANES)),
            dspec((lt, cols, LANES)),
            dspec((1, N_STATE)),
            dspec((1, N_STATE)),
            dspec((batch, N_STATE)),
            dspec((batch, N_STATE)),
            pl.BlockSpec(memory_space=pl.ANY),
        ],
        out_specs=[
            pl.BlockSpec((None, rows, D_MODEL), lambda d, c: (d, chunk(d, c), 0)),
            dspec((batch, N_STATE)),
            dspec((batch, N_STATE)),
        ],
        scratch_shapes=[pltpu.VMEM((rows, N_STATE), F32),
                        pltpu.VMEM((rows, N_STATE), F32),
                        pltpu.VMEM((batch, N_STATE), F32),
                        pltpu.VMEM((batch, N_STATE), F32)],
        input_output_aliases={8: 0},
        compiler_params=_cparams(("arbitrary", "arbitrary")),
        name="ssm_scan_b%d" % batch,
    )(u, bt, ct_re, ct_im, a_re, a_im, h0_re, h0_im, y_prev)


def _ssm_out_kernel(u_ref, y0_ref, y1_ref, dsk_ref, wglu_ref, x_ref, g1_ref,
                    nf_ref, sc2_ref, sh2_ref, rwt_ref, rb_ref, tri_ref,
                    cnt_in_ref, x1_ref, h2_ref, ri_ref, rw_ref, cnt_ref, carry):
    y = u_ref[...] * dsk_ref[...] + y0_ref[...] + y1_ref[...]
    ge = _gelu_tanh(y).astype(BF16)
    vg = jnp.dot(ge, wglu_ref[...], preferred_element_type=F32)
    m = vg[:, :D_MODEL] * _sigmoid(vg[:, D_MODEL:])
    _post_mixer(m, x_ref[...], g1_ref[...], nf_ref[...], sc2_ref[...],
                sh2_ref[...], rwt_ref, rb_ref, tri_ref, cnt_in_ref,
                x1_ref, h2_ref, ri_ref, rw_ref, cnt_ref, carry,
                pl.program_id(0) == 0)


def _route_out_shapes(n):
    return [jax.ShapeDtypeStruct((n, D_MODEL), F32),
            jax.ShapeDtypeStruct((n, D_MODEL), F32),
            jax.ShapeDtypeStruct((SUBLANES, n), I32),
            jax.ShapeDtypeStruct((SUBLANES, n), F32),
            jax.ShapeDtypeStruct((N_EXPERTS, LANES), F32)]


def _ssm_out(u, y, d_skip, w_glu_bf, x, modpat, pat_of_tile, norm_ffn,
             rwt, rb, tri, cnt_in):
    n = x.shape[0]
    tm = TOKEN_TILE
    p = modpat.shape[1]
    mod = lambda k: pl.BlockSpec((None, p, D_MODEL),
                                 lambda i: (pat_of_tile(i), 0, k))
    full = lambda shape: pl.BlockSpec(shape, lambda i: (0,) * len(shape))
    rowblk = pl.BlockSpec((tm, D_MODEL), lambda i: (i, 0))
    return pl.pallas_call(
        _ssm_out_kernel,
        out_shape=_route_out_shapes(n),
        grid=(n // tm,),
        in_specs=[
            rowblk,
            pl.BlockSpec((None, tm, D_MODEL), lambda i: (0, i, 0)),
            pl.BlockSpec((None, tm, D_MODEL), lambda i: (1, i, 0)),
            full((1, D_MODEL)),
            full((D_MODEL, 2 * D_MODEL)),
            rowblk,
            mod(2), full((1, D_MODEL)), mod(4), mod(3),
            full((N_EXPERTS, D_MODEL)), full((N_EXPERTS, 1)), full((tm, tm)),
            full((N_EXPERTS, LANES)),
        ],
        out_specs=[rowblk, rowblk,
                   pl.BlockSpec((SUBLANES, tm), lambda i: (0, i)),
                   pl.BlockSpec((SUBLANES, tm), lambda i: (0, i)),
                   full((N_EXPERTS, LANES))],
        scratch_shapes=[pltpu.VMEM((N_EXPERTS, LANES), F32)],
        compiler_params=_cparams(("arbitrary",)),
        name="ssm_out",
    )(u, y, y, d_skip, w_glu_bf, x, modpat, norm_ffn, modpat, modpat,
      rwt, rb, tri, cnt_in)


def _fnet_kernel(xseq_ref, nm_ref, sc1_ref, sh1_ref, cs_ref, dft_ref, wout_ref,
                 xrow_ref, g1_ref, nf_ref, sc2_ref, sh2_ref, rwt_ref, rb_ref,
                 tri_ref, cnt_in_ref, p1, p2, p3, p4,
                 x1_ref, h2_ref, ri_ref, rw_ref, cnt_ref, ucs, carry,
                 *, seq, chunk, scale):
    del p1, p2, p3, p4
    b = pl.program_id(0)
    i = pl.program_id(1)
    n_groups = D_MODEL // FNET_GROUP

    @pl.when(i == 0)
    def _():
        def stage1(r, _):
            rows = pl.ds(pl.multiple_of(r * chunk, chunk), chunk)
            h = _norm_mod(xseq_ref[rows, :], nm_ref[...], sc1_ref[...],
                          sh1_ref[...]).astype(BF16)
            for k in range(n_groups):
                cols = slice(FNET_GROUP * k, FNET_GROUP * (k + 1))
                t = jnp.dot(h[:, cols], cs_ref[...], preferred_element_type=F32)
                ucs[rows, cols] = t[:, :FNET_GROUP].astype(BF16)
                ucs[pl.ds(pl.multiple_of(seq + r * chunk, chunk), chunk), cols] = (
                    t[:, FNET_GROUP:].astype(BF16))
            return 0
        lax.fori_loop(0, seq // chunk, stage1, 0)

    y = jnp.dot(dft_ref[...], ucs[...], preferred_element_type=F32) * scale
    m = jnp.dot(y.astype(BF16), wout_ref[...], preferred_element_type=F32)
    _post_mixer(m, xrow_ref[...], g1_ref[...], nf_ref[...], sc2_ref[...],
                sh2_ref[...], rwt_ref, rb_ref, tri_ref, cnt_in_ref,
                x1_ref, h2_ref, ri_ref, rw_ref, cnt_ref, carry,
                jnp.logical_and(b == 0, i == 0))


def _fnet(x, prev, norm_mix, modpat, pat_of_batch, w_out_bf, norm_ffn,
          rwt, rb, cnt_in, *, batch, seq, row_off):
    n = x.shape[0]
    tr = min(seq, TOKEN_TILE)
    n_tiles = seq // tr
    chunk = min(seq, 256)
    p = modpat.shape[1]
    seq_off = row_off // seq
    tile_off = row_off // tr
    scale = float((seq * FNET_GROUP) ** -0.5)

    k = jnp.arange(seq, dtype=I32)
    ang = ((k[:, None] * k[None, :]) % seq).astype(F32) * (2.0 * jnp.pi / seq)
    dft = jnp.concatenate([jnp.cos(ang), -jnp.sin(ang)], axis=1).astype(BF16)
    kc = jnp.arange(FNET_GROUP, dtype=I32)
    angc = ((kc[:, None] * kc[None, :]) % FNET_GROUP).astype(F32) * (
        2.0 * jnp.pi / FNET_GROUP)
    cs = jnp.concatenate([jnp.cos(angc), jnp.sin(angc)], axis=1).astype(BF16)
    tri = jnp.triu(jnp.ones((tr, tr), BF16), k=1)

    mod = lambda kk: pl.BlockSpec((None, p, D_MODEL),
                                  lambda b, i: (pat_of_batch(b), 0, kk))
    full = lambda shape: pl.BlockSpec(shape, lambda b, i: (0,) * len(shape))
    rowblk = pl.BlockSpec((tr, D_MODEL),
                          lambda b, i: (tile_off + b * n_tiles + i, 0))
    colblk = pl.BlockSpec((SUBLANES, tr),
                          lambda b, i: (0, tile_off + b * n_tiles + i))
    anyspec = pl.BlockSpec(memory_space=pl.ANY)
    kern = functools.partial(_fnet_kernel, seq=seq, chunk=chunk, scale=scale)
    return pl.pallas_call(
        kern,
        out_shape=_route_out_shapes(n),
        grid=(batch, n_tiles),
        in_specs=[
            pl.BlockSpec((seq, D_MODEL), lambda b, i: (seq_off + b, 0)),
            full((1, D_MODEL)), mod(1), mod(0),
            full((FNET_GROUP, 2 * FNET_GROUP)),
            pl.BlockSpec((tr, 2 * seq), lambda b, i: (i, 0)),
            full((D_MODEL, D_MODEL)),
            rowblk,
            mod(2), full((1, D_MODEL)), mod(4), mod(3),
            full((N_EXPERTS, D_MODEL)), full((N_EXPERTS, 1)), full((tr, tr)),
            full((N_EXPERTS, LANES)),
            anyspec, anyspec, anyspec, anyspec,
        ],
        out_specs=[rowblk, rowblk, colblk, colblk, full((N_EXPERTS, LANES))],
        scratch_shapes=[pltpu.VMEM((2 * seq, D_MODEL), BF16),
                        pltpu.VMEM((N_EXPERTS, LANES), F32)],
        input_output_aliases={16: 0, 17: 1, 18: 2, 19: 3},
        compiler_params=_cparams(("arbitrary", "arbitrary")),
        name="fnet_b%d" % batch,
    )(x, norm_mix, modpat, modpat, cs, dft, w_out_bf, x, modpat, norm_ffn,
      modpat, modpat, rwt, rb, tri, cnt_in, *prev)


def _row_copy(src, src_row, dst, dst_row, sem):
    return pltpu.make_async_copy(src.at[pl.ds(src_row, 1), :],
                                 dst.at[pl.ds(dst_row, 1), :], sem)


def _dispatch_kernel(s1_ref, s2_ref, h_ref, xs_in, xs_ref, sem):
    del xs_in
    tm = h_ref.shape[0]
    base = pl.program_id(0) * tm

    def issue(r, _):
        _row_copy(h_ref, r, xs_ref, s1_ref[base + r], sem).start()
        _row_copy(h_ref, r, xs_ref, s2_ref[base + r], sem).start()
        return 0
    lax.fori_loop(0, tm, issue, 0)

    def drain(r, _):
        _row_copy(h_ref, 0, xs_ref, 0, sem).wait()
        _row_copy(h_ref, 0, xs_ref, 0, sem).wait()
        return 0
    lax.fori_loop(0, tm, drain, 0)


def _dispatch(slot1, slot2, h2, n_slots):
    n = h2.shape[0]
    tm = TOKEN_TILE
    xs0 = jnp.zeros((n_slots, D_MODEL), F32)
    return pl.pallas_call(
        _dispatch_kernel,
        out_shape=jax.ShapeDtypeStruct((n_slots, D_MODEL), F32),
        grid_spec=pltpu.PrefetchScalarGridSpec(
            num_scalar_prefetch=2,
            grid=(n // tm,),
            in_specs=[pl.BlockSpec((tm, D_MODEL), lambda i, s1, s2: (i, 0)),
                      pl.BlockSpec(memory_space=pl.ANY)],
            out_specs=pl.BlockSpec(memory_space=pl.ANY),
            scratch_shapes=[pltpu.SemaphoreType.DMA],
        ),
        input_output_aliases={3: 0},
        compiler_params=_cparams(("arbitrary",)),
        name="moe_dispatch",
    )(slot1, slot2, h2, xs0)


def _ffn_kernel(te_ref, tf_ref, nu_ref, x_ref, wg_ref, wu_ref, wd_ref, o_ref,
                wgb, wub, wdb):
    i = pl.program_id(0)

    @pl.when(i < nu_ref[0])
    def _():
        @pl.when(tf_ref[i] == 1)
        def _():
            wgb[...] = wg_ref[...].astype(BF16)
            wub[...] = wu_ref[...].astype(BF16)
            wdb[...] = wd_ref[...].astype(BF16)

        x = x_ref[...].astype(BF16)
        g = jnp.dot(x, wgb[...], preferred_element_type=F32)
        u = jnp.dot(x, wub[...], preferred_element_type=F32)
        a = (g * _sigmoid(g)) * u
        o_ref[...] = jnp.dot(a.astype(BF16), wdb[...],
                             preferred_element_type=F32)

    @pl.when(i >= nu_ref[0])
    def _():
        o_ref[...] = jnp.zeros(o_ref.shape, F32)


def _expert_ffn(tile_expert, tile_first, n_used, xs, w_gate, w_up, w_down):
    n_slots = xs.shape[0]
    n_tiles = n_slots // FFN_TILE
    row = lambda i, te, tf, nu: (jnp.minimum(i, nu[0] - 1), 0)
    wspec = lambda a, b: pl.BlockSpec((None, a, b),
                                      lambda i, te, tf, nu: (te[i], 0, 0))
    return pl.pallas_call(
        _ffn_kernel,
        out_shape=jax.ShapeDtypeStruct((n_slots, D_MODEL), F32),
        grid_spec=pltpu.PrefetchScalarGridSpec(
            num_scalar_prefetch=3,
            grid=(n_tiles,),
            in_specs=[pl.BlockSpec((FFN_TILE, D_MODEL), row),
                      wspec(D_MODEL, D_EXPERT), wspec(D_MODEL, D_EXPERT),
                      wspec(D_EXPERT, D_MODEL)],
            out_specs=pl.BlockSpec((FFN_TILE, D_MODEL),
                                   lambda i, te, tf, nu: (i, 0)),
            scratch_shapes=[pltpu.VMEM((D_MODEL, D_EXPERT), BF16),
                            pltpu.VMEM((D_MODEL, D_EXPERT), BF16),
                            pltpu.VMEM((D_EXPERT, D_MODEL), BF16)],
        ),
        compiler_params=_cparams(("arbitrary",)),
        name="moe_ffn",
    )(tile_expert, tile_first, n_used, xs, w_gate, w_up, w_down)


def _combine_kernel(s1_ref, s2_ref, w_ref, x1_ref, g2_ref, nfin_ref, ys_ref,
                    x2_ref, o1, o2, sem, *, final):
    tm = x1_ref.shape[0]
    base = pl.program_id(0) * tm

    def issue(r, _):
        _row_copy(ys_ref, s1_ref[base + r], o1, r, sem).start()
        _row_copy(ys_ref, s2_ref[base + r], o2, r, sem).start()
        return 0
    lax.fori_loop(0, tm, issue, 0)

    def drain(r, _):
        _row_copy(ys_ref, 0, o1, 0, sem).wait()
        _row_copy(ys_ref, 0, o2, 0, sem).wait()
        return 0
    lax.fori_loop(0, tm, drain, 0)

    w = w_ref[...]
    y = w[:, 0:1] * o1[...] + w[:, 1:2] * o2[...]
    x2 = x1_ref[...] + _per_row(y, g2_ref[...], lambda a, b: a * b)
    if final:
        ms = jnp.mean(x2 * x2, axis=-1, keepdims=True)
        x2 = x2 * lax.rsqrt(ms + EPS) * nfin_ref[...]
    x2_ref[...] = x2


def _combine(slot1, slot2, w_cols, x1, modpat, pat_of_tile, norm_final, ys,
             *, final):
    n = x1.shape[0]
    tm = TOKEN_TILE
    p = modpat.shape[1]
    rowblk = pl.BlockSpec((tm, D_MODEL), lambda i, s1, s2: (i, 0))
    kern = functools.partial(_combine_kernel, final=final)
    return pl.pallas_call(
        kern,
        out_shape=jax.ShapeDtypeStruct((n, D_MODEL), F32),
        grid_spec=pltpu.PrefetchScalarGridSpec(
            num_scalar_prefetch=2,
            grid=(n // tm,),
            in_specs=[
                pl.BlockSpec((tm, 2), lambda i, s1, s2: (i, 0)),
                rowblk,
                pl.BlockSpec((None, p, D_MODEL),
                             lambda i, s1, s2: (pat_of_tile(i), 0, 5)),
                pl.BlockSpec((1, D_MODEL), lambda i, s1, s2: (0, 0)),
                pl.BlockSpec(memory_space=pl.ANY),
            ],
            out_specs=rowblk,
            scratch_shapes=[pltpu.VMEM((tm, D_MODEL), F32),
                            pltpu.VMEM((tm, D_MODEL), F32),
                            pltpu.SemaphoreType.DMA],
        ),
        compiler_params=_cparams(("arbitrary",)),
        name="moe_combine",
    )(slot1, slot2, w_cols, x1, modpat, norm_final, ys)


def _moe(h2, ri, rw, cnt, x1, modpat, pat_of_tile, w_gate, w_up, w_down,
         norm_final, *, final):
    n = h2.shape[0]
    max_tiles = (2 * n) // FFN_TILE + N_EXPERTS
    n_slots = max_tiles * FFN_TILE
    counts = cnt[:, 0].astype(I32)
    padded = ((counts + FFN_TILE - 1) // FFN_TILE) * FFN_TILE
    ends = jnp.cumsum(padded)
    offs = ends - padded
    slot1 = jnp.take(offs, ri[0]) + ri[2]
    slot2 = jnp.take(offs, ri[1]) + ri[3]
    n_used = ends[-1] // FFN_TILE
    starts = jnp.arange(max_tiles, dtype=I32) * FFN_TILE
    te = jnp.sum((starts[:, None] >= ends[None, :]).astype(I32), axis=1)
    te = jnp.minimum(te, N_EXPERTS - 1)
    te = jnp.where(jnp.arange(max_tiles) < n_used, te, jnp.take(te, n_used - 1))
    tf = jnp.concatenate([jnp.ones((1,), I32),
                          (te[1:] != te[:-1]).astype(I32)])
    xs = _dispatch(slot1, slot2, h2, n_slots)
    ys = _expert_ffn(te, tf, n_used.reshape(1), xs, w_gate, w_up, w_down)
    w_cols = rw[0:2].T
    return _combine(slot1, slot2, w_cols, x1, modpat, pat_of_tile, norm_final,
                    ys, final=final)


def _block_diag_weights(bb_re, bb_im, c_re, c_im):
    lt = D_MODEL // LANES
    gl = SSM_GROUPS // lt
    eye = jnp.eye(gl, dtype=F32)
    def in_map(bb):
        b5 = bb.reshape(2, lt, gl, STATE_DIM, SSM_GROUP)
        t = jnp.einsum("dinph,kn->dikhnp", b5, eye)
        return t.reshape(2, lt, gl * SSM_GROUP, gl * STATE_DIM)
    def out_map(cc):
        c5 = cc.reshape(2, lt, gl, SSM_GROUP, STATE_DIM)
        t = jnp.einsum("dikhp,kn->dikpnh", c5, eye)
        return t.reshape(2, lt, gl * STATE_DIM, gl * SSM_GROUP)
    bt = jnp.concatenate([in_map(bb_re), in_map(bb_im)], axis=-1).astype(BF16)
    return bt, out_map(c_re).astype(BF16), out_map(c_im).astype(BF16)


def kernel(x_prompt, x_sample, c, state_ssm_re, state_ssm_im, c_ctx, norm_mix, norm_ffn, w_ada, b_ada, ssm_w_in, ssm_lam_re, ssm_lam_im, ssm_log_dt, ssm_b_re, ssm_b_im, ssm_c_re, ssm_c_im, ssm_d, ssm_w_glu, fnet_w_out, router_w, router_b, moe_w_gate, moe_w_up, moe_w_down, norm_final):
    bp, lp, _ = x_prompt.shape
    bs, ls, _ = x_sample.shape
    n_p = bp * lp
    n_s = bs * ls
    n = n_p + n_s
    tm = TOKEN_TILE

    cond = jnp.zeros((N_COND, D_MODEL), F32).at[0].set(c_ctx).at[1:1 + bs].set(c)
    modtab = _ada_table(cond, w_ada, b_ada).reshape(DEPTH, N_COND, N_MOD, D_MODEL)

    rwt = router_w.T
    rb = router_b.reshape(N_EXPERTS, 1)
    cnt0 = jnp.zeros((N_EXPERTS, LANES), F32)
    row = lambda v: v.reshape(1, D_MODEL)

    x_tm = jnp.concatenate([x_prompt.transpose(1, 0, 2).reshape(n_p, D_MODEL),
                            x_sample.transpose(1, 0, 2).reshape(n_s, D_MODEL)])
    period = max(bp, bs)
    pat_tm = jnp.stack([
        jnp.broadcast_to(modtab[0, 0], (period, N_MOD, D_MODEL)),
        jnp.tile(modtab[0, 1:1 + bs], (period // bs, 1, 1))])
    pat_tm = pat_tm.reshape(2, period, N_MOD * D_MODEL)
    pat_of_tile_tm = lambda i: jnp.where(i < n_p // tm, 0, 1)

    a_re, a_im, bb_re, bb_im = _zoh(ssm_lam_re[0], ssm_lam_im[0], ssm_log_dt[0],
                                    ssm_b_re[0], ssm_b_im[0])
    bt, ct_re, ct_im = _block_diag_weights(bb_re, bb_im, ssm_c_re[0], ssm_c_im[0])
    a_re = a_re.reshape(2, 1, N_STATE)
    a_im = a_im.reshape(2, 1, N_STATE)

    u = _ssm_in(x_tm, row(norm_mix[0]), pat_tm, pat_of_tile_tm,
                ssm_w_in[0].astype(BF16))
    zeros_p = jnp.zeros((2, bp, N_STATE), F32)
    h0s_re = state_ssm_re[:, 0].reshape(bs, 2, N_STATE).transpose(1, 0, 2)
    h0s_im = state_ssm_im[:, 0].reshape(bs, 2, N_STATE).transpose(1, 0, 2)
    y_buf = jnp.zeros((2, n, D_MODEL), F32)
    y_buf, fin_re, fin_im = _ssm_scan(u, y_buf, bt, ct_re, ct_im, a_re, a_im,
                                      zeros_p, zeros_p, batch=bp, seq=lp,
                                      row_off=0, steps=32)
    y_buf, _, _ = _ssm_scan(u, y_buf, bt, ct_re, ct_im, a_re, a_im,
                            h0s_re, h0s_im, batch=bs, seq=ls,
                            row_off=n_p, steps=64)
    tri = jnp.triu(jnp.ones((tm, tm), BF16), k=1)
    x1, h2, ri, rw, cnt = _ssm_out(u, y_buf, row(ssm_d[0]),
                                   ssm_w_glu[0].astype(BF16), x_tm, pat_tm,
                                   pat_of_tile_tm, row(norm_ffn[0]), rwt, rb,
                                   tri, cnt0)
    x2_tm = _moe(h2, ri, rw, cnt, x1, pat_tm, pat_of_tile_tm, moe_w_gate[0],
                 moe_w_up[0], moe_w_down[0], row(norm_final), final=False)

    x2 = jnp.concatenate([
        x2_tm[:n_p].reshape(lp, bp, D_MODEL).transpose(1, 0, 2).reshape(n_p, D_MODEL),
        x2_tm[n_p:].reshape(ls, bs, D_MODEL).transpose(1, 0, 2).reshape(n_s, D_MODEL)])
    pat_bm = jnp.broadcast_to(modtab[1][:1 + bs, None],
                              (1 + bs, SUBLANES, N_MOD, D_MODEL))
    pat_bm = pat_bm.reshape(1 + bs, SUBLANES, N_MOD * D_MODEL)
    pat_of_tile_bm = lambda i: jnp.where(i < n_p // tm, 0,
                                         1 + (i - n_p // tm) // (ls // tm))
    w_out_bf = fnet_w_out[0].astype(BF16)
    prev = tuple(jnp.zeros(s.shape, s.dtype) for s in _route_out_shapes(n)[:4])
    outs = _fnet(x2, prev, row(norm_mix[1]), pat_bm, lambda b: 0, w_out_bf,
                 row(norm_ffn[1]), rwt, rb, cnt0, batch=bp, seq=lp, row_off=0)
    outs = _fnet(x2, tuple(outs[:4]), row(norm_mix[1]), pat_bm, lambda b: 1 + b,
                 w_out_bf, row(norm_ffn[1]), rwt, rb, outs[4], batch=bs, seq=ls,
                 row_off=n_p)
    x3, h2, ri, rw, cnt = outs
    y_all = _moe(h2, ri, rw, cnt, x3, pat_bm, pat_of_tile_bm, moe_w_gate[1],
                 moe_w_up[1], moe_w_down[1], row(norm_final), final=True)

    y_prompt = y_all[:n_p].reshape(bp, lp, D_MODEL)
    y_sample = y_all[n_p:].reshape(bs, ls, D_MODEL)
    st = lambda f: f.transpose(1, 0, 2).reshape(bp, 1, 2, SSM_GROUPS, STATE_DIM)
    return (y_prompt, y_sample, st(fin_re), st(fin_im))
```

```python
import functools

import jax
import jax.numpy as jnp
from jax import lax
from jax.experimental import pallas as pl
from jax.experimental.pallas import tpu as pltpu

F32 = jnp.float32
BF16 = jnp.bfloat16
I32 = jnp.int32
HIGHEST = lax.Precision.HIGHEST

D_MODEL = 1024
DEPTH = 2
SSM_GROUP = 16
SSM_GROUPS = 64
STATE_DIM = 64
N_STATE = SSM_GROUPS * STATE_DIM
FNET_GROUP = 128
N_EXPERTS = 16
N_EXPERT_GROUPS = 4
EXPERTS_PER_GROUP = 4
D_EXPERT = 1024
N_MOD = 6
EPS = 1e-6

LANES = 128
SUBLANES = 8
LANE_TILES = D_MODEL // LANES
VMEM_LIMIT = 56 * 1024 * 1024

TOKEN_TILE = 512
FNET_TILE = 256
FFN_TILE = 256
N_COND = 16

PAIRS = ((0, 1), (0, 2), (0, 3), (1, 3), (1, 2), (3, 2))
N_PAIRS = N_EXPERT_GROUPS * len(PAIRS)
N_CLASS_ROWS = 32
H_WIDTH = D_MODEL + LANES


def _cparams(sem, vmem=VMEM_LIMIT):
    return pltpu.CompilerParams(dimension_semantics=sem, vmem_limit_bytes=vmem)


def _sigmoid(x):
    return 1.0 / (1.0 + jnp.exp(-x))


def _gelu_tanh(x):
    c = 0.7978845608028654
    return x * (0.5 * (1.0 + jnp.tanh(c * (x + 0.044715 * (x * x * x)))))


def _per_row(v, pat, fn):
    tm, d = v.shape
    p = pat.shape[0]
    return fn(v.reshape(tm // p, p, d), pat[None]).reshape(tm, d)


def _rms(x):
    ms = jnp.mean(x * x, axis=-1, keepdims=True)
    return x * lax.rsqrt(ms + EPS)


def _norm_mod(x, g, sc, sh):
    y = _rms(x) * g
    y = _per_row(y, sc, lambda a, b: a * (1.0 + b))
    return _per_row(y, sh, lambda a, b: a + b)


def _lane_tiles(v):
    return [v[:, LANES * k:LANES * (k + 1)] for k in range(v.shape[1] // LANES)]


def _to_time_major(x_ref, scr, batch):
    tt = x_ref.shape[1]
    for b in range(batch):
        for k, piece in enumerate(_lane_tiles(x_ref[b])):
            scr[k, pl.ds(b, tt, stride=batch), :] = piece
    return jnp.concatenate([scr[k] for k in range(LANE_TILES)], axis=1)


def _from_time_major(v, o_ref, scr, batch):
    tt = v.shape[0] // batch
    for k, piece in enumerate(_lane_tiles(v)):
        scr[k] = piece
    for b in range(batch):
        o_ref[b] = jnp.concatenate(
            [scr[k, pl.ds(b, tt, stride=batch), :] for k in range(LANE_TILES)],
            axis=1)


def _route(logits_t, rb, tri, carry):
    ne, tm = logits_t.shape
    s = _sigmoid(logits_t)
    bz = s + rb
    row = lambda a, r: a[r:r + 1, :]
    gs = []
    for g in range(N_EXPERT_GROUPS):
        v0, v1, v2, v3 = (row(bz, EXPERTS_PER_GROUP * g + j) for j in range(4))
        hi1, lo1 = jnp.maximum(v0, v1), jnp.minimum(v0, v1)
        hi2, lo2 = jnp.maximum(v2, v3), jnp.minimum(v2, v3)
        top1 = jnp.maximum(hi1, hi2)
        top2 = jnp.maximum(jnp.minimum(hi1, hi2), jnp.maximum(lo1, lo2))
        gs.append(top1 + top2)
    bg = jnp.zeros((1, tm), I32)
    bv = gs[0]
    for g in range(1, N_EXPERT_GROUPS):
        upd = gs[g] > bv
        bg = jnp.where(upd, g, bg)
        bv = jnp.where(upd, gs[g], bv)
    cb, cs = [], []
    for j in range(EXPERTS_PER_GROUP):
        vb, vs = row(bz, j), row(s, j)
        for g in range(1, N_EXPERT_GROUPS):
            sel = bg == g
            vb = jnp.where(sel, row(bz, EXPERTS_PER_GROUP * g + j), vb)
            vs = jnp.where(sel, row(s, EXPERTS_PER_GROUP * g + j), vs)
        cb.append(vb)
        cs.append(vs)
    i1 = jnp.zeros((1, tm), I32)
    b1, s1 = cb[0], cs[0]
    for j in range(1, EXPERTS_PER_GROUP):
        upd = cb[j] > b1
        i1 = jnp.where(upd, j, i1)
        b1 = jnp.where(upd, cb[j], b1)
        s1 = jnp.where(upd, cs[j], s1)
    i2 = jnp.zeros((1, tm), I32)
    b2 = jnp.full((1, tm), -jnp.inf, F32)
    s2 = jnp.zeros((1, tm), F32)
    for j in range(EXPERTS_PER_GROUP):
        cand = jnp.where(i1 == j, -jnp.inf, cb[j])
        upd = cand > b2
        i2 = jnp.where(upd, j, i2)
        b2 = jnp.where(upd, cand, b2)
        s2 = jnp.where(upd, cs[j], s2)
    den = s1 + s2
    w1 = s1 / den
    w2 = s2 / den
    lo = jnp.minimum(i1, i2)
    hi = jnp.maximum(i1, i2)
    pidx = jnp.where(lo == 0, hi - 1,
                     jnp.where(lo == 1, jnp.where(hi == 3, 3, 4), 5))
    first = jnp.where(pidx < 3, 0, jnp.where(pidx < 5, 1, 3))
    wa = jnp.where(i1 == first, w1, w2)
    wb = jnp.where(i1 == first, w2, w1)
    q = len(PAIRS) * bg + pidx
    qio = lax.broadcasted_iota(I32, (N_CLASS_ROWS, tm), 0)
    oh = qio == q
    ohf = jnp.where(oh, 1.0, 0.0)
    cum = jnp.dot(ohf.astype(BF16), tri, preferred_element_type=F32) + carry
    rank = jnp.sum(jnp.where(oh, cum, 0.0), axis=0, keepdims=True)
    new_carry = carry + jnp.sum(ohf, axis=1, keepdims=True)
    return q, rank.astype(I32), wa, wb, new_carry


def _post_mixer(m, x, g1, nf, sc2, sh2, rwt_ref, rb_ref, tri_ref,
                x1_ref, h2_ref, ri_ref, cnt_ref, carry, first):
    @pl.when(first)
    def _():
        carry[...] = jnp.zeros(carry.shape, F32)

    tm = x.shape[0]
    x1 = x + _per_row(m, g1, lambda a, b: a * b)
    x1_ref[...] = x1
    h2 = _norm_mod(x1, nf, sc2, sh2)
    h2_ref[:, :D_MODEL] = h2
    logits_t = lax.dot_general(rwt_ref[...], h2, (((1,), (1,)), ((), ())),
                               precision=HIGHEST, preferred_element_type=F32)
    q, rank, wa, wb, nc = _route(logits_t, rb_ref[...], tri_ref[...],
                                 carry[:, 0:1])
    rio = lax.broadcasted_iota(I32, (LANES, tm), 0)
    wrows = jnp.where(rio == 0, wa, jnp.where(rio == 1, wb, 0.0))
    h2_ref[:, D_MODEL:] = wrows.T
    ri_ref[0:1, :] = q
    ri_ref[1:2, :] = rank
    ri_ref[2:8, :] = jnp.zeros((6, tm), I32)
    carry[...] = jnp.broadcast_to(nc, carry.shape)
    cnt_ref[...] = carry[...]


def _ada_kernel(cond_ref, w_ref, b_ref, o_ref):
    c = cond_ref[...]
    s = c * _sigmoid(c)
    o_ref[...] = jnp.dot(s, w_ref[...], precision=HIGHEST,
                         preferred_element_type=F32) + b_ref[...]


def _ada_table(cond, w_ada, b_ada):
    tn = 1536
    n_out = N_MOD * D_MODEL
    return pl.pallas_call(
        _ada_kernel,
        out_shape=jax.ShapeDtypeStruct((DEPTH, N_COND, n_out), F32),
        grid=(DEPTH, n_out // tn),
        in_specs=[
            pl.BlockSpec((N_COND, D_MODEL), lambda l, j: (0, 0)),
            pl.BlockSpec((None, D_MODEL, tn), lambda l, j: (l, 0, j)),
            pl.BlockSpec((None, 1, tn), lambda l, j: (l, 0, j)),
        ],
        out_specs=pl.BlockSpec((None, N_COND, tn), lambda l, j: (l, 0, j)),
        compiler_params=_cparams(("arbitrary", "arbitrary")),
        name="ada_table",
    )(cond, w_ada, b_ada.reshape(DEPTH, 1, n_out))


def _zoh_kernel(lr_ref, li_ref, ldt_ref, br_ref, bi_ref,
                are_ref, aim_ref, bbre_ref, bbim_ref):
    lr = lr_ref[...]
    li = li_ref[...]
    dt = jnp.exp(ldt_ref[...])
    mag = jnp.exp(lr * dt)
    a_re = mag * jnp.cos(li * dt)
    a_im = mag * jnp.sin(li * dt)
    den = lr * lr + li * li
    nr = a_re - 1.0
    f_re = (nr * lr + a_im * li) / den
    f_im = (a_im * lr - nr * li) / den
    br = br_ref[...]
    bi = bi_ref[...]
    are_ref[...] = a_re
    aim_ref[...] = a_im
    bbre_ref[...] = f_re * br - f_im * bi
    bbim_ref[...] = f_re * bi + f_im * br


def _zoh(lam_re, lam_im, log_dt, b_re, b_im):
    shape = b_re.shape
    flat = (shape[0] * shape[1] * shape[2] * shape[3] // LANES, LANES)
    bc = lambda a: jnp.broadcast_to(a, shape).reshape(flat)
    args = (bc(lam_re[..., None]), bc(lam_im[..., None]),
            bc(log_dt[:, :, None, None]), b_re.reshape(flat), b_im.reshape(flat))
    outs = pl.pallas_call(
        _zoh_kernel,
        out_shape=[jax.ShapeDtypeStruct(flat, F32)] * 4,
        name="zoh_discretize",
    )(*args)
    a_re, a_im, bb_re, bb_im = (o.reshape(shape) for o in outs)
    return a_re[..., 0], a_im[..., 0], bb_re, bb_im


def _stream_specs(xp, xs, n_p_tiles):
    bp, bs = xp.shape[0], xs.shape[0]
    return [
        pl.BlockSpec((bp, TOKEN_TILE // bp, D_MODEL),
                     lambda i: (0, jnp.minimum(i, n_p_tiles - 1), 0)),
        pl.BlockSpec((bs, TOKEN_TILE // bs, D_MODEL),
                     lambda i: (0, jnp.maximum(i - n_p_tiles, 0), 0)),
    ]


def _ssm_in_kernel(xp_ref, xs_ref, g_ref, sc_ref, sh_ref, w_ref, u_ref, scr,
                   *, n_p_tiles):
    def run(x_ref):
        x = _to_time_major(x_ref, scr, x_ref.shape[0])
        h = _norm_mod(x, g_ref[...], sc_ref[...], sh_ref[...])
        u_ref[...] = jnp.dot(h.astype(BF16), w_ref[...],
                             preferred_element_type=F32)

    i = pl.program_id(0)
    pl.when(i < n_p_tiles)(lambda: run(xp_ref))
    pl.when(i >= n_p_tiles)(lambda: run(xs_ref))


def _ssm_in(xp, xs, g, modpat, w_in_bf):
    n = (xp.shape[0] * xp.shape[1] + xs.shape[0] * xs.shape[1])
    tm = TOKEN_TILE
    n_p_tiles = xp.shape[0] * xp.shape[1] // tm
    p = modpat.shape[1]
    pat = lambda i: jnp.where(i < n_p_tiles, 0, 1)
    mod = lambda k: pl.BlockSpec((None, p, D_MODEL), lambda i: (pat(i), 0, k))
    return pl.pallas_call(
        functools.partial(_ssm_in_kernel, n_p_tiles=n_p_tiles),
        out_shape=jax.ShapeDtypeStruct((n, D_MODEL), F32),
        grid=(n // tm,),
        in_specs=_stream_specs(xp, xs, n_p_tiles) + [
            pl.BlockSpec((1, D_MODEL), lambda i: (0, 0)),
            mod(1), mod(0),
            pl.BlockSpec((D_MODEL, D_MODEL), lambda i: (0, 0)),
        ],
        out_specs=pl.BlockSpec((tm, D_MODEL), lambda i: (i, 0)),
        scratch_shapes=[pltpu.VMEM((LANE_TILES, tm, LANES), F32)],
        compiler_params=_cparams(("arbitrary",)),
        name="ssm_in",
    )(xp, xs, g, modpat, modpat, w_in_bf)


def _scan_kernel(u_ref, bt_ref, cre_ref, cim_ref, are_ref, aim_ref,
                 h0re_ref, h0im_ref, y_ref, fre_ref, fim_ref,
                 xre, xim, hre, him, *, n_p_chunks, bp, bs):
    d = pl.program_id(0)
    c = pl.program_id(1)
    cols = N_STATE // LANE_TILES
    rows = u_ref.shape[0]

    def chunk(batch, state_tile):
        steps = rows // batch
        u = u_ref[...].astype(BF16)
        for i in range(LANE_TILES):
            xt = jnp.dot(u[:, LANES * i:LANES * (i + 1)], bt_ref[i],
                         preferred_element_type=F32)
            xre[:, cols * i:cols * (i + 1)] = xt[:, :cols]
            xim[:, cols * i:cols * (i + 1)] = xt[:, cols:]

        for s in range(N_STATE // state_tile):
            sl = slice(s * state_tile, (s + 1) * state_tile)
            ar = jnp.broadcast_to(are_ref[:, sl], (batch, state_tile))
            ai = jnp.broadcast_to(aim_ref[:, sl], (batch, state_tile))

            def body(t, carry, sl=sl, ar=ar, ai=ai):
                hr, hi = carry
                tt = jnp.where(d == 0, t, steps - 1 - t)
                r = pl.ds(pl.multiple_of(tt * batch, batch), batch)
                nr = ar * hr - ai * hi + xre[r, sl]
                ni = ar * hi + ai * hr + xim[r, sl]
                xre[r, sl] = nr
                xim[r, sl] = ni
                return nr, ni

            hr, hi = lax.fori_loop(0, steps, body,
                                   (hre[0:batch, sl], him[0:batch, sl]))
            hre[0:batch, sl] = hr
            him[0:batch, sl] = hi

        for i in range(LANE_TILES):
            hr = xre[:, cols * i:cols * (i + 1)].astype(BF16)
            hi = xim[:, cols * i:cols * (i + 1)].astype(BF16)
            y_ref[:, LANES * i:LANES * (i + 1)] = (
                jnp.dot(hr, cre_ref[i], preferred_element_type=F32)
                - jnp.dot(hi, cim_ref[i], preferred_element_type=F32))

    @pl.when(c == 0)
    def _():
        hre[...] = jnp.zeros(hre.shape, F32)
        him[...] = jnp.zeros(him.shape, F32)

    @pl.when(c == n_p_chunks)
    def _():
        hre[0:bs, :] = h0re_ref[...]
        him[0:bs, :] = h0im_ref[...]

    pl.when(c < n_p_chunks)(lambda: chunk(bp, 512))
    pl.when(c >= n_p_chunks)(lambda: chunk(bs, 1024))

    @pl.when(c == n_p_chunks - 1)
    def _():
        fre_ref[...] = hre[0:bp, :]
        fim_ref[...] = him[0:bp, :]


def _ssm_scan(u, bt, ct_re, ct_im, a_re, a_im, h0_re, h0_im, *, bp, lp, bs, ls):
    n = u.shape[0]
    rows = TOKEN_TILE
    n_p_chunks = bp * lp // rows
    n_s_chunks = bs * ls // rows

    def chunk(d, c):
        in_p = jnp.where(d == 0, c, n_p_chunks - 1 - c)
        cs = c - n_p_chunks
        in_s = n_p_chunks + jnp.where(d == 0, cs, n_s_chunks - 1 - cs)
        return jnp.where(c < n_p_chunks, in_p, in_s)

    cols = N_STATE // LANE_TILES
    dspec = lambda shape: pl.BlockSpec((None,) + shape,
                                       lambda d, c: (d,) + (0,) * len(shape))
    kern = functools.partial(_scan_kernel, n_p_chunks=n_p_chunks, bp=bp, bs=bs)
    return pl.pallas_call(
        kern,
        out_shape=[jax.ShapeDtypeStruct((2, n, D_MODEL), F32),
                   jax.ShapeDtypeStruct((2, bp, N_STATE), F32),
                   jax.ShapeDtypeStruct((2, bp, N_STATE), F32)],
        grid=(2, n_p_chunks + n_s_chunks),
        in_specs=[
            pl.BlockSpec((rows, D_MODEL), lambda d, c: (chunk(d, c), 0)),
            dspec((LANE_TILES, LANES, 2 * cols)),
            dspec((LANE_TILES, cols, LANES)),
            dspec((LANE_TILES, cols, LANES)),
            dspec((1, N_STATE)),
            dspec((1, N_STATE)),
            dspec((bs, N_STATE)),
            dspec((bs, N_STATE)),
        ],
        out_specs=[
            pl.BlockSpec((None, rows, D_MODEL), lambda d, c: (d, chunk(d, c), 0)),
            dspec((bp, N_STATE)),
            dspec((bp, N_STATE)),
        ],
        scratch_shapes=[pltpu.VMEM((rows, N_STATE), F32),
                        pltpu.VMEM((rows, N_STATE), F32),
                        pltpu.VMEM((max(bp, bs), N_STATE), F32),
                        pltpu.VMEM((max(bp, bs), N_STATE), F32)],
        compiler_params=_cparams(("arbitrary", "arbitrary")),
        name="ssm_scan",
    )(u, bt, ct_re, ct_im, a_re, a_im, h0_re, h0_im)


def _ssm_out_kernel(u_ref, y0_ref, y1_ref, dsk_ref, wglu_ref, xp_ref, xs_ref,
                    g1_ref, nf_ref, sc2_ref, sh2_ref, rwt_ref, rb_ref, tri_ref,
                    x1_ref, h2_ref, ri_ref, cnt_ref, scr, carry, *, n_p_tiles):
    i = pl.program_id(0)

    def run(x_ref):
        x = _to_time_major(x_ref, scr, x_ref.shape[0])
        y = u_ref[...] * dsk_ref[...] + y0_ref[...] + y1_ref[...]
        ge = _gelu_tanh(y).astype(BF16)
        vg = jnp.dot(ge, wglu_ref[...], preferred_element_type=F32)
        m = vg[:, :D_MODEL] * _sigmoid(vg[:, D_MODEL:])
        _post_mixer(m, x, g1_ref[...], nf_ref[...], sc2_ref[...], sh2_ref[...],
                    rwt_ref, rb_ref, tri_ref, x1_ref, h2_ref, ri_ref, cnt_ref,
                    carry, i == 0)

    pl.when(i < n_p_tiles)(lambda: run(xp_ref))
    pl.when(i >= n_p_tiles)(lambda: run(xs_ref))


def _route_out_shapes(n):
    return [jax.ShapeDtypeStruct((n, D_MODEL), F32),
            jax.ShapeDtypeStruct((n, H_WIDTH), F32),
            jax.ShapeDtypeStruct((SUBLANES, n), I32),
            jax.ShapeDtypeStruct((N_CLASS_ROWS, LANES), F32)]


def _ssm_out(u, y, d_skip, w_glu_bf, xp, xs, modpat, norm_ffn, rwt, rb, tri):
    n = u.shape[0]
    tm = TOKEN_TILE
    n_p_tiles = xp.shape[0] * xp.shape[1] // tm
    p = modpat.shape[1]
    pat = lambda i: jnp.where(i < n_p_tiles, 0, 1)
    mod = lambda k: pl.BlockSpec((None, p, D_MODEL), lambda i: (pat(i), 0, k))
    full = lambda shape: pl.BlockSpec(shape, lambda i: (0,) * len(shape))
    rowblk = pl.BlockSpec((tm, D_MODEL), lambda i: (i, 0))
    return pl.pallas_call(
        functools.partial(_ssm_out_kernel, n_p_tiles=n_p_tiles),
        out_shape=_route_out_shapes(n),
        grid=(n // tm,),
        in_specs=[
            rowblk,
            pl.BlockSpec((None, tm, D_MODEL), lambda i: (0, i, 0)),
            pl.BlockSpec((None, tm, D_MODEL), lambda i: (1, i, 0)),
            full((1, D_MODEL)),
            full((D_MODEL, 2 * D_MODEL)),
        ] + _stream_specs(xp, xs, n_p_tiles) + [
            mod(2), full((1, D_MODEL)), mod(4), mod(3),
            full((N_EXPERTS, D_MODEL)), full((N_EXPERTS, 1)), full((tm, tm)),
        ],
        out_specs=[rowblk,
                   pl.BlockSpec((tm, H_WIDTH), lambda i: (i, 0)),
                   pl.BlockSpec((SUBLANES, tm), lambda i: (0, i)),
                   full((N_CLASS_ROWS, LANES))],
        scratch_shapes=[pltpu.VMEM((LANE_TILES, tm, LANES), F32),
                        pltpu.VMEM((N_CLASS_ROWS, LANES), F32)],
        compiler_params=_cparams(("arbitrary",)),
        name="ssm_out",
    )(u, y, y, d_skip, w_glu_bf, xp, xs, modpat, norm_ffn, modpat, modpat,
      rwt, rb, tri)


def _fnet_kernel(xp_ref, xs_ref, nm_ref, sc1_ref, sh1_ref, cs_ref, dftp_ref,
                 dfts_ref, wout_ref, g1_ref, nf_ref, sc2_ref, sh2_ref, rwt_ref,
                 rb_ref, tri_ref, x1_ref, h2_ref, ri_ref, cnt_ref, ucs, carry,
                 *, bp, tiles_s):
    s = pl.program_id(0)
    n_groups = D_MODEL // FNET_GROUP
    tr = FNET_TILE

    def run(x_ref, dft_ref, i):
        seq = x_ref.shape[0]
        scale = float((seq * FNET_GROUP) ** -0.5)

        @pl.when(i == 0)
        def _():
            def stage1(r, _):
                rows = pl.ds(pl.multiple_of(r * tr, tr), tr)
                h = _norm_mod(x_ref[rows, :], nm_ref[...], sc1_ref[...],
                              sh1_ref[...]).astype(BF16)
                for k in range(n_groups):
                    cols = slice(FNET_GROUP * k, FNET_GROUP * (k + 1))
                    t = jnp.dot(h[:, cols], cs_ref[...],
                                preferred_element_type=F32)
                    ucs[rows, cols] = t[:, :FNET_GROUP].astype(BF16)
                    ucs[pl.ds(pl.multiple_of(seq + r * tr, tr), tr), cols] = (
                        t[:, FNET_GROUP:].astype(BF16))
                return 0
            lax.fori_loop(0, seq // tr, stage1, 0)

        y = jnp.dot(dft_ref[...], ucs[0:2 * seq, :],
                    preferred_element_type=F32) * scale
        m = jnp.dot(y.astype(BF16), wout_ref[...], preferred_element_type=F32)
        xrow = x_ref[pl.ds(pl.multiple_of(i * tr, tr), tr), :]
        _post_mixer(m, xrow, g1_ref[...], nf_ref[...], sc2_ref[...],
                    sh2_ref[...], rwt_ref, rb_ref, tri_ref, x1_ref, h2_ref,
                    ri_ref, cnt_ref, carry, s == 0)

    pl.when(s < bp)(lambda: run(xp_ref, dftp_ref, 0 * s))
    pl.when(s >= bp)(lambda: run(xs_ref, dfts_ref, (s - bp) % tiles_s))


def _dft_table(seq):
    k = jnp.arange(seq, dtype=I32)
    ang = ((k[:, None] * k[None, :]) % seq).astype(F32) * (2.0 * jnp.pi / seq)
    return jnp.concatenate([jnp.cos(ang), -jnp.sin(ang)], axis=1).astype(BF16)


def _fnet(xp, xs, norm_mix, modpat, w_out_bf, norm_ffn, rwt, rb):
    bp, lp, _ = xp.shape
    bs, ls, _ = xs.shape
    tr = FNET_TILE
    assert lp == tr
    tiles_s = ls // tr
    n = bp * lp + bs * ls
    p = modpat.shape[1]
    kc = jnp.arange(FNET_GROUP, dtype=I32)
    angc = ((kc[:, None] * kc[None, :]) % FNET_GROUP).astype(F32) * (
        2.0 * jnp.pi / FNET_GROUP)
    cs = jnp.concatenate([jnp.cos(angc), jnp.sin(angc)], axis=1).astype(BF16)
    tri = jnp.triu(jnp.ones((tr, tr), BF16), k=1)

    sb = lambda s: jnp.clip((s - bp) // tiles_s, 0, bs - 1)
    pat = lambda s: jnp.where(s < bp, 0, 1 + sb(s))
    mod = lambda kk: pl.BlockSpec((None, p, D_MODEL), lambda s: (pat(s), 0, kk))
    full = lambda shape: pl.BlockSpec(shape, lambda s: (0,) * len(shape))
    kern = functools.partial(_fnet_kernel, bp=bp, tiles_s=tiles_s)
    return pl.pallas_call(
        kern,
        out_shape=_route_out_shapes(n),
        grid=(bp + bs * tiles_s,),
        in_specs=[
            pl.BlockSpec((None, lp, D_MODEL),
                         lambda s: (jnp.minimum(s, bp - 1), 0, 0)),
            pl.BlockSpec((None, ls, D_MODEL), lambda s: (sb(s), 0, 0)),
            full((1, D_MODEL)), mod(1), mod(0),
            full((FNET_GROUP, 2 * FNET_GROUP)),
            full((tr, 2 * lp)),
            pl.BlockSpec((tr, 2 * ls),
                         lambda s: (jnp.maximum(s - bp, 0) % tiles_s, 0)),
            full((D_MODEL, D_MODEL)),
            mod(2), full((1, D_MODEL)), mod(4), mod(3),
            full((N_EXPERTS, D_MODEL)), full((N_EXPERTS, 1)), full((tr, tr)),
        ],
        out_specs=[pl.BlockSpec((tr, D_MODEL), lambda s: (s, 0)),
                   pl.BlockSpec((tr, H_WIDTH), lambda s: (s, 0)),
                   pl.BlockSpec((SUBLANES, tr), lambda s: (0, s)),
                   full((N_CLASS_ROWS, LANES))],
        scratch_shapes=[pltpu.VMEM((2 * ls, D_MODEL), BF16),
                        pltpu.VMEM((N_CLASS_ROWS, LANES), F32)],
        compiler_params=_cparams(("arbitrary",)),
        name="fnet",
    )(xp, xs, norm_mix, modpat, modpat, cs, _dft_table(lp), _dft_table(ls),
      w_out_bf, modpat, norm_ffn, modpat, modpat, rwt, rb, tri)


def _invert_kernel(slot_ref, src_ref):
    n = slot_ref.shape[0]
    unroll = 8

    def fill(j, _):
        for k in range(unroll):
            src_ref[j * unroll + k] = n
        return 0
    lax.fori_loop(0, src_ref.shape[0] // unroll, fill, 0)

    def body(j, _):
        for k in range(unroll):
            t = j * unroll + k
            src_ref[slot_ref[t]] = t
        return 0
    lax.fori_loop(0, n // unroll, body, 0)


def _invert(slot, n_slots):
    return pl.pallas_call(
        _invert_kernel,
        out_shape=jax.ShapeDtypeStruct((n_slots,), I32),
        in_specs=[pl.BlockSpec(memory_space=pltpu.SMEM)],
        out_specs=pl.BlockSpec(memory_space=pltpu.SMEM),
        name="moe_invert",
    )(slot)


def _ffn_kernel(tea_ref, teb_ref, nu_ref, src_ref, h_hbm,
                wga, wua, wda, wgb, wub, wdb, y_hbm,
                xbuf0, xbuf1, obuf0, obuf1, gsem, ssem, *, n_tok):
    del tea_ref, teb_ref
    i = pl.program_id(0)
    nu = nu_ref[0]
    t = FFN_TILE

    def row(ref, r):
        return ref.at[pl.ds(r, 1), :]

    def gather_start(tile, xbuf, sem):
        base = tile * t
        for r in range(t):
            tok = jnp.minimum(src_ref[base + r], n_tok - 1)
            pltpu.make_async_copy(row(h_hbm, tok), row(xbuf, r), sem).start()

    def gather_wait(xbuf, sem):
        pltpu.make_async_copy(h_hbm.at[pl.ds(0, t), :], xbuf, sem).wait()

    def scatter_start(tile, obuf, sem, live):
        base = tile * t
        for r in range(t):
            tok = src_ref[base + r]
            keep = jnp.logical_and(live, tok < n_tok)
            dst = jnp.where(keep, tok, n_tok + r)
            pltpu.make_async_copy(row(obuf, r), row(y_hbm, dst), sem).start()

    def scatter_wait(obuf, sem):
        pltpu.make_async_copy(obuf, y_hbm.at[pl.ds(0, t), :], sem).wait()

    def expert(xb, wg, wu, wd):
        g = jnp.dot(xb, wg[...], preferred_element_type=F32)
        u = jnp.dot(xb, wu[...], preferred_element_type=F32)
        a = (g * _sigmoid(g)) * u
        return jnp.dot(a.astype(BF16), wd[...], preferred_element_type=F32)

    def step(xc, xn, oc, op, gc, gn, sc, sp):
        @pl.when(i == 0)
        def _():
            op[...] = jnp.zeros(op.shape, F32)
            gather_start(0, xc, gc)

        @pl.when(i >= 1)
        def _():
            scatter_wait(oc, sc)

        gather_wait(xc, gc)
        gather_start(jnp.minimum(i + 1, nu - 1), xn, gn)
        scatter_start(jnp.maximum(i - 1, 0), op, sp, i >= 1)
        x = xc[...]
        xb = x[:, :D_MODEL].astype(BF16)
        ya = expert(xb, wga, wua, wda)
        yb = expert(xb, wgb, wub, wdb)
        oc[...] = (x[:, D_MODEL:D_MODEL + 1] * ya
                   + x[:, D_MODEL + 1:D_MODEL + 2] * yb)

    def drain(xc, oc, op, gc, sc, sp):
        scatter_wait(oc, sc)
        gather_wait(xc, gc)
        scatter_start(nu - 1, op, sp, True)
        scatter_wait(op, sp)

    even = (i % 2) == 0
    odd = jnp.logical_not(even)
    g0, g1, s0, s1 = gsem.at[0], gsem.at[1], ssem.at[0], ssem.at[1]
    pl.when(jnp.logical_and(i < nu, even))(
        lambda: step(xbuf0, xbuf1, obuf0, obuf1, g0, g1, s0, s1))
    pl.when(jnp.logical_and(i < nu, odd))(
        lambda: step(xbuf1, xbuf0, obuf1, obuf0, g1, g0, s1, s0))
    pl.when(jnp.logical_and(i == nu, even))(
        lambda: drain(xbuf0, obuf0, obuf1, g0, s0, s1))
    pl.when(jnp.logical_and(i == nu, odd))(
        lambda: drain(xbuf1, obuf1, obuf0, g1, s1, s0))


def _expert_ffn(tea, teb, n_used, src, h2, wg, wu, wd):
    n_tok = h2.shape[0]
    max_tiles = src.shape[0] // FFN_TILE
    wa = lambda a, b: pl.BlockSpec((None, a, b),
                                   lambda i, ta, tb, nu, sr: (ta[i], 0, 0))
    wb = lambda a, b: pl.BlockSpec((None, a, b),
                                   lambda i, ta, tb, nu, sr: (tb[i], 0, 0))
    anyspec = pl.BlockSpec(memory_space=pl.ANY)
    return pl.pallas_call(
        functools.partial(_ffn_kernel, n_tok=n_tok),
        out_shape=jax.ShapeDtypeStruct((n_tok + FFN_TILE, D_MODEL), F32),
        grid_spec=pltpu.PrefetchScalarGridSpec(
            num_scalar_prefetch=4,
            grid=(max_tiles + 1,),
            in_specs=[anyspec,
                      wa(D_MODEL, D_EXPERT), wa(D_MODEL, D_EXPERT),
                      wa(D_EXPERT, D_MODEL),
                      wb(D_MODEL, D_EXPERT), wb(D_MODEL, D_EXPERT),
                      wb(D_EXPERT, D_MODEL)],
            out_specs=anyspec,
            scratch_shapes=[pltpu.VMEM((FFN_TILE, H_WIDTH), F32),
                            pltpu.VMEM((FFN_TILE, H_WIDTH), F32),
                            pltpu.VMEM((FFN_TILE, D_MODEL), F32),
                            pltpu.VMEM((FFN_TILE, D_MODEL), F32),
                            pltpu.SemaphoreType.DMA((2,)),
                            pltpu.SemaphoreType.DMA((2,))],
        ),
        compiler_params=_cparams(("arbitrary",)),
        name="moe_ffn",
    )(tea, teb, n_used, src, h2, wg, wu, wd, wg, wu, wd)


def _moe(h2, ri, cnt, w_gate, w_up, w_down):
    n = h2.shape[0]
    max_tiles = n // FFN_TILE + N_PAIRS
    n_slots = max_tiles * FFN_TILE
    counts = cnt[:N_PAIRS, 0].astype(I32)
    padded = ((counts + FFN_TILE - 1) // FFN_TILE) * FFN_TILE
    ends = jnp.cumsum(padded)
    offs = ends - padded
    q, rank = ri[0], ri[1]
    cls = jnp.arange(N_PAIRS, dtype=I32)
    slot = rank + jnp.sum(jnp.where(q[None, :] == cls[:, None], offs[:, None], 0),
                          axis=0)
    n_used = ends[-1] // FFN_TILE
    tile = jnp.arange(max_tiles + 1, dtype=I32)
    tq = jnp.sum((tile[:, None] * FFN_TILE >= ends[None, :]).astype(I32), axis=1)
    tq_last = jnp.sum(((n_used - 1) * FFN_TILE >= ends).astype(I32))
    tq = jnp.where(tile < n_used, tq, tq_last)
    pa = jnp.array([a for a, _ in PAIRS], I32)
    pb = jnp.array([b for _, b in PAIRS], I32)
    grp, pidx = tq // len(PAIRS), tq % len(PAIRS)
    tea = EXPERTS_PER_GROUP * grp + jnp.take(pa, pidx)
    teb = EXPERTS_PER_GROUP * grp + jnp.take(pb, pidx)
    src = _invert(slot, n_slots)
    return _expert_ffn(tea, teb, n_used.reshape(1), src, h2,
                       w_gate.astype(BF16), w_up.astype(BF16),
                       w_down.astype(BF16))


def _moe_out_tm_kernel(x1_ref, y_ref, g2_ref, op_ref, os_ref, scr, *, n_p_tiles):
    i = pl.program_id(0)
    x2 = x1_ref[...] + _per_row(y_ref[...], g2_ref[...], lambda a, b: a * b)
    pl.when(i < n_p_tiles)(
        lambda: _from_time_major(x2, op_ref, scr, op_ref.shape[0]))
    pl.when(i >= n_p_tiles)(
        lambda: _from_time_major(x2, os_ref, scr, os_ref.shape[0]))


def _moe_out_tm(x1, y, modpat, shape_p, shape_s):
    n = x1.shape[0]
    tm = TOKEN_TILE
    n_p_tiles = shape_p[0] * shape_p[1] // tm
    p = modpat.shape[1]
    rowblk = pl.BlockSpec((tm, D_MODEL), lambda i: (i, 0))
    stream = _stream_specs(jax.ShapeDtypeStruct(shape_p, F32),
                           jax.ShapeDtypeStruct(shape_s, F32), n_p_tiles)
    return pl.pallas_call(
        functools.partial(_moe_out_tm_kernel, n_p_tiles=n_p_tiles),
        out_shape=[jax.ShapeDtypeStruct(shape_p, F32),
                   jax.ShapeDtypeStruct(shape_s, F32)],
        grid=(n // tm,),
        in_specs=[rowblk, rowblk,
                  pl.BlockSpec((None, p, D_MODEL),
                               lambda i: (jnp.where(i < n_p_tiles, 0, 1), 0, 5))],
        out_specs=stream,
        scratch_shapes=[pltpu.VMEM((LANE_TILES, tm, LANES), F32)],
        compiler_params=_cparams(("arbitrary",)),
        name="moe_out_tm",
    )(x1, y, modpat)


def _moe_out_final_kernel(x1_ref, y_ref, g2_ref, nfin_ref, op_ref, os_ref,
                          *, n_p_tiles):
    i = pl.program_id(0)
    x2 = x1_ref[...] + _per_row(y_ref[...], g2_ref[...], lambda a, b: a * b)
    out = _rms(x2) * nfin_ref[...]

    @pl.when(i < n_p_tiles)
    def _():
        op_ref[...] = out

    @pl.when(i >= n_p_tiles)
    def _():
        os_ref[...] = out


def _moe_out_final(x1, y, modpat, norm_final, n_p, rows_per_request):
    n = x1.shape[0]
    tm = TOKEN_TILE
    n_p_tiles = n_p // tm
    p = modpat.shape[1]
    rowblk = pl.BlockSpec((tm, D_MODEL), lambda i: (i, 0))
    pat = lambda i: jnp.where(
        i < n_p_tiles, 0, 1 + (i - n_p_tiles) // (rows_per_request // tm))
    return pl.pallas_call(
        functools.partial(_moe_out_final_kernel, n_p_tiles=n_p_tiles),
        out_shape=[jax.ShapeDtypeStruct((n_p, D_MODEL), F32),
                   jax.ShapeDtypeStruct((n - n_p, D_MODEL), F32)],
        grid=(n // tm,),
        in_specs=[rowblk, rowblk,
                  pl.BlockSpec((None, p, D_MODEL), lambda i: (pat(i), 0, 5)),
                  pl.BlockSpec((1, D_MODEL), lambda i: (0, 0))],
        out_specs=[
            pl.BlockSpec((tm, D_MODEL),
                         lambda i: (jnp.minimum(i, n_p_tiles - 1), 0)),
            pl.BlockSpec((tm, D_MODEL),
                         lambda i: (jnp.maximum(i - n_p_tiles, 0), 0))],
        compiler_params=_cparams(("arbitrary",)),
        name="moe_out_final",
    )(x1, y, modpat, norm_final)


def _block_diag_weights(bb_re, bb_im, c_re, c_im):
    lt = LANE_TILES
    gl = SSM_GROUPS // lt
    eye = jnp.eye(gl, dtype=F32)

    def in_map(bb):
        b5 = bb.reshape(2, lt, gl, STATE_DIM, SSM_GROUP)
        t = jnp.einsum("dinph,kn->dikhnp", b5, eye)
        return t.reshape(2, lt, gl * SSM_GROUP, gl * STATE_DIM)

    def out_map(cc):
        c5 = cc.reshape(2, lt, gl, SSM_GROUP, STATE_DIM)
        t = jnp.einsum("dikhp,kn->dikpnh", c5, eye)
        return t.reshape(2, lt, gl * STATE_DIM, gl * SSM_GROUP)

    bt = jnp.concatenate([in_map(bb_re), in_map(bb_im)], axis=-1).astype(BF16)
    return bt, out_map(c_re).astype(BF16), out_map(c_im).astype(BF16)


def kernel(x_prompt, x_sample, c, state_ssm_re, state_ssm_im, c_ctx, norm_mix, norm_ffn, w_ada, b_ada, ssm_w_in, ssm_lam_re, ssm_lam_im, ssm_log_dt, ssm_b_re, ssm_b_im, ssm_c_re, ssm_c_im, ssm_d, ssm_w_glu, fnet_w_out, router_w, router_b, moe_w_gate, moe_w_up, moe_w_down, norm_final):
    bp, lp, _ = x_prompt.shape
    bs, ls, _ = x_sample.shape
    n_p = bp * lp
    n_s = bs * ls
    n = n_p + n_s
    tm = TOKEN_TILE

    cond = jnp.zeros((N_COND, D_MODEL), F32).at[0].set(c_ctx).at[1:1 + bs].set(c)
    modtab = _ada_table(cond, w_ada, b_ada).reshape(DEPTH, N_COND, N_MOD, D_MODEL)

    rwt = router_w.T
    rb = router_b.reshape(N_EXPERTS, 1)
    row = lambda v: v.reshape(1, D_MODEL)

    period = max(bp, bs)
    pat_tm = jnp.stack([
        jnp.broadcast_to(modtab[0, 0], (period, N_MOD, D_MODEL)),
        jnp.tile(modtab[0, 1:1 + bs], (period // bs, 1, 1))])
    pat_tm = pat_tm.reshape(2, period, N_MOD * D_MODEL)

    a_re, a_im, bb_re, bb_im = _zoh(ssm_lam_re[0], ssm_lam_im[0], ssm_log_dt[0],
                                    ssm_b_re[0], ssm_b_im[0])
    bt, ct_re, ct_im = _block_diag_weights(bb_re, bb_im, ssm_c_re[0], ssm_c_im[0])
    a_re = a_re.reshape(2, 1, N_STATE)
    a_im = a_im.reshape(2, 1, N_STATE)

    u = _ssm_in(x_prompt, x_sample, row(norm_mix[0]), pat_tm,
                ssm_w_in[0].astype(BF16))
    h0s_re = state_ssm_re[:, 0].reshape(bs, 2, N_STATE).transpose(1, 0, 2)
    h0s_im = state_ssm_im[:, 0].reshape(bs, 2, N_STATE).transpose(1, 0, 2)
    y_scan, fin_re, fin_im = _ssm_scan(u, bt, ct_re, ct_im, a_re, a_im,
                                       h0s_re, h0s_im, bp=bp, lp=lp, bs=bs, ls=ls)
    tri = jnp.triu(jnp.ones((tm, tm), BF16), k=1)
    x1, h2, ri, cnt = _ssm_out(u, y_scan, row(ssm_d[0]),
                               ssm_w_glu[0].astype(BF16), x_prompt, x_sample,
                               pat_tm, row(norm_ffn[0]), rwt, rb, tri)
    y_moe = _moe(h2, ri, cnt, moe_w_gate[0], moe_w_up[0], moe_w_down[0])
    x2_p, x2_s = _moe_out_tm(x1, y_moe, pat_tm, x_prompt.shape, x_sample.shape)

    pat_bm = jnp.broadcast_to(modtab[1][:1 + bs, None],
                              (1 + bs, SUBLANES, N_MOD, D_MODEL))
    pat_bm = pat_bm.reshape(1 + bs, SUBLANES, N_MOD * D_MODEL)
    x3, h2, ri, cnt = _fnet(x2_p, x2_s, row(norm_mix[1]), pat_bm,
                            fnet_w_out[0].astype(BF16), row(norm_ffn[1]), rwt, rb)
    y_moe = _moe(h2, ri, cnt, moe_w_gate[1], moe_w_up[1], moe_w_down[1])
    y_p, y_s = _moe_out_final(x3, y_moe, pat_bm, row(norm_final), n_p, ls)

    st = lambda f: f.transpose(1, 0, 2).reshape(bp, 1, 2, SSM_GROUPS, STATE_DIM)
    return (y_p.reshape(bp, lp, D_MODEL), y_s.reshape(bs, ls, D_MODEL),
            st(fin_re), st(fin_im))
```

```python
import functools

import jax
import jax.numpy as jnp
from jax import lax
from jax.experimental import pallas as pl
from jax.experimental.pallas import tpu as pltpu

F32 = jnp.float32
BF16 = jnp.bfloat16
I32 = jnp.int32
HIGHEST = lax.Precision.HIGHEST

D_MODEL = 1024
DEPTH = 2
SSM_GROUP = 16
SSM_GROUPS = 64
STATE_DIM = 64
N_STATE = SSM_GROUPS * STATE_DIM
FNET_GROUP = 128
N_EXPERTS = 16
N_EXPERT_GROUPS = 4
EXPERTS_PER_GROUP = 4
D_EXPERT = 1024
N_MOD = 6
EPS = 1e-6

LANES = 128
SUBLANES = 8
LANE_TILES = D_MODEL // LANES
VMEM_LIMIT = 56 * 1024 * 1024

TOKEN_TILE = 512
FNET_TILE = 256
FFN_TILE = 256
N_COND = 16

PAIRS = ((0, 1), (0, 2), (0, 3), (1, 3), (1, 2), (3, 2))
N_PAIRS = N_EXPERT_GROUPS * len(PAIRS)
N_CLASS_ROWS = 32
H_WIDTH = D_MODEL + LANES


def _cparams(sem, vmem=VMEM_LIMIT):
    return pltpu.CompilerParams(dimension_semantics=sem, vmem_limit_bytes=vmem)


def _sigmoid(x):
    return 1.0 / (1.0 + jnp.exp(-x))


def _gelu_tanh(x):
    c = 0.7978845608028654
    return x * (0.5 * (1.0 + jnp.tanh(c * (x + 0.044715 * (x * x * x)))))


def _per_row(v, pat, fn):
    tm, d = v.shape
    p = pat.shape[0]
    return fn(v.reshape(tm // p, p, d), pat[None]).reshape(tm, d)


def _rms(x):
    ms = jnp.mean(x * x, axis=-1, keepdims=True)
    return x * lax.rsqrt(ms + EPS)


def _norm_mod(x, g, sc, sh):
    y = _rms(x) * g
    y = _per_row(y, sc, lambda a, b: a * (1.0 + b))
    return _per_row(y, sh, lambda a, b: a + b)


def _lane_tiles(v):
    return [v[:, LANES * k:LANES * (k + 1)] for k in range(v.shape[1] // LANES)]


def _to_time_major(x_ref, scr, batch):
    tt = x_ref.shape[1]
    for b in range(batch):
        for k, piece in enumerate(_lane_tiles(x_ref[b])):
            scr[k, pl.ds(b, tt, stride=batch), :] = piece
    return jnp.concatenate([scr[k] for k in range(LANE_TILES)], axis=1)


def _from_time_major(v, o_ref, scr, batch):
    tt = v.shape[0] // batch
    for k, piece in enumerate(_lane_tiles(v)):
        scr[k] = piece
    for b in range(batch):
        o_ref[b] = jnp.concatenate(
            [scr[k, pl.ds(b, tt, stride=batch), :] for k in range(LANE_TILES)],
            axis=1)


def _route(logits_t, rb, tri, carry):
    ne, tm = logits_t.shape
    s = _sigmoid(logits_t)
    bz = s + rb
    row = lambda a, r: a[r:r + 1, :]
    gs = []
    for g in range(N_EXPERT_GROUPS):
        v0, v1, v2, v3 = (row(bz, EXPERTS_PER_GROUP * g + j) for j in range(4))
        hi1, lo1 = jnp.maximum(v0, v1), jnp.minimum(v0, v1)
        hi2, lo2 = jnp.maximum(v2, v3), jnp.minimum(v2, v3)
        top1 = jnp.maximum(hi1, hi2)
        top2 = jnp.maximum(jnp.minimum(hi1, hi2), jnp.maximum(lo1, lo2))
        gs.append(top1 + top2)
    bg = jnp.zeros((1, tm), I32)
    bv = gs[0]
    for g in range(1, N_EXPERT_GROUPS):
        upd = gs[g] > bv
        bg = jnp.where(upd, g, bg)
        bv = jnp.where(upd, gs[g], bv)
    cb, cs = [], []
    for j in range(EXPERTS_PER_GROUP):
        vb, vs = row(bz, j), row(s, j)
        for g in range(1, N_EXPERT_GROUPS):
            sel = bg == g
            vb = jnp.where(sel, row(bz, EXPERTS_PER_GROUP * g + j), vb)
            vs = jnp.where(sel, row(s, EXPERTS_PER_GROUP * g + j), vs)
        cb.append(vb)
        cs.append(vs)
    i1 = jnp.zeros((1, tm), I32)
    b1, s1 = cb[0], cs[0]
    for j in range(1, EXPERTS_PER_GROUP):
        upd = cb[j] > b1
        i1 = jnp.where(upd, j, i1)
        b1 = jnp.where(upd, cb[j], b1)
        s1 = jnp.where(upd, cs[j], s1)
    i2 = jnp.zeros((1, tm), I32)
    b2 = jnp.full((1, tm), -jnp.inf, F32)
    s2 = jnp.zeros((1, tm), F32)
    for j in range(EXPERTS_PER_GROUP):
        cand = jnp.where(i1 == j, -jnp.inf, cb[j])
        upd = cand > b2
        i2 = jnp.where(upd, j, i2)
        b2 = jnp.where(upd, cand, b2)
        s2 = jnp.where(upd, cs[j], s2)
    den = s1 + s2
    w1 = s1 / den
    w2 = s2 / den
    lo = jnp.minimum(i1, i2)
    hi = jnp.maximum(i1, i2)
    pidx = jnp.where(lo == 0, hi - 1,
                     jnp.where(lo == 1, jnp.where(hi == 3, 3, 4), 5))
    first = jnp.where(pidx < 3, 0, jnp.where(pidx < 5, 1, 3))
    wa = jnp.where(i1 == first, w1, w2)
    wb = jnp.where(i1 == first, w2, w1)
    q = len(PAIRS) * bg + pidx
    qio = lax.broadcasted_iota(I32, (N_CLASS_ROWS, tm), 0)
    oh = qio == q
    ohf = jnp.where(oh, 1.0, 0.0)
    cum = jnp.dot(ohf.astype(BF16), tri, preferred_element_type=F32) + carry
    rank = jnp.sum(jnp.where(oh, cum, 0.0), axis=0, keepdims=True)
    new_carry = carry + jnp.sum(ohf, axis=1, keepdims=True)
    return q, rank.astype(I32), wa, wb, new_carry


def _post_mixer(m, x, g1, nf, sc2, sh2, rwt_ref, rb_ref, tri_ref,
                x1_ref, h2_ref, ri_ref, cnt_ref, carry, first):
    @pl.when(first)
    def _():
        carry[...] = jnp.zeros(carry.shape, F32)

    tm = x.shape[0]
    x1 = x + _per_row(m, g1, lambda a, b: a * b)
    x1_ref[...] = x1
    h2 = _norm_mod(x1, nf, sc2, sh2)
    h2_ref[:, :D_MODEL] = h2
    logits_t = lax.dot_general(rwt_ref[...], h2, (((1,), (1,)), ((), ())),
                               precision=HIGHEST, preferred_element_type=F32)
    q, rank, wa, wb, nc = _route(logits_t, rb_ref[...], tri_ref[...],
                                 carry[:, 0:1])
    rio = lax.broadcasted_iota(I32, (LANES, tm), 0)
    wrows = jnp.where(rio == 0, wa, jnp.where(rio == 1, wb, 0.0))
    h2_ref[:, D_MODEL:] = wrows.T
    ri_ref[0:1, :] = q
    ri_ref[1:2, :] = rank
    ri_ref[2:8, :] = jnp.zeros((6, tm), I32)
    carry[...] = jnp.broadcast_to(nc, carry.shape)
    cnt_ref[...] = carry[...]


def _ada_kernel(cond_ref, w_ref, b_ref, o_ref):
    c = cond_ref[...]
    s = c * _sigmoid(c)
    o_ref[...] = jnp.dot(s, w_ref[...], precision=HIGHEST,
                         preferred_element_type=F32) + b_ref[...]


def _ada_table(cond, w_ada, b_ada):
    tn = 1536
    n_out = N_MOD * D_MODEL
    return pl.pallas_call(
        _ada_kernel,
        out_shape=jax.ShapeDtypeStruct((DEPTH, N_COND, n_out), F32),
        grid=(DEPTH, n_out // tn),
        in_specs=[
            pl.BlockSpec((N_COND, D_MODEL), lambda l, j: (0, 0)),
            pl.BlockSpec((None, D_MODEL, tn), lambda l, j: (l, 0, j)),
            pl.BlockSpec((None, 1, tn), lambda l, j: (l, 0, j)),
        ],
        out_specs=pl.BlockSpec((None, N_COND, tn), lambda l, j: (l, 0, j)),
        compiler_params=_cparams(("arbitrary", "arbitrary")),
        name="ada_table",
    )(cond, w_ada, b_ada.reshape(DEPTH, 1, n_out))


def _zoh_kernel(lr_ref, li_ref, ldt_ref, br_ref, bi_ref,
                are_ref, aim_ref, bbre_ref, bbim_ref):
    lr = lr_ref[...]
    li = li_ref[...]
    dt = jnp.exp(ldt_ref[...])
    mag = jnp.exp(lr * dt)
    a_re = mag * jnp.cos(li * dt)
    a_im = mag * jnp.sin(li * dt)
    den = lr * lr + li * li
    nr = a_re - 1.0
    f_re = (nr * lr + a_im * li) / den
    f_im = (a_im * lr - nr * li) / den
    br = br_ref[...]
    bi = bi_ref[...]
    are_ref[...] = a_re
    aim_ref[...] = a_im
    bbre_ref[...] = f_re * br - f_im * bi
    bbim_ref[...] = f_re * bi + f_im * br


def _zoh(lam_re, lam_im, log_dt, b_re, b_im):
    shape = b_re.shape
    flat = (shape[0] * shape[1] * shape[2] * shape[3] // LANES, LANES)
    bc = lambda a: jnp.broadcast_to(a, shape).reshape(flat)
    args = (bc(lam_re[..., None]), bc(lam_im[..., None]),
            bc(log_dt[:, :, None, None]), b_re.reshape(flat), b_im.reshape(flat))
    outs = pl.pallas_call(
        _zoh_kernel,
        out_shape=[jax.ShapeDtypeStruct(flat, F32)] * 4,
        name="zoh_discretize",
    )(*args)
    a_re, a_im, bb_re, bb_im = (o.reshape(shape) for o in outs)
    return a_re[..., 0], a_im[..., 0], bb_re, bb_im


def _stream_specs(xp, xs, n_p_tiles):
    bp, bs = xp.shape[0], xs.shape[0]
    return [
        pl.BlockSpec((bp, TOKEN_TILE // bp, D_MODEL),
                     lambda i: (0, jnp.minimum(i, n_p_tiles - 1), 0)),
        pl.BlockSpec((bs, TOKEN_TILE // bs, D_MODEL),
                     lambda i: (0, jnp.maximum(i - n_p_tiles, 0), 0)),
    ]


def _ssm_in_kernel(xp_ref, xs_ref, g_ref, sc_ref, sh_ref, w_ref, u_ref, scr,
                   *, n_p_tiles):
    def run(x_ref):
        x = _to_time_major(x_ref, scr, x_ref.shape[0])
        h = _norm_mod(x, g_ref[...], sc_ref[...], sh_ref[...])
        u_ref[...] = jnp.dot(h.astype(BF16), w_ref[...],
                             preferred_element_type=F32)

    i = pl.program_id(0)
    pl.when(i < n_p_tiles)(lambda: run(xp_ref))
    pl.when(i >= n_p_tiles)(lambda: run(xs_ref))


def _ssm_in(xp, xs, g, modpat, w_in_bf):
    n = (xp.shape[0] * xp.shape[1] + xs.shape[0] * xs.shape[1])
    tm = TOKEN_TILE
    n_p_tiles = xp.shape[0] * xp.shape[1] // tm
    p = modpat.shape[1]
    pat = lambda i: jnp.where(i < n_p_tiles, 0, 1)
    mod = lambda k: pl.BlockSpec((None, p, D_MODEL), lambda i: (pat(i), 0, k))
    return pl.pallas_call(
        functools.partial(_ssm_in_kernel, n_p_tiles=n_p_tiles),
        out_shape=jax.ShapeDtypeStruct((n, D_MODEL), F32),
        grid=(n // tm,),
        in_specs=_stream_specs(xp, xs, n_p_tiles) + [
            pl.BlockSpec((1, D_MODEL), lambda i: (0, 0)),
            mod(1), mod(0),
            pl.BlockSpec((D_MODEL, D_MODEL), lambda i: (0, 0)),
        ],
        out_specs=pl.BlockSpec((tm, D_MODEL), lambda i: (i, 0)),
        scratch_shapes=[pltpu.VMEM((LANE_TILES, tm, LANES), F32)],
        compiler_params=_cparams(("arbitrary",)),
        name="ssm_in",
    )(xp, xs, g, modpat, modpat, w_in_bf)


def _scan_kernel(u_ref, bt_ref, cre_ref, cim_ref, are_ref, aim_ref,
                 h0re_ref, h0im_ref, y_ref, fre_ref, fim_ref,
                 xre, xim, hre, him, *, n_p_chunks, bp, bs):
    d = pl.program_id(0)
    c = pl.program_id(1)
    cols = N_STATE // LANE_TILES
    rows = u_ref.shape[0]

    def chunk(batch, state_tile):
        steps = rows // batch
        u = u_ref[...].astype(BF16)
        for i in range(LANE_TILES):
            xt = jnp.dot(u[:, LANES * i:LANES * (i + 1)], bt_ref[i],
                         preferred_element_type=F32)
            xre[:, cols * i:cols * (i + 1)] = xt[:, :cols]
            xim[:, cols * i:cols * (i + 1)] = xt[:, cols:]

        for s in range(N_STATE // state_tile):
            sl = slice(s * state_tile, (s + 1) * state_tile)
            ar = jnp.broadcast_to(are_ref[:, sl], (batch, state_tile))
            ai = jnp.broadcast_to(aim_ref[:, sl], (batch, state_tile))

            def body(t, carry, sl=sl, ar=ar, ai=ai):
                hr, hi = carry
                tt = jnp.where(d == 0, t, steps - 1 - t)
                r = pl.ds(pl.multiple_of(tt * batch, batch), batch)
                nr = ar * hr - ai * hi + xre[r, sl]
                ni = ar * hi + ai * hr + xim[r, sl]
                xre[r, sl] = nr
                xim[r, sl] = ni
                return nr, ni

            hr, hi = lax.fori_loop(0, steps, body,
                                   (hre[0:batch, sl], him[0:batch, sl]))
            hre[0:batch, sl] = hr
            him[0:batch, sl] = hi

        for i in range(LANE_TILES):
            hr = xre[:, cols * i:cols * (i + 1)].astype(BF16)
            hi = xim[:, cols * i:cols * (i + 1)].astype(BF16)
            y_ref[:, LANES * i:LANES * (i + 1)] = (
                jnp.dot(hr, cre_ref[i], preferred_element_type=F32)
                - jnp.dot(hi, cim_ref[i], preferred_element_type=F32))

    @pl.when(c == 0)
    def _():
        hre[...] = jnp.zeros(hre.shape, F32)
        him[...] = jnp.zeros(him.shape, F32)

    @pl.when(c == n_p_chunks)
    def _():
        hre[0:bs, :] = h0re_ref[...]
        him[0:bs, :] = h0im_ref[...]

    pl.when(c < n_p_chunks)(lambda: chunk(bp, 512))
    pl.when(c >= n_p_chunks)(lambda: chunk(bs, 1024))

    @pl.when(c == n_p_chunks - 1)
    def _():
        fre_ref[...] = hre[0:bp, :]
        fim_ref[...] = him[0:bp, :]


def _ssm_scan(u, bt, ct_re, ct_im, a_re, a_im, h0_re, h0_im, *, bp, lp, bs, ls):
    n = u.shape[0]
    rows = TOKEN_TILE
    n_p_chunks = bp * lp // rows
    n_s_chunks = bs * ls // rows

    def chunk(d, c):
        in_p = jnp.where(d == 0, c, n_p_chunks - 1 - c)
        cs = c - n_p_chunks
        in_s = n_p_chunks + jnp.where(d == 0, cs, n_s_chunks - 1 - cs)
        return jnp.where(c < n_p_chunks, in_p, in_s)

    cols = N_STATE // LANE_TILES
    dspec = lambda shape: pl.BlockSpec((None,) + shape,
                                       lambda d, c: (d,) + (0,) * len(shape))
    kern = functools.partial(_scan_kernel, n_p_chunks=n_p_chunks, bp=bp, bs=bs)
    return pl.pallas_call(
        kern,
        out_shape=[jax.ShapeDtypeStruct((2, n, D_MODEL), F32),
                   jax.ShapeDtypeStruct((2, bp, N_STATE), F32),
                   jax.ShapeDtypeStruct((2, bp, N_STATE), F32)],
        grid=(2, n_p_chunks + n_s_chunks),
        in_specs=[
            pl.BlockSpec((rows, D_MODEL), lambda d, c: (chunk(d, c), 0)),
            dspec((LANE_TILES, LANES, 2 * cols)),
            dspec((LANE_TILES, cols, LANES)),
            dspec((LANE_TILES, cols, LANES)),
            dspec((1, N_STATE)),
            dspec((1, N_STATE)),
            dspec((bs, N_STATE)),
            dspec((bs, N_STATE)),
        ],
        out_specs=[
            pl.BlockSpec((None, rows, D_MODEL), lambda d, c: (d, chunk(d, c), 0)),
            dspec((bp, N_STATE)),
            dspec((bp, N_STATE)),
        ],
        scratch_shapes=[pltpu.VMEM((rows, N_STATE), F32),
                        pltpu.VMEM((rows, N_STATE), F32),
                        pltpu.VMEM((max(bp, bs), N_STATE), F32),
                        pltpu.VMEM((max(bp, bs), N_STATE), F32)],
        compiler_params=_cparams(("arbitrary", "arbitrary")),
        name="ssm_scan",
    )(u, bt, ct_re, ct_im, a_re, a_im, h0_re, h0_im)


def _ssm_out_kernel(u_ref, y0_ref, y1_ref, dsk_ref, wglu_ref, xp_ref, xs_ref,
                    g1_ref, nf_ref, sc2_ref, sh2_ref, rwt_ref, rb_ref, tri_ref,
                    x1_ref, h2_ref, ri_ref, cnt_ref, scr, carry, *, n_p_tiles):
    i = pl.program_id(0)

    def run(x_ref):
        x = _to_time_major(x_ref, scr, x_ref.shape[0])
        y = u_ref[...] * dsk_ref[...] + y0_ref[...] + y1_ref[...]
        ge = _gelu_tanh(y).astype(BF16)
        vg = jnp.dot(ge, wglu_ref[...], preferred_element_type=F32)
        m = vg[:, :D_MODEL] * _sigmoid(vg[:, D_MODEL:])
        _post_mixer(m, x, g1_ref[...], nf_ref[...], sc2_ref[...], sh2_ref[...],
                    rwt_ref, rb_ref, tri_ref, x1_ref, h2_ref, ri_ref, cnt_ref,
                    carry, i == 0)

    pl.when(i < n_p_tiles)(lambda: run(xp_ref))
    pl.when(i >= n_p_tiles)(lambda: run(xs_ref))


def _route_out_shapes(n):
    return [jax.ShapeDtypeStruct((n, D_MODEL), F32),
            jax.ShapeDtypeStruct((n, H_WIDTH), F32),
            jax.ShapeDtypeStruct((SUBLANES, n), I32),
            jax.ShapeDtypeStruct((N_CLASS_ROWS, LANES), F32)]


def _ssm_out(u, y, d_skip, w_glu_bf, xp, xs, modpat, norm_ffn, rwt, rb, tri):
    n = u.shape[0]
    tm = TOKEN_TILE
    n_p_tiles = xp.shape[0] * xp.shape[1] // tm
    p = modpat.shape[1]
    pat = lambda i: jnp.where(i < n_p_tiles, 0, 1)
    mod = lambda k: pl.BlockSpec((None, p, D_MODEL), lambda i: (pat(i), 0, k))
    full = lambda shape: pl.BlockSpec(shape, lambda i: (0,) * len(shape))
    rowblk = pl.BlockSpec((tm, D_MODEL), lambda i: (i, 0))
    return pl.pallas_call(
        functools.partial(_ssm_out_kernel, n_p_tiles=n_p_tiles),
        out_shape=_route_out_shapes(n),
        grid=(n // tm,),
        in_specs=[
            rowblk,
            pl.BlockSpec((None, tm, D_MODEL), lambda i: (0, i, 0)),
            pl.BlockSpec((None, tm, D_MODEL), lambda i: (1, i, 0)),
            full((1, D_MODEL)),
            full((D_MODEL, 2 * D_MODEL)),
        ] + _stream_specs(xp, xs, n_p_tiles) + [
            mod(2), full((1, D_MODEL)), mod(4), mod(3),
            full((N_EXPERTS, D_MODEL)), full((N_EXPERTS, 1)), full((tm, tm)),
        ],
        out_specs=[rowblk,
                   pl.BlockSpec((tm, H_WIDTH), lambda i: (i, 0)),
                   pl.BlockSpec((SUBLANES, tm), lambda i: (0, i)),
                   full((N_CLASS_ROWS, LANES))],
        scratch_shapes=[pltpu.VMEM((LANE_TILES, tm, LANES), F32),
                        pltpu.VMEM((N_CLASS_ROWS, LANES), F32)],
        compiler_params=_cparams(("arbitrary",)),
        name="ssm_out",
    )(u, y, y, d_skip, w_glu_bf, xp, xs, modpat, norm_ffn, modpat, modpat,
      rwt, rb, tri)


def _fnet_kernel(xp_ref, xs_ref, nm_ref, sc1_ref, sh1_ref, cs_ref, dftp_ref,
                 dfts_ref, wout_ref, g1_ref, nf_ref, sc2_ref, sh2_ref, rwt_ref,
                 rb_ref, tri_ref, x1_ref, h2_ref, ri_ref, cnt_ref, ucs, carry,
                 *, bp, tiles_s):
    s = pl.program_id(0)
    n_groups = D_MODEL // FNET_GROUP
    tr = FNET_TILE

    def run(x_ref, dft_ref, i):
        seq = x_ref.shape[0]
        scale = float((seq * FNET_GROUP) ** -0.5)

        @pl.when(i == 0)
        def _():
            def stage1(r, _):
                rows = pl.ds(pl.multiple_of(r * tr, tr), tr)
                h = _norm_mod(x_ref[rows, :], nm_ref[...], sc1_ref[...],
                              sh1_ref[...]).astype(BF16)
                for k in range(n_groups):
                    cols = slice(FNET_GROUP * k, FNET_GROUP * (k + 1))
                    t = jnp.dot(h[:, cols], cs_ref[...],
                                preferred_element_type=F32)
                    ucs[rows, cols] = t[:, :FNET_GROUP].astype(BF16)
                    ucs[pl.ds(pl.multiple_of(seq + r * tr, tr), tr), cols] = (
                        t[:, FNET_GROUP:].astype(BF16))
                return 0
            lax.fori_loop(0, seq // tr, stage1, 0)

        y = jnp.dot(dft_ref[...], ucs[0:2 * seq, :],
                    preferred_element_type=F32) * scale
        m = jnp.dot(y.astype(BF16), wout_ref[...], preferred_element_type=F32)
        xrow = x_ref[pl.ds(pl.multiple_of(i * tr, tr), tr), :]
        _post_mixer(m, xrow, g1_ref[...], nf_ref[...], sc2_ref[...],
                    sh2_ref[...], rwt_ref, rb_ref, tri_ref, x1_ref, h2_ref,
                    ri_ref, cnt_ref, carry, s == 0)

    pl.when(s < bp)(lambda: run(xp_ref, dftp_ref, 0 * s))
    pl.when(s >= bp)(lambda: run(xs_ref, dfts_ref, (s - bp) % tiles_s))


def _dft_table(seq):
    k = jnp.arange(seq, dtype=I32)
    ang = ((k[:, None] * k[None, :]) % seq).astype(F32) * (2.0 * jnp.pi / seq)
    return jnp.concatenate([jnp.cos(ang), -jnp.sin(ang)], axis=1).astype(BF16)


def _fnet(xp, xs, norm_mix, modpat, w_out_bf, norm_ffn, rwt, rb):
    bp, lp, _ = xp.shape
    bs, ls, _ = xs.shape
    tr = FNET_TILE
    assert lp == tr
    tiles_s = ls // tr
    n = bp * lp + bs * ls
    p = modpat.shape[1]
    kc = jnp.arange(FNET_GROUP, dtype=I32)
    angc = ((kc[:, None] * kc[None, :]) % FNET_GROUP).astype(F32) * (
        2.0 * jnp.pi / FNET_GROUP)
    cs = jnp.concatenate([jnp.cos(angc), jnp.sin(angc)], axis=1).astype(BF16)
    tri = jnp.triu(jnp.ones((tr, tr), BF16), k=1)

    sb = lambda s: jnp.clip((s - bp) // tiles_s, 0, bs - 1)
    pat = lambda s: jnp.where(s < bp, 0, 1 + sb(s))
    mod = lambda kk: pl.BlockSpec((None, p, D_MODEL), lambda s: (pat(s), 0, kk))
    full = lambda shape: pl.BlockSpec(shape, lambda s: (0,) * len(shape))
    kern = functools.partial(_fnet_kernel, bp=bp, tiles_s=tiles_s)
    return pl.pallas_call(
        kern,
        out_shape=_route_out_shapes(n),
        grid=(bp + bs * tiles_s,),
        in_specs=[
            pl.BlockSpec((None, lp, D_MODEL),
                         lambda s: (jnp.minimum(s, bp - 1), 0, 0)),
            pl.BlockSpec((None, ls, D_MODEL), lambda s: (sb(s), 0, 0)),
            full((1, D_MODEL)), mod(1), mod(0),
            full((FNET_GROUP, 2 * FNET_GROUP)),
            full((tr, 2 * lp)),
            pl.BlockSpec((tr, 2 * ls),
                         lambda s: (jnp.maximum(s - bp, 0) % tiles_s, 0)),
            full((D_MODEL, D_MODEL)),
            mod(2), full((1, D_MODEL)), mod(4), mod(3),
            full((N_EXPERTS, D_MODEL)), full((N_EXPERTS, 1)), full((tr, tr)),
        ],
        out_specs=[pl.BlockSpec((tr, D_MODEL), lambda s: (s, 0)),
                   pl.BlockSpec((tr, H_WIDTH), lambda s: (s, 0)),
                   pl.BlockSpec((SUBLANES, tr), lambda s: (0, s)),
                   full((N_CLASS_ROWS, LANES))],
        scratch_shapes=[pltpu.VMEM((2 * ls, D_MODEL), BF16),
                        pltpu.VMEM((N_CLASS_ROWS, LANES), F32)],
        compiler_params=_cparams(("arbitrary",)),
        name="fnet",
    )(xp, xs, norm_mix, modpat, modpat, cs, _dft_table(lp), _dft_table(ls),
      w_out_bf, modpat, norm_ffn, modpat, modpat, rwt, rb, tri)


def _invert_kernel(slot_ref, seg_ref, gsrc_ref, sdst_ref, *, n_pad_rows):
    n = slot_ref.shape[0]
    n_slots = gsrc_ref.shape[0]
    unroll = 8

    def pad(j, _):
        gsrc_ref[j] = 0
        sdst_ref[j] = n + j % n_pad_rows
        return 0
    for q in range(N_PAIRS):
        lax.fori_loop(seg_ref[2 * q], seg_ref[2 * q + 1], pad, 0)
    lax.fori_loop(seg_ref[2 * N_PAIRS - 1], n_slots, pad, 0)

    def body(j, _):
        for k in range(unroll):
            t = j * unroll + k
            s = slot_ref[t]
            gsrc_ref[s] = t
            sdst_ref[s] = t
        return 0
    lax.fori_loop(0, n // unroll, body, 0)


def _invert(slot, seg, n_slots):
    smem = pl.BlockSpec(memory_space=pltpu.SMEM)
    return pl.pallas_call(
        functools.partial(_invert_kernel, n_pad_rows=FFN_TILE),
        out_shape=[jax.ShapeDtypeStruct((n_slots,), I32)] * 2,
        in_specs=[smem, smem],
        out_specs=[smem, smem],
        name="moe_invert",
    )(slot, seg)


def _ffn_kernel(tea_ref, teb_ref, nu_ref, gsrc_ref, sdst_ref, h_hbm,
                wga, wua, wda, wgb, wub, wdb, y_hbm,
                xbuf0, xbuf1, obuf0, obuf1, gsem, ssem):
    del tea_ref, teb_ref
    i = pl.program_id(0)
    nu = nu_ref[0]
    t = FFN_TILE

    def row(ref, r):
        return ref.at[pl.ds(r, 1), :]

    def gather_start(tile, xbuf, sem):
        base = tile * t
        for r in range(t):
            pltpu.make_async_copy(row(h_hbm, gsrc_ref[base + r]), row(xbuf, r),
                                  sem).start()

    def gather_wait(xbuf, sem):
        pltpu.make_async_copy(h_hbm.at[pl.ds(0, t), :], xbuf, sem).wait()

    def scatter_start(tile, obuf, sem):
        base = tile * t
        for r in range(t):
            pltpu.make_async_copy(row(obuf, r), row(y_hbm, sdst_ref[base + r]),
                                  sem).start()

    def scatter_wait(obuf, sem):
        pltpu.make_async_copy(obuf, y_hbm.at[pl.ds(0, t), :], sem).wait()

    def expert(xb, wg, wu, wd):
        g = jnp.dot(xb, wg[...], preferred_element_type=F32)
        u = jnp.dot(xb, wu[...], preferred_element_type=F32)
        a = (g * _sigmoid(g)) * u
        return jnp.dot(a.astype(BF16), wd[...], preferred_element_type=F32)

    def step(xc, xn, oc, op, gc, gn, sc, sp):
        pl.when(i == 0)(lambda: gather_start(0, xc, gc))
        pl.when(i >= 2)(lambda: scatter_wait(oc, sc))
        gather_wait(xc, gc)
        pl.when(i + 1 < nu)(lambda: gather_start(i + 1, xn, gn))
        pl.when(i >= 1)(lambda: scatter_start(i - 1, op, sp))
        x = xc[...]
        xb = x[:, :D_MODEL].astype(BF16)
        ya = expert(xb, wga, wua, wda)
        yb = expert(xb, wgb, wub, wdb)
        oc[...] = (x[:, D_MODEL:D_MODEL + 1] * ya
                   + x[:, D_MODEL + 1:D_MODEL + 2] * yb)

    def drain(oc, op, sc, sp):
        pl.when(i >= 2)(lambda: scatter_wait(oc, sc))
        scatter_start(i - 1, op, sp)
        scatter_wait(op, sp)
        n_tok = y_hbm.shape[0] - t
        oc[...] = jnp.zeros(oc.shape, F32)
        spare = pltpu.make_async_copy(oc, y_hbm.at[pl.ds(n_tok, t), :], sc)
        spare.start()
        spare.wait()

    even = (i % 2) == 0
    odd = jnp.logical_not(even)
    g0, g1, s0, s1 = gsem.at[0], gsem.at[1], ssem.at[0], ssem.at[1]
    pl.when(jnp.logical_and(i < nu, even))(
        lambda: step(xbuf0, xbuf1, obuf0, obuf1, g0, g1, s0, s1))
    pl.when(jnp.logical_and(i < nu, odd))(
        lambda: step(xbuf1, xbuf0, obuf1, obuf0, g1, g0, s1, s0))
    pl.when(jnp.logical_and(i == nu, even))(
        lambda: drain(obuf0, obuf1, s0, s1))
    pl.when(jnp.logical_and(i == nu, odd))(
        lambda: drain(obuf1, obuf0, s1, s0))


def _expert_ffn(tea, teb, n_used, gsrc, sdst, h2, wg, wu, wd, layer):
    n_tok = h2.shape[0]
    max_tiles = gsrc.shape[0] // FFN_TILE
    wa = lambda a, b: pl.BlockSpec(
        (None, None, a, b), lambda i, ta, tb, nu, gs, sd: (layer, ta[i], 0, 0))
    wb = lambda a, b: pl.BlockSpec(
        (None, None, a, b), lambda i, ta, tb, nu, gs, sd: (layer, tb[i], 0, 0))
    anyspec = pl.BlockSpec(memory_space=pl.ANY)
    return pl.pallas_call(
        _ffn_kernel,
        out_shape=jax.ShapeDtypeStruct((n_tok + FFN_TILE, D_MODEL), F32),
        grid_spec=pltpu.PrefetchScalarGridSpec(
            num_scalar_prefetch=5,
            grid=(max_tiles + 1,),
            in_specs=[anyspec,
                      wa(D_MODEL, D_EXPERT), wa(D_MODEL, D_EXPERT),
                      wa(D_EXPERT, D_MODEL),
                      wb(D_MODEL, D_EXPERT), wb(D_MODEL, D_EXPERT),
                      wb(D_EXPERT, D_MODEL)],
            out_specs=anyspec,
            scratch_shapes=[pltpu.VMEM((FFN_TILE, H_WIDTH), F32),
                            pltpu.VMEM((FFN_TILE, H_WIDTH), F32),
                            pltpu.VMEM((FFN_TILE, D_MODEL), F32),
                            pltpu.VMEM((FFN_TILE, D_MODEL), F32),
                            pltpu.SemaphoreType.DMA((2,)),
                            pltpu.SemaphoreType.DMA((2,))],
        ),
        compiler_params=_cparams(("arbitrary",)),
        name="moe_ffn",
    )(tea, teb, n_used, gsrc, sdst, h2, wg, wu, wd, wg, wu, wd)


def _moe(h2, ri, cnt, wg_bf, wu_bf, wd_bf, layer):
    n = h2.shape[0]
    max_tiles = n // FFN_TILE + N_PAIRS
    n_slots = max_tiles * FFN_TILE
    counts = cnt[:N_PAIRS, 0].astype(I32)
    padded = ((counts + FFN_TILE - 1) // FFN_TILE) * FFN_TILE
    ends = jnp.cumsum(padded)
    offs = ends - padded
    q, rank = ri[0], ri[1]
    cls = jnp.arange(N_PAIRS, dtype=I32)
    slot = rank + jnp.sum(jnp.where(q[None, :] == cls[:, None], offs[:, None], 0),
                          axis=0)
    seg = jnp.stack([offs + counts, ends], axis=1).reshape(2 * N_PAIRS)
    n_used = ends[-1] // FFN_TILE
    tile = jnp.arange(max_tiles + 1, dtype=I32)
    tq = jnp.sum((tile[:, None] * FFN_TILE >= ends[None, :]).astype(I32), axis=1)
    tq_last = jnp.sum(((n_used - 1) * FFN_TILE >= ends).astype(I32))
    tq = jnp.where(tile < n_used, tq, tq_last)
    pa = jnp.array([a for a, _ in PAIRS], I32)
    pb = jnp.array([b for _, b in PAIRS], I32)
    grp, pidx = tq // len(PAIRS), tq % len(PAIRS)
    tea = EXPERTS_PER_GROUP * grp + jnp.take(pa, pidx)
    teb = EXPERTS_PER_GROUP * grp + jnp.take(pb, pidx)
    gsrc, sdst = _invert(slot, seg, n_slots)
    return _expert_ffn(tea, teb, n_used.reshape(1), gsrc, sdst, h2,
                       wg_bf, wu_bf, wd_bf, layer)


def _moe_out_tm_kernel(x1_ref, y_ref, g2_ref, op_ref, os_ref, scr, *, n_p_tiles):
    i = pl.program_id(0)
    x2 = x1_ref[...] + _per_row(y_ref[...], g2_ref[...], lambda a, b: a * b)
    pl.when(i < n_p_tiles)(
        lambda: _from_time_major(x2, op_ref, scr, op_ref.shape[0]))
    pl.when(i >= n_p_tiles)(
        lambda: _from_time_major(x2, os_ref, scr, os_ref.shape[0]))


def _moe_out_tm(x1, y, modpat, shape_p, shape_s):
    n = x1.shape[0]
    tm = TOKEN_TILE
    n_p_tiles = shape_p[0] * shape_p[1] // tm
    p = modpat.shape[1]
    rowblk = pl.BlockSpec((tm, D_MODEL), lambda i: (i, 0))
    stream = _stream_specs(jax.ShapeDtypeStruct(shape_p, F32),
                           jax.ShapeDtypeStruct(shape_s, F32), n_p_tiles)
    return pl.pallas_call(
        functools.partial(_moe_out_tm_kernel, n_p_tiles=n_p_tiles),
        out_shape=[jax.ShapeDtypeStruct(shape_p, F32),
                   jax.ShapeDtypeStruct(shape_s, F32)],
        grid=(n // tm,),
        in_specs=[rowblk, rowblk,
                  pl.BlockSpec((None, p, D_MODEL),
                               lambda i: (jnp.where(i < n_p_tiles, 0, 1), 0, 5))],
        out_specs=stream,
        scratch_shapes=[pltpu.VMEM((LANE_TILES, tm, LANES), F32)],
        compiler_params=_cparams(("arbitrary",)),
        name="moe_out_tm",
    )(x1, y, modpat)


def _moe_out_final_kernel(x1_ref, y_ref, g2_ref, nfin_ref, op_ref, os_ref,
                          *, n_p_tiles):
    i = pl.program_id(0)
    x2 = x1_ref[...] + _per_row(y_ref[...], g2_ref[...], lambda a, b: a * b)
    out = _rms(x2) * nfin_ref[...]

    @pl.when(i < n_p_tiles)
    def _():
        op_ref[...] = out

    @pl.when(i >= n_p_tiles)
    def _():
        os_ref[...] = out


def _moe_out_final(x1, y, modpat, norm_final, n_p, rows_per_request):
    n = x1.shape[0]
    tm = TOKEN_TILE
    n_p_tiles = n_p // tm
    p = modpat.shape[1]
    rowblk = pl.BlockSpec((tm, D_MODEL), lambda i: (i, 0))
    pat = lambda i: jnp.where(
        i < n_p_tiles, 0, 1 + (i - n_p_tiles) // (rows_per_request // tm))
    return pl.pallas_call(
        functools.partial(_moe_out_final_kernel, n_p_tiles=n_p_tiles),
        out_shape=[jax.ShapeDtypeStruct((n_p, D_MODEL), F32),
                   jax.ShapeDtypeStruct((n - n_p, D_MODEL), F32)],
        grid=(n // tm,),
        in_specs=[rowblk, rowblk,
                  pl.BlockSpec((None, p, D_MODEL), lambda i: (pat(i), 0, 5)),
                  pl.BlockSpec((1, D_MODEL), lambda i: (0, 0))],
        out_specs=[
            pl.BlockSpec((tm, D_MODEL),
                         lambda i: (jnp.minimum(i, n_p_tiles - 1), 0)),
            pl.BlockSpec((tm, D_MODEL),
                         lambda i: (jnp.maximum(i - n_p_tiles, 0), 0))],
        compiler_params=_cparams(("arbitrary",)),
        name="moe_out_final",
    )(x1, y, modpat, norm_final)


def _block_diag_weights(bb_re, bb_im, c_re, c_im):
    lt = LANE_TILES
    gl = SSM_GROUPS // lt
    eye = jnp.eye(gl, dtype=F32)

    def in_map(bb):
        b5 = bb.reshape(2, lt, gl, STATE_DIM, SSM_GROUP)
        t = jnp.einsum("dinph,kn->dikhnp", b5, eye)
        return t.reshape(2, lt, gl * SSM_GROUP, gl * STATE_DIM)

    def out_map(cc):
        c5 = cc.reshape(2, lt, gl, SSM_GROUP, STATE_DIM)
        t = jnp.einsum("dikhp,kn->dikpnh", c5, eye)
        return t.reshape(2, lt, gl * STATE_DIM, gl * SSM_GROUP)

    bt = jnp.concatenate([in_map(bb_re), in_map(bb_im)], axis=-1).astype(BF16)
    return bt, out_map(c_re).astype(BF16), out_map(c_im).astype(BF16)


def kernel(x_prompt, x_sample, c, state_ssm_re, state_ssm_im, c_ctx, norm_mix, norm_ffn, w_ada, b_ada, ssm_w_in, ssm_lam_re, ssm_lam_im, ssm_log_dt, ssm_b_re, ssm_b_im, ssm_c_re, ssm_c_im, ssm_d, ssm_w_glu, fnet_w_out, router_w, router_b, moe_w_gate, moe_w_up, moe_w_down, norm_final):
    bp, lp, _ = x_prompt.shape
    bs, ls, _ = x_sample.shape
    n_p = bp * lp
    n_s = bs * ls
    n = n_p + n_s
    tm = TOKEN_TILE

    cond = jnp.zeros((N_COND, D_MODEL), F32).at[0].set(c_ctx).at[1:1 + bs].set(c)
    modtab = _ada_table(cond, w_ada, b_ada).reshape(DEPTH, N_COND, N_MOD, D_MODEL)

    rwt = router_w.T
    rb = router_b.reshape(N_EXPERTS, 1)
    row = lambda v: v.reshape(1, D_MODEL)

    period = max(bp, bs)
    pat_tm = jnp.stack([
        jnp.broadcast_to(modtab[0, 0], (period, N_MOD, D_MODEL)),
        jnp.tile(modtab[0, 1:1 + bs], (period // bs, 1, 1))])
    pat_tm = pat_tm.reshape(2, period, N_MOD * D_MODEL)

    a_re, a_im, bb_re, bb_im = _zoh(ssm_lam_re[0], ssm_lam_im[0], ssm_log_dt[0],
                                    ssm_b_re[0], ssm_b_im[0])
    bt, ct_re, ct_im = _block_diag_weights(bb_re, bb_im, ssm_c_re[0], ssm_c_im[0])
    a_re = a_re.reshape(2, 1, N_STATE)
    a_im = a_im.reshape(2, 1, N_STATE)

    u = _ssm_in(x_prompt, x_sample, row(norm_mix[0]), pat_tm,
                ssm_w_in[0].astype(BF16))
    h0s_re = state_ssm_re[:, 0].reshape(bs, 2, N_STATE).transpose(1, 0, 2)
    h0s_im = state_ssm_im[:, 0].reshape(bs, 2, N_STATE).transpose(1, 0, 2)
    y_scan, fin_re, fin_im = _ssm_scan(u, bt, ct_re, ct_im, a_re, a_im,
                                       h0s_re, h0s_im, bp=bp, lp=lp, bs=bs, ls=ls)
    tri = jnp.triu(jnp.ones((tm, tm), BF16), k=1)
    x1, h2, ri, cnt = _ssm_out(u, y_scan, row(ssm_d[0]),
                               ssm_w_glu[0].astype(BF16), x_prompt, x_sample,
                               pat_tm, row(norm_ffn[0]), rwt, rb, tri)
    wg_bf, wu_bf, wd_bf = (w.astype(BF16) for w in (moe_w_gate, moe_w_up,
                                                    moe_w_down))
    y_moe = _moe(h2, ri, cnt, wg_bf, wu_bf, wd_bf, 0)
    x2_p, x2_s = _moe_out_tm(x1, y_moe, pat_tm, x_prompt.shape, x_sample.shape)

    pat_bm = jnp.broadcast_to(modtab[1][:1 + bs, None],
                              (1 + bs, SUBLANES, N_MOD, D_MODEL))
    pat_bm = pat_bm.reshape(1 + bs, SUBLANES, N_MOD * D_MODEL)
    x3, h2, ri, cnt = _fnet(x2_p, x2_s, row(norm_mix[1]), pat_bm,
                            fnet_w_out[0].astype(BF16), row(norm_ffn[1]), rwt, rb)
    y_moe = _moe(h2, ri, cnt, wg_bf, wu_bf, wd_bf, 1)
    y_p, y_s = _moe_out_final(x3, y_moe, pat_bm, row(norm_final), n_p, ls)

    st = lambda f: f.transpose(1, 0, 2).reshape(bp, 1, 2, SSM_GROUPS, STATE_DIM)
    return (y_p.reshape(bp, lp, D_MODEL), y_s.reshape(bs, ls, D_MODEL),
            st(fin_re), st(fin_im))
```

```python
import functools

import jax
import jax.numpy as jnp
from jax import lax
from jax.experimental import pallas as pl
from jax.experimental.pallas import tpu as pltpu

F32 = jnp.float32
BF16 = jnp.bfloat16
I32 = jnp.int32
HIGHEST = lax.Precision.HIGHEST

D_MODEL = 1024
DEPTH = 2
SSM_GROUP = 16
SSM_GROUPS = 64
STATE_DIM = 64
N_STATE = SSM_GROUPS * STATE_DIM
FNET_GROUP = 128
N_EXPERTS = 16
N_EXPERT_GROUPS = 4
EXPERTS_PER_GROUP = 4
D_EXPERT = 1024
N_MOD = 6
EPS = 1e-6

LANES = 128
SUBLANES = 8
LANE_TILES = D_MODEL // LANES
VMEM_LIMIT = 56 * 1024 * 1024

TOKEN_TILE = 512
FNET_TILE = 256
FFN_TILE = 256
N_COND = 16

PAIRS = ((0, 1), (0, 2), (0, 3), (1, 3), (1, 2), (3, 2))
N_PAIRS = N_EXPERT_GROUPS * len(PAIRS)
N_CLASS_ROWS = 32
H_WIDTH = D_MODEL + LANES


def _cparams(sem, vmem=VMEM_LIMIT):
    return pltpu.CompilerParams(dimension_semantics=sem, vmem_limit_bytes=vmem)


def _sigmoid(x):
    return 1.0 / (1.0 + jnp.exp(-x))


def _gelu_tanh(x):
    c = 0.7978845608028654
    return x * (0.5 * (1.0 + jnp.tanh(c * (x + 0.044715 * (x * x * x)))))


def _per_row(v, pat, fn):
    tm, d = v.shape
    p = pat.shape[0]
    return fn(v.reshape(tm // p, p, d), pat[None]).reshape(tm, d)


def _rms(x):
    ms = jnp.mean(x * x, axis=-1, keepdims=True)
    return x * lax.rsqrt(ms + EPS)


def _norm_mod(x, g, sc, sh):
    y = _rms(x) * g
    y = _per_row(y, sc, lambda a, b: a * (1.0 + b))
    return _per_row(y, sh, lambda a, b: a + b)


def _lane_tiles(v):
    return [v[:, LANES * k:LANES * (k + 1)] for k in range(v.shape[1] // LANES)]


def _to_time_major(x_ref, scr, batch):
    tt = x_ref.shape[1]
    for b in range(batch):
        for k, piece in enumerate(_lane_tiles(x_ref[b])):
            scr[k, pl.ds(b, tt, stride=batch), :] = piece
    return jnp.concatenate([scr[k] for k in range(LANE_TILES)], axis=1)


def _from_time_major(v, o_ref, scr, batch):
    tt = v.shape[0] // batch
    for k, piece in enumerate(_lane_tiles(v)):
        scr[k] = piece
    for b in range(batch):
        o_ref[b] = jnp.concatenate(
            [scr[k, pl.ds(b, tt, stride=batch), :] for k in range(LANE_TILES)],
            axis=1)


def _route(logits_t, rb, tri, carry):
    ne, tm = logits_t.shape
    s = _sigmoid(logits_t)
    bz = s + rb
    row = lambda a, r: a[r:r + 1, :]
    gs = []
    for g in range(N_EXPERT_GROUPS):
        v0, v1, v2, v3 = (row(bz, EXPERTS_PER_GROUP * g + j) for j in range(4))
        hi1, lo1 = jnp.maximum(v0, v1), jnp.minimum(v0, v1)
        hi2, lo2 = jnp.maximum(v2, v3), jnp.minimum(v2, v3)
        top1 = jnp.maximum(hi1, hi2)
        top2 = jnp.maximum(jnp.minimum(hi1, hi2), jnp.maximum(lo1, lo2))
        gs.append(top1 + top2)
    bg = jnp.zeros((1, tm), I32)
    bv = gs[0]
    for g in range(1, N_EXPERT_GROUPS):
        upd = gs[g] > bv
        bg = jnp.where(upd, g, bg)
        bv = jnp.where(upd, gs[g], bv)
    cb, cs = [], []
    for j in range(EXPERTS_PER_GROUP):
        vb, vs = row(bz, j), row(s, j)
        for g in range(1, N_EXPERT_GROUPS):
            sel = bg == g
            vb = jnp.where(sel, row(bz, EXPERTS_PER_GROUP * g + j), vb)
            vs = jnp.where(sel, row(s, EXPERTS_PER_GROUP * g + j), vs)
        cb.append(vb)
        cs.append(vs)
    i1 = jnp.zeros((1, tm), I32)
    b1, s1 = cb[0], cs[0]
    for j in range(1, EXPERTS_PER_GROUP):
        upd = cb[j] > b1
        i1 = jnp.where(upd, j, i1)
        b1 = jnp.where(upd, cb[j], b1)
        s1 = jnp.where(upd, cs[j], s1)
    i2 = jnp.zeros((1, tm), I32)
    b2 = jnp.full((1, tm), -jnp.inf, F32)
    s2 = jnp.zeros((1, tm), F32)
    for j in range(EXPERTS_PER_GROUP):
        cand = jnp.where(i1 == j, -jnp.inf, cb[j])
        upd = cand > b2
        i2 = jnp.where(upd, j, i2)
        b2 = jnp.where(upd, cand, b2)
        s2 = jnp.where(upd, cs[j], s2)
    den = s1 + s2
    w1 = s1 / den
    w2 = s2 / den
    lo = jnp.minimum(i1, i2)
    hi = jnp.maximum(i1, i2)
    pidx = jnp.where(lo == 0, hi - 1,
                     jnp.where(lo == 1, jnp.where(hi == 3, 3, 4), 5))
    first = jnp.where(pidx < 3, 0, jnp.where(pidx < 5, 1, 3))
    wa = jnp.where(i1 == first, w1, w2)
    wb = jnp.where(i1 == first, w2, w1)
    q = len(PAIRS) * bg + pidx
    qio = lax.broadcasted_iota(I32, (N_CLASS_ROWS, tm), 0)
    oh = qio == q
    ohf = jnp.where(oh, 1.0, 0.0)
    cum = jnp.dot(ohf.astype(BF16), tri, preferred_element_type=F32) + carry
    rank = jnp.sum(jnp.where(oh, cum, 0.0), axis=0, keepdims=True)
    new_carry = carry + jnp.sum(ohf, axis=1, keepdims=True)
    return q, rank.astype(I32), wa, wb, new_carry


def _post_mixer(m, x, g1, nf, sc2, sh2, rwt_ref, rb_ref, tri_ref,
                x1_ref, h2_ref, ri_ref, cnt_ref, carry, first):
    @pl.when(first)
    def _():
        carry[...] = jnp.zeros(carry.shape, F32)

    tm = x.shape[0]
    x1 = x + _per_row(m, g1, lambda a, b: a * b)
    x1_ref[...] = x1
    h2 = _norm_mod(x1, nf, sc2, sh2)
    h2_ref[:, :D_MODEL] = h2
    logits_t = lax.dot_general(rwt_ref[...], h2, (((1,), (1,)), ((), ())),
                               precision=HIGHEST, preferred_element_type=F32)
    q, rank, wa, wb, nc = _route(logits_t, rb_ref[...], tri_ref[...],
                                 carry[:, 0:1])
    rio = lax.broadcasted_iota(I32, (LANES, tm), 0)
    wrows = jnp.where(rio == 0, wa, jnp.where(rio == 1, wb, 0.0))
    h2_ref[:, D_MODEL:] = wrows.T
    ri_ref[0:1, :] = q
    ri_ref[1:2, :] = rank
    ri_ref[2:8, :] = jnp.zeros((6, tm), I32)
    carry[...] = jnp.broadcast_to(nc, carry.shape)
    cnt_ref[...] = carry[...]


def _ada_kernel(cond_ref, w_ref, b_ref, o_ref):
    c = cond_ref[...]
    s = c * _sigmoid(c)
    o_ref[...] = jnp.dot(s, w_ref[...], precision=HIGHEST,
                         preferred_element_type=F32) + b_ref[...]


def _ada_table(cond, w_ada, b_ada):
    tn = 1536
    n_out = N_MOD * D_MODEL
    return pl.pallas_call(
        _ada_kernel,
        out_shape=jax.ShapeDtypeStruct((DEPTH, N_COND, n_out), F32),
        grid=(DEPTH, n_out // tn),
        in_specs=[
            pl.BlockSpec((N_COND, D_MODEL), lambda l, j: (0, 0)),
            pl.BlockSpec((None, D_MODEL, tn), lambda l, j: (l, 0, j)),
            pl.BlockSpec((None, 1, tn), lambda l, j: (l, 0, j)),
        ],
        out_specs=pl.BlockSpec((None, N_COND, tn), lambda l, j: (l, 0, j)),
        compiler_params=_cparams(("arbitrary", "arbitrary")),
        name="ada_table",
    )(cond, w_ada, b_ada.reshape(DEPTH, 1, n_out))


def _zoh_kernel(lr_ref, li_ref, ldt_ref, br_ref, bi_ref,
                are_ref, aim_ref, bbre_ref, bbim_ref):
    lr = lr_ref[...]
    li = li_ref[...]
    dt = jnp.exp(ldt_ref[...])
    mag = jnp.exp(lr * dt)
    a_re = mag * jnp.cos(li * dt)
    a_im = mag * jnp.sin(li * dt)
    den = lr * lr + li * li
    nr = a_re - 1.0
    f_re = (nr * lr + a_im * li) / den
    f_im = (a_im * lr - nr * li) / den
    br = br_ref[...]
    bi = bi_ref[...]
    are_ref[...] = a_re
    aim_ref[...] = a_im
    bbre_ref[...] = f_re * br - f_im * bi
    bbim_ref[...] = f_re * bi + f_im * br


def _zoh(lam_re, lam_im, log_dt, b_re, b_im):
    shape = b_re.shape
    flat = (shape[0] * shape[1] * shape[2] * shape[3] // LANES, LANES)
    bc = lambda a: jnp.broadcast_to(a, shape).reshape(flat)
    args = (bc(lam_re[..., None]), bc(lam_im[..., None]),
            bc(log_dt[:, :, None, None]), b_re.reshape(flat), b_im.reshape(flat))
    outs = pl.pallas_call(
        _zoh_kernel,
        out_shape=[jax.ShapeDtypeStruct(flat, F32)] * 4,
        name="zoh_discretize",
    )(*args)
    a_re, a_im, bb_re, bb_im = (o.reshape(shape) for o in outs)
    return a_re[..., 0], a_im[..., 0], bb_re, bb_im


def _stream_specs(xp, xs, n_p_tiles):
    bp, bs = xp.shape[0], xs.shape[0]
    return [
        pl.BlockSpec((bp, TOKEN_TILE // bp, D_MODEL),
                     lambda i: (0, jnp.minimum(i, n_p_tiles - 1), 0)),
        pl.BlockSpec((bs, TOKEN_TILE // bs, D_MODEL),
                     lambda i: (0, jnp.maximum(i - n_p_tiles, 0), 0)),
    ]


def _ssm_in_kernel(xp_ref, xs_ref, g_ref, sc_ref, sh_ref, w_ref, u_ref, scr,
                   *, n_p_tiles):
    def run(x_ref):
        x = _to_time_major(x_ref, scr, x_ref.shape[0])
        h = _norm_mod(x, g_ref[...], sc_ref[...], sh_ref[...])
        u_ref[...] = jnp.dot(h.astype(BF16), w_ref[...],
                             preferred_element_type=F32)

    i = pl.program_id(0)
    pl.when(i < n_p_tiles)(lambda: run(xp_ref))
    pl.when(i >= n_p_tiles)(lambda: run(xs_ref))


def _ssm_in(xp, xs, g, modpat, w_in_bf):
    n = (xp.shape[0] * xp.shape[1] + xs.shape[0] * xs.shape[1])
    tm = TOKEN_TILE
    n_p_tiles = xp.shape[0] * xp.shape[1] // tm
    p = modpat.shape[1]
    pat = lambda i: jnp.where(i < n_p_tiles, 0, 1)
    mod = lambda k: pl.BlockSpec((None, p, D_MODEL), lambda i: (pat(i), 0, k))
    return pl.pallas_call(
        functools.partial(_ssm_in_kernel, n_p_tiles=n_p_tiles),
        out_shape=jax.ShapeDtypeStruct((n, D_MODEL), F32),
        grid=(n // tm,),
        in_specs=_stream_specs(xp, xs, n_p_tiles) + [
            pl.BlockSpec((1, D_MODEL), lambda i: (0, 0)),
            mod(1), mod(0),
            pl.BlockSpec((D_MODEL, D_MODEL), lambda i: (0, 0)),
        ],
        out_specs=pl.BlockSpec((tm, D_MODEL), lambda i: (i, 0)),
        scratch_shapes=[pltpu.VMEM((LANE_TILES, tm, LANES), F32)],
        compiler_params=_cparams(("arbitrary",)),
        name="ssm_in",
    )(xp, xs, g, modpat, modpat, w_in_bf)


def _scan_kernel(u_ref, bt_ref, cre_ref, cim_ref, are_ref, aim_ref,
                 h0re_ref, h0im_ref, y_ref, fre_ref, fim_ref,
                 *scratch, n_p_chunks, bp, bs):
    xre = scratch[:LANE_TILES]
    xim = scratch[LANE_TILES:2 * LANE_TILES]
    hre, him = scratch[2 * LANE_TILES:]
    d = pl.program_id(0)
    c = pl.program_id(1)
    cols = N_STATE // LANE_TILES
    rows = u_ref.shape[0]

    def chunk(batch, reverse):
        steps = rows // batch

        def b_proj(s):
            u = u_ref[:, LANES * s:LANES * (s + 1)].astype(BF16)
            xt = jnp.dot(u, bt_ref[s], preferred_element_type=F32)
            xre[s][...] = xt[:, :cols]
            xim[s][...] = xt[:, cols:]

        def scan(s):
            sl = slice(cols * s, cols * (s + 1))
            ar = jnp.broadcast_to(are_ref[:, sl], (batch, cols))
            ai = jnp.broadcast_to(aim_ref[:, sl], (batch, cols))
            hr = hre[0:batch, sl]
            hi = him[0:batch, sl]
            for t in range(steps):
                tt = steps - 1 - t if reverse else t
                r = slice(tt * batch, (tt + 1) * batch)
                nr = ar * hr - ai * hi + xre[s][r, :]
                ni = ar * hi + ai * hr + xim[s][r, :]
                xre[s][r, :] = nr
                xim[s][r, :] = ni
                hr, hi = nr, ni
            hre[0:batch, sl] = hr
            him[0:batch, sl] = hi

        def c_proj(s):
            y_ref[:, LANES * s:LANES * (s + 1)] = (
                jnp.dot(xre[s][...].astype(BF16), cre_ref[s],
                        preferred_element_type=F32)
                - jnp.dot(xim[s][...].astype(BF16), cim_ref[s],
                          preferred_element_type=F32))

        for stage in range(LANE_TILES + 2):
            if stage < LANE_TILES:
                b_proj(stage)
            if 0 <= stage - 1 < LANE_TILES:
                scan(stage - 1)
            if 0 <= stage - 2 < LANE_TILES:
                c_proj(stage - 2)

    @pl.when(c == 0)
    def _():
        hre[...] = jnp.zeros(hre.shape, F32)
        him[...] = jnp.zeros(him.shape, F32)

    @pl.when(c == n_p_chunks)
    def _():
        hre[0:bs, :] = h0re_ref[...]
        him[0:bs, :] = h0im_ref[...]

    in_p = c < n_p_chunks
    in_s = jnp.logical_not(in_p)
    fwd = d == 0
    bwd = jnp.logical_not(fwd)
    pl.when(jnp.logical_and(in_p, fwd))(lambda: chunk(bp, False))
    pl.when(jnp.logical_and(in_p, bwd))(lambda: chunk(bp, True))
    pl.when(jnp.logical_and(in_s, fwd))(lambda: chunk(bs, False))
    pl.when(jnp.logical_and(in_s, bwd))(lambda: chunk(bs, True))

    @pl.when(c == n_p_chunks - 1)
    def _():
        fre_ref[...] = hre[0:bp, :]
        fim_ref[...] = him[0:bp, :]


def _ssm_scan(u, bt, ct_re, ct_im, a_re, a_im, h0_re, h0_im, *, bp, lp, bs, ls):
    n = u.shape[0]
    rows = TOKEN_TILE
    n_p_chunks = bp * lp // rows
    n_s_chunks = bs * ls // rows

    def chunk(d, c):
        in_p = jnp.where(d == 0, c, n_p_chunks - 1 - c)
        cs = c - n_p_chunks
        in_s = n_p_chunks + jnp.where(d == 0, cs, n_s_chunks - 1 - cs)
        return jnp.where(c < n_p_chunks, in_p, in_s)

    cols = N_STATE // LANE_TILES
    dspec = lambda shape: pl.BlockSpec((None,) + shape,
                                       lambda d, c: (d,) + (0,) * len(shape))
    kern = functools.partial(_scan_kernel, n_p_chunks=n_p_chunks, bp=bp, bs=bs)
    return pl.pallas_call(
        kern,
        out_shape=[jax.ShapeDtypeStruct((2, n, D_MODEL), F32),
                   jax.ShapeDtypeStruct((2, bp, N_STATE), F32),
                   jax.ShapeDtypeStruct((2, bp, N_STATE), F32)],
        grid=(2, n_p_chunks + n_s_chunks),
        in_specs=[
            pl.BlockSpec((rows, D_MODEL), lambda d, c: (chunk(d, c), 0)),
            dspec((LANE_TILES, LANES, 2 * cols)),
            dspec((LANE_TILES, cols, LANES)),
            dspec((LANE_TILES, cols, LANES)),
            dspec((1, N_STATE)),
            dspec((1, N_STATE)),
            dspec((bs, N_STATE)),
            dspec((bs, N_STATE)),
        ],
        out_specs=[
            pl.BlockSpec((None, rows, D_MODEL), lambda d, c: (d, chunk(d, c), 0)),
            dspec((bp, N_STATE)),
            dspec((bp, N_STATE)),
        ],
        scratch_shapes=(
            [pltpu.VMEM((rows, cols), F32)] * (2 * LANE_TILES)
            + [pltpu.VMEM((max(bp, bs), N_STATE), F32)] * 2),
        compiler_params=_cparams(("arbitrary", "arbitrary")),
        name="ssm_scan",
    )(u, bt, ct_re, ct_im, a_re, a_im, h0_re, h0_im)


def _ssm_out_kernel(u_ref, y0_ref, y1_ref, dsk_ref, wglu_ref, xp_ref, xs_ref,
                    g1_ref, nf_ref, sc2_ref, sh2_ref, rwt_ref, rb_ref, tri_ref,
                    x1_ref, h2_ref, ri_ref, cnt_ref, scr, carry, *, n_p_tiles):
    i = pl.program_id(0)

    def run(x_ref):
        x = _to_time_major(x_ref, scr, x_ref.shape[0])
        y = u_ref[...] * dsk_ref[...] + y0_ref[...] + y1_ref[...]
        ge = _gelu_tanh(y).astype(BF16)
        vg = jnp.dot(ge, wglu_ref[...], preferred_element_type=F32)
        m = vg[:, :D_MODEL] * _sigmoid(vg[:, D_MODEL:])
        _post_mixer(m, x, g1_ref[...], nf_ref[...], sc2_ref[...], sh2_ref[...],
                    rwt_ref, rb_ref, tri_ref, x1_ref, h2_ref, ri_ref, cnt_ref,
                    carry, i == 0)

    pl.when(i < n_p_tiles)(lambda: run(xp_ref))
    pl.when(i >= n_p_tiles)(lambda: run(xs_ref))


def _route_out_shapes(n):
    return [jax.ShapeDtypeStruct((n, D_MODEL), F32),
            jax.ShapeDtypeStruct((n, H_WIDTH), F32),
            jax.ShapeDtypeStruct((SUBLANES, n), I32),
            jax.ShapeDtypeStruct((N_CLASS_ROWS, LANES), F32)]


def _ssm_out(u, y, d_skip, w_glu_bf, xp, xs, modpat, norm_ffn, rwt, rb, tri):
    n = u.shape[0]
    tm = TOKEN_TILE
    n_p_tiles = xp.shape[0] * xp.shape[1] // tm
    p = modpat.shape[1]
    pat = lambda i: jnp.where(i < n_p_tiles, 0, 1)
    mod = lambda k: pl.BlockSpec((None, p, D_MODEL), lambda i: (pat(i), 0, k))
    full = lambda shape: pl.BlockSpec(shape, lambda i: (0,) * len(shape))
    rowblk = pl.BlockSpec((tm, D_MODEL), lambda i: (i, 0))
    return pl.pallas_call(
        functools.partial(_ssm_out_kernel, n_p_tiles=n_p_tiles),
        out_shape=_route_out_shapes(n),
        grid=(n // tm,),
        in_specs=[
            rowblk,
            pl.BlockSpec((None, tm, D_MODEL), lambda i: (0, i, 0)),
            pl.BlockSpec((None, tm, D_MODEL), lambda i: (1, i, 0)),
            full((1, D_MODEL)),
            full((D_MODEL, 2 * D_MODEL)),
        ] + _stream_specs(xp, xs, n_p_tiles) + [
            mod(2), full((1, D_MODEL)), mod(4), mod(3),
            full((N_EXPERTS, D_MODEL)), full((N_EXPERTS, 1)), full((tm, tm)),
        ],
        out_specs=[rowblk,
                   pl.BlockSpec((tm, H_WIDTH), lambda i: (i, 0)),
                   pl.BlockSpec((SUBLANES, tm), lambda i: (0, i)),
                   full((N_CLASS_ROWS, LANES))],
        scratch_shapes=[pltpu.VMEM((LANE_TILES, tm, LANES), F32),
                        pltpu.VMEM((N_CLASS_ROWS, LANES), F32)],
        compiler_params=_cparams(("arbitrary",)),
        name="ssm_out",
    )(u, y, y, d_skip, w_glu_bf, xp, xs, modpat, norm_ffn, modpat, modpat,
      rwt, rb, tri)


def _fnet_kernel(xp_ref, xs_ref, nm_ref, sc1_ref, sh1_ref, cs_ref, dftp_ref,
                 dfts_ref, wout_ref, g1_ref, nf_ref, sc2_ref, sh2_ref, rwt_ref,
                 rb_ref, tri_ref, x1_ref, h2_ref, ri_ref, cnt_ref, ucs, carry,
                 *, bp, tiles_s):
    s = pl.program_id(0)
    n_groups = D_MODEL // FNET_GROUP
    tr = FNET_TILE

    def run(x_ref, dft_ref, i):
        seq = x_ref.shape[0]
        scale = float((seq * FNET_GROUP) ** -0.5)

        @pl.when(i == 0)
        def _():
            def stage1(r, _):
                rows = pl.ds(pl.multiple_of(r * tr, tr), tr)
                h = _norm_mod(x_ref[rows, :], nm_ref[...], sc1_ref[...],
                              sh1_ref[...]).astype(BF16)
                for k in range(n_groups):
                    cols = slice(FNET_GROUP * k, FNET_GROUP * (k + 1))
                    t = jnp.dot(h[:, cols], cs_ref[...],
                                preferred_element_type=F32)
                    ucs[rows, cols] = t[:, :FNET_GROUP].astype(BF16)
                    ucs[pl.ds(pl.multiple_of(seq + r * tr, tr), tr), cols] = (
                        t[:, FNET_GROUP:].astype(BF16))
                return 0
            lax.fori_loop(0, seq // tr, stage1, 0)

        y = jnp.dot(dft_ref[...], ucs[0:2 * seq, :],
                    preferred_element_type=F32) * scale
        m = jnp.dot(y.astype(BF16), wout_ref[...], preferred_element_type=F32)
        xrow = x_ref[pl.ds(pl.multiple_of(i * tr, tr), tr), :]
        _post_mixer(m, xrow, g1_ref[...], nf_ref[...], sc2_ref[...],
                    sh2_ref[...], rwt_ref, rb_ref, tri_ref, x1_ref, h2_ref,
                    ri_ref, cnt_ref, carry, s == 0)

    pl.when(s < bp)(lambda: run(xp_ref, dftp_ref, 0 * s))
    pl.when(s >= bp)(lambda: run(xs_ref, dfts_ref, (s - bp) % tiles_s))


def _dft_table(seq):
    k = jnp.arange(seq, dtype=I32)
    ang = ((k[:, None] * k[None, :]) % seq).astype(F32) * (2.0 * jnp.pi / seq)
    return jnp.concatenate([jnp.cos(ang), -jnp.sin(ang)], axis=1).astype(BF16)


def _fnet(xp, xs, norm_mix, modpat, w_out_bf, norm_ffn, rwt, rb):
    bp, lp, _ = xp.shape
    bs, ls, _ = xs.shape
    tr = FNET_TILE
    assert lp == tr
    tiles_s = ls // tr
    n = bp * lp + bs * ls
    p = modpat.shape[1]
    kc = jnp.arange(FNET_GROUP, dtype=I32)
    angc = ((kc[:, None] * kc[None, :]) % FNET_GROUP).astype(F32) * (
        2.0 * jnp.pi / FNET_GROUP)
    cs = jnp.concatenate([jnp.cos(angc), jnp.sin(angc)], axis=1).astype(BF16)
    tri = jnp.triu(jnp.ones((tr, tr), BF16), k=1)

    sb = lambda s: jnp.clip((s - bp) // tiles_s, 0, bs - 1)
    pat = lambda s: jnp.where(s < bp, 0, 1 + sb(s))
    mod = lambda kk: pl.BlockSpec((None, p, D_MODEL), lambda s: (pat(s), 0, kk))
    full = lambda shape: pl.BlockSpec(shape, lambda s: (0,) * len(shape))
    kern = functools.partial(_fnet_kernel, bp=bp, tiles_s=tiles_s)
    return pl.pallas_call(
        kern,
        out_shape=_route_out_shapes(n),
        grid=(bp + bs * tiles_s,),
        in_specs=[
            pl.BlockSpec((None, lp, D_MODEL),
                         lambda s: (jnp.minimum(s, bp - 1), 0, 0)),
            pl.BlockSpec((None, ls, D_MODEL), lambda s: (sb(s), 0, 0)),
            full((1, D_MODEL)), mod(1), mod(0),
            full((FNET_GROUP, 2 * FNET_GROUP)),
            full((tr, 2 * lp)),
            pl.BlockSpec((tr, 2 * ls),
                         lambda s: (jnp.maximum(s - bp, 0) % tiles_s, 0)),
            full((D_MODEL, D_MODEL)),
            mod(2), full((1, D_MODEL)), mod(4), mod(3),
            full((N_EXPERTS, D_MODEL)), full((N_EXPERTS, 1)), full((tr, tr)),
        ],
        out_specs=[pl.BlockSpec((tr, D_MODEL), lambda s: (s, 0)),
                   pl.BlockSpec((tr, H_WIDTH), lambda s: (s, 0)),
                   pl.BlockSpec((SUBLANES, tr), lambda s: (0, s)),
                   full((N_CLASS_ROWS, LANES))],
        scratch_shapes=[pltpu.VMEM((2 * ls, D_MODEL), BF16),
                        pltpu.VMEM((N_CLASS_ROWS, LANES), F32)],
        compiler_params=_cparams(("arbitrary",)),
        name="fnet",
    )(xp, xs, norm_mix, modpat, modpat, cs, _dft_table(lp), _dft_table(ls),
      w_out_bf, modpat, norm_ffn, modpat, modpat, rwt, rb, tri)


def _invert_kernel(slot_ref, seg_ref, gsrc_ref, sdst_ref, *, n_pad_rows):
    n = slot_ref.shape[0]
    n_slots = gsrc_ref.shape[0]
    unroll = 8

    def pad(j, _):
        gsrc_ref[j] = 0
        sdst_ref[j] = n + j % n_pad_rows
        return 0
    for q in range(N_PAIRS):
        lax.fori_loop(seg_ref[2 * q], seg_ref[2 * q + 1], pad, 0)
    lax.fori_loop(seg_ref[2 * N_PAIRS - 1], n_slots, pad, 0)

    def body(j, _):
        for k in range(unroll):
            t = j * unroll + k
            s = slot_ref[t]
            gsrc_ref[s] = t
            sdst_ref[s] = t
        return 0
    lax.fori_loop(0, n // unroll, body, 0)


def _invert(slot, seg, n_slots):
    smem = pl.BlockSpec(memory_space=pltpu.SMEM)
    return pl.pallas_call(
        functools.partial(_invert_kernel, n_pad_rows=FFN_TILE),
        out_shape=[jax.ShapeDtypeStruct((n_slots,), I32)] * 2,
        in_specs=[smem, smem],
        out_specs=[smem, smem],
        name="moe_invert",
    )(slot, seg)


def _ffn_kernel(tea_ref, teb_ref, nu_ref, gsrc_ref, sdst_ref, h_hbm,
                wga, wua, wda, wgb, wub, wdb, y_hbm,
                xbuf0, xbuf1, obuf0, obuf1, gsem, ssem):
    del tea_ref, teb_ref
    i = pl.program_id(0)
    nu = nu_ref[0]
    t = FFN_TILE

    def row(ref, r):
        return ref.at[pl.ds(r, 1), :]

    def gather_start(tile, xbuf, sem):
        base = tile * t
        for r in range(t):
            pltpu.make_async_copy(row(h_hbm, gsrc_ref[base + r]), row(xbuf, r),
                                  sem).start()

    def gather_wait(xbuf, sem):
        pltpu.make_async_copy(h_hbm.at[pl.ds(0, t), :], xbuf, sem).wait()

    def scatter_start(tile, obuf, sem):
        base = tile * t
        for r in range(t):
            pltpu.make_async_copy(row(obuf, r), row(y_hbm, sdst_ref[base + r]),
                                  sem).start()

    def scatter_wait(obuf, sem):
        pltpu.make_async_copy(obuf, y_hbm.at[pl.ds(0, t), :], sem).wait()

    def expert(xb, wg, wu, wd):
        g = jnp.dot(xb, wg[...], preferred_element_type=F32)
        u = jnp.dot(xb, wu[...], preferred_element_type=F32)
        a = (g * _sigmoid(g)) * u
        return jnp.dot(a.astype(BF16), wd[...], preferred_element_type=F32)

    def step(xc, xn, oc, op, gc, gn, sc, sp):
        pl.when(i == 0)(lambda: gather_start(0, xc, gc))
        pl.when(i >= 2)(lambda: scatter_wait(oc, sc))
        gather_wait(xc, gc)
        pl.when(i + 1 < nu)(lambda: gather_start(i + 1, xn, gn))
        pl.when(i >= 1)(lambda: scatter_start(i - 1, op, sp))
        x = xc[...]
        xb = x[:, :D_MODEL].astype(BF16)
        ya = expert(xb, wga, wua, wda)
        yb = expert(xb, wgb, wub, wdb)
        oc[...] = (x[:, D_MODEL:D_MODEL + 1] * ya
                   + x[:, D_MODEL + 1:D_MODEL + 2] * yb)

    def drain(oc, op, sc, sp):
        pl.when(i >= 2)(lambda: scatter_wait(oc, sc))
        scatter_start(i - 1, op, sp)
        scatter_wait(op, sp)
        n_tok = y_hbm.shape[0] - t
        oc[...] = jnp.zeros(oc.shape, F32)
        spare = pltpu.make_async_copy(oc, y_hbm.at[pl.ds(n_tok, t), :], sc)
        spare.start()
        spare.wait()

    even = (i % 2) == 0
    odd = jnp.logical_not(even)
    g0, g1, s0, s1 = gsem.at[0], gsem.at[1], ssem.at[0], ssem.at[1]
    pl.when(jnp.logical_and(i < nu, even))(
        lambda: step(xbuf0, xbuf1, obuf0, obuf1, g0, g1, s0, s1))
    pl.when(jnp.logical_and(i < nu, odd))(
        lambda: step(xbuf1, xbuf0, obuf1, obuf0, g1, g0, s1, s0))
    pl.when(jnp.logical_and(i == nu, even))(
        lambda: drain(obuf0, obuf1, s0, s1))
    pl.when(jnp.logical_and(i == nu, odd))(
        lambda: drain(obuf1, obuf0, s1, s0))


def _expert_ffn(tea, teb, n_used, gsrc, sdst, h2, wg, wu, wd, layer):
    n_tok = h2.shape[0]
    max_tiles = gsrc.shape[0] // FFN_TILE
    wa = lambda a, b: pl.BlockSpec(
        (None, None, a, b), lambda i, ta, tb, nu, gs, sd: (layer, ta[i], 0, 0))
    wb = lambda a, b: pl.BlockSpec(
        (None, None, a, b), lambda i, ta, tb, nu, gs, sd: (layer, tb[i], 0, 0))
    anyspec = pl.BlockSpec(memory_space=pl.ANY)
    return pl.pallas_call(
        _ffn_kernel,
        out_shape=jax.ShapeDtypeStruct((n_tok + FFN_TILE, D_MODEL), F32),
        grid_spec=pltpu.PrefetchScalarGridSpec(
            num_scalar_prefetch=5,
            grid=(max_tiles + 1,),
            in_specs=[anyspec,
                      wa(D_MODEL, D_EXPERT), wa(D_MODEL, D_EXPERT),
                      wa(D_EXPERT, D_MODEL),
                      wb(D_MODEL, D_EXPERT), wb(D_MODEL, D_EXPERT),
                      wb(D_EXPERT, D_MODEL)],
            out_specs=anyspec,
            scratch_shapes=[pltpu.VMEM((FFN_TILE, H_WIDTH), F32),
                            pltpu.VMEM((FFN_TILE, H_WIDTH), F32),
                            pltpu.VMEM((FFN_TILE, D_MODEL), F32),
                            pltpu.VMEM((FFN_TILE, D_MODEL), F32),
                            pltpu.SemaphoreType.DMA((2,)),
                            pltpu.SemaphoreType.DMA((2,))],
        ),
        compiler_params=_cparams(("arbitrary",)),
        name="moe_ffn",
    )(tea, teb, n_used, gsrc, sdst, h2, wg, wu, wd, wg, wu, wd)


def _moe(h2, ri, cnt, wg_bf, wu_bf, wd_bf, layer):
    n = h2.shape[0]
    max_tiles = n // FFN_TILE + N_PAIRS
    n_slots = max_tiles * FFN_TILE
    counts = cnt[:N_PAIRS, 0].astype(I32)
    padded = ((counts + FFN_TILE - 1) // FFN_TILE) * FFN_TILE
    ends = jnp.cumsum(padded)
    offs = ends - padded
    q, rank = ri[0], ri[1]
    cls = jnp.arange(N_PAIRS, dtype=I32)
    slot = rank + jnp.sum(jnp.where(q[None, :] == cls[:, None], offs[:, None], 0),
                          axis=0)
    seg = jnp.stack([offs + counts, ends], axis=1).reshape(2 * N_PAIRS)
    n_used = ends[-1] // FFN_TILE
    tile = jnp.arange(max_tiles + 1, dtype=I32)
    tq = jnp.sum((tile[:, None] * FFN_TILE >= ends[None, :]).astype(I32), axis=1)
    tq_last = jnp.sum(((n_used - 1) * FFN_TILE >= ends).astype(I32))
    tq = jnp.where(tile < n_used, tq, tq_last)
    pa = jnp.array([a for a, _ in PAIRS], I32)
    pb = jnp.array([b for _, b in PAIRS], I32)
    grp, pidx = tq // len(PAIRS), tq % len(PAIRS)
    tea = EXPERTS_PER_GROUP * grp + jnp.take(pa, pidx)
    teb = EXPERTS_PER_GROUP * grp + jnp.take(pb, pidx)
    gsrc, sdst = _invert(slot, seg, n_slots)
    return _expert_ffn(tea, teb, n_used.reshape(1), gsrc, sdst, h2,
                       wg_bf, wu_bf, wd_bf, layer)


def _moe_out_tm_kernel(x1_ref, y_ref, g2_ref, op_ref, os_ref, scr, *, n_p_tiles):
    i = pl.program_id(0)
    x2 = x1_ref[...] + _per_row(y_ref[...], g2_ref[...], lambda a, b: a * b)
    pl.when(i < n_p_tiles)(
        lambda: _from_time_major(x2, op_ref, scr, op_ref.shape[0]))
    pl.when(i >= n_p_tiles)(
        lambda: _from_time_major(x2, os_ref, scr, os_ref.shape[0]))


def _moe_out_tm(x1, y, modpat, shape_p, shape_s):
    n = x1.shape[0]
    tm = TOKEN_TILE
    n_p_tiles = shape_p[0] * shape_p[1] // tm
    p = modpat.shape[1]
    rowblk = pl.BlockSpec((tm, D_MODEL), lambda i: (i, 0))
    stream = _stream_specs(jax.ShapeDtypeStruct(shape_p, F32),
                           jax.ShapeDtypeStruct(shape_s, F32), n_p_tiles)
    return pl.pallas_call(
        functools.partial(_moe_out_tm_kernel, n_p_tiles=n_p_tiles),
        out_shape=[jax.ShapeDtypeStruct(shape_p, F32),
                   jax.ShapeDtypeStruct(shape_s, F32)],
        grid=(n // tm,),
        in_specs=[rowblk, rowblk,
                  pl.BlockSpec((None, p, D_MODEL),
                               lambda i: (jnp.where(i < n_p_tiles, 0, 1), 0, 5))],
        out_specs=stream,
        scratch_shapes=[pltpu.VMEM((LANE_TILES, tm, LANES), F32)],
        compiler_params=_cparams(("arbitrary",)),
        name="moe_out_tm",
    )(x1, y, modpat)


def _moe_out_final_kernel(x1_ref, y_ref, g2_ref, nfin_ref, op_ref, os_ref,
                          *, n_p_tiles):
    i = pl.program_id(0)
    x2 = x1_ref[...] + _per_row(y_ref[...], g2_ref[...], lambda a, b: a * b)
    out = _rms(x2) * nfin_ref[...]

    @pl.when(i < n_p_tiles)
    def _():
        op_ref[...] = out

    @pl.when(i >= n_p_tiles)
    def _():
        os_ref[...] = out


def _moe_out_final(x1, y, modpat, norm_final, n_p, rows_per_request):
    n = x1.shape[0]
    tm = TOKEN_TILE
    n_p_tiles = n_p // tm
    p = modpat.shape[1]
    rowblk = pl.BlockSpec((tm, D_MODEL), lambda i: (i, 0))
    pat = lambda i: jnp.where(
        i < n_p_tiles, 0, 1 + (i - n_p_tiles) // (rows_per_request // tm))
    return pl.pallas_call(
        functools.partial(_moe_out_final_kernel, n_p_tiles=n_p_tiles),
        out_shape=[jax.ShapeDtypeStruct((n_p, D_MODEL), F32),
                   jax.ShapeDtypeStruct((n - n_p, D_MODEL), F32)],
        grid=(n // tm,),
        in_specs=[rowblk, rowblk,
                  pl.BlockSpec((None, p, D_MODEL), lambda i: (pat(i), 0, 5)),
                  pl.BlockSpec((1, D_MODEL), lambda i: (0, 0))],
        out_specs=[
            pl.BlockSpec((tm, D_MODEL),
                         lambda i: (jnp.minimum(i, n_p_tiles - 1), 0)),
            pl.BlockSpec((tm, D_MODEL),
                         lambda i: (jnp.maximum(i - n_p_tiles, 0), 0))],
        compiler_params=_cparams(("arbitrary",)),
        name="moe_out_final",
    )(x1, y, modpat, norm_final)


def _block_diag_weights(bb_re, bb_im, c_re, c_im):
    lt = LANE_TILES
    gl = SSM_GROUPS // lt
    eye = jnp.eye(gl, dtype=F32)

    def in_map(bb):
        b5 = bb.reshape(2, lt, gl, STATE_DIM, SSM_GROUP)
        t = jnp.einsum("dinph,kn->dikhnp", b5, eye)
        return t.reshape(2, lt, gl * SSM_GROUP, gl * STATE_DIM)

    def out_map(cc):
        c5 = cc.reshape(2, lt, gl, SSM_GROUP, STATE_DIM)
        t = jnp.einsum("dikhp,kn->dikpnh", c5, eye)
        return t.reshape(2, lt, gl * STATE_DIM, gl * SSM_GROUP)

    bt = jnp.concatenate([in_map(bb_re), in_map(bb_im)], axis=-1).astype(BF16)
    return bt, out_map(c_re).astype(BF16), out_map(c_im).astype(BF16)


def kernel(x_prompt, x_sample, c, state_ssm_re, state_ssm_im, c_ctx, norm_mix, norm_ffn, w_ada, b_ada, ssm_w_in, ssm_lam_re, ssm_lam_im, ssm_log_dt, ssm_b_re, ssm_b_im, ssm_c_re, ssm_c_im, ssm_d, ssm_w_glu, fnet_w_out, router_w, router_b, moe_w_gate, moe_w_up, moe_w_down, norm_final):
    bp, lp, _ = x_prompt.shape
    bs, ls, _ = x_sample.shape
    n_p = bp * lp
    n_s = bs * ls
    n = n_p + n_s
    tm = TOKEN_TILE

    cond = jnp.zeros((N_COND, D_MODEL), F32).at[0].set(c_ctx).at[1:1 + bs].set(c)
    modtab = _ada_table(cond, w_ada, b_ada).reshape(DEPTH, N_COND, N_MOD, D_MODEL)

    rwt = router_w.T
    rb = router_b.reshape(N_EXPERTS, 1)
    row = lambda v: v.reshape(1, D_MODEL)

    period = max(bp, bs)
    pat_tm = jnp.stack([
        jnp.broadcast_to(modtab[0, 0], (period, N_MOD, D_MODEL)),
        jnp.tile(modtab[0, 1:1 + bs], (period // bs, 1, 1))])
    pat_tm = pat_tm.reshape(2, period, N_MOD * D_MODEL)

    a_re, a_im, bb_re, bb_im = _zoh(ssm_lam_re[0], ssm_lam_im[0], ssm_log_dt[0],
                                    ssm_b_re[0], ssm_b_im[0])
    bt, ct_re, ct_im = _block_diag_weights(bb_re, bb_im, ssm_c_re[0], ssm_c_im[0])
    a_re = a_re.reshape(2, 1, N_STATE)
    a_im = a_im.reshape(2, 1, N_STATE)

    u = _ssm_in(x_prompt, x_sample, row(norm_mix[0]), pat_tm,
                ssm_w_in[0].astype(BF16))
    h0s_re = state_ssm_re[:, 0].reshape(bs, 2, N_STATE).transpose(1, 0, 2)
    h0s_im = state_ssm_im[:, 0].reshape(bs, 2, N_STATE).transpose(1, 0, 2)
    y_scan, fin_re, fin_im = _ssm_scan(u, bt, ct_re, ct_im, a_re, a_im,
                                       h0s_re, h0s_im, bp=bp, lp=lp, bs=bs, ls=ls)
    tri = jnp.triu(jnp.ones((tm, tm), BF16), k=1)
    x1, h2, ri, cnt = _ssm_out(u, y_scan, row(ssm_d[0]),
                               ssm_w_glu[0].astype(BF16), x_prompt, x_sample,
                               pat_tm, row(norm_ffn[0]), rwt, rb, tri)
    wg_bf, wu_bf, wd_bf = (w.astype(BF16) for w in (moe_w_gate, moe_w_up,
                                                    moe_w_down))
    y_moe = _moe(h2, ri, cnt, wg_bf, wu_bf, wd_bf, 0)
    x2_p, x2_s = _moe_out_tm(x1, y_moe, pat_tm, x_prompt.shape, x_sample.shape)

    pat_bm = jnp.broadcast_to(modtab[1][:1 + bs, None],
                              (1 + bs, SUBLANES, N_MOD, D_MODEL))
    pat_bm = pat_bm.reshape(1 + bs, SUBLANES, N_MOD * D_MODEL)
    x3, h2, ri, cnt = _fnet(x2_p, x2_s, row(norm_mix[1]), pat_bm,
                            fnet_w_out[0].astype(BF16), row(norm_ffn[1]), rwt, rb)
    y_moe = _moe(h2, ri, cnt, wg_bf, wu_bf, wd_bf, 1)
    y_p, y_s = _moe_out_final(x3, y_moe, pat_bm, row(norm_final), n_p, ls)

    st = lambda f: f.transpose(1, 0, 2).reshape(bp, 1, 2, SSM_GROUPS, STATE_DIM)
    return (y_p.reshape(bp, lp, D_MODEL), y_s.reshape(bs, ls, D_MODEL),
            st(fin_re), st(fin_im))
```

```python
import functools

import jax
import jax.numpy as jnp
from jax import lax
from jax.experimental import pallas as pl
from jax.experimental.pallas import tpu as pltpu

F32 = jnp.float32
BF16 = jnp.bfloat16
I32 = jnp.int32
HIGHEST = lax.Precision.HIGHEST

D_MODEL = 1024
DEPTH = 2
SSM_GROUP = 16
SSM_GROUPS = 64
STATE_DIM = 64
N_STATE = SSM_GROUPS * STATE_DIM
FNET_GROUP = 128
N_EXPERTS = 16
N_EXPERT_GROUPS = 4
EXPERTS_PER_GROUP = 4
D_EXPERT = 1024
N_MOD = 6
EPS = 1e-6

LANES = 128
SUBLANES = 8
LANE_TILES = D_MODEL // LANES
VMEM_LIMIT = 56 * 1024 * 1024

TOKEN_TILE = 512
FNET_TILE = 256
FFN_TILE = 256
N_COND = 16

PAIRS = ((0, 1), (0, 2), (0, 3), (1, 3), (1, 2), (3, 2))
N_PAIRS = N_EXPERT_GROUPS * len(PAIRS)
N_CLASS_ROWS = 32
SLAB = SUBLANES


def _cparams(sem, vmem=VMEM_LIMIT):
    return pltpu.CompilerParams(dimension_semantics=sem, vmem_limit_bytes=vmem)


def _sigmoid(x):
    return 1.0 / (1.0 + jnp.exp(-x))


def _gelu_tanh(x):
    c = 0.7978845608028654
    return x * (0.5 * (1.0 + jnp.tanh(c * (x + 0.044715 * (x * x * x)))))


def _per_row(v, pat, fn):
    tm, d = v.shape
    p = pat.shape[0]
    return fn(v.reshape(tm // p, p, d), pat[None]).reshape(tm, d)


def _rms(x):
    ms = jnp.mean(x * x, axis=-1, keepdims=True)
    return x * lax.rsqrt(ms + EPS)


def _norm_mod(x, g, sc, sh):
    y = _rms(x) * g
    y = _per_row(y, sc, lambda a, b: a * (1.0 + b))
    return _per_row(y, sh, lambda a, b: a + b)


def _lane_tiles(v):
    return [v[:, LANES * k:LANES * (k + 1)] for k in range(v.shape[1] // LANES)]


def _to_time_major(x_ref, scr, batch):
    tt = x_ref.shape[1]
    for b in range(batch):
        for k, piece in enumerate(_lane_tiles(x_ref[b])):
            scr[k, pl.ds(b, tt, stride=batch), :] = piece
    return jnp.concatenate([scr[k] for k in range(LANE_TILES)], axis=1)


def _from_time_major(v, o_ref, scr, batch):
    tt = v.shape[0] // batch
    for k, piece in enumerate(_lane_tiles(v)):
        scr[k] = piece
    for b in range(batch):
        o_ref[b] = jnp.concatenate(
            [scr[k, pl.ds(b, tt, stride=batch), :] for k in range(LANE_TILES)],
            axis=1)


def _slab_rows(ref, k, n):
    return ref[pl.ds(k, n, stride=SLAB), :]


def _store_slabs(v, ref):
    n = v.shape[0]
    for k, piece in enumerate(_lane_tiles(v)):
        ref[pl.ds(k, n, stride=SLAB), :] = piece


def _rows_from_slabs(y_ref, n):
    return jnp.concatenate([_slab_rows(y_ref, k, n) for k in range(SLAB)], axis=1)


def _route(logits_t, rb, tri, carry):
    ne, tm = logits_t.shape
    s = _sigmoid(logits_t)
    bz = s + rb
    row = lambda a, r: a[r:r + 1, :]
    gs = []
    for g in range(N_EXPERT_GROUPS):
        v0, v1, v2, v3 = (row(bz, EXPERTS_PER_GROUP * g + j) for j in range(4))
        hi1, lo1 = jnp.maximum(v0, v1), jnp.minimum(v0, v1)
        hi2, lo2 = jnp.maximum(v2, v3), jnp.minimum(v2, v3)
        top1 = jnp.maximum(hi1, hi2)
        top2 = jnp.maximum(jnp.minimum(hi1, hi2), jnp.maximum(lo1, lo2))
        gs.append(top1 + top2)
    bg = jnp.zeros((1, tm), I32)
    bv = gs[0]
    for g in range(1, N_EXPERT_GROUPS):
        upd = gs[g] > bv
        bg = jnp.where(upd, g, bg)
        bv = jnp.where(upd, gs[g], bv)
    cb, cs = [], []
    for j in range(EXPERTS_PER_GROUP):
        vb, vs = row(bz, j), row(s, j)
        for g in range(1, N_EXPERT_GROUPS):
            sel = bg == g
            vb = jnp.where(sel, row(bz, EXPERTS_PER_GROUP * g + j), vb)
            vs = jnp.where(sel, row(s, EXPERTS_PER_GROUP * g + j), vs)
        cb.append(vb)
        cs.append(vs)
    i1 = jnp.zeros((1, tm), I32)
    b1, s1 = cb[0], cs[0]
    for j in range(1, EXPERTS_PER_GROUP):
        upd = cb[j] > b1
        i1 = jnp.where(upd, j, i1)
        b1 = jnp.where(upd, cb[j], b1)
        s1 = jnp.where(upd, cs[j], s1)
    i2 = jnp.zeros((1, tm), I32)
    b2 = jnp.full((1, tm), -jnp.inf, F32)
    s2 = jnp.zeros((1, tm), F32)
    for j in range(EXPERTS_PER_GROUP):
        cand = jnp.where(i1 == j, -jnp.inf, cb[j])
        upd = cand > b2
        i2 = jnp.where(upd, j, i2)
        b2 = jnp.where(upd, cand, b2)
        s2 = jnp.where(upd, cs[j], s2)
    den = s1 + s2
    w1 = s1 / den
    w2 = s2 / den
    lo = jnp.minimum(i1, i2)
    hi = jnp.maximum(i1, i2)
    pidx = jnp.where(lo == 0, hi - 1,
                     jnp.where(lo == 1, jnp.where(hi == 3, 3, 4), 5))
    first = jnp.where(pidx < 3, 0, jnp.where(pidx < 5, 1, 3))
    wa = jnp.where(i1 == first, w1, w2)
    wb = jnp.where(i1 == first, w2, w1)
    q = len(PAIRS) * bg + pidx
    qio = lax.broadcasted_iota(I32, (N_CLASS_ROWS, tm), 0)
    oh = qio == q
    ohf = jnp.where(oh, 1.0, 0.0)
    cum = jnp.dot(ohf.astype(BF16), tri, preferred_element_type=F32) + carry
    rank = jnp.sum(jnp.where(oh, cum, 0.0), axis=0, keepdims=True)
    new_carry = carry + jnp.sum(ohf, axis=1, keepdims=True)
    return q, rank.astype(I32), wa, wb, new_carry


def _post_mixer(m, x, g1, nf, sc2, sh2, rwt_ref, rb_ref, tri_ref,
                x1_ref, h2_ref, ri_ref, rw_ref, cnt_ref, carry, first):
    @pl.when(first)
    def _():
        carry[...] = jnp.zeros(carry.shape, F32)

    tm = x.shape[0]
    x1 = x + _per_row(m, g1, lambda a, b: a * b)
    x1_ref[...] = x1
    h2 = _norm_mod(x1, nf, sc2, sh2)
    logits_t = lax.dot_general(rwt_ref[...], h2, (((1,), (1,)), ((), ())),
                               precision=HIGHEST, preferred_element_type=F32)
    q, rank, wa, wb, nc = _route(logits_t, rb_ref[...], tri_ref[...],
                                 carry[:, 0:1])
    _store_slabs(h2, h2_ref)
    rw_ref[0:1, :] = wa
    rw_ref[1:2, :] = wb
    rw_ref[2:8, :] = jnp.zeros((6, tm), F32)
    ri_ref[0:1, :] = q
    ri_ref[1:2, :] = rank
    ri_ref[2:8, :] = jnp.zeros((6, tm), I32)
    carry[...] = jnp.broadcast_to(nc, carry.shape)
    cnt_ref[...] = carry[...]


def _ada_kernel(cond_ref, w_ref, b_ref, o_ref):
    c = cond_ref[...]
    s = c * _sigmoid(c)
    o_ref[...] = jnp.dot(s, w_ref[...], precision=HIGHEST,
                         preferred_element_type=F32) + b_ref[...]


def _ada_table(cond, w_ada, b_ada):
    tn = 1536
    n_out = N_MOD * D_MODEL
    return pl.pallas_call(
        _ada_kernel,
        out_shape=jax.ShapeDtypeStruct((DEPTH, N_COND, n_out), F32),
        grid=(DEPTH, n_out // tn),
        in_specs=[
            pl.BlockSpec((N_COND, D_MODEL), lambda l, j: (0, 0)),
            pl.BlockSpec((None, D_MODEL, tn), lambda l, j: (l, 0, j)),
            pl.BlockSpec((None, 1, tn), lambda l, j: (l, 0, j)),
        ],
        out_specs=pl.BlockSpec((None, N_COND, tn), lambda l, j: (l, 0, j)),
        compiler_params=_cparams(("arbitrary", "arbitrary")),
        name="ada_table",
    )(cond, w_ada, b_ada.reshape(DEPTH, 1, n_out))


def _zoh_kernel(lr_ref, li_ref, ldt_ref, br_ref, bi_ref,
                are_ref, aim_ref, bbre_ref, bbim_ref):
    lr = lr_ref[...]
    li = li_ref[...]
    dt = jnp.exp(ldt_ref[...])
    mag = jnp.exp(lr * dt)
    a_re = mag * jnp.cos(li * dt)
    a_im = mag * jnp.sin(li * dt)
    den = lr * lr + li * li
    nr = a_re - 1.0
    f_re = (nr * lr + a_im * li) / den
    f_im = (a_im * lr - nr * li) / den
    br = br_ref[...]
    bi = bi_ref[...]
    are_ref[...] = a_re
    aim_ref[...] = a_im
    bbre_ref[...] = f_re * br - f_im * bi
    bbim_ref[...] = f_re * bi + f_im * br


def _zoh(lam_re, lam_im, log_dt, b_re, b_im):
    shape = b_re.shape
    flat = (shape[0] * shape[1] * shape[2] * shape[3] // LANES, LANES)
    bc = lambda a: jnp.broadcast_to(a, shape).reshape(flat)
    args = (bc(lam_re[..., None]), bc(lam_im[..., None]),
            bc(log_dt[:, :, None, None]), b_re.reshape(flat), b_im.reshape(flat))
    outs = pl.pallas_call(
        _zoh_kernel,
        out_shape=[jax.ShapeDtypeStruct(flat, F32)] * 4,
        name="zoh_discretize",
    )(*args)
    a_re, a_im, bb_re, bb_im = (o.reshape(shape) for o in outs)
    return a_re[..., 0], a_im[..., 0], bb_re, bb_im


def _stream_specs(xp, xs, n_p_tiles):
    bp, bs = xp.shape[0], xs.shape[0]
    return [
        pl.BlockSpec((bp, TOKEN_TILE // bp, D_MODEL),
                     lambda i: (0, jnp.minimum(i, n_p_tiles - 1), 0)),
        pl.BlockSpec((bs, TOKEN_TILE // bs, D_MODEL),
                     lambda i: (0, jnp.maximum(i - n_p_tiles, 0), 0)),
    ]


def _ssm_in_kernel(xp_ref, xs_ref, g_ref, sc_ref, sh_ref, w_ref, u_ref, scr,
                   *, n_p_tiles):
    def run(x_ref):
        x = _to_time_major(x_ref, scr, x_ref.shape[0])
        h = _norm_mod(x, g_ref[...], sc_ref[...], sh_ref[...])
        u_ref[...] = jnp.dot(h.astype(BF16), w_ref[...],
                             preferred_element_type=F32)

    i = pl.program_id(0)
    pl.when(i < n_p_tiles)(lambda: run(xp_ref))
    pl.when(i >= n_p_tiles)(lambda: run(xs_ref))


def _ssm_in(xp, xs, g, modpat, w_in_bf):
    n = (xp.shape[0] * xp.shape[1] + xs.shape[0] * xs.shape[1])
    tm = TOKEN_TILE
    n_p_tiles = xp.shape[0] * xp.shape[1] // tm
    p = modpat.shape[1]
    pat = lambda i: jnp.where(i < n_p_tiles, 0, 1)
    mod = lambda k: pl.BlockSpec((None, p, D_MODEL), lambda i: (pat(i), 0, k))
    return pl.pallas_call(
        functools.partial(_ssm_in_kernel, n_p_tiles=n_p_tiles),
        out_shape=jax.ShapeDtypeStruct((n, D_MODEL), F32),
        grid=(n // tm,),
        in_specs=_stream_specs(xp, xs, n_p_tiles) + [
            pl.BlockSpec((1, D_MODEL), lambda i: (0, 0)),
            mod(1), mod(0),
            pl.BlockSpec((D_MODEL, D_MODEL), lambda i: (0, 0)),
        ],
        out_specs=pl.BlockSpec((tm, D_MODEL), lambda i: (i, 0)),
        scratch_shapes=[pltpu.VMEM((LANE_TILES, tm, LANES), F32)],
        compiler_params=_cparams(("arbitrary",)),
        name="ssm_in",
    )(xp, xs, g, modpat, modpat, w_in_bf)


def _scan_kernel(u_ref, bt_ref, cre_ref, cim_ref, are_ref, aim_ref,
                 h0re_ref, h0im_ref, y_ref, fre_ref, fim_ref,
                 *scratch, n_p_chunks, bp, bs):
    xre = scratch[:LANE_TILES]
    xim = scratch[LANE_TILES:2 * LANE_TILES]
    hre, him = scratch[2 * LANE_TILES:]
    d = pl.program_id(0)
    c = pl.program_id(1)
    cols = N_STATE // LANE_TILES
    rows = u_ref.shape[0]

    def chunk(batch, reverse):
        steps = rows // batch

        def b_proj(s):
            u = u_ref[:, LANES * s:LANES * (s + 1)].astype(BF16)
            xt = jnp.dot(u, bt_ref[s], preferred_element_type=F32)
            xre[s][...] = xt[:, :cols]
            xim[s][...] = xt[:, cols:]

        def scan(s):
            sl = slice(cols * s, cols * (s + 1))
            ar = jnp.broadcast_to(are_ref[:, sl], (batch, cols))
            ai = jnp.broadcast_to(aim_ref[:, sl], (batch, cols))
            hr = hre[0:batch, sl]
            hi = him[0:batch, sl]
            for t in range(steps):
                tt = steps - 1 - t if reverse else t
                r = slice(tt * batch, (tt + 1) * batch)
                nr = ar * hr - ai * hi + xre[s][r, :]
                ni = ar * hi + ai * hr + xim[s][r, :]
                xre[s][r, :] = nr
                xim[s][r, :] = ni
                hr, hi = nr, ni
            hre[0:batch, sl] = hr
            him[0:batch, sl] = hi

        def c_proj(s):
            y_ref[:, LANES * s:LANES * (s + 1)] = (
                jnp.dot(xre[s][...].astype(BF16), cre_ref[s],
                        preferred_element_type=F32)
                - jnp.dot(xim[s][...].astype(BF16), cim_ref[s],
                          preferred_element_type=F32))

        for stage in range(LANE_TILES + 2):
            if stage < LANE_TILES:
                b_proj(stage)
            if 0 <= stage - 1 < LANE_TILES:
                scan(stage - 1)
            if 0 <= stage - 2 < LANE_TILES:
                c_proj(stage - 2)

    @pl.when(c == 0)
    def _():
        hre[...] = jnp.zeros(hre.shape, F32)
        him[...] = jnp.zeros(him.shape, F32)

    @pl.when(c == n_p_chunks)
    def _():
        hre[0:bs, :] = h0re_ref[...]
        him[0:bs, :] = h0im_ref[...]

    in_p = c < n_p_chunks
    in_s = jnp.logical_not(in_p)
    fwd = d == 0
    bwd = jnp.logical_not(fwd)
    pl.when(jnp.logical_and(in_p, fwd))(lambda: chunk(bp, False))
    pl.when(jnp.logical_and(in_p, bwd))(lambda: chunk(bp, True))
    pl.when(jnp.logical_and(in_s, fwd))(lambda: chunk(bs, False))
    pl.when(jnp.logical_and(in_s, bwd))(lambda: chunk(bs, True))

    @pl.when(c == n_p_chunks - 1)
    def _():
        fre_ref[...] = hre[0:bp, :]
        fim_ref[...] = him[0:bp, :]


def _ssm_scan(u, bt, ct_re, ct_im, a_re, a_im, h0_re, h0_im, *, bp, lp, bs, ls):
    n = u.shape[0]
    rows = TOKEN_TILE
    n_p_chunks = bp * lp // rows
    n_s_chunks = bs * ls // rows

    def chunk(d, c):
        in_p = jnp.where(d == 0, c, n_p_chunks - 1 - c)
        cs = c - n_p_chunks
        in_s = n_p_chunks + jnp.where(d == 0, cs, n_s_chunks - 1 - cs)
        return jnp.where(c < n_p_chunks, in_p, in_s)

    cols = N_STATE // LANE_TILES
    dspec = lambda shape: pl.BlockSpec((None,) + shape,
                                       lambda d, c: (d,) + (0,) * len(shape))
    kern = functools.partial(_scan_kernel, n_p_chunks=n_p_chunks, bp=bp, bs=bs)
    return pl.pallas_call(
        kern,
        out_shape=[jax.ShapeDtypeStruct((2, n, D_MODEL), F32),
                   jax.ShapeDtypeStruct((2, bp, N_STATE), F32),
                   jax.ShapeDtypeStruct((2, bp, N_STATE), F32)],
        grid=(2, n_p_chunks + n_s_chunks),
        in_specs=[
            pl.BlockSpec((rows, D_MODEL), lambda d, c: (chunk(d, c), 0)),
            dspec((LANE_TILES, LANES, 2 * cols)),
            dspec((LANE_TILES, cols, LANES)),
            dspec((LANE_TILES, cols, LANES)),
            dspec((1, N_STATE)),
            dspec((1, N_STATE)),
            dspec((bs, N_STATE)),
            dspec((bs, N_STATE)),
        ],
        out_specs=[
            pl.BlockSpec((None, rows, D_MODEL), lambda d, c: (d, chunk(d, c), 0)),
            dspec((bp, N_STATE)),
            dspec((bp, N_STATE)),
        ],
        scratch_shapes=(
            [pltpu.VMEM((rows, cols), F32)] * (2 * LANE_TILES)
            + [pltpu.VMEM((max(bp, bs), N_STATE), F32)] * 2),
        compiler_params=_cparams(("arbitrary", "arbitrary")),
        name="ssm_scan",
    )(u, bt, ct_re, ct_im, a_re, a_im, h0_re, h0_im)


def _ssm_out_kernel(u_ref, y0_ref, y1_ref, dsk_ref, wglu_ref, xp_ref, xs_ref,
                    g1_ref, nf_ref, sc2_ref, sh2_ref, rwt_ref, rb_ref, tri_ref,
                    x1_ref, h2_ref, ri_ref, rw_ref, cnt_ref, scr, carry,
                    *, n_p_tiles):
    i = pl.program_id(0)

    def run(x_ref):
        x = _to_time_major(x_ref, scr, x_ref.shape[0])
        y = u_ref[...] * dsk_ref[...] + y0_ref[...] + y1_ref[...]
        ge = _gelu_tanh(y).astype(BF16)
        vg = jnp.dot(ge, wglu_ref[...], preferred_element_type=F32)
        m = vg[:, :D_MODEL] * _sigmoid(vg[:, D_MODEL:])
        _post_mixer(m, x, g1_ref[...], nf_ref[...], sc2_ref[...], sh2_ref[...],
                    rwt_ref, rb_ref, tri_ref, x1_ref, h2_ref, ri_ref, rw_ref,
                    cnt_ref, carry, i == 0)

    pl.when(i < n_p_tiles)(lambda: run(xp_ref))
    pl.when(i >= n_p_tiles)(lambda: run(xs_ref))


def _route_out_shapes(n):
    return [jax.ShapeDtypeStruct((n, D_MODEL), F32),
            jax.ShapeDtypeStruct((n * SLAB, LANES), F32),
            jax.ShapeDtypeStruct((SUBLANES, n), I32),
            jax.ShapeDtypeStruct((SUBLANES, n), F32),
            jax.ShapeDtypeStruct((N_CLASS_ROWS, LANES), F32)]


def _ssm_out(u, y, d_skip, w_glu_bf, xp, xs, modpat, norm_ffn, rwt, rb, tri):
    n = u.shape[0]
    tm = TOKEN_TILE
    n_p_tiles = xp.shape[0] * xp.shape[1] // tm
    p = modpat.shape[1]
    pat = lambda i: jnp.where(i < n_p_tiles, 0, 1)
    mod = lambda k: pl.BlockSpec((None, p, D_MODEL), lambda i: (pat(i), 0, k))
    full = lambda shape: pl.BlockSpec(shape, lambda i: (0,) * len(shape))
    rowblk = pl.BlockSpec((tm, D_MODEL), lambda i: (i, 0))
    return pl.pallas_call(
        functools.partial(_ssm_out_kernel, n_p_tiles=n_p_tiles),
        out_shape=_route_out_shapes(n),
        grid=(n // tm,),
        in_specs=[
            rowblk,
            pl.BlockSpec((None, tm, D_MODEL), lambda i: (0, i, 0)),
            pl.BlockSpec((None, tm, D_MODEL), lambda i: (1, i, 0)),
            full((1, D_MODEL)),
            full((D_MODEL, 2 * D_MODEL)),
        ] + _stream_specs(xp, xs, n_p_tiles) + [
            mod(2), full((1, D_MODEL)), mod(4), mod(3),
            full((N_EXPERTS, D_MODEL)), full((N_EXPERTS, 1)), full((tm, tm)),
        ],
        out_specs=[rowblk,
                   pl.BlockSpec((tm * SLAB, LANES), lambda i: (i, 0)),
                   pl.BlockSpec((SUBLANES, tm), lambda i: (0, i)),
                   pl.BlockSpec((SUBLANES, tm), lambda i: (0, i)),
                   full((N_CLASS_ROWS, LANES))],
        scratch_shapes=[pltpu.VMEM((LANE_TILES, tm, LANES), F32),
                        pltpu.VMEM((N_CLASS_ROWS, LANES), F32)],
        compiler_params=_cparams(("arbitrary",)),
        name="ssm_out",
    )(u, y, y, d_skip, w_glu_bf, xp, xs, modpat, norm_ffn, modpat, modpat,
      rwt, rb, tri)


def _fnet_kernel(xp_ref, xs_ref, nm_ref, sc1_ref, sh1_ref, cs_ref, dftp_ref,
                 dfts_ref, wout_ref, g1_ref, nf_ref, sc2_ref, sh2_ref, rwt_ref,
                 rb_ref, tri_ref, x1_ref, h2_ref, ri_ref, rw_ref, cnt_ref, ucs,
                 carry, *, bp, tiles_s):
    s = pl.program_id(0)
    n_groups = D_MODEL // FNET_GROUP
    tr = FNET_TILE

    def run(x_ref, dft_ref, i):
        seq = x_ref.shape[0]
        scale = float((seq * FNET_GROUP) ** -0.5)

        @pl.when(i == 0)
        def _():
            def stage1(r, _):
                rows = pl.ds(pl.multiple_of(r * tr, tr), tr)
                h = _norm_mod(x_ref[rows, :], nm_ref[...], sc1_ref[...],
                              sh1_ref[...]).astype(BF16)
                for k in range(n_groups):
                    cols = slice(FNET_GROUP * k, FNET_GROUP * (k + 1))
                    t = jnp.dot(h[:, cols], cs_ref[...],
                                preferred_element_type=F32)
                    ucs[rows, cols] = t[:, :FNET_GROUP].astype(BF16)
                    ucs[pl.ds(pl.multiple_of(seq + r * tr, tr), tr), cols] = (
                        t[:, FNET_GROUP:].astype(BF16))
                return 0
            lax.fori_loop(0, seq // tr, stage1, 0)

        y = jnp.dot(dft_ref[...], ucs[0:2 * seq, :],
                    preferred_element_type=F32) * scale
        m = jnp.dot(y.astype(BF16), wout_ref[...], preferred_element_type=F32)
        xrow = x_ref[pl.ds(pl.multiple_of(i * tr, tr), tr), :]
        _post_mixer(m, xrow, g1_ref[...], nf_ref[...], sc2_ref[...],
                    sh2_ref[...], rwt_ref, rb_ref, tri_ref, x1_ref, h2_ref,
                    ri_ref, rw_ref, cnt_ref, carry, s == 0)

    pl.when(s < bp)(lambda: run(xp_ref, dftp_ref, 0 * s))
    pl.when(s >= bp)(lambda: run(xs_ref, dfts_ref, (s - bp) % tiles_s))


def _dft_table(seq):
    inner = 64
    k = jnp.arange(seq, dtype=I32)[:, None]
    t1 = jnp.arange(seq // inner, dtype=I32)[None, :] * inner
    t2 = jnp.arange(inner, dtype=I32)[None, :]
    ang = lambda t: ((k * t) % seq).astype(F32) * (2.0 * jnp.pi / seq)
    ca, sa = jnp.cos(ang(t1))[:, :, None], jnp.sin(ang(t1))[:, :, None]
    cb, sb = jnp.cos(ang(t2))[:, None, :], jnp.sin(ang(t2))[:, None, :]
    cos = (ca * cb - sa * sb).reshape(seq, seq)
    sin = (sa * cb + ca * sb).reshape(seq, seq)
    return jnp.concatenate([cos, -sin], axis=1).astype(BF16)


def _fnet(xp, xs, norm_mix, modpat, w_out_bf, norm_ffn, rwt, rb):
    bp, lp, _ = xp.shape
    bs, ls, _ = xs.shape
    tr = FNET_TILE
    assert lp == tr
    tiles_s = ls // tr
    n = bp * lp + bs * ls
    p = modpat.shape[1]
    kc = jnp.arange(FNET_GROUP, dtype=I32)
    angc = ((kc[:, None] * kc[None, :]) % FNET_GROUP).astype(F32) * (
        2.0 * jnp.pi / FNET_GROUP)
    cs = jnp.concatenate([jnp.cos(angc), jnp.sin(angc)], axis=1).astype(BF16)
    tri = jnp.triu(jnp.ones((tr, tr), BF16), k=1)

    sb = lambda s: jnp.clip((s - bp) // tiles_s, 0, bs - 1)
    pat = lambda s: jnp.where(s < bp, 0, 1 + sb(s))
    mod = lambda kk: pl.BlockSpec((None, p, D_MODEL), lambda s: (pat(s), 0, kk))
    full = lambda shape: pl.BlockSpec(shape, lambda s: (0,) * len(shape))
    kern = functools.partial(_fnet_kernel, bp=bp, tiles_s=tiles_s)
    return pl.pallas_call(
        kern,
        out_shape=_route_out_shapes(n),
        grid=(bp + bs * tiles_s,),
        in_specs=[
            pl.BlockSpec((None, lp, D_MODEL),
                         lambda s: (jnp.minimum(s, bp - 1), 0, 0)),
            pl.BlockSpec((None, ls, D_MODEL), lambda s: (sb(s), 0, 0)),
            full((1, D_MODEL)), mod(1), mod(0),
            full((FNET_GROUP, 2 * FNET_GROUP)),
            full((tr, 2 * lp)),
            pl.BlockSpec((tr, 2 * ls),
                         lambda s: (jnp.maximum(s - bp, 0) % tiles_s, 0)),
            full((D_MODEL, D_MODEL)),
            mod(2), full((1, D_MODEL)), mod(4), mod(3),
            full((N_EXPERTS, D_MODEL)), full((N_EXPERTS, 1)), full((tr, tr)),
        ],
        out_specs=[pl.BlockSpec((tr, D_MODEL), lambda s: (s, 0)),
                   pl.BlockSpec((tr * SLAB, LANES), lambda s: (s, 0)),
                   pl.BlockSpec((SUBLANES, tr), lambda s: (0, s)),
                   pl.BlockSpec((SUBLANES, tr), lambda s: (0, s)),
                   full((N_CLASS_ROWS, LANES))],
        scratch_shapes=[pltpu.VMEM((2 * ls, D_MODEL), BF16),
                        pltpu.VMEM((N_CLASS_ROWS, LANES), F32)],
        compiler_params=_cparams(("arbitrary",)),
        name="fnet",
    )(xp, xs, norm_mix, modpat, modpat, cs, _dft_table(lp), _dft_table(ls),
      w_out_bf, modpat, norm_ffn, modpat, modpat, rwt, rb, tri)


def _invert_kernel(slot_ref, seg_ref, gsrc_ref, sdst_ref, *, n_pad_rows):
    n = slot_ref.shape[0]
    n_slots = gsrc_ref.shape[0]
    unroll = 8

    def pad(j, _):
        gsrc_ref[j] = 0
        sdst_ref[j] = n + j % n_pad_rows
        return 0
    for q in range(N_PAIRS):
        lax.fori_loop(seg_ref[2 * q], seg_ref[2 * q + 1], pad, 0)
    lax.fori_loop(seg_ref[2 * N_PAIRS - 1], n_slots, pad, 0)

    def body(j, _):
        for k in range(unroll):
            t = j * unroll + k
            s = slot_ref[t]
            gsrc_ref[s] = t
            sdst_ref[s] = t
        return 0
    lax.fori_loop(0, n // unroll, body, 0)


def _invert(slot, seg, n_slots):
    smem = pl.BlockSpec(memory_space=pltpu.SMEM)
    return pl.pallas_call(
        functools.partial(_invert_kernel, n_pad_rows=FFN_TILE),
        out_shape=[jax.ShapeDtypeStruct((n_slots,), I32)] * 2,
        in_specs=[smem, smem],
        out_specs=[smem, smem],
        name="moe_invert",
    )(slot, seg)


def _ffn_kernel(tea_ref, teb_ref, nu_ref, gsrc_ref, sdst_ref, wa_ref, wb_ref,
                h_hbm, wga, wua, wda, wgb, wub, wdb, y_hbm,
                xbuf0, xbuf1, obuf0, obuf1, wcola, wcolb, gsem, ssem):
    del tea_ref, teb_ref
    i = pl.program_id(0)
    nu = nu_ref[0]
    t = FFN_TILE

    def slab(ref, r):
        return ref.at[pl.ds(pl.multiple_of(r * SLAB, SLAB), SLAB), :]

    def gather_start(tile, xbuf, sem):
        base = tile * t
        for r in range(t):
            pltpu.make_async_copy(slab(h_hbm, gsrc_ref[base + r]), slab(xbuf, r),
                                  sem).start()

    def gather_wait(xbuf, sem):
        pltpu.make_async_copy(h_hbm.at[pl.ds(0, t * SLAB), :], xbuf, sem).wait()

    def scatter_start(tile, obuf, sem):
        base = tile * t
        for r in range(t):
            pltpu.make_async_copy(slab(obuf, r), slab(y_hbm, sdst_ref[base + r]),
                                  sem).start()

    def scatter_wait(obuf, sem):
        pltpu.make_async_copy(obuf, y_hbm.at[pl.ds(0, t * SLAB), :], sem).wait()

    def expert(xb, wg, wu, wd):
        g = jnp.dot(xb, wg[...], preferred_element_type=F32)
        u = jnp.dot(xb, wu[...], preferred_element_type=F32)
        a = (g * _sigmoid(g)) * u
        return jnp.dot(a.astype(BF16), wd[...], preferred_element_type=F32)

    def step(xc, xn, oc, op, gc, gn, sc, sp):
        pl.when(i == 0)(lambda: gather_start(0, xc, gc))
        pl.when(i >= 2)(lambda: scatter_wait(oc, sc))
        gather_wait(xc, gc)
        pl.when(i + 1 < nu)(lambda: gather_start(i + 1, xn, gn))
        pl.when(i >= 1)(lambda: scatter_start(i - 1, op, sp))
        for r in range(t):
            tok = gsrc_ref[i * t + r]
            wcola[r:r + 1, :] = jnp.full((1, LANES), wa_ref[tok], F32)
            wcolb[r:r + 1, :] = jnp.full((1, LANES), wb_ref[tok], F32)
        xb = _rows_from_slabs(xc, t).astype(BF16)
        ya = expert(xb, wga, wua, wda)
        yb = expert(xb, wgb, wub, wdb)
        y = wcola[:, 0:1] * ya + wcolb[:, 0:1] * yb
        for k, piece in enumerate(_lane_tiles(y)):
            oc[pl.ds(k, t, stride=SLAB), :] = piece

    def drain(oc, op, sc, sp):
        pl.when(i >= 2)(lambda: scatter_wait(oc, sc))
        scatter_start(i - 1, op, sp)
        scatter_wait(op, sp)
        spare_row = y_hbm.shape[0] - t * SLAB
        oc[...] = jnp.zeros(oc.shape, F32)
        spare = pltpu.make_async_copy(
            oc, y_hbm.at[pl.ds(spare_row, t * SLAB), :], sc)
        spare.start()
        spare.wait()

    even = (i % 2) == 0
    odd = jnp.logical_not(even)
    g0, g1, s0, s1 = gsem.at[0], gsem.at[1], ssem.at[0], ssem.at[1]
    pl.when(jnp.logical_and(i < nu, even))(
        lambda: step(xbuf0, xbuf1, obuf0, obuf1, g0, g1, s0, s1))
    pl.when(jnp.logical_and(i < nu, odd))(
        lambda: step(xbuf1, xbuf0, obuf1, obuf0, g1, g0, s1, s0))
    pl.when(jnp.logical_and(i == nu, even))(
        lambda: drain(obuf0, obuf1, s0, s1))
    pl.when(jnp.logical_and(i == nu, odd))(
        lambda: drain(obuf1, obuf0, s1, s0))


def _expert_ffn(tea, teb, n_used, gsrc, sdst, wa_tok, wb_tok, h2, wg, wu, wd,
                layer):
    n_tok = h2.shape[0] // SLAB
    max_tiles = gsrc.shape[0] // FFN_TILE
    wa = lambda a, b: pl.BlockSpec(
        (None, None, a, b), lambda i, ta, tb, *_: (layer, ta[i], 0, 0))
    wb = lambda a, b: pl.BlockSpec(
        (None, None, a, b), lambda i, ta, tb, *_: (layer, tb[i], 0, 0))
    anyspec = pl.BlockSpec(memory_space=pl.ANY)
    return pl.pallas_call(
        _ffn_kernel,
        out_shape=jax.ShapeDtypeStruct(((n_tok + FFN_TILE) * SLAB, LANES), F32),
        grid_spec=pltpu.PrefetchScalarGridSpec(
            num_scalar_prefetch=7,
            grid=(max_tiles + 1,),
            in_specs=[anyspec,
                      wa(D_MODEL, D_EXPERT), wa(D_MODEL, D_EXPERT),
                      wa(D_EXPERT, D_MODEL),
                      wb(D_MODEL, D_EXPERT), wb(D_MODEL, D_EXPERT),
                      wb(D_EXPERT, D_MODEL)],
            out_specs=anyspec,
            scratch_shapes=[pltpu.VMEM((FFN_TILE * SLAB, LANES), F32)] * 4 + [
                            pltpu.VMEM((FFN_TILE, LANES), F32),
                            pltpu.VMEM((FFN_TILE, LANES), F32),
                            pltpu.SemaphoreType.DMA((2,)),
                            pltpu.SemaphoreType.DMA((2,))],
        ),
        compiler_params=_cparams(("arbitrary",)),
        name="moe_ffn",
    )(tea, teb, n_used, gsrc, sdst, wa_tok, wb_tok, h2, wg, wu, wd, wg, wu, wd)


def _moe(h2, ri, rw, cnt, wg_bf, wu_bf, wd_bf, layer):
    n = h2.shape[0] // SLAB
    max_tiles = n // FFN_TILE + N_PAIRS
    n_slots = max_tiles * FFN_TILE
    counts = cnt[:N_PAIRS, 0].astype(I32)
    padded = ((counts + FFN_TILE - 1) // FFN_TILE) * FFN_TILE
    ends = jnp.cumsum(padded)
    offs = ends - padded
    q, rank = ri[0], ri[1]
    cls = jnp.arange(N_PAIRS, dtype=I32)
    slot = rank + jnp.sum(jnp.where(q[None, :] == cls[:, None], offs[:, None], 0),
                          axis=0)
    seg = jnp.stack([offs + counts, ends], axis=1).reshape(2 * N_PAIRS)
    n_used = ends[-1] // FFN_TILE
    tile = jnp.arange(max_tiles + 1, dtype=I32)
    tq = jnp.sum((tile[:, None] * FFN_TILE >= ends[None, :]).astype(I32), axis=1)
    tq_last = jnp.sum(((n_used - 1) * FFN_TILE >= ends).astype(I32))
    tq = jnp.where(tile < n_used, tq, tq_last)
    pa = jnp.array([a for a, _ in PAIRS], I32)
    pb = jnp.array([b for _, b in PAIRS], I32)
    grp, pidx = tq // len(PAIRS), tq % len(PAIRS)
    tea = EXPERTS_PER_GROUP * grp + jnp.take(pa, pidx)
    teb = EXPERTS_PER_GROUP * grp + jnp.take(pb, pidx)
    gsrc, sdst = _invert(slot, seg, n_slots)
    return _expert_ffn(tea, teb, n_used.reshape(1), gsrc, sdst, rw[0], rw[1], h2,
                       wg_bf, wu_bf, wd_bf, layer)


def _moe_out_tm_kernel(x1_ref, y_ref, g2_ref, op_ref, os_ref, scr, *, n_p_tiles):
    i = pl.program_id(0)
    y = _rows_from_slabs(y_ref, x1_ref.shape[0])
    x2 = x1_ref[...] + _per_row(y, g2_ref[...], lambda a, b: a * b)
    pl.when(i < n_p_tiles)(
        lambda: _from_time_major(x2, op_ref, scr, op_ref.shape[0]))
    pl.when(i >= n_p_tiles)(
        lambda: _from_time_major(x2, os_ref, scr, os_ref.shape[0]))


def _moe_out_tm(x1, y, modpat, shape_p, shape_s):
    n = x1.shape[0]
    tm = TOKEN_TILE
    n_p_tiles = shape_p[0] * shape_p[1] // tm
    p = modpat.shape[1]
    rowblk = pl.BlockSpec((tm, D_MODEL), lambda i: (i, 0))
    stream = _stream_specs(jax.ShapeDtypeStruct(shape_p, F32),
                           jax.ShapeDtypeStruct(shape_s, F32), n_p_tiles)
    return pl.pallas_call(
        functools.partial(_moe_out_tm_kernel, n_p_tiles=n_p_tiles),
        out_shape=[jax.ShapeDtypeStruct(shape_p, F32),
                   jax.ShapeDtypeStruct(shape_s, F32)],
        grid=(n // tm,),
        in_specs=[rowblk,
                  pl.BlockSpec((tm * SLAB, LANES), lambda i: (i, 0)),
                  pl.BlockSpec((None, p, D_MODEL),
                               lambda i: (jnp.where(i < n_p_tiles, 0, 1), 0, 5))],
        out_specs=stream,
        scratch_shapes=[pltpu.VMEM((LANE_TILES, tm, LANES), F32)],
        compiler_params=_cparams(("arbitrary",)),
        name="moe_out_tm",
    )(x1, y, modpat)


def _moe_out_final_kernel(x1_ref, y_ref, g2_ref, nfin_ref, op_ref, os_ref,
                          *, n_p_tiles):
    i = pl.program_id(0)
    y = _rows_from_slabs(y_ref, x1_ref.shape[0])
    x2 = x1_ref[...] + _per_row(y, g2_ref[...], lambda a, b: a * b)
    out = _rms(x2) * nfin_ref[...]

    @pl.when(i < n_p_tiles)
    def _():
        op_ref[...] = out

    @pl.when(i >= n_p_tiles)
    def _():
        os_ref[...] = out


def _moe_out_final(x1, y, modpat, norm_final, n_p, rows_per_request):
    n = x1.shape[0]
    tm = TOKEN_TILE
    n_p_tiles = n_p // tm
    p = modpat.shape[1]
    rowblk = pl.BlockSpec((tm, D_MODEL), lambda i: (i, 0))
    pat = lambda i: jnp.where(
        i < n_p_tiles, 0, 1 + (i - n_p_tiles) // (rows_per_request // tm))
    return pl.pallas_call(
        functools.partial(_moe_out_final_kernel, n_p_tiles=n_p_tiles),
        out_shape=[jax.ShapeDtypeStruct((n_p, D_MODEL), F32),
                   jax.ShapeDtypeStruct((n - n_p, D_MODEL), F32)],
        grid=(n // tm,),
        in_specs=[rowblk,
                  pl.BlockSpec((tm * SLAB, LANES), lambda i: (i, 0)),
                  pl.BlockSpec((None, p, D_MODEL), lambda i: (pat(i), 0, 5)),
                  pl.BlockSpec((1, D_MODEL), lambda i: (0, 0))],
        out_specs=[
            pl.BlockSpec((tm, D_MODEL),
                         lambda i: (jnp.minimum(i, n_p_tiles - 1), 0)),
            pl.BlockSpec((tm, D_MODEL),
                         lambda i: (jnp.maximum(i - n_p_tiles, 0), 0))],
        compiler_params=_cparams(("arbitrary",)),
        name="moe_out_final",
    )(x1, y, modpat, norm_final)


def _block_diag_weights(bb_re, bb_im, c_re, c_im):
    lt = LANE_TILES
    gl = SSM_GROUPS // lt
    eye = jnp.eye(gl, dtype=F32)

    def in_map(bb):
        b5 = bb.reshape(2, lt, gl, STATE_DIM, SSM_GROUP)
        t = jnp.einsum("dinph,kn->dikhnp", b5, eye)
        return t.reshape(2, lt, gl * SSM_GROUP, gl * STATE_DIM)

    def out_map(cc):
        c5 = cc.reshape(2, lt, gl, SSM_GROUP, STATE_DIM)
        t = jnp.einsum("dikhp,kn->dikpnh", c5, eye)
        return t.reshape(2, lt, gl * STATE_DIM, gl * SSM_GROUP)

    bt = jnp.concatenate([in_map(bb_re), in_map(bb_im)], axis=-1).astype(BF16)
    return bt, out_map(c_re).astype(BF16), out_map(c_im).astype(BF16)


def kernel(x_prompt, x_sample, c, state_ssm_re, state_ssm_im, c_ctx, norm_mix, norm_ffn, w_ada, b_ada, ssm_w_in, ssm_lam_re, ssm_lam_im, ssm_log_dt, ssm_b_re, ssm_b_im, ssm_c_re, ssm_c_im, ssm_d, ssm_w_glu, fnet_w_out, router_w, router_b, moe_w_gate, moe_w_up, moe_w_down, norm_final):
    bp, lp, _ = x_prompt.shape
    bs, ls, _ = x_sample.shape
    n_p = bp * lp
    n_s = bs * ls
    n = n_p + n_s
    tm = TOKEN_TILE

    cond = jnp.zeros((N_COND, D_MODEL), F32).at[0].set(c_ctx).at[1:1 + bs].set(c)
    modtab = _ada_table(cond, w_ada, b_ada).reshape(DEPTH, N_COND, N_MOD, D_MODEL)

    rwt = router_w.T
    rb = router_b.reshape(N_EXPERTS, 1)
    row = lambda v: v.reshape(1, D_MODEL)

    period = max(bp, bs)
    pat_tm = jnp.stack([
        jnp.broadcast_to(modtab[0, 0], (period, N_MOD, D_MODEL)),
        jnp.tile(modtab[0, 1:1 + bs], (period // bs, 1, 1))])
    pat_tm = pat_tm.reshape(2, period, N_MOD * D_MODEL)

    a_re, a_im, bb_re, bb_im = _zoh(ssm_lam_re[0], ssm_lam_im[0], ssm_log_dt[0],
                                    ssm_b_re[0], ssm_b_im[0])
    bt, ct_re, ct_im = _block_diag_weights(bb_re, bb_im, ssm_c_re[0], ssm_c_im[0])
    a_re = a_re.reshape(2, 1, N_STATE)
    a_im = a_im.reshape(2, 1, N_STATE)

    u = _ssm_in(x_prompt, x_sample, row(norm_mix[0]), pat_tm,
                ssm_w_in[0].astype(BF16))
    h0s_re = state_ssm_re[:, 0].reshape(bs, 2, N_STATE).transpose(1, 0, 2)
    h0s_im = state_ssm_im[:, 0].reshape(bs, 2, N_STATE).transpose(1, 0, 2)
    y_scan, fin_re, fin_im = _ssm_scan(u, bt, ct_re, ct_im, a_re, a_im,
                                       h0s_re, h0s_im, bp=bp, lp=lp, bs=bs, ls=ls)
    tri = jnp.triu(jnp.ones((tm, tm), BF16), k=1)
    x1, h2, ri, rw, cnt = _ssm_out(u, y_scan, row(ssm_d[0]),
                               ssm_w_glu[0].astype(BF16), x_prompt, x_sample,
                               pat_tm, row(norm_ffn[0]), rwt, rb, tri)
    wg_bf, wu_bf, wd_bf = (w.astype(BF16) for w in (moe_w_gate, moe_w_up,
                                                    moe_w_down))
    y_moe = _moe(h2, ri, rw, cnt, wg_bf, wu_bf, wd_bf, 0)
    x2_p, x2_s = _moe_out_tm(x1, y_moe, pat_tm, x_prompt.shape, x_sample.shape)

    pat_bm = jnp.broadcast_to(modtab[1][:1 + bs, None],
                              (1 + bs, SUBLANES, N_MOD, D_MODEL))
    pat_bm = pat_bm.reshape(1 + bs, SUBLANES, N_MOD * D_MODEL)
    x3, h2, ri, rw, cnt = _fnet(x2_p, x2_s, row(norm_mix[1]), pat_bm,
                            fnet_w_out[0].astype(BF16), row(norm_ffn[1]), rwt, rb)
    y_moe = _moe(h2, ri, rw, cnt, wg_bf, wu_bf, wd_bf, 1)
    y_p, y_s = _moe_out_final(x3, y_moe, pat_bm, row(norm_final), n_p, ls)

    st = lambda f: f.transpose(1, 0, 2).reshape(bp, 1, 2, SSM_GROUPS, STATE_DIM)
    return (y_p.reshape(bp, lp, D_MODEL), y_s.reshape(bs, ls, D_MODEL),
            st(fin_re), st(fin_im))
```

```python
import functools

import jax
import jax.numpy as jnp
from jax import lax
from jax.experimental import pallas as pl
from jax.experimental.pallas import tpu as pltpu

F32 = jnp.float32
BF16 = jnp.bfloat16
I32 = jnp.int32
HIGHEST = lax.Precision.HIGHEST

D_MODEL = 1024
DEPTH = 2
SSM_GROUP = 16
SSM_GROUPS = 64
STATE_DIM = 64
N_STATE = SSM_GROUPS * STATE_DIM
FNET_GROUP = 128
N_EXPERTS = 16
N_EXPERT_GROUPS = 4
EXPERTS_PER_GROUP = 4
D_EXPERT = 1024
N_MOD = 6
EPS = 1e-6

LANES = 128
SUBLANES = 8
LANE_TILES = D_MODEL // LANES
VMEM_LIMIT = 56 * 1024 * 1024

TOKEN_TILE = 512
FNET_TILE = 256
FFN_TILE = 256
N_COND = 16

PAIRS = ((0, 1), (0, 2), (0, 3), (1, 3), (1, 2), (3, 2))
N_PAIRS = N_EXPERT_GROUPS * len(PAIRS)
N_CLASS_ROWS = 32
SLAB = SUBLANES


def _cparams(sem, vmem=VMEM_LIMIT):
    return pltpu.CompilerParams(dimension_semantics=sem, vmem_limit_bytes=vmem)


def _sigmoid(x):
    return 1.0 / (1.0 + jnp.exp(-x))


def _gelu_tanh(x):
    c = 0.7978845608028654
    return x * (0.5 * (1.0 + jnp.tanh(c * (x + 0.044715 * (x * x * x)))))


def _per_row(v, pat, fn):
    tm, d = v.shape
    p = pat.shape[0]
    return fn(v.reshape(tm // p, p, d), pat[None]).reshape(tm, d)


def _rms(x):
    ms = jnp.mean(x * x, axis=-1, keepdims=True)
    return x * lax.rsqrt(ms + EPS)


def _norm_mod(x, g, sc, sh):
    y = _rms(x) * g
    y = _per_row(y, sc, lambda a, b: a * (1.0 + b))
    return _per_row(y, sh, lambda a, b: a + b)


def _lane_tiles(v):
    return [v[:, LANES * k:LANES * (k + 1)] for k in range(v.shape[1] // LANES)]


def _to_time_major(x_ref, scr, batch):
    tt = x_ref.shape[1]
    for b in range(batch):
        for k, piece in enumerate(_lane_tiles(x_ref[b])):
            scr[k, pl.ds(b, tt, stride=batch), :] = piece
    return jnp.concatenate([scr[k] for k in range(LANE_TILES)], axis=1)


def _from_time_major(v, o_ref, scr, batch):
    tt = v.shape[0] // batch
    for k, piece in enumerate(_lane_tiles(v)):
        scr[k] = piece
    for b in range(batch):
        o_ref[b] = jnp.concatenate(
            [scr[k, pl.ds(b, tt, stride=batch), :] for k in range(LANE_TILES)],
            axis=1)


def _slab_rows(ref, k, n):
    return ref[pl.ds(k, n, stride=SLAB), :]


def _store_slabs(v, ref):
    n = v.shape[0]
    for k, piece in enumerate(_lane_tiles(v)):
        ref[pl.ds(k, n, stride=SLAB), :] = piece


def _rows_from_slabs(y_ref, n):
    return jnp.concatenate([_slab_rows(y_ref, k, n) for k in range(SLAB)], axis=1)


def _route(logits_t, rb, tri, carry):
    ne, tm = logits_t.shape
    s = _sigmoid(logits_t)
    bz = s + rb
    row = lambda a, r: a[r:r + 1, :]
    gs = []
    for g in range(N_EXPERT_GROUPS):
        v0, v1, v2, v3 = (row(bz, EXPERTS_PER_GROUP * g + j) for j in range(4))
        hi1, lo1 = jnp.maximum(v0, v1), jnp.minimum(v0, v1)
        hi2, lo2 = jnp.maximum(v2, v3), jnp.minimum(v2, v3)
        top1 = jnp.maximum(hi1, hi2)
        top2 = jnp.maximum(jnp.minimum(hi1, hi2), jnp.maximum(lo1, lo2))
        gs.append(top1 + top2)
    bg = jnp.zeros((1, tm), I32)
    bv = gs[0]
    for g in range(1, N_EXPERT_GROUPS):
        upd = gs[g] > bv
        bg = jnp.where(upd, g, bg)
        bv = jnp.where(upd, gs[g], bv)
    cb, cs = [], []
    for j in range(EXPERTS_PER_GROUP):
        vb, vs = row(bz, j), row(s, j)
        for g in range(1, N_EXPERT_GROUPS):
            sel = bg == g
            vb = jnp.where(sel, row(bz, EXPERTS_PER_GROUP * g + j), vb)
            vs = jnp.where(sel, row(s, EXPERTS_PER_GROUP * g + j), vs)
        cb.append(vb)
        cs.append(vs)
    i1 = jnp.zeros((1, tm), I32)
    b1, s1 = cb[0], cs[0]
    for j in range(1, EXPERTS_PER_GROUP):
        upd = cb[j] > b1
        i1 = jnp.where(upd, j, i1)
        b1 = jnp.where(upd, cb[j], b1)
        s1 = jnp.where(upd, cs[j], s1)
    i2 = jnp.zeros((1, tm), I32)
    b2 = jnp.full((1, tm), -jnp.inf, F32)
    s2 = jnp.zeros((1, tm), F32)
    for j in range(EXPERTS_PER_GROUP):
        cand = jnp.where(i1 == j, -jnp.inf, cb[j])
        upd = cand > b2
        i2 = jnp.where(upd, j, i2)
        b2 = jnp.where(upd, cand, b2)
        s2 = jnp.where(upd, cs[j], s2)
    den = s1 + s2
    w1 = s1 / den
    w2 = s2 / den
    lo = jnp.minimum(i1, i2)
    hi = jnp.maximum(i1, i2)
    pidx = jnp.where(lo == 0, hi - 1,
                     jnp.where(lo == 1, jnp.where(hi == 3, 3, 4), 5))
    first = jnp.where(pidx < 3, 0, jnp.where(pidx < 5, 1, 3))
    wa = jnp.where(i1 == first, w1, w2)
    wb = jnp.where(i1 == first, w2, w1)
    q = len(PAIRS) * bg + pidx
    qio = lax.broadcasted_iota(I32, (N_CLASS_ROWS, tm), 0)
    oh = qio == q
    ohf = jnp.where(oh, 1.0, 0.0)
    cum = jnp.dot(ohf.astype(BF16), tri, preferred_element_type=F32) + carry
    rank = jnp.sum(jnp.where(oh, cum, 0.0), axis=0, keepdims=True)
    new_carry = carry + jnp.sum(ohf, axis=1, keepdims=True)
    return q, rank.astype(I32), wa, wb, new_carry


def _post_mixer(m, x, g1, nf, sc2, sh2, rwt_ref, rb_ref, tri_ref,
                x1_ref, h2_ref, ri_ref, rw_ref, cnt_ref, carry):
    tm = x.shape[0]
    x1 = x + _per_row(m, g1, lambda a, b: a * b)
    x1_ref[...] = x1
    h2 = _norm_mod(x1, nf, sc2, sh2)
    logits_t = lax.dot_general(rwt_ref[...], h2, (((1,), (1,)), ((), ())),
                               precision=HIGHEST, preferred_element_type=F32)
    q, rank, wa, wb, nc = _route(logits_t, rb_ref[...], tri_ref[...],
                                 carry[:, 0:1])
    _store_slabs(h2, h2_ref)
    rw_ref[0:1, :] = wa
    rw_ref[1:2, :] = wb
    rw_ref[2:8, :] = jnp.zeros((6, tm), F32)
    ri_ref[0:1, :] = q
    ri_ref[1:2, :] = rank
    ri_ref[2:8, :] = jnp.zeros((6, tm), I32)
    carry[...] = jnp.broadcast_to(nc, carry.shape)
    cnt_ref[...] = carry[...]


def _ada_kernel(cond_ref, w_ref, b_ref, o_ref):
    c = cond_ref[...]
    s = c * _sigmoid(c)
    o_ref[...] = jnp.dot(s, w_ref[...], precision=HIGHEST,
                         preferred_element_type=F32) + b_ref[...]


def _ada_table(cond, w_ada, b_ada):
    tn = 1536
    n_out = N_MOD * D_MODEL
    return pl.pallas_call(
        _ada_kernel,
        out_shape=jax.ShapeDtypeStruct((DEPTH, N_COND, n_out), F32),
        grid=(DEPTH, n_out // tn),
        in_specs=[
            pl.BlockSpec((N_COND, D_MODEL), lambda l, j: (0, 0)),
            pl.BlockSpec((None, D_MODEL, tn), lambda l, j: (l, 0, j)),
            pl.BlockSpec((None, 1, tn), lambda l, j: (l, 0, j)),
        ],
        out_specs=pl.BlockSpec((None, N_COND, tn), lambda l, j: (l, 0, j)),
        compiler_params=_cparams(("arbitrary", "arbitrary")),
        name="ada_table",
    )(cond, w_ada, b_ada.reshape(DEPTH, 1, n_out))


def _zoh_kernel(lr_ref, li_ref, ldt_ref, br_ref, bi_ref,
                are_ref, aim_ref, bbre_ref, bbim_ref):
    lr = lr_ref[...]
    li = li_ref[...]
    dt = jnp.exp(ldt_ref[...])
    mag = jnp.exp(lr * dt)
    a_re = mag * jnp.cos(li * dt)
    a_im = mag * jnp.sin(li * dt)
    den = lr * lr + li * li
    nr = a_re - 1.0
    f_re = (nr * lr + a_im * li) / den
    f_im = (a_im * lr - nr * li) / den
    br = br_ref[...]
    bi = bi_ref[...]
    are_ref[...] = a_re
    aim_ref[...] = a_im
    bbre_ref[...] = f_re * br - f_im * bi
    bbim_ref[...] = f_re * bi + f_im * br


def _zoh(lam_re, lam_im, log_dt, b_re, b_im):
    shape = b_re.shape
    flat = (shape[0] * shape[1] * shape[2] * shape[3] // LANES, LANES)
    bc = lambda a: jnp.broadcast_to(a, shape).reshape(flat)
    args = (bc(lam_re[..., None]), bc(lam_im[..., None]),
            bc(log_dt[:, :, None, None]), b_re.reshape(flat), b_im.reshape(flat))
    outs = pl.pallas_call(
        _zoh_kernel,
        out_shape=[jax.ShapeDtypeStruct(flat, F32)] * 4,
        name="zoh_discretize",
    )(*args)
    a_re, a_im, bb_re, bb_im = (o.reshape(shape) for o in outs)
    return a_re[..., 0], a_im[..., 0], bb_re, bb_im


def _stream_specs(xp, xs, n_p_tiles):
    bp, bs = xp.shape[0], xs.shape[0]
    return [
        pl.BlockSpec((bp, TOKEN_TILE // bp, D_MODEL),
                     lambda i: (0, jnp.minimum(i, n_p_tiles - 1), 0)),
        pl.BlockSpec((bs, TOKEN_TILE // bs, D_MODEL),
                     lambda i: (0, jnp.maximum(i - n_p_tiles, 0), 0)),
    ]


def _ssm_in_kernel(xp_ref, xs_ref, g_ref, sc_ref, sh_ref, w_ref, u_ref, scr,
                   *, n_p_tiles):
    def run(x_ref):
        x = _to_time_major(x_ref, scr, x_ref.shape[0])
        h = _norm_mod(x, g_ref[...], sc_ref[...], sh_ref[...])
        u_ref[...] = jnp.dot(h.astype(BF16), w_ref[...],
                             preferred_element_type=F32)

    i = pl.program_id(0)
    pl.when(i < n_p_tiles)(lambda: run(xp_ref))
    pl.when(i >= n_p_tiles)(lambda: run(xs_ref))


def _ssm_in(xp, xs, g, modpat, w_in_bf):
    n = (xp.shape[0] * xp.shape[1] + xs.shape[0] * xs.shape[1])
    tm = TOKEN_TILE
    n_p_tiles = xp.shape[0] * xp.shape[1] // tm
    p = modpat.shape[1]
    pat = lambda i: jnp.where(i < n_p_tiles, 0, 1)
    mod = lambda k: pl.BlockSpec((None, p, D_MODEL), lambda i: (pat(i), 0, k))
    return pl.pallas_call(
        functools.partial(_ssm_in_kernel, n_p_tiles=n_p_tiles),
        out_shape=jax.ShapeDtypeStruct((n, D_MODEL), F32),
        grid=(n // tm,),
        in_specs=_stream_specs(xp, xs, n_p_tiles) + [
            pl.BlockSpec((1, D_MODEL), lambda i: (0, 0)),
            mod(1), mod(0),
            pl.BlockSpec((D_MODEL, D_MODEL), lambda i: (0, 0)),
        ],
        out_specs=pl.BlockSpec((tm, D_MODEL), lambda i: (i, 0)),
        scratch_shapes=[pltpu.VMEM((LANE_TILES, tm, LANES), F32)],
        compiler_params=_cparams(("arbitrary",)),
        name="ssm_in",
    )(xp, xs, g, modpat, modpat, w_in_bf)


def _scan_kernel(u_ref, bt_ref, cre_ref, cim_ref, are_ref, aim_ref,
                 h0re_ref, h0im_ref, y_ref, fre_ref, fim_ref,
                 *scratch, n_p_chunks, bp, bs):
    xre = scratch[:LANE_TILES]
    xim = scratch[LANE_TILES:2 * LANE_TILES]
    hre, him = scratch[2 * LANE_TILES:]
    d = pl.program_id(0)
    c = pl.program_id(1)
    cols = N_STATE // LANE_TILES
    rows = u_ref.shape[0]

    def chunk(batch, reverse):
        steps = rows // batch

        def b_proj(s):
            u = u_ref[:, LANES * s:LANES * (s + 1)].astype(BF16)
            xt = jnp.dot(u, bt_ref[s], preferred_element_type=F32)
            xre[s][...] = xt[:, :cols]
            xim[s][...] = xt[:, cols:]

        def scan(s):
            sl = slice(cols * s, cols * (s + 1))
            ar = jnp.broadcast_to(are_ref[:, sl], (batch, cols))
            ai = jnp.broadcast_to(aim_ref[:, sl], (batch, cols))
            hr = hre[0:batch, sl]
            hi = him[0:batch, sl]
            for t in range(steps):
                tt = steps - 1 - t if reverse else t
                r = slice(tt * batch, (tt + 1) * batch)
                nr = ar * hr - ai * hi + xre[s][r, :]
                ni = ar * hi + ai * hr + xim[s][r, :]
                xre[s][r, :] = nr
                xim[s][r, :] = ni
                hr, hi = nr, ni
            hre[0:batch, sl] = hr
            him[0:batch, sl] = hi

        def c_proj(s):
            y_ref[:, LANES * s:LANES * (s + 1)] = (
                jnp.dot(xre[s][...].astype(BF16), cre_ref[s],
                        preferred_element_type=F32)
                - jnp.dot(xim[s][...].astype(BF16), cim_ref[s],
                          preferred_element_type=F32))

        for stage in range(LANE_TILES + 2):
            if stage < LANE_TILES:
                b_proj(stage)
            if 0 <= stage - 1 < LANE_TILES:
                scan(stage - 1)
            if 0 <= stage - 2 < LANE_TILES:
                c_proj(stage - 2)

    @pl.when(c == 0)
    def _():
        hre[...] = jnp.zeros(hre.shape, F32)
        him[...] = jnp.zeros(him.shape, F32)

    @pl.when(c == n_p_chunks)
    def _():
        hre[0:bs, :] = h0re_ref[...]
        him[0:bs, :] = h0im_ref[...]

    in_p = c < n_p_chunks
    in_s = jnp.logical_not(in_p)
    fwd = d == 0
    bwd = jnp.logical_not(fwd)
    pl.when(jnp.logical_and(in_p, fwd))(lambda: chunk(bp, False))
    pl.when(jnp.logical_and(in_p, bwd))(lambda: chunk(bp, True))
    pl.when(jnp.logical_and(in_s, fwd))(lambda: chunk(bs, False))
    pl.when(jnp.logical_and(in_s, bwd))(lambda: chunk(bs, True))

    @pl.when(c == n_p_chunks - 1)
    def _():
        fre_ref[...] = hre[0:bp, :]
        fim_ref[...] = him[0:bp, :]


def _ssm_scan(u, bt, ct_re, ct_im, a_re, a_im, h0_re, h0_im, *, bp, lp, bs, ls):
    n = u.shape[0]
    rows = TOKEN_TILE
    n_p_chunks = bp * lp // rows
    n_s_chunks = bs * ls // rows

    def chunk(d, c):
        in_p = jnp.where(d == 0, c, n_p_chunks - 1 - c)
        cs = c - n_p_chunks
        in_s = n_p_chunks + jnp.where(d == 0, cs, n_s_chunks - 1 - cs)
        return jnp.where(c < n_p_chunks, in_p, in_s)

    cols = N_STATE // LANE_TILES
    dspec = lambda shape: pl.BlockSpec((None,) + shape,
                                       lambda d, c: (d,) + (0,) * len(shape))
    kern = functools.partial(_scan_kernel, n_p_chunks=n_p_chunks, bp=bp, bs=bs)
    return pl.pallas_call(
        kern,
        out_shape=[jax.ShapeDtypeStruct((2, n, D_MODEL), F32),
                   jax.ShapeDtypeStruct((2, bp, N_STATE), F32),
                   jax.ShapeDtypeStruct((2, bp, N_STATE), F32)],
        grid=(2, n_p_chunks + n_s_chunks),
        in_specs=[
            pl.BlockSpec((rows, D_MODEL), lambda d, c: (chunk(d, c), 0)),
            dspec((LANE_TILES, LANES, 2 * cols)),
            dspec((LANE_TILES, cols, LANES)),
            dspec((LANE_TILES, cols, LANES)),
            dspec((1, N_STATE)),
            dspec((1, N_STATE)),
            dspec((bs, N_STATE)),
            dspec((bs, N_STATE)),
        ],
        out_specs=[
            pl.BlockSpec((None, rows, D_MODEL), lambda d, c: (d, chunk(d, c), 0)),
            dspec((bp, N_STATE)),
            dspec((bp, N_STATE)),
        ],
        scratch_shapes=(
            [pltpu.VMEM((rows, cols), F32)] * (2 * LANE_TILES)
            + [pltpu.VMEM((max(bp, bs), N_STATE), F32)] * 2),
        compiler_params=_cparams(("arbitrary", "arbitrary")),
        name="ssm_scan",
    )(u, bt, ct_re, ct_im, a_re, a_im, h0_re, h0_im)


def _ssm_out_kernel(u_ref, y0_ref, y1_ref, dsk_ref, wglu_ref, xp_ref, xs_ref,
                    g1_ref, nf_ref, sc2_ref, sh2_ref, rwt_ref, rb_ref, tri_ref,
                    x1_ref, h2_ref, ri_ref, rw_ref, cnt_ref, scr, carry,
                    *, n_p_tiles):
    i = pl.program_id(0)

    @pl.when(i == 0)
    def _():
        carry[...] = jnp.zeros(carry.shape, F32)

    def run(x_ref):
        x = _to_time_major(x_ref, scr, x_ref.shape[0])
        y = u_ref[...] * dsk_ref[...] + y0_ref[...] + y1_ref[...]
        ge = _gelu_tanh(y).astype(BF16)
        vg = jnp.dot(ge, wglu_ref[...], preferred_element_type=F32)
        m = vg[:, :D_MODEL] * _sigmoid(vg[:, D_MODEL:])
        _post_mixer(m, x, g1_ref[...], nf_ref[...], sc2_ref[...], sh2_ref[...],
                    rwt_ref, rb_ref, tri_ref, x1_ref, h2_ref, ri_ref, rw_ref,
                    cnt_ref, carry)

    pl.when(i < n_p_tiles)(lambda: run(xp_ref))
    pl.when(i >= n_p_tiles)(lambda: run(xs_ref))


def _route_out_shapes(n):
    return [jax.ShapeDtypeStruct((n, D_MODEL), F32),
            jax.ShapeDtypeStruct((n * SLAB, LANES), F32),
            jax.ShapeDtypeStruct((SUBLANES, n), I32),
            jax.ShapeDtypeStruct((SUBLANES, n), F32),
            jax.ShapeDtypeStruct((N_CLASS_ROWS, LANES), F32)]


def _ssm_out(u, y, d_skip, w_glu_bf, xp, xs, modpat, norm_ffn, rwt, rb, tri):
    n = u.shape[0]
    tm = TOKEN_TILE
    n_p_tiles = xp.shape[0] * xp.shape[1] // tm
    p = modpat.shape[1]
    pat = lambda i: jnp.where(i < n_p_tiles, 0, 1)
    mod = lambda k: pl.BlockSpec((None, p, D_MODEL), lambda i: (pat(i), 0, k))
    full = lambda shape: pl.BlockSpec(shape, lambda i: (0,) * len(shape))
    rowblk = pl.BlockSpec((tm, D_MODEL), lambda i: (i, 0))
    return pl.pallas_call(
        functools.partial(_ssm_out_kernel, n_p_tiles=n_p_tiles),
        out_shape=_route_out_shapes(n),
        grid=(n // tm,),
        in_specs=[
            rowblk,
            pl.BlockSpec((None, tm, D_MODEL), lambda i: (0, i, 0)),
            pl.BlockSpec((None, tm, D_MODEL), lambda i: (1, i, 0)),
            full((1, D_MODEL)),
            full((D_MODEL, 2 * D_MODEL)),
        ] + _stream_specs(xp, xs, n_p_tiles) + [
            mod(2), full((1, D_MODEL)), mod(4), mod(3),
            full((N_EXPERTS, D_MODEL)), full((N_EXPERTS, 1)), full((tm, tm)),
        ],
        out_specs=[rowblk,
                   pl.BlockSpec((tm * SLAB, LANES), lambda i: (i, 0)),
                   pl.BlockSpec((SUBLANES, tm), lambda i: (0, i)),
                   pl.BlockSpec((SUBLANES, tm), lambda i: (0, i)),
                   full((N_CLASS_ROWS, LANES))],
        scratch_shapes=[pltpu.VMEM((LANE_TILES, tm, LANES), F32),
                        pltpu.VMEM((N_CLASS_ROWS, LANES), F32)],
        compiler_params=_cparams(("arbitrary",)),
        name="ssm_out",
    )(u, y, y, d_skip, w_glu_bf, xp, xs, modpat, norm_ffn, modpat, modpat,
      rwt, rb, tri)


def _fnet_kernel(xp_ref, xs_ref, nm_ref, sc1_ref, sh1_ref, cs_ref, dftp_ref,
                 dfts_ref, wout_ref, g1_ref, nf_ref, sc2_ref, sh2_ref, rwt_ref,
                 rb_ref, tri_ref, x1_ref, h2_ref, ri_ref, rw_ref, cnt_ref, ucs,
                 carry, *, bp, tiles_s):
    s = pl.program_id(0)
    n_groups = D_MODEL // FNET_GROUP
    tr = FNET_TILE

    @pl.when(s == 0)
    def _():
        carry[...] = jnp.zeros(carry.shape, F32)

    def run(x_ref, dft_ref, i):
        seq = x_ref.shape[0]
        scale = float((seq * FNET_GROUP) ** -0.5)

        @pl.when(i == 0)
        def _():
            def stage1(r, _):
                rows = pl.ds(pl.multiple_of(r * tr, tr), tr)
                h = _norm_mod(x_ref[rows, :], nm_ref[...], sc1_ref[...],
                              sh1_ref[...]).astype(BF16)
                for k in range(n_groups):
                    cols = slice(FNET_GROUP * k, FNET_GROUP * (k + 1))
                    t = jnp.dot(h[:, cols], cs_ref[...],
                                preferred_element_type=F32)
                    ucs[rows, cols] = t[:, :FNET_GROUP].astype(BF16)
                    ucs[pl.ds(pl.multiple_of(seq + r * tr, tr), tr), cols] = (
                        t[:, FNET_GROUP:].astype(BF16))
                return 0
            lax.fori_loop(0, seq // tr, stage1, 0)

        y = jnp.dot(dft_ref[...], ucs[0:2 * seq, :],
                    preferred_element_type=F32) * scale
        m = jnp.dot(y.astype(BF16), wout_ref[...], preferred_element_type=F32)
        xrow = x_ref[pl.ds(pl.multiple_of(i * tr, tr), tr), :]
        _post_mixer(m, xrow, g1_ref[...], nf_ref[...], sc2_ref[...],
                    sh2_ref[...], rwt_ref, rb_ref, tri_ref, x1_ref, h2_ref,
                    ri_ref, rw_ref, cnt_ref, carry)

    pl.when(s < bp)(lambda: run(xp_ref, dftp_ref, 0 * s))
    pl.when(s >= bp)(lambda: run(xs_ref, dfts_ref, (s - bp) % tiles_s))


def _dft_table(seq):
    inner = 64
    k = jnp.arange(seq, dtype=I32)[:, None]
    t1 = jnp.arange(seq // inner, dtype=I32)[None, :] * inner
    t2 = jnp.arange(inner, dtype=I32)[None, :]
    ang = lambda t: ((k * t) % seq).astype(F32) * (2.0 * jnp.pi / seq)
    ca, sa = jnp.cos(ang(t1))[:, :, None], jnp.sin(ang(t1))[:, :, None]
    cb, sb = jnp.cos(ang(t2))[:, None, :], jnp.sin(ang(t2))[:, None, :]
    cos = (ca * cb - sa * sb).reshape(seq, seq)
    sin = (sa * cb + ca * sb).reshape(seq, seq)
    return jnp.concatenate([cos, -sin], axis=1).astype(BF16)


def _fnet(xp, xs, norm_mix, modpat, w_out_bf, norm_ffn, rwt, rb):
    bp, lp, _ = xp.shape
    bs, ls, _ = xs.shape
    tr = FNET_TILE
    assert lp == tr
    tiles_s = ls // tr
    n = bp * lp + bs * ls
    p = modpat.shape[1]
    kc = jnp.arange(FNET_GROUP, dtype=I32)
    angc = ((kc[:, None] * kc[None, :]) % FNET_GROUP).astype(F32) * (
        2.0 * jnp.pi / FNET_GROUP)
    cs = jnp.concatenate([jnp.cos(angc), jnp.sin(angc)], axis=1).astype(BF16)
    tri = jnp.triu(jnp.ones((tr, tr), BF16), k=1)

    sb = lambda s: jnp.clip((s - bp) // tiles_s, 0, bs - 1)
    pat = lambda s: jnp.where(s < bp, 0, 1 + sb(s))
    mod = lambda kk: pl.BlockSpec((None, p, D_MODEL), lambda s: (pat(s), 0, kk))
    full = lambda shape: pl.BlockSpec(shape, lambda s: (0,) * len(shape))
    kern = functools.partial(_fnet_kernel, bp=bp, tiles_s=tiles_s)
    return pl.pallas_call(
        kern,
        out_shape=_route_out_shapes(n),
        grid=(bp + bs * tiles_s,),
        in_specs=[
            pl.BlockSpec((None, lp, D_MODEL),
                         lambda s: (jnp.minimum(s, bp - 1), 0, 0)),
            pl.BlockSpec((None, ls, D_MODEL), lambda s: (sb(s), 0, 0)),
            full((1, D_MODEL)), mod(1), mod(0),
            full((FNET_GROUP, 2 * FNET_GROUP)),
            full((tr, 2 * lp)),
            pl.BlockSpec((tr, 2 * ls),
                         lambda s: (jnp.maximum(s - bp, 0) % tiles_s, 0)),
            full((D_MODEL, D_MODEL)),
            mod(2), full((1, D_MODEL)), mod(4), mod(3),
            full((N_EXPERTS, D_MODEL)), full((N_EXPERTS, 1)), full((tr, tr)),
        ],
        out_specs=[pl.BlockSpec((tr, D_MODEL), lambda s: (s, 0)),
                   pl.BlockSpec((tr * SLAB, LANES), lambda s: (s, 0)),
                   pl.BlockSpec((SUBLANES, tr), lambda s: (0, s)),
                   pl.BlockSpec((SUBLANES, tr), lambda s: (0, s)),
                   full((N_CLASS_ROWS, LANES))],
        scratch_shapes=[pltpu.VMEM((2 * ls, D_MODEL), BF16),
                        pltpu.VMEM((N_CLASS_ROWS, LANES), F32)],
        compiler_params=_cparams(("arbitrary",)),
        name="fnet",
    )(xp, xs, norm_mix, modpat, modpat, cs, _dft_table(lp), _dft_table(ls),
      w_out_bf, modpat, norm_ffn, modpat, modpat, rwt, rb, tri)


def _invert_kernel(slot_ref, ends_ref, gsrc_ref, sdst_ref):
    n = slot_ref.shape[0]
    n_slots = gsrc_ref.shape[0]
    t = FFN_TILE
    unroll = 8

    def pad_tile(base):
        def pad(j, _):
            for k in range(unroll):
                r = j * unroll + k
                gsrc_ref[base + r] = 0
                sdst_ref[base + r] = n + r
            return 0
        lax.fori_loop(0, t // unroll, pad, 0)

    for q in range(N_PAIRS):
        pad_tile(jnp.maximum(ends_ref[q] - t, 0))

    def unused(b, _):
        pad_tile(b * t)
        return 0
    lax.fori_loop(ends_ref[N_PAIRS - 1] // t, n_slots // t, unused, 0)

    def body(j, _):
        for k in range(unroll):
            t = j * unroll + k
            s = slot_ref[t]
            gsrc_ref[s] = t
            sdst_ref[s] = t
        return 0
    lax.fori_loop(0, n // unroll, body, 0)


def _invert(slot, ends, n_slots):
    smem = pl.BlockSpec(memory_space=pltpu.SMEM)
    return pl.pallas_call(
        _invert_kernel,
        out_shape=[jax.ShapeDtypeStruct((n_slots,), I32)] * 2,
        in_specs=[smem, smem],
        out_specs=[smem, smem],
        name="moe_invert",
    )(slot, ends)


def _ffn_kernel(tea_ref, teb_ref, nu_ref, gsrc_ref, sdst_ref, wa_ref, wb_ref,
                h_hbm, wga, wua, wda, wgb, wub, wdb, y_hbm,
                xbuf0, xbuf1, obuf0, obuf1, wcola, wcolb, gsem, ssem):
    del tea_ref, teb_ref
    i = pl.program_id(0)
    nu = nu_ref[0]
    t = FFN_TILE

    def slab(ref, r):
        return ref.at[pl.ds(pl.multiple_of(r * SLAB, SLAB), SLAB), :]

    def gather_start(tile, xbuf, sem):
        base = tile * t
        for r in range(t):
            pltpu.make_async_copy(slab(h_hbm, gsrc_ref[base + r]), slab(xbuf, r),
                                  sem).start(priority=r % 2)

    def gather_wait(xbuf, sem):
        pltpu.make_async_copy(h_hbm.at[pl.ds(0, t * SLAB), :], xbuf, sem).wait()

    def scatter_start(tile, obuf, sem):
        base = tile * t
        for r in range(t):
            pltpu.make_async_copy(slab(obuf, r), slab(y_hbm, sdst_ref[base + r]),
                                  sem).start(priority=r % 2)

    def scatter_wait(obuf, sem):
        pltpu.make_async_copy(obuf, y_hbm.at[pl.ds(0, t * SLAB), :], sem).wait()

    def expert(xb, wg, wu, wd):
        g = jnp.dot(xb, wg[...], preferred_element_type=F32)
        u = jnp.dot(xb, wu[...], preferred_element_type=F32)
        a = (g * _sigmoid(g)) * u
        return jnp.dot(a.astype(BF16), wd[...], preferred_element_type=F32)

    def step(xc, xn, oc, op, gc, gn, sc, sp):
        pl.when(i == 0)(lambda: gather_start(0, xc, gc))
        pl.when(i >= 2)(lambda: scatter_wait(oc, sc))
        gather_wait(xc, gc)
        pl.when(i + 1 < nu)(lambda: gather_start(i + 1, xn, gn))
        pl.when(i >= 1)(lambda: scatter_start(i - 1, op, sp))
        for r in range(t):
            tok = gsrc_ref[i * t + r]
            wcola[r:r + 1, :] = jnp.full((1, LANES), wa_ref[tok], F32)
            wcolb[r:r + 1, :] = jnp.full((1, LANES), wb_ref[tok], F32)
        xb = _rows_from_slabs(xc, t).astype(BF16)
        ya = expert(xb, wga, wua, wda)
        yb = expert(xb, wgb, wub, wdb)
        y = wcola[:, 0:1] * ya + wcolb[:, 0:1] * yb
        for k, piece in enumerate(_lane_tiles(y)):
            oc[pl.ds(k, t, stride=SLAB), :] = piece

    def drain(oc, op, sc, sp):
        pl.when(i >= 2)(lambda: scatter_wait(oc, sc))
        scatter_start(i - 1, op, sp)
        scatter_wait(op, sp)
        spare_row = y_hbm.shape[0] - t * SLAB
        oc[...] = jnp.zeros(oc.shape, F32)
        spare = pltpu.make_async_copy(
            oc, y_hbm.at[pl.ds(spare_row, t * SLAB), :], sc)
        spare.start()
        spare.wait()

    even = (i % 2) == 0
    odd = jnp.logical_not(even)
    g0, g1, s0, s1 = gsem.at[0], gsem.at[1], ssem.at[0], ssem.at[1]
    pl.when(jnp.logical_and(i < nu, even))(
        lambda: step(xbuf0, xbuf1, obuf0, obuf1, g0, g1, s0, s1))
    pl.when(jnp.logical_and(i < nu, odd))(
        lambda: step(xbuf1, xbuf0, obuf1, obuf0, g1, g0, s1, s0))
    pl.when(jnp.logical_and(i == nu, even))(
        lambda: drain(obuf0, obuf1, s0, s1))
    pl.when(jnp.logical_and(i == nu, odd))(
        lambda: drain(obuf1, obuf0, s1, s0))


def _expert_ffn(tea, teb, n_used, gsrc, sdst, wa_tok, wb_tok, h2, wg, wu, wd,
                layer):
    n_tok = h2.shape[0] // SLAB
    max_tiles = gsrc.shape[0] // FFN_TILE
    wa = lambda a, b: pl.BlockSpec(
        (None, None, a, b), lambda i, ta, tb, *_: (layer, ta[i], 0, 0))
    wb = lambda a, b: pl.BlockSpec(
        (None, None, a, b), lambda i, ta, tb, *_: (layer, tb[i], 0, 0))
    anyspec = pl.BlockSpec(memory_space=pl.ANY)
    return pl.pallas_call(
        _ffn_kernel,
        out_shape=jax.ShapeDtypeStruct(((n_tok + FFN_TILE) * SLAB, LANES), F32),
        grid_spec=pltpu.PrefetchScalarGridSpec(
            num_scalar_prefetch=7,
            grid=(max_tiles + 1,),
            in_specs=[anyspec,
                      wa(D_MODEL, D_EXPERT), wa(D_MODEL, D_EXPERT),
                      wa(D_EXPERT, D_MODEL),
                      wb(D_MODEL, D_EXPERT), wb(D_MODEL, D_EXPERT),
                      wb(D_EXPERT, D_MODEL)],
            out_specs=anyspec,
            scratch_shapes=[pltpu.VMEM((FFN_TILE * SLAB, LANES), F32)] * 4 + [
                            pltpu.VMEM((FFN_TILE, LANES), F32),
                            pltpu.VMEM((FFN_TILE, LANES), F32),
                            pltpu.SemaphoreType.DMA((2,)),
                            pltpu.SemaphoreType.DMA((2,))],
        ),
        compiler_params=_cparams(("arbitrary",)),
        name="moe_ffn",
    )(tea, teb, n_used, gsrc, sdst, wa_tok, wb_tok, h2, wg, wu, wd, wg, wu, wd)


def _moe(h2, ri, rw, cnt, wg_bf, wu_bf, wd_bf, layer):
    n = h2.shape[0] // SLAB
    max_tiles = n // FFN_TILE + N_PAIRS
    n_slots = max_tiles * FFN_TILE
    counts = cnt[:N_PAIRS, 0].astype(I32)
    padded = ((counts + FFN_TILE - 1) // FFN_TILE) * FFN_TILE
    ends = jnp.cumsum(padded)
    offs = ends - padded
    q, rank = ri[0], ri[1]
    cls = jnp.arange(N_PAIRS, dtype=I32)
    slot = rank + jnp.sum(jnp.where(q[None, :] == cls[:, None], offs[:, None], 0),
                          axis=0)
    n_used = ends[-1] // FFN_TILE
    tile = jnp.arange(max_tiles + 1, dtype=I32)
    tq = jnp.sum((tile[:, None] * FFN_TILE >= ends[None, :]).astype(I32), axis=1)
    tq_last = jnp.sum(((n_used - 1) * FFN_TILE >= ends).astype(I32))
    tq = jnp.where(tile < n_used, tq, tq_last)
    pa = jnp.array([a for a, _ in PAIRS], I32)
    pb = jnp.array([b for _, b in PAIRS], I32)
    grp, pidx = tq // len(PAIRS), tq % len(PAIRS)
    tea = EXPERTS_PER_GROUP * grp + jnp.take(pa, pidx)
    teb = EXPERTS_PER_GROUP * grp + jnp.take(pb, pidx)
    gsrc, sdst = _invert(slot, ends, n_slots)
    return _expert_ffn(tea, teb, n_used.reshape(1), gsrc, sdst, rw[0], rw[1], h2,
                       wg_bf, wu_bf, wd_bf, layer)


def _moe_out_tm_kernel(x1_ref, y_ref, g2_ref, op_ref, os_ref, scr, *, n_p_tiles):
    i = pl.program_id(0)
    y = _rows_from_slabs(y_ref, x1_ref.shape[0])
    x2 = x1_ref[...] + _per_row(y, g2_ref[...], lambda a, b: a * b)
    pl.when(i < n_p_tiles)(
        lambda: _from_time_major(x2, op_ref, scr, op_ref.shape[0]))
    pl.when(i >= n_p_tiles)(
        lambda: _from_time_major(x2, os_ref, scr, os_ref.shape[0]))


def _moe_out_tm(x1, y, modpat, shape_p, shape_s):
    n = x1.shape[0]
    tm = TOKEN_TILE
    n_p_tiles = shape_p[0] * shape_p[1] // tm
    p = modpat.shape[1]
    rowblk = pl.BlockSpec((tm, D_MODEL), lambda i: (i, 0))
    stream = _stream_specs(jax.ShapeDtypeStruct(shape_p, F32),
                           jax.ShapeDtypeStruct(shape_s, F32), n_p_tiles)
    return pl.pallas_call(
        functools.partial(_moe_out_tm_kernel, n_p_tiles=n_p_tiles),
        out_shape=[jax.ShapeDtypeStruct(shape_p, F32),
                   jax.ShapeDtypeStruct(shape_s, F32)],
        grid=(n // tm,),
        in_specs=[rowblk,
                  pl.BlockSpec((tm * SLAB, LANES), lambda i: (i, 0)),
                  pl.BlockSpec((None, p, D_MODEL),
                               lambda i: (jnp.where(i < n_p_tiles, 0, 1), 0, 5))],
        out_specs=stream,
        scratch_shapes=[pltpu.VMEM((LANE_TILES, tm, LANES), F32)],
        compiler_params=_cparams(("arbitrary",)),
        name="moe_out_tm",
    )(x1, y, modpat)


def _moe_out_final_kernel(x1_ref, y_ref, g2_ref, nfin_ref, op_ref, os_ref,
                          *, n_p_tiles):
    i = pl.program_id(0)
    y = _rows_from_slabs(y_ref, x1_ref.shape[0])
    x2 = x1_ref[...] + _per_row(y, g2_ref[...], lambda a, b: a * b)
    out = _rms(x2) * nfin_ref[...]

    @pl.when(i < n_p_tiles)
    def _():
        op_ref[...] = out

    @pl.when(i >= n_p_tiles)
    def _():
        os_ref[...] = out


def _moe_out_final(x1, y, modpat, norm_final, n_p, rows_per_request):
    n = x1.shape[0]
    tm = TOKEN_TILE
    n_p_tiles = n_p // tm
    p = modpat.shape[1]
    rowblk = pl.BlockSpec((tm, D_MODEL), lambda i: (i, 0))
    pat = lambda i: jnp.where(
        i < n_p_tiles, 0, 1 + (i - n_p_tiles) // (rows_per_request // tm))
    return pl.pallas_call(
        functools.partial(_moe_out_final_kernel, n_p_tiles=n_p_tiles),
        out_shape=[jax.ShapeDtypeStruct((n_p, D_MODEL), F32),
                   jax.ShapeDtypeStruct((n - n_p, D_MODEL), F32)],
        grid=(n // tm,),
        in_specs=[rowblk,
                  pl.BlockSpec((tm * SLAB, LANES), lambda i: (i, 0)),
                  pl.BlockSpec((None, p, D_MODEL), lambda i: (pat(i), 0, 5)),
                  pl.BlockSpec((1, D_MODEL), lambda i: (0, 0))],
        out_specs=[
            pl.BlockSpec((tm, D_MODEL),
                         lambda i: (jnp.minimum(i, n_p_tiles - 1), 0)),
            pl.BlockSpec((tm, D_MODEL),
                         lambda i: (jnp.maximum(i - n_p_tiles, 0), 0))],
        compiler_params=_cparams(("arbitrary",)),
        name="moe_out_final",
    )(x1, y, modpat, norm_final)


def _block_diag_weights(bb_re, bb_im, c_re, c_im):
    lt = LANE_TILES
    gl = SSM_GROUPS // lt
    eye = jnp.eye(gl, dtype=F32)

    def in_map(bb):
        b5 = bb.reshape(2, lt, gl, STATE_DIM, SSM_GROUP)
        t = jnp.einsum("dinph,kn->dikhnp", b5, eye)
        return t.reshape(2, lt, gl * SSM_GROUP, gl * STATE_DIM)

    def out_map(cc):
        c5 = cc.reshape(2, lt, gl, SSM_GROUP, STATE_DIM)
        t = jnp.einsum("dikhp,kn->dikpnh", c5, eye)
        return t.reshape(2, lt, gl * STATE_DIM, gl * SSM_GROUP)

    bt = jnp.concatenate([in_map(bb_re), in_map(bb_im)], axis=-1).astype(BF16)
    return bt, out_map(c_re).astype(BF16), out_map(c_im).astype(BF16)


def kernel(x_prompt, x_sample, c, state_ssm_re, state_ssm_im, c_ctx, norm_mix, norm_ffn, w_ada, b_ada, ssm_w_in, ssm_lam_re, ssm_lam_im, ssm_log_dt, ssm_b_re, ssm_b_im, ssm_c_re, ssm_c_im, ssm_d, ssm_w_glu, fnet_w_out, router_w, router_b, moe_w_gate, moe_w_up, moe_w_down, norm_final):
    bp, lp, _ = x_prompt.shape
    bs, ls, _ = x_sample.shape
    n_p = bp * lp
    n_s = bs * ls
    n = n_p + n_s
    tm = TOKEN_TILE

    cond = jnp.zeros((N_COND, D_MODEL), F32).at[0].set(c_ctx).at[1:1 + bs].set(c)
    modtab = _ada_table(cond, w_ada, b_ada).reshape(DEPTH, N_COND, N_MOD, D_MODEL)

    rwt = router_w.T
    rb = router_b.reshape(N_EXPERTS, 1)
    row = lambda v: v.reshape(1, D_MODEL)

    period = max(bp, bs)
    pat_tm = jnp.stack([
        jnp.broadcast_to(modtab[0, 0], (period, N_MOD, D_MODEL)),
        jnp.tile(modtab[0, 1:1 + bs], (period // bs, 1, 1))])
    pat_tm = pat_tm.reshape(2, period, N_MOD * D_MODEL)

    a_re, a_im, bb_re, bb_im = _zoh(ssm_lam_re[0], ssm_lam_im[0], ssm_log_dt[0],
                                    ssm_b_re[0], ssm_b_im[0])
    bt, ct_re, ct_im = _block_diag_weights(bb_re, bb_im, ssm_c_re[0], ssm_c_im[0])
    a_re = a_re.reshape(2, 1, N_STATE)
    a_im = a_im.reshape(2, 1, N_STATE)

    u = _ssm_in(x_prompt, x_sample, row(norm_mix[0]), pat_tm,
                ssm_w_in[0].astype(BF16))
    h0s_re = state_ssm_re[:, 0].reshape(bs, 2, N_STATE).transpose(1, 0, 2)
    h0s_im = state_ssm_im[:, 0].reshape(bs, 2, N_STATE).transpose(1, 0, 2)
    y_scan, fin_re, fin_im = _ssm_scan(u, bt, ct_re, ct_im, a_re, a_im,
                                       h0s_re, h0s_im, bp=bp, lp=lp, bs=bs, ls=ls)
    tri = jnp.triu(jnp.ones((tm, tm), BF16), k=1)
    x1, h2, ri, rw, cnt = _ssm_out(u, y_scan, row(ssm_d[0]),
                               ssm_w_glu[0].astype(BF16), x_prompt, x_sample,
                               pat_tm, row(norm_ffn[0]), rwt, rb, tri)
    wg_bf, wu_bf, wd_bf = (w.astype(BF16) for w in (moe_w_gate, moe_w_up,
                                                    moe_w_down))
    y_moe = _moe(h2, ri, rw, cnt, wg_bf, wu_bf, wd_bf, 0)
    x2_p, x2_s = _moe_out_tm(x1, y_moe, pat_tm, x_prompt.shape, x_sample.shape)

    pat_bm = jnp.broadcast_to(modtab[1][:1 + bs, None],
                              (1 + bs, SUBLANES, N_MOD, D_MODEL))
    pat_bm = pat_bm.reshape(1 + bs, SUBLANES, N_MOD * D_MODEL)
    x3, h2, ri, rw, cnt = _fnet(x2_p, x2_s, row(norm_mix[1]), pat_bm,
                            fnet_w_out[0].astype(BF16), row(norm_ffn[1]), rwt, rb)
    y_moe = _moe(h2, ri, rw, cnt, wg_bf, wu_bf, wd_bf, 1)
    y_p, y_s = _moe_out_final(x3, y_moe, pat_bm, row(norm_final), n_p, ls)

    st = lambda f: f.transpose(1, 0, 2).reshape(bp, 1, 2, SSM_GROUPS, STATE_DIM)
    return (y_p.reshape(bp, lp, D_MODEL), y_s.reshape(bs, ls, D_MODEL),
            st(fin_re), st(fin_im))
```

```python
import functools

import jax
import jax.numpy as jnp
from jax import lax
from jax.experimental import pallas as pl
from jax.experimental.pallas import tpu as pltpu

F32 = jnp.float32
BF16 = jnp.bfloat16
I32 = jnp.int32
HIGHEST = lax.Precision.HIGHEST

D_MODEL = 1024
DEPTH = 2
SSM_GROUP = 16
SSM_GROUPS = 64
STATE_DIM = 64
N_STATE = SSM_GROUPS * STATE_DIM
FNET_GROUP = 128
N_EXPERTS = 16
N_EXPERT_GROUPS = 4
EXPERTS_PER_GROUP = 4
D_EXPERT = 1024
N_MOD = 6
EPS = 1e-6

LANES = 128
SUBLANES = 8
LANE_TILES = D_MODEL // LANES
VMEM_LIMIT = 56 * 1024 * 1024

TOKEN_TILE = 512
FNET_TILE = 256
FFN_TILE = 256
N_COND = 16

PAIRS = ((0, 1), (0, 2), (0, 3), (1, 3), (1, 2), (3, 2))
N_PAIRS = N_EXPERT_GROUPS * len(PAIRS)
N_CLASS_ROWS = 32
SLAB = SUBLANES


def _cparams(sem, vmem=VMEM_LIMIT):
    return pltpu.CompilerParams(dimension_semantics=sem, vmem_limit_bytes=vmem)


def _sigmoid(x):
    return 1.0 / (1.0 + jnp.exp(-x))


def _gelu_tanh(x):
    c = 0.7978845608028654
    return x * (0.5 * (1.0 + jnp.tanh(c * (x + 0.044715 * (x * x * x)))))


def _per_row(v, pat, fn):
    tm, d = v.shape
    p = pat.shape[0]
    return fn(v.reshape(tm // p, p, d), pat[None]).reshape(tm, d)


def _rms(x):
    ms = jnp.mean(x * x, axis=-1, keepdims=True)
    return x * lax.rsqrt(ms + EPS)


def _norm_mod(x, g, sc, sh):
    y = _rms(x) * g
    y = _per_row(y, sc, lambda a, b: a * (1.0 + b))
    return _per_row(y, sh, lambda a, b: a + b)


def _lane_tiles(v):
    return [v[:, LANES * k:LANES * (k + 1)] for k in range(v.shape[1] // LANES)]


def _to_time_major(x_ref, scr, batch):
    tt = x_ref.shape[1]
    for b in range(batch):
        for k, piece in enumerate(_lane_tiles(x_ref[b])):
            scr[k, pl.ds(b, tt, stride=batch), :] = piece
    return jnp.concatenate([scr[k] for k in range(LANE_TILES)], axis=1)


def _from_time_major(v, o_ref, scr, batch):
    tt = v.shape[0] // batch
    for k, piece in enumerate(_lane_tiles(v)):
        scr[k] = piece
    for b in range(batch):
        o_ref[b] = jnp.concatenate(
            [scr[k, pl.ds(b, tt, stride=batch), :] for k in range(LANE_TILES)],
            axis=1)


def _slab_rows(ref, k, n):
    return ref[pl.ds(k, n, stride=SLAB), :]


def _store_slabs(v, ref):
    n = v.shape[0]
    for k, piece in enumerate(_lane_tiles(v)):
        ref[pl.ds(k, n, stride=SLAB), :] = piece


def _rows_from_slabs(y_ref, n):
    return jnp.concatenate([_slab_rows(y_ref, k, n) for k in range(SLAB)], axis=1)


def _route(logits_t, rb, tri, carry):
    ne, tm = logits_t.shape
    s = _sigmoid(logits_t)
    bz = s + rb
    row = lambda a, r: a[r:r + 1, :]
    gs = []
    for g in range(N_EXPERT_GROUPS):
        v0, v1, v2, v3 = (row(bz, EXPERTS_PER_GROUP * g + j) for j in range(4))
        hi1, lo1 = jnp.maximum(v0, v1), jnp.minimum(v0, v1)
        hi2, lo2 = jnp.maximum(v2, v3), jnp.minimum(v2, v3)
        top1 = jnp.maximum(hi1, hi2)
        top2 = jnp.maximum(jnp.minimum(hi1, hi2), jnp.maximum(lo1, lo2))
        gs.append(top1 + top2)
    bg = jnp.zeros((1, tm), I32)
    bv = gs[0]
    for g in range(1, N_EXPERT_GROUPS):
        upd = gs[g] > bv
        bg = jnp.where(upd, g, bg)
        bv = jnp.where(upd, gs[g], bv)
    cb, cs = [], []
    for j in range(EXPERTS_PER_GROUP):
        vb, vs = row(bz, j), row(s, j)
        for g in range(1, N_EXPERT_GROUPS):
            sel = bg == g
            vb = jnp.where(sel, row(bz, EXPERTS_PER_GROUP * g + j), vb)
            vs = jnp.where(sel, row(s, EXPERTS_PER_GROUP * g + j), vs)
        cb.append(vb)
        cs.append(vs)
    i1 = jnp.zeros((1, tm), I32)
    b1, s1 = cb[0], cs[0]
    for j in range(1, EXPERTS_PER_GROUP):
        upd = cb[j] > b1
        i1 = jnp.where(upd, j, i1)
        b1 = jnp.where(upd, cb[j], b1)
        s1 = jnp.where(upd, cs[j], s1)
    i2 = jnp.zeros((1, tm), I32)
    b2 = jnp.full((1, tm), -jnp.inf, F32)
    s2 = jnp.zeros((1, tm), F32)
    for j in range(EXPERTS_PER_GROUP):
        cand = jnp.where(i1 == j, -jnp.inf, cb[j])
        upd = cand > b2
        i2 = jnp.where(upd, j, i2)
        b2 = jnp.where(upd, cand, b2)
        s2 = jnp.where(upd, cs[j], s2)
    den = s1 + s2
    w1 = s1 / den
    w2 = s2 / den
    lo = jnp.minimum(i1, i2)
    hi = jnp.maximum(i1, i2)
    pidx = jnp.where(lo == 0, hi - 1,
                     jnp.where(lo == 1, jnp.where(hi == 3, 3, 4), 5))
    first = jnp.where(pidx < 3, 0, jnp.where(pidx < 5, 1, 3))
    wa = jnp.where(i1 == first, w1, w2)
    wb = jnp.where(i1 == first, w2, w1)
    q = len(PAIRS) * bg + pidx
    qio = lax.broadcasted_iota(I32, (N_CLASS_ROWS, tm), 0)
    oh = qio == q
    ohf = jnp.where(oh, 1.0, 0.0)
    cum = jnp.dot(ohf.astype(BF16), tri, preferred_element_type=F32) + carry
    rank = jnp.sum(jnp.where(oh, cum, 0.0), axis=0, keepdims=True)
    new_carry = carry + jnp.sum(ohf, axis=1, keepdims=True)
    return q, rank.astype(I32), wa, wb, new_carry


def _post_mixer(m, x, g1, nf, sc2, sh2, rwt_ref, rb_ref, tri_ref,
                x1_ref, h2_ref, ri_ref, rw_ref, cnt_ref, carry):
    tm = x.shape[0]
    x1 = x + _per_row(m, g1, lambda a, b: a * b)
    x1_ref[...] = x1
    h2 = _norm_mod(x1, nf, sc2, sh2)
    logits_t = lax.dot_general(rwt_ref[...], h2, (((1,), (1,)), ((), ())),
                               precision=HIGHEST, preferred_element_type=F32)
    q, rank, wa, wb, nc = _route(logits_t, rb_ref[...], tri_ref[...],
                                 carry[:, 0:1])
    _store_slabs(h2, h2_ref)
    rw_ref[0:1, :] = wa
    rw_ref[1:2, :] = wb
    rw_ref[2:8, :] = jnp.zeros((6, tm), F32)
    ri_ref[0:1, :] = q
    ri_ref[1:2, :] = rank
    ri_ref[2:8, :] = jnp.zeros((6, tm), I32)
    carry[...] = jnp.broadcast_to(nc, carry.shape)
    cnt_ref[...] = carry[...]


def _ada_kernel(cond_ref, w_ref, b_ref, o_ref):
    c = cond_ref[...]
    s = c * _sigmoid(c)
    o_ref[...] = jnp.dot(s, w_ref[...], precision=HIGHEST,
                         preferred_element_type=F32) + b_ref[...]


def _ada_table(cond, w_ada, b_ada):
    tn = 1536
    n_out = N_MOD * D_MODEL
    return pl.pallas_call(
        _ada_kernel,
        out_shape=jax.ShapeDtypeStruct((DEPTH, N_COND, n_out), F32),
        grid=(DEPTH, n_out // tn),
        in_specs=[
            pl.BlockSpec((N_COND, D_MODEL), lambda l, j: (0, 0)),
            pl.BlockSpec((None, D_MODEL, tn), lambda l, j: (l, 0, j)),
            pl.BlockSpec((None, 1, tn), lambda l, j: (l, 0, j)),
        ],
        out_specs=pl.BlockSpec((None, N_COND, tn), lambda l, j: (l, 0, j)),
        compiler_params=_cparams(("arbitrary", "arbitrary")),
        name="ada_table",
    )(cond, w_ada, b_ada.reshape(DEPTH, 1, n_out))


def _zoh_kernel(lr_ref, li_ref, ldt_ref, br_ref, bi_ref,
                are_ref, aim_ref, bbre_ref, bbim_ref):
    lr = lr_ref[...]
    li = li_ref[...]
    dt = jnp.exp(ldt_ref[...])
    mag = jnp.exp(lr * dt)
    a_re = mag * jnp.cos(li * dt)
    a_im = mag * jnp.sin(li * dt)
    den = lr * lr + li * li
    nr = a_re - 1.0
    f_re = (nr * lr + a_im * li) / den
    f_im = (a_im * lr - nr * li) / den
    br = br_ref[...]
    bi = bi_ref[...]
    are_ref[...] = a_re
    aim_ref[...] = a_im
    bbre_ref[...] = f_re * br - f_im * bi
    bbim_ref[...] = f_re * bi + f_im * br


def _zoh(lam_re, lam_im, log_dt, b_re, b_im):
    shape = b_re.shape
    flat = (shape[0] * shape[1] * shape[2] * shape[3] // LANES, LANES)
    bc = lambda a: jnp.broadcast_to(a, shape).reshape(flat)
    args = (bc(lam_re[..., None]), bc(lam_im[..., None]),
            bc(log_dt[:, :, None, None]), b_re.reshape(flat), b_im.reshape(flat))
    outs = pl.pallas_call(
        _zoh_kernel,
        out_shape=[jax.ShapeDtypeStruct(flat, F32)] * 4,
        name="zoh_discretize",
    )(*args)
    a_re, a_im, bb_re, bb_im = (o.reshape(shape) for o in outs)
    return a_re[..., 0], a_im[..., 0], bb_re, bb_im


def _stream_specs(xp, xs, n_p_tiles):
    bp, bs = xp.shape[0], xs.shape[0]
    return [
        pl.BlockSpec((bp, TOKEN_TILE // bp, D_MODEL),
                     lambda i: (0, jnp.minimum(i, n_p_tiles - 1), 0)),
        pl.BlockSpec((bs, TOKEN_TILE // bs, D_MODEL),
                     lambda i: (0, jnp.maximum(i - n_p_tiles, 0), 0)),
    ]


def _ssm_in_kernel(xp_ref, xs_ref, g_ref, sc_ref, sh_ref, w_ref, u_ref, scr,
                   *, n_p_tiles):
    def run(x_ref):
        x = _to_time_major(x_ref, scr, x_ref.shape[0])
        h = _norm_mod(x, g_ref[...], sc_ref[...], sh_ref[...])
        u_ref[...] = jnp.dot(h.astype(BF16), w_ref[...],
                             preferred_element_type=F32).astype(BF16)

    i = pl.program_id(0)
    pl.when(i < n_p_tiles)(lambda: run(xp_ref))
    pl.when(i >= n_p_tiles)(lambda: run(xs_ref))


def _ssm_in(xp, xs, g, modpat, w_in_bf):
    n = (xp.shape[0] * xp.shape[1] + xs.shape[0] * xs.shape[1])
    tm = TOKEN_TILE
    n_p_tiles = xp.shape[0] * xp.shape[1] // tm
    p = modpat.shape[1]
    pat = lambda i: jnp.where(i < n_p_tiles, 0, 1)
    mod = lambda k: pl.BlockSpec((None, p, D_MODEL), lambda i: (pat(i), 0, k))
    return pl.pallas_call(
        functools.partial(_ssm_in_kernel, n_p_tiles=n_p_tiles),
        out_shape=jax.ShapeDtypeStruct((n, D_MODEL), BF16),
        grid=(n // tm,),
        in_specs=_stream_specs(xp, xs, n_p_tiles) + [
            pl.BlockSpec((1, D_MODEL), lambda i: (0, 0)),
            mod(1), mod(0),
            pl.BlockSpec((D_MODEL, D_MODEL), lambda i: (0, 0)),
        ],
        out_specs=pl.BlockSpec((tm, D_MODEL), lambda i: (i, 0)),
        scratch_shapes=[pltpu.VMEM((LANE_TILES, tm, LANES), F32)],
        compiler_params=_cparams(("arbitrary",)),
        name="ssm_in",
    )(xp, xs, g, modpat, modpat, w_in_bf)


def _scan_kernel(u_ref, bt_ref, cre_ref, cim_ref, are_ref, aim_ref,
                 h0re_ref, h0im_ref, y_ref, fre_ref, fim_ref,
                 *scratch, n_p_chunks, bp, bs):
    xre = scratch[:LANE_TILES]
    xim = scratch[LANE_TILES:2 * LANE_TILES]
    hre, him = scratch[2 * LANE_TILES:]
    d = pl.program_id(0)
    c = pl.program_id(1)
    cols = N_STATE // LANE_TILES
    rows = u_ref.shape[0]

    def chunk(batch, reverse):
        steps = rows // batch

        def b_proj(s):
            xt = jnp.dot(u_ref[:, LANES * s:LANES * (s + 1)], bt_ref[s],
                         preferred_element_type=F32)
            xre[s][...] = xt[:, :cols]
            xim[s][...] = xt[:, cols:]

        def scan(s):
            sl = slice(cols * s, cols * (s + 1))
            ar = jnp.broadcast_to(are_ref[:, sl], (batch, cols))
            ai = jnp.broadcast_to(aim_ref[:, sl], (batch, cols))
            hr = hre[0:batch, sl]
            hi = him[0:batch, sl]
            for t in range(steps):
                tt = steps - 1 - t if reverse else t
                r = slice(tt * batch, (tt + 1) * batch)
                nr = ar * hr - ai * hi + xre[s][r, :]
                ni = ar * hi + ai * hr + xim[s][r, :]
                xre[s][r, :] = nr
                xim[s][r, :] = ni
                hr, hi = nr, ni
            hre[0:batch, sl] = hr
            him[0:batch, sl] = hi

        def c_proj(s):
            y_ref[:, LANES * s:LANES * (s + 1)] = (
                jnp.dot(xre[s][...].astype(BF16), cre_ref[s],
                        preferred_element_type=F32)
                - jnp.dot(xim[s][...].astype(BF16), cim_ref[s],
                          preferred_element_type=F32)).astype(BF16)

        for stage in range(LANE_TILES + 2):
            if stage < LANE_TILES:
                b_proj(stage)
            if 0 <= stage - 1 < LANE_TILES:
                scan(stage - 1)
            if 0 <= stage - 2 < LANE_TILES:
                c_proj(stage - 2)

    @pl.when(c == 0)
    def _():
        hre[...] = jnp.zeros(hre.shape, F32)
        him[...] = jnp.zeros(him.shape, F32)

    @pl.when(c == n_p_chunks)
    def _():
        hre[0:bs, :] = h0re_ref[...]
        him[0:bs, :] = h0im_ref[...]

    in_p = c < n_p_chunks
    in_s = jnp.logical_not(in_p)
    fwd = d == 0
    bwd = jnp.logical_not(fwd)
    pl.when(jnp.logical_and(in_p, fwd))(lambda: chunk(bp, False))
    pl.when(jnp.logical_and(in_p, bwd))(lambda: chunk(bp, True))
    pl.when(jnp.logical_and(in_s, fwd))(lambda: chunk(bs, False))
    pl.when(jnp.logical_and(in_s, bwd))(lambda: chunk(bs, True))

    @pl.when(c == n_p_chunks - 1)
    def _():
        fre_ref[...] = hre[0:bp, :]
        fim_ref[...] = him[0:bp, :]


def _ssm_scan(u, bt, ct_re, ct_im, a_re, a_im, h0_re, h0_im, *, bp, lp, bs, ls):
    n = u.shape[0]
    rows = TOKEN_TILE
    n_p_chunks = bp * lp // rows
    n_s_chunks = bs * ls // rows

    def chunk(d, c):
        in_p = jnp.where(d == 0, c, n_p_chunks - 1 - c)
        cs = c - n_p_chunks
        in_s = n_p_chunks + jnp.where(d == 0, cs, n_s_chunks - 1 - cs)
        return jnp.where(c < n_p_chunks, in_p, in_s)

    cols = N_STATE // LANE_TILES
    dspec = lambda shape: pl.BlockSpec((None,) + shape,
                                       lambda d, c: (d,) + (0,) * len(shape))
    kern = functools.partial(_scan_kernel, n_p_chunks=n_p_chunks, bp=bp, bs=bs)
    return pl.pallas_call(
        kern,
        out_shape=[jax.ShapeDtypeStruct((2, n, D_MODEL), BF16),
                   jax.ShapeDtypeStruct((2, bp, N_STATE), F32),
                   jax.ShapeDtypeStruct((2, bp, N_STATE), F32)],
        grid=(2, n_p_chunks + n_s_chunks),
        in_specs=[
            pl.BlockSpec((rows, D_MODEL), lambda d, c: (chunk(d, c), 0)),
            dspec((LANE_TILES, LANES, 2 * cols)),
            dspec((LANE_TILES, cols, LANES)),
            dspec((LANE_TILES, cols, LANES)),
            dspec((1, N_STATE)),
            dspec((1, N_STATE)),
            dspec((bs, N_STATE)),
            dspec((bs, N_STATE)),
        ],
        out_specs=[
            pl.BlockSpec((None, rows, D_MODEL), lambda d, c: (d, chunk(d, c), 0)),
            dspec((bp, N_STATE)),
            dspec((bp, N_STATE)),
        ],
        scratch_shapes=(
            [pltpu.VMEM((rows, cols), F32)] * (2 * LANE_TILES)
            + [pltpu.VMEM((max(bp, bs), N_STATE), F32)] * 2),
        compiler_params=_cparams(("arbitrary", "arbitrary")),
        name="ssm_scan",
    )(u, bt, ct_re, ct_im, a_re, a_im, h0_re, h0_im)


def _ssm_out_kernel(u_ref, y0_ref, y1_ref, dsk_ref, wglu_ref, xp_ref, xs_ref,
                    g1_ref, nf_ref, sc2_ref, sh2_ref, rwt_ref, rb_ref, tri_ref,
                    x1_ref, h2_ref, ri_ref, rw_ref, cnt_ref, scr, carry,
                    *, n_p_tiles):
    i = pl.program_id(0)

    @pl.when(i == 0)
    def _():
        carry[...] = jnp.zeros(carry.shape, F32)

    def run(x_ref):
        x = _to_time_major(x_ref, scr, x_ref.shape[0])
        y = (u_ref[...].astype(F32) * dsk_ref[...] + y0_ref[...].astype(F32)
             + y1_ref[...].astype(F32))
        ge = _gelu_tanh(y).astype(BF16)
        vg = jnp.dot(ge, wglu_ref[...], preferred_element_type=F32)
        m = vg[:, :D_MODEL] * _sigmoid(vg[:, D_MODEL:])
        _post_mixer(m, x, g1_ref[...], nf_ref[...], sc2_ref[...], sh2_ref[...],
                    rwt_ref, rb_ref, tri_ref, x1_ref, h2_ref, ri_ref, rw_ref,
                    cnt_ref, carry)

    pl.when(i < n_p_tiles)(lambda: run(xp_ref))
    pl.when(i >= n_p_tiles)(lambda: run(xs_ref))


def _route_out_shapes(n):
    return [jax.ShapeDtypeStruct((n, D_MODEL), F32),
            jax.ShapeDtypeStruct((n * SLAB, LANES), F32),
            jax.ShapeDtypeStruct((SUBLANES, n), I32),
            jax.ShapeDtypeStruct((SUBLANES, n), F32),
            jax.ShapeDtypeStruct((N_CLASS_ROWS, LANES), F32)]


def _ssm_out(u, y, d_skip, w_glu_bf, xp, xs, modpat, norm_ffn, rwt, rb, tri):
    n = u.shape[0]
    tm = TOKEN_TILE
    n_p_tiles = xp.shape[0] * xp.shape[1] // tm
    p = modpat.shape[1]
    pat = lambda i: jnp.where(i < n_p_tiles, 0, 1)
    mod = lambda k: pl.BlockSpec((None, p, D_MODEL), lambda i: (pat(i), 0, k))
    full = lambda shape: pl.BlockSpec(shape, lambda i: (0,) * len(shape))
    rowblk = pl.BlockSpec((tm, D_MODEL), lambda i: (i, 0))
    return pl.pallas_call(
        functools.partial(_ssm_out_kernel, n_p_tiles=n_p_tiles),
        out_shape=_route_out_shapes(n),
        grid=(n // tm,),
        in_specs=[
            rowblk,
            pl.BlockSpec((None, tm, D_MODEL), lambda i: (0, i, 0)),
            pl.BlockSpec((None, tm, D_MODEL), lambda i: (1, i, 0)),
            full((1, D_MODEL)),
            full((D_MODEL, 2 * D_MODEL)),
        ] + _stream_specs(xp, xs, n_p_tiles) + [
            mod(2), full((1, D_MODEL)), mod(4), mod(3),
            full((N_EXPERTS, D_MODEL)), full((N_EXPERTS, 1)), full((tm, tm)),
        ],
        out_specs=[rowblk,
                   pl.BlockSpec((tm * SLAB, LANES), lambda i: (i, 0)),
                   pl.BlockSpec((SUBLANES, tm), lambda i: (0, i)),
                   pl.BlockSpec((SUBLANES, tm), lambda i: (0, i)),
                   full((N_CLASS_ROWS, LANES))],
        scratch_shapes=[pltpu.VMEM((LANE_TILES, tm, LANES), F32),
                        pltpu.VMEM((N_CLASS_ROWS, LANES), F32)],
        compiler_params=_cparams(("arbitrary",)),
        name="ssm_out",
    )(u, y, y, d_skip, w_glu_bf, xp, xs, modpat, norm_ffn, modpat, modpat,
      rwt, rb, tri)


def _fnet_kernel(xp_ref, xs_ref, nm_ref, sc1_ref, sh1_ref, cs_ref, dftp_ref,
                 dfts_ref, wout_ref, g1_ref, nf_ref, sc2_ref, sh2_ref, rwt_ref,
                 rb_ref, tri_ref, x1_ref, h2_ref, ri_ref, rw_ref, cnt_ref, ucs,
                 carry, *, bp, tiles_s):
    s = pl.program_id(0)
    n_groups = D_MODEL // FNET_GROUP
    tr = FNET_TILE

    @pl.when(s == 0)
    def _():
        carry[...] = jnp.zeros(carry.shape, F32)

    def run(x_ref, dft_ref, i):
        seq = x_ref.shape[0]
        scale = float((seq * FNET_GROUP) ** -0.5)

        @pl.when(i == 0)
        def _():
            def stage1(r, _):
                rows = pl.ds(pl.multiple_of(r * tr, tr), tr)
                h = _norm_mod(x_ref[rows, :], nm_ref[...], sc1_ref[...],
                              sh1_ref[...]).astype(BF16)
                for k in range(n_groups):
                    cols = slice(FNET_GROUP * k, FNET_GROUP * (k + 1))
                    t = jnp.dot(h[:, cols], cs_ref[...],
                                preferred_element_type=F32)
                    ucs[rows, cols] = t[:, :FNET_GROUP].astype(BF16)
                    ucs[pl.ds(pl.multiple_of(seq + r * tr, tr), tr), cols] = (
                        t[:, FNET_GROUP:].astype(BF16))
                return 0
            lax.fori_loop(0, seq // tr, stage1, 0)

        y = jnp.dot(dft_ref[...], ucs[0:2 * seq, :],
                    preferred_element_type=F32) * scale
        m = jnp.dot(y.astype(BF16), wout_ref[...], preferred_element_type=F32)
        xrow = x_ref[pl.ds(pl.multiple_of(i * tr, tr), tr), :]
        _post_mixer(m, xrow, g1_ref[...], nf_ref[...], sc2_ref[...],
                    sh2_ref[...], rwt_ref, rb_ref, tri_ref, x1_ref, h2_ref,
                    ri_ref, rw_ref, cnt_ref, carry)

    pl.when(s < bp)(lambda: run(xp_ref, dftp_ref, 0 * s))
    pl.when(s >= bp)(lambda: run(xs_ref, dfts_ref, (s - bp) % tiles_s))


def _dft_table(seq):
    inner = 64
    k = jnp.arange(seq, dtype=I32)[:, None]
    t1 = jnp.arange(seq // inner, dtype=I32)[None, :] * inner
    t2 = jnp.arange(inner, dtype=I32)[None, :]
    ang = lambda t: ((k * t) % seq).astype(F32) * (2.0 * jnp.pi / seq)
    ca, sa = jnp.cos(ang(t1))[:, :, None], jnp.sin(ang(t1))[:, :, None]
    cb, sb = jnp.cos(ang(t2))[:, None, :], jnp.sin(ang(t2))[:, None, :]
    cos = (ca * cb - sa * sb).reshape(seq, seq)
    sin = (sa * cb + ca * sb).reshape(seq, seq)
    return jnp.concatenate([cos, -sin], axis=1).astype(BF16)


def _fnet(xp, xs, norm_mix, modpat, w_out_bf, norm_ffn, rwt, rb):
    bp, lp, _ = xp.shape
    bs, ls, _ = xs.shape
    tr = FNET_TILE
    assert lp == tr
    tiles_s = ls // tr
    n = bp * lp + bs * ls
    p = modpat.shape[1]
    kc = jnp.arange(FNET_GROUP, dtype=I32)
    angc = ((kc[:, None] * kc[None, :]) % FNET_GROUP).astype(F32) * (
        2.0 * jnp.pi / FNET_GROUP)
    cs = jnp.concatenate([jnp.cos(angc), jnp.sin(angc)], axis=1).astype(BF16)
    tri = jnp.triu(jnp.ones((tr, tr), BF16), k=1)

    sb = lambda s: jnp.clip((s - bp) // tiles_s, 0, bs - 1)
    pat = lambda s: jnp.where(s < bp, 0, 1 + sb(s))
    mod = lambda kk: pl.BlockSpec((None, p, D_MODEL), lambda s: (pat(s), 0, kk))
    full = lambda shape: pl.BlockSpec(shape, lambda s: (0,) * len(shape))
    kern = functools.partial(_fnet_kernel, bp=bp, tiles_s=tiles_s)
    return pl.pallas_call(
        kern,
        out_shape=_route_out_shapes(n),
        grid=(bp + bs * tiles_s,),
        in_specs=[
            pl.BlockSpec((None, lp, D_MODEL),
                         lambda s: (jnp.minimum(s, bp - 1), 0, 0)),
            pl.BlockSpec((None, ls, D_MODEL), lambda s: (sb(s), 0, 0)),
            full((1, D_MODEL)), mod(1), mod(0),
            full((FNET_GROUP, 2 * FNET_GROUP)),
            full((tr, 2 * lp)),
            pl.BlockSpec((tr, 2 * ls),
                         lambda s: (jnp.maximum(s - bp, 0) % tiles_s, 0)),
            full((D_MODEL, D_MODEL)),
            mod(2), full((1, D_MODEL)), mod(4), mod(3),
            full((N_EXPERTS, D_MODEL)), full((N_EXPERTS, 1)), full((tr, tr)),
        ],
        out_specs=[pl.BlockSpec((tr, D_MODEL), lambda s: (s, 0)),
                   pl.BlockSpec((tr * SLAB, LANES), lambda s: (s, 0)),
                   pl.BlockSpec((SUBLANES, tr), lambda s: (0, s)),
                   pl.BlockSpec((SUBLANES, tr), lambda s: (0, s)),
                   full((N_CLASS_ROWS, LANES))],
        scratch_shapes=[pltpu.VMEM((2 * ls, D_MODEL), BF16),
                        pltpu.VMEM((N_CLASS_ROWS, LANES), F32)],
        compiler_params=_cparams(("arbitrary",)),
        name="fnet",
    )(xp, xs, norm_mix, modpat, modpat, cs, _dft_table(lp), _dft_table(ls),
      w_out_bf, modpat, norm_ffn, modpat, modpat, rwt, rb, tri)


def _invert_kernel(slot_ref, ends_ref, gsrc_ref, sdst_ref):
    n = slot_ref.shape[0]
    n_slots = gsrc_ref.shape[0]
    t = FFN_TILE
    unroll = 8

    def pad_tile(base):
        def pad(j, _):
            for k in range(unroll):
                r = j * unroll + k
                gsrc_ref[base + r] = 0
                sdst_ref[base + r] = n + r
            return 0
        lax.fori_loop(0, t // unroll, pad, 0)

    for q in range(N_PAIRS):
        pad_tile(jnp.maximum(ends_ref[q] - t, 0))

    def unused(b, _):
        pad_tile(b * t)
        return 0
    lax.fori_loop(ends_ref[N_PAIRS - 1] // t, n_slots // t, unused, 0)

    def body(j, _):
        for k in range(unroll):
            t = j * unroll + k
            s = slot_ref[t]
            gsrc_ref[s] = t
            sdst_ref[s] = t
        return 0
    lax.fori_loop(0, n // unroll, body, 0)


def _invert(slot, ends, n_slots):
    smem = pl.BlockSpec(memory_space=pltpu.SMEM)
    return pl.pallas_call(
        _invert_kernel,
        out_shape=[jax.ShapeDtypeStruct((n_slots,), I32)] * 2,
        in_specs=[smem, smem],
        out_specs=[smem, smem],
        name="moe_invert",
    )(slot, ends)


def _ffn_kernel(tea_ref, teb_ref, nu_ref, gsrc_ref, sdst_ref, wa_ref, wb_ref,
                h_hbm, wga, wua, wda, wgb, wub, wdb, y_hbm,
                xbuf0, xbuf1, obuf0, obuf1, wcola, wcolb, gsem, ssem):
    del tea_ref, teb_ref
    i = pl.program_id(0)
    nu = nu_ref[0]
    t = FFN_TILE

    def slab(ref, r):
        return ref.at[pl.ds(pl.multiple_of(r * SLAB, SLAB), SLAB), :]

    def gather_start(tile, xbuf, sem):
        base = tile * t
        for r in range(t):
            pltpu.make_async_copy(slab(h_hbm, gsrc_ref[base + r]), slab(xbuf, r),
                                  sem).start(priority=r % 2)

    def gather_wait(xbuf, sem):
        pltpu.make_async_copy(h_hbm.at[pl.ds(0, t * SLAB), :], xbuf, sem).wait()

    def scatter_start(tile, obuf, sem):
        base = tile * t
        for r in range(t):
            pltpu.make_async_copy(slab(obuf, r), slab(y_hbm, sdst_ref[base + r]),
                                  sem).start(priority=r % 2)

    def scatter_wait(obuf, sem):
        pltpu.make_async_copy(obuf, y_hbm.at[pl.ds(0, t * SLAB), :], sem).wait()

    def expert(xb, wg, wu, wd):
        g = jnp.dot(xb, wg[...], preferred_element_type=F32)
        u = jnp.dot(xb, wu[...], preferred_element_type=F32)
        a = (g * _sigmoid(g)) * u
        return jnp.dot(a.astype(BF16), wd[...], preferred_element_type=F32)

    def step(xc, xn, oc, op, gc, gn, sc, sp):
        pl.when(i == 0)(lambda: gather_start(0, xc, gc))
        pl.when(i >= 2)(lambda: scatter_wait(oc, sc))
        gather_wait(xc, gc)
        pl.when(i + 1 < nu)(lambda: gather_start(i + 1, xn, gn))
        pl.when(i >= 1)(lambda: scatter_start(i - 1, op, sp))
        for r in range(t):
            tok = gsrc_ref[i * t + r]
            wcola[r:r + 1, :] = jnp.full((1, LANES), wa_ref[tok], F32)
            wcolb[r:r + 1, :] = jnp.full((1, LANES), wb_ref[tok], F32)
        xb = _rows_from_slabs(xc, t).astype(BF16)
        ya = expert(xb, wga, wua, wda)
        yb = expert(xb, wgb, wub, wdb)
        y = wcola[:, 0:1] * ya + wcolb[:, 0:1] * yb
        for k, piece in enumerate(_lane_tiles(y)):
            oc[pl.ds(k, t, stride=SLAB), :] = piece

    def drain(oc, op, sc, sp):
        pl.when(i >= 2)(lambda: scatter_wait(oc, sc))
        scatter_start(i - 1, op, sp)
        scatter_wait(op, sp)
        spare_row = y_hbm.shape[0] - t * SLAB
        oc[...] = jnp.zeros(oc.shape, F32)
        spare = pltpu.make_async_copy(
            oc, y_hbm.at[pl.ds(spare_row, t * SLAB), :], sc)
        spare.start()
        spare.wait()

    even = (i % 2) == 0
    odd = jnp.logical_not(even)
    g0, g1, s0, s1 = gsem.at[0], gsem.at[1], ssem.at[0], ssem.at[1]
    pl.when(jnp.logical_and(i < nu, even))(
        lambda: step(xbuf0, xbuf1, obuf0, obuf1, g0, g1, s0, s1))
    pl.when(jnp.logical_and(i < nu, odd))(
        lambda: step(xbuf1, xbuf0, obuf1, obuf0, g1, g0, s1, s0))
    pl.when(jnp.logical_and(i == nu, even))(
        lambda: drain(obuf0, obuf1, s0, s1))
    pl.when(jnp.logical_and(i == nu, odd))(
        lambda: drain(obuf1, obuf0, s1, s0))


def _expert_ffn(tea, teb, n_used, gsrc, sdst, wa_tok, wb_tok, h2, wg, wu, wd,
                layer):
    n_tok = h2.shape[0] // SLAB
    max_tiles = gsrc.shape[0] // FFN_TILE
    wa = lambda a, b: pl.BlockSpec(
        (None, None, a, b), lambda i, ta, tb, *_: (layer, ta[i], 0, 0))
    wb = lambda a, b: pl.BlockSpec(
        (None, None, a, b), lambda i, ta, tb, *_: (layer, tb[i], 0, 0))
    anyspec = pl.BlockSpec(memory_space=pl.ANY)
    return pl.pallas_call(
        _ffn_kernel,
        out_shape=jax.ShapeDtypeStruct(((n_tok + FFN_TILE) * SLAB, LANES), F32),
        grid_spec=pltpu.PrefetchScalarGridSpec(
            num_scalar_prefetch=7,
            grid=(max_tiles + 1,),
            in_specs=[anyspec,
                      wa(D_MODEL, D_EXPERT), wa(D_MODEL, D_EXPERT),
                      wa(D_EXPERT, D_MODEL),
                      wb(D_MODEL, D_EXPERT), wb(D_MODEL, D_EXPERT),
                      wb(D_EXPERT, D_MODEL)],
            out_specs=anyspec,
            scratch_shapes=[pltpu.VMEM((FFN_TILE * SLAB, LANES), F32)] * 4 + [
                            pltpu.VMEM((FFN_TILE, LANES), F32),
                            pltpu.VMEM((FFN_TILE, LANES), F32),
                            pltpu.SemaphoreType.DMA((2,)),
                            pltpu.SemaphoreType.DMA((2,))],
        ),
        compiler_params=_cparams(("arbitrary",)),
        name="moe_ffn",
    )(tea, teb, n_used, gsrc, sdst, wa_tok, wb_tok, h2, wg, wu, wd, wg, wu, wd)


def _moe(h2, ri, rw, cnt, wg_bf, wu_bf, wd_bf, layer):
    n = h2.shape[0] // SLAB
    max_tiles = n // FFN_TILE + N_PAIRS
    n_slots = max_tiles * FFN_TILE
    counts = cnt[:N_PAIRS, 0].astype(I32)
    padded = ((counts + FFN_TILE - 1) // FFN_TILE) * FFN_TILE
    ends = jnp.cumsum(padded)
    offs = ends - padded
    q, rank = ri[0], ri[1]
    cls = jnp.arange(N_PAIRS, dtype=I32)
    slot = rank + jnp.sum(jnp.where(q[None, :] == cls[:, None], offs[:, None], 0),
                          axis=0)
    n_used = ends[-1] // FFN_TILE
    tile = jnp.arange(max_tiles + 1, dtype=I32)
    tq = jnp.sum((tile[:, None] * FFN_TILE >= ends[None, :]).astype(I32), axis=1)
    tq_last = jnp.sum(((n_used - 1) * FFN_TILE >= ends).astype(I32))
    tq = jnp.where(tile < n_used, tq, tq_last)
    pa = jnp.array([a for a, _ in PAIRS], I32)
    pb = jnp.array([b for _, b in PAIRS], I32)
    grp, pidx = tq // len(PAIRS), tq % len(PAIRS)
    tea = EXPERTS_PER_GROUP * grp + jnp.take(pa, pidx)
    teb = EXPERTS_PER_GROUP * grp + jnp.take(pb, pidx)
    gsrc, sdst = _invert(slot, ends, n_slots)
    return _expert_ffn(tea, teb, n_used.reshape(1), gsrc, sdst, rw[0], rw[1], h2,
                       wg_bf, wu_bf, wd_bf, layer)


def _moe_out_tm_kernel(x1_ref, y_ref, g2_ref, op_ref, os_ref, scr, *, n_p_tiles):
    i = pl.program_id(0)
    y = _rows_from_slabs(y_ref, x1_ref.shape[0])
    x2 = x1_ref[...] + _per_row(y, g2_ref[...], lambda a, b: a * b)
    pl.when(i < n_p_tiles)(
        lambda: _from_time_major(x2, op_ref, scr, op_ref.shape[0]))
    pl.when(i >= n_p_tiles)(
        lambda: _from_time_major(x2, os_ref, scr, os_ref.shape[0]))


def _moe_out_tm(x1, y, modpat, shape_p, shape_s):
    n = x1.shape[0]
    tm = TOKEN_TILE
    n_p_tiles = shape_p[0] * shape_p[1] // tm
    p = modpat.shape[1]
    rowblk = pl.BlockSpec((tm, D_MODEL), lambda i: (i, 0))
    stream = _stream_specs(jax.ShapeDtypeStruct(shape_p, F32),
                           jax.ShapeDtypeStruct(shape_s, F32), n_p_tiles)
    return pl.pallas_call(
        functools.partial(_moe_out_tm_kernel, n_p_tiles=n_p_tiles),
        out_shape=[jax.ShapeDtypeStruct(shape_p, F32),
                   jax.ShapeDtypeStruct(shape_s, F32)],
        grid=(n // tm,),
        in_specs=[rowblk,
                  pl.BlockSpec((tm * SLAB, LANES), lambda i: (i, 0)),
                  pl.BlockSpec((None, p, D_MODEL),
                               lambda i: (jnp.where(i < n_p_tiles, 0, 1), 0, 5))],
        out_specs=stream,
        scratch_shapes=[pltpu.VMEM((LANE_TILES, tm, LANES), F32)],
        compiler_params=_cparams(("arbitrary",)),
        name="moe_out_tm",
    )(x1, y, modpat)


def _moe_out_final_kernel(x1_ref, y_ref, g2_ref, nfin_ref, op_ref, os_ref,
                          *, n_p_tiles):
    i = pl.program_id(0)
    y = _rows_from_slabs(y_ref, x1_ref.shape[0])
    x2 = x1_ref[...] + _per_row(y, g2_ref[...], lambda a, b: a * b)
    out = _rms(x2) * nfin_ref[...]

    @pl.when(i < n_p_tiles)
    def _():
        op_ref[...] = out

    @pl.when(i >= n_p_tiles)
    def _():
        os_ref[...] = out


def _moe_out_final(x1, y, modpat, norm_final, n_p, rows_per_request):
    n = x1.shape[0]
    tm = TOKEN_TILE
    n_p_tiles = n_p // tm
    p = modpat.shape[1]
    rowblk = pl.BlockSpec((tm, D_MODEL), lambda i: (i, 0))
    pat = lambda i: jnp.where(
        i < n_p_tiles, 0, 1 + (i - n_p_tiles) // (rows_per_request // tm))
    return pl.pallas_call(
        functools.partial(_moe_out_final_kernel, n_p_tiles=n_p_tiles),
        out_shape=[jax.ShapeDtypeStruct((n_p, D_MODEL), F32),
                   jax.ShapeDtypeStruct((n - n_p, D_MODEL), F32)],
        grid=(n // tm,),
        in_specs=[rowblk,
                  pl.BlockSpec((tm * SLAB, LANES), lambda i: (i, 0)),
                  pl.BlockSpec((None, p, D_MODEL), lambda i: (pat(i), 0, 5)),
                  pl.BlockSpec((1, D_MODEL), lambda i: (0, 0))],
        out_specs=[
            pl.BlockSpec((tm, D_MODEL),
                         lambda i: (jnp.minimum(i, n_p_tiles - 1), 0)),
            pl.BlockSpec((tm, D_MODEL),
                         lambda i: (jnp.maximum(i - n_p_tiles, 0), 0))],
        compiler_params=_cparams(("arbitrary",)),
        name="moe_out_final",
    )(x1, y, modpat, norm_final)


def _block_diag_weights(bb_re, bb_im, c_re, c_im):
    lt = LANE_TILES
    gl = SSM_GROUPS // lt
    eye = jnp.eye(gl, dtype=F32)

    def in_map(bb):
        b5 = bb.reshape(2, lt, gl, STATE_DIM, SSM_GROUP)
        t = jnp.einsum("dinph,kn->dikhnp", b5, eye)
        return t.reshape(2, lt, gl * SSM_GROUP, gl * STATE_DIM)

    def out_map(cc):
        c5 = cc.reshape(2, lt, gl, SSM_GROUP, STATE_DIM)
        t = jnp.einsum("dikhp,kn->dikpnh", c5, eye)
        return t.reshape(2, lt, gl * STATE_DIM, gl * SSM_GROUP)

    bt = jnp.concatenate([in_map(bb_re), in_map(bb_im)], axis=-1).astype(BF16)
    return bt, out_map(c_re).astype(BF16), out_map(c_im).astype(BF16)


def kernel(x_prompt, x_sample, c, state_ssm_re, state_ssm_im, c_ctx, norm_mix, norm_ffn, w_ada, b_ada, ssm_w_in, ssm_lam_re, ssm_lam_im, ssm_log_dt, ssm_b_re, ssm_b_im, ssm_c_re, ssm_c_im, ssm_d, ssm_w_glu, fnet_w_out, router_w, router_b, moe_w_gate, moe_w_up, moe_w_down, norm_final):
    bp, lp, _ = x_prompt.shape
    bs, ls, _ = x_sample.shape
    n_p = bp * lp
    n_s = bs * ls
    n = n_p + n_s
    tm = TOKEN_TILE

    cond = jnp.zeros((N_COND, D_MODEL), F32).at[0].set(c_ctx).at[1:1 + bs].set(c)
    modtab = _ada_table(cond, w_ada, b_ada).reshape(DEPTH, N_COND, N_MOD, D_MODEL)

    rwt = router_w.T
    rb = router_b.reshape(N_EXPERTS, 1)
    row = lambda v: v.reshape(1, D_MODEL)

    period = max(bp, bs)
    pat_tm = jnp.stack([
        jnp.broadcast_to(modtab[0, 0], (period, N_MOD, D_MODEL)),
        jnp.tile(modtab[0, 1:1 + bs], (period // bs, 1, 1))])
    pat_tm = pat_tm.reshape(2, period, N_MOD * D_MODEL)

    a_re, a_im, bb_re, bb_im = _zoh(ssm_lam_re[0], ssm_lam_im[0], ssm_log_dt[0],
                                    ssm_b_re[0], ssm_b_im[0])
    bt, ct_re, ct_im = _block_diag_weights(bb_re, bb_im, ssm_c_re[0], ssm_c_im[0])
    a_re = a_re.reshape(2, 1, N_STATE)
    a_im = a_im.reshape(2, 1, N_STATE)

    u = _ssm_in(x_prompt, x_sample, row(norm_mix[0]), pat_tm,
                ssm_w_in[0].astype(BF16))
    h0s_re = state_ssm_re[:, 0].reshape(bs, 2, N_STATE).transpose(1, 0, 2)
    h0s_im = state_ssm_im[:, 0].reshape(bs, 2, N_STATE).transpose(1, 0, 2)
    y_scan, fin_re, fin_im = _ssm_scan(u, bt, ct_re, ct_im, a_re, a_im,
                                       h0s_re, h0s_im, bp=bp, lp=lp, bs=bs, ls=ls)
    tri = jnp.triu(jnp.ones((tm, tm), BF16), k=1)
    x1, h2, ri, rw, cnt = _ssm_out(u, y_scan, row(ssm_d[0]),
                               ssm_w_glu[0].astype(BF16), x_prompt, x_sample,
                               pat_tm, row(norm_ffn[0]), rwt, rb, tri)
    wg_bf, wu_bf, wd_bf = (w.astype(BF16) for w in (moe_w_gate, moe_w_up,
                                                    moe_w_down))
    y_moe = _moe(h2, ri, rw, cnt, wg_bf, wu_bf, wd_bf, 0)
    x2_p, x2_s = _moe_out_tm(x1, y_moe, pat_tm, x_prompt.shape, x_sample.shape)

    pat_bm = jnp.broadcast_to(modtab[1][:1 + bs, None],
                              (1 + bs, SUBLANES, N_MOD, D_MODEL))
    pat_bm = pat_bm.reshape(1 + bs, SUBLANES, N_MOD * D_MODEL)
    x3, h2, ri, rw, cnt = _fnet(x2_p, x2_s, row(norm_mix[1]), pat_bm,
                            fnet_w_out[0].astype(BF16), row(norm_ffn[1]), rwt, rb)
    y_moe = _moe(h2, ri, rw, cnt, wg_bf, wu_bf, wd_bf, 1)
    y_p, y_s = _moe_out_final(x3, y_moe, pat_bm, row(norm_final), n_p, ls)

    st = lambda f: f.transpose(1, 0, 2).reshape(bp, 1, 2, SSM_GROUPS, STATE_DIM)
    return (y_p.reshape(bp, lp, D_MODEL), y_s.reshape(bs, ls, D_MODEL),
            st(fin_re), st(fin_im))
```

```python
import functools

import jax
import jax.numpy as jnp
from jax import lax
from jax.experimental import pallas as pl
from jax.experimental.pallas import tpu as pltpu

F32 = jnp.float32
BF16 = jnp.bfloat16
I32 = jnp.int32
HIGHEST = lax.Precision.HIGHEST

D_MODEL = 1024
DEPTH = 2
SSM_GROUP = 16
SSM_GROUPS = 64
STATE_DIM = 64
N_STATE = SSM_GROUPS * STATE_DIM
FNET_GROUP = 128
N_EXPERTS = 16
N_EXPERT_GROUPS = 4
EXPERTS_PER_GROUP = 4
D_EXPERT = 1024
N_MOD = 6
EPS = 1e-6

LANES = 128
SUBLANES = 8
LANE_TILES = D_MODEL // LANES
VMEM_LIMIT = 56 * 1024 * 1024

TOKEN_TILE = 512
FNET_TILE = 256
FFN_TILE = 256
N_COND = 16

PAIRS = ((0, 1), (0, 2), (0, 3), (1, 3), (1, 2), (3, 2))
N_PAIRS = N_EXPERT_GROUPS * len(PAIRS)
N_CLASS_ROWS = 32
SLAB = SUBLANES


def _cparams(sem, vmem=VMEM_LIMIT):
    return pltpu.CompilerParams(dimension_semantics=sem, vmem_limit_bytes=vmem)


def _sigmoid(x):
    return 1.0 / (1.0 + jnp.exp(-x))


def _gelu_tanh(x):
    c = 0.7978845608028654
    return x * (0.5 * (1.0 + jnp.tanh(c * (x + 0.044715 * (x * x * x)))))


def _per_row(v, pat, fn):
    tm, d = v.shape
    p = pat.shape[0]
    return fn(v.reshape(tm // p, p, d), pat[None]).reshape(tm, d)


def _rms(x):
    ms = jnp.mean(x * x, axis=-1, keepdims=True)
    return x * lax.rsqrt(ms + EPS)


def _norm_mod(x, g, sc, sh):
    y = _rms(x) * g
    y = _per_row(y, sc, lambda a, b: a * (1.0 + b))
    return _per_row(y, sh, lambda a, b: a + b)


def _lane_tiles(v):
    return [v[:, LANES * k:LANES * (k + 1)] for k in range(v.shape[1] // LANES)]


def _to_time_major(x_ref, scr, batch):
    tt = x_ref.shape[1]
    for b in range(batch):
        for k, piece in enumerate(_lane_tiles(x_ref[b])):
            scr[k, pl.ds(b, tt, stride=batch), :] = piece
    return jnp.concatenate([scr[k] for k in range(LANE_TILES)], axis=1)


def _from_time_major(v, o_ref, scr, batch):
    tt = v.shape[0] // batch
    for k, piece in enumerate(_lane_tiles(v)):
        scr[k] = piece
    for b in range(batch):
        o_ref[b] = jnp.concatenate(
            [scr[k, pl.ds(b, tt, stride=batch), :] for k in range(LANE_TILES)],
            axis=1)


def _slab_rows(ref, k, n):
    return ref[pl.ds(k, n, stride=SLAB), :]


def _store_slabs(v, ref):
    n = v.shape[0]
    for k, piece in enumerate(_lane_tiles(v)):
        ref[pl.ds(k, n, stride=SLAB), :] = piece


def _rows_from_slabs(y_ref, n):
    return jnp.concatenate([_slab_rows(y_ref, k, n) for k in range(SLAB)], axis=1)


def _route(logits_t, rb, tri, carry):
    ne, tm = logits_t.shape
    s = _sigmoid(logits_t)
    bz = s + rb
    row = lambda a, r: a[r:r + 1, :]
    gs = []
    for g in range(N_EXPERT_GROUPS):
        v0, v1, v2, v3 = (row(bz, EXPERTS_PER_GROUP * g + j) for j in range(4))
        hi1, lo1 = jnp.maximum(v0, v1), jnp.minimum(v0, v1)
        hi2, lo2 = jnp.maximum(v2, v3), jnp.minimum(v2, v3)
        top1 = jnp.maximum(hi1, hi2)
        top2 = jnp.maximum(jnp.minimum(hi1, hi2), jnp.maximum(lo1, lo2))
        gs.append(top1 + top2)
    bg = jnp.zeros((1, tm), I32)
    bv = gs[0]
    for g in range(1, N_EXPERT_GROUPS):
        upd = gs[g] > bv
        bg = jnp.where(upd, g, bg)
        bv = jnp.where(upd, gs[g], bv)
    cb, cs = [], []
    for j in range(EXPERTS_PER_GROUP):
        vb, vs = row(bz, j), row(s, j)
        for g in range(1, N_EXPERT_GROUPS):
            sel = bg == g
            vb = jnp.where(sel, row(bz, EXPERTS_PER_GROUP * g + j), vb)
            vs = jnp.where(sel, row(s, EXPERTS_PER_GROUP * g + j), vs)
        cb.append(vb)
        cs.append(vs)
    i1 = jnp.zeros((1, tm), I32)
    b1, s1 = cb[0], cs[0]
    for j in range(1, EXPERTS_PER_GROUP):
        upd = cb[j] > b1
        i1 = jnp.where(upd, j, i1)
        b1 = jnp.where(upd, cb[j], b1)
        s1 = jnp.where(upd, cs[j], s1)
    i2 = jnp.zeros((1, tm), I32)
    b2 = jnp.full((1, tm), -jnp.inf, F32)
    s2 = jnp.zeros((1, tm), F32)
    for j in range(EXPERTS_PER_GROUP):
        cand = jnp.where(i1 == j, -jnp.inf, cb[j])
        upd = cand > b2
        i2 = jnp.where(upd, j, i2)
        b2 = jnp.where(upd, cand, b2)
        s2 = jnp.where(upd, cs[j], s2)
    den = s1 + s2
    w1 = s1 / den
    w2 = s2 / den
    lo = jnp.minimum(i1, i2)
    hi = jnp.maximum(i1, i2)
    pidx = jnp.where(lo == 0, hi - 1,
                     jnp.where(lo == 1, jnp.where(hi == 3, 3, 4), 5))
    first = jnp.where(pidx < 3, 0, jnp.where(pidx < 5, 1, 3))
    wa = jnp.where(i1 == first, w1, w2)
    wb = jnp.where(i1 == first, w2, w1)
    q = len(PAIRS) * bg + pidx
    qio = lax.broadcasted_iota(I32, (N_CLASS_ROWS, tm), 0)
    oh = qio == q
    ohf = jnp.where(oh, 1.0, 0.0)
    cum = jnp.dot(ohf.astype(BF16), tri, preferred_element_type=F32) + carry
    rank = jnp.sum(jnp.where(oh, cum, 0.0), axis=0, keepdims=True)
    new_carry = carry + jnp.sum(ohf, axis=1, keepdims=True)
    return q, rank.astype(I32), wa, wb, new_carry


def _post_mixer(m, x, g1, nf, sc2, sh2, rwt_ref, rb_ref, tri_ref,
                x1_ref, h2_ref, ri_ref, rw_ref, cnt_ref, carry, valid=None):
    tm = x.shape[0]
    x1 = x + _per_row(m, g1, lambda a, b: a * b)
    x1_ref[...] = x1
    h2 = _norm_mod(x1, nf, sc2, sh2)
    logits_t = lax.dot_general(rwt_ref[...], h2, (((1,), (1,)), ((), ())),
                               precision=HIGHEST, preferred_element_type=F32)
    q, rank, wa, wb, nc = _route(logits_t, rb_ref[...], tri_ref[...],
                                 carry[:, 0:1])
    _store_slabs(h2, h2_ref)
    rw_ref[0:1, :] = wa
    rw_ref[1:2, :] = wb
    rw_ref[2:8, :] = jnp.zeros((6, tm), F32)
    ri_ref[0:1, :] = q
    ri_ref[1:2, :] = rank
    ri_ref[2:8, :] = jnp.zeros((6, tm), I32)
    if valid is not None:
        nc = jnp.where(valid, nc, carry[:, 0:1])
    carry[...] = jnp.broadcast_to(nc, carry.shape)
    cnt_ref[...] = carry[...]


def _ada_kernel(cond_ref, w_ref, b_ref, o_ref):
    c = cond_ref[...]
    s = c * _sigmoid(c)
    o_ref[...] = jnp.dot(s, w_ref[...], precision=HIGHEST,
                         preferred_element_type=F32) + b_ref[...]


def _ada_table(cond, w_ada, b_ada):
    tn = 1536
    n_out = N_MOD * D_MODEL
    return pl.pallas_call(
        _ada_kernel,
        out_shape=jax.ShapeDtypeStruct((DEPTH, N_COND, n_out), F32),
        grid=(DEPTH, n_out // tn),
        in_specs=[
            pl.BlockSpec((N_COND, D_MODEL), lambda l, j: (0, 0)),
            pl.BlockSpec((None, D_MODEL, tn), lambda l, j: (l, 0, j)),
            pl.BlockSpec((None, 1, tn), lambda l, j: (l, 0, j)),
        ],
        out_specs=pl.BlockSpec((None, N_COND, tn), lambda l, j: (l, 0, j)),
        compiler_params=_cparams(("arbitrary", "arbitrary")),
        name="ada_table",
    )(cond, w_ada, b_ada.reshape(DEPTH, 1, n_out))


def _zoh_kernel(lr_ref, li_ref, ldt_ref, br_ref, bi_ref,
                are_ref, aim_ref, bbre_ref, bbim_ref):
    lr = lr_ref[...]
    li = li_ref[...]
    dt = jnp.exp(ldt_ref[...])
    mag = jnp.exp(lr * dt)
    a_re = mag * jnp.cos(li * dt)
    a_im = mag * jnp.sin(li * dt)
    den = lr * lr + li * li
    nr = a_re - 1.0
    f_re = (nr * lr + a_im * li) / den
    f_im = (a_im * lr - nr * li) / den
    br = br_ref[...]
    bi = bi_ref[...]
    are_ref[...] = a_re
    aim_ref[...] = a_im
    bbre_ref[...] = f_re * br - f_im * bi
    bbim_ref[...] = f_re * bi + f_im * br


def _zoh(lam_re, lam_im, log_dt, b_re, b_im):
    shape = b_re.shape
    flat = (shape[0] * shape[1] * shape[2] * shape[3] // LANES, LANES)
    bc = lambda a: jnp.broadcast_to(a, shape).reshape(flat)
    args = (bc(lam_re[..., None]), bc(lam_im[..., None]),
            bc(log_dt[:, :, None, None]), b_re.reshape(flat), b_im.reshape(flat))
    outs = pl.pallas_call(
        _zoh_kernel,
        out_shape=[jax.ShapeDtypeStruct(flat, F32)] * 4,
        name="zoh_discretize",
    )(*args)
    a_re, a_im, bb_re, bb_im = (o.reshape(shape) for o in outs)
    return a_re[..., 0], a_im[..., 0], bb_re, bb_im


def _stream_specs(xp, xs, n_p_tiles):
    bp, bs = xp.shape[0], xs.shape[0]
    return [
        pl.BlockSpec((bp, TOKEN_TILE // bp, D_MODEL),
                     lambda i: (0, jnp.minimum(i, n_p_tiles - 1), 0)),
        pl.BlockSpec((bs, TOKEN_TILE // bs, D_MODEL),
                     lambda i: (0, jnp.maximum(i - n_p_tiles, 0), 0)),
    ]


def _ssm_in_kernel(xp_ref, xs_ref, g_ref, sc_ref, sh_ref, w_ref, u_ref, scr,
                   *, n_p_tiles):
    def run(x_ref):
        x = _to_time_major(x_ref, scr, x_ref.shape[0])
        h = _norm_mod(x, g_ref[...], sc_ref[...], sh_ref[...])
        u_ref[...] = jnp.dot(h.astype(BF16), w_ref[...],
                             preferred_element_type=F32)

    i = pl.program_id(0)
    pl.when(i < n_p_tiles)(lambda: run(xp_ref))
    pl.when(i >= n_p_tiles)(lambda: run(xs_ref))


def _ssm_in(xp, xs, g, modpat, w_in_bf):
    n = (xp.shape[0] * xp.shape[1] + xs.shape[0] * xs.shape[1])
    tm = TOKEN_TILE
    n_p_tiles = xp.shape[0] * xp.shape[1] // tm
    p = modpat.shape[1]
    pat = lambda i: jnp.where(i < n_p_tiles, 0, 1)
    mod = lambda k: pl.BlockSpec((None, p, D_MODEL), lambda i: (pat(i), 0, k))
    return pl.pallas_call(
        functools.partial(_ssm_in_kernel, n_p_tiles=n_p_tiles),
        out_shape=jax.ShapeDtypeStruct((n, D_MODEL), F32),
        grid=(n // tm,),
        in_specs=_stream_specs(xp, xs, n_p_tiles) + [
            pl.BlockSpec((1, D_MODEL), lambda i: (0, 0)),
            mod(1), mod(0),
            pl.BlockSpec((D_MODEL, D_MODEL), lambda i: (0, 0)),
        ],
        out_specs=pl.BlockSpec((tm, D_MODEL), lambda i: (i, 0)),
        scratch_shapes=[pltpu.VMEM((LANE_TILES, tm, LANES), F32)],
        compiler_params=_cparams(("arbitrary",)),
        name="ssm_in",
    )(xp, xs, g, modpat, modpat, w_in_bf)


def _scan_kernel(u_ref, bt_ref, cre_ref, cim_ref, are_ref, aim_ref,
                 h0re_ref, h0im_ref, y_ref, fre_ref, fim_ref,
                 *scratch, n_p_chunks, bp, bs):
    xre = scratch[:LANE_TILES]
    xim = scratch[LANE_TILES:2 * LANE_TILES]
    hre, him = scratch[2 * LANE_TILES:]
    d = pl.program_id(0)
    c = pl.program_id(1)
    cols = N_STATE // LANE_TILES
    rows = u_ref.shape[0]

    def chunk(batch, reverse):
        steps = rows // batch

        def b_proj(s):
            u = u_ref[:, LANES * s:LANES * (s + 1)].astype(BF16)
            xt = jnp.dot(u, bt_ref[s], preferred_element_type=F32)
            xre[s][...] = xt[:, :cols]
            xim[s][...] = xt[:, cols:]

        def scan(s):
            sl = slice(cols * s, cols * (s + 1))
            ar = jnp.broadcast_to(are_ref[:, sl], (batch, cols))
            ai = jnp.broadcast_to(aim_ref[:, sl], (batch, cols))
            hr = hre[0:batch, sl]
            hi = him[0:batch, sl]
            for t in range(steps):
                tt = steps - 1 - t if reverse else t
                r = slice(tt * batch, (tt + 1) * batch)
                nr = ar * hr - ai * hi + xre[s][r, :]
                ni = ar * hi + ai * hr + xim[s][r, :]
                xre[s][r, :] = nr
                xim[s][r, :] = ni
                hr, hi = nr, ni
            hre[0:batch, sl] = hr
            him[0:batch, sl] = hi

        def c_proj(s):
            y_ref[:, LANES * s:LANES * (s + 1)] = (
                jnp.dot(xre[s][...].astype(BF16), cre_ref[s],
                        preferred_element_type=F32)
                - jnp.dot(xim[s][...].astype(BF16), cim_ref[s],
                          preferred_element_type=F32))

        for stage in range(LANE_TILES + 2):
            if stage < LANE_TILES:
                b_proj(stage)
            if 0 <= stage - 1 < LANE_TILES:
                scan(stage - 1)
            if 0 <= stage - 2 < LANE_TILES:
                c_proj(stage - 2)

    @pl.when(c == 0)
    def _():
        hre[...] = jnp.zeros(hre.shape, F32)
        him[...] = jnp.zeros(him.shape, F32)

    @pl.when(c == n_p_chunks)
    def _():
        hre[0:bs, :] = h0re_ref[...]
        him[0:bs, :] = h0im_ref[...]

    in_p = c < n_p_chunks
    in_s = jnp.logical_not(in_p)
    fwd = d == 0
    bwd = jnp.logical_not(fwd)
    pl.when(jnp.logical_and(in_p, fwd))(lambda: chunk(bp, False))
    pl.when(jnp.logical_and(in_p, bwd))(lambda: chunk(bp, True))
    pl.when(jnp.logical_and(in_s, fwd))(lambda: chunk(bs, False))
    pl.when(jnp.logical_and(in_s, bwd))(lambda: chunk(bs, True))

    @pl.when(c == n_p_chunks - 1)
    def _():
        fre_ref[...] = hre[0:bp, :]
        fim_ref[...] = him[0:bp, :]


def _ssm_scan(u, bt, ct_re, ct_im, a_re, a_im, h0_re, h0_im, *, bp, lp, bs, ls):
    n = u.shape[0]
    rows = TOKEN_TILE
    n_p_chunks = bp * lp // rows
    n_s_chunks = bs * ls // rows

    def chunk(d, c):
        in_p = jnp.where(d == 0, c, n_p_chunks - 1 - c)
        cs = c - n_p_chunks
        in_s = n_p_chunks + jnp.where(d == 0, cs, n_s_chunks - 1 - cs)
        return jnp.where(c < n_p_chunks, in_p, in_s)

    cols = N_STATE // LANE_TILES
    dspec = lambda shape: pl.BlockSpec((None,) + shape,
                                       lambda d, c: (d,) + (0,) * len(shape))
    kern = functools.partial(_scan_kernel, n_p_chunks=n_p_chunks, bp=bp, bs=bs)
    return pl.pallas_call(
        kern,
        out_shape=[jax.ShapeDtypeStruct((2, n, D_MODEL), F32),
                   jax.ShapeDtypeStruct((2, bp, N_STATE), F32),
                   jax.ShapeDtypeStruct((2, bp, N_STATE), F32)],
        grid=(2, n_p_chunks + n_s_chunks),
        in_specs=[
            pl.BlockSpec((rows, D_MODEL), lambda d, c: (chunk(d, c), 0)),
            dspec((LANE_TILES, LANES, 2 * cols)),
            dspec((LANE_TILES, cols, LANES)),
            dspec((LANE_TILES, cols, LANES)),
            dspec((1, N_STATE)),
            dspec((1, N_STATE)),
            dspec((bs, N_STATE)),
            dspec((bs, N_STATE)),
        ],
        out_specs=[
            pl.BlockSpec((None, rows, D_MODEL), lambda d, c: (d, chunk(d, c), 0)),
            dspec((bp, N_STATE)),
            dspec((bp, N_STATE)),
        ],
        scratch_shapes=(
            [pltpu.VMEM((rows, cols), F32)] * (2 * LANE_TILES)
            + [pltpu.VMEM((max(bp, bs), N_STATE), F32)] * 2),
        compiler_params=_cparams(("arbitrary", "arbitrary")),
        name="ssm_scan",
    )(u, bt, ct_re, ct_im, a_re, a_im, h0_re, h0_im)


def _ssm_out_kernel(u_ref, y0_ref, y1_ref, dsk_ref, wglu_ref, xp_ref, xs_ref,
                    g1_ref, nf_ref, sc2_ref, sh2_ref, rwt_ref, rb_ref, tri_ref,
                    x1_ref, h2_ref, ri_ref, rw_ref, cnt_ref, scr, carry,
                    *, n_p_tiles):
    i = pl.program_id(0)

    @pl.when(i == 0)
    def _():
        carry[...] = jnp.zeros(carry.shape, F32)

    def run(x_ref):
        x = _to_time_major(x_ref, scr, x_ref.shape[0])
        y = u_ref[...] * dsk_ref[...] + y0_ref[...] + y1_ref[...]
        ge = _gelu_tanh(y).astype(BF16)
        vg = jnp.dot(ge, wglu_ref[...], preferred_element_type=F32)
        m = vg[:, :D_MODEL] * _sigmoid(vg[:, D_MODEL:])
        _post_mixer(m, x, g1_ref[...], nf_ref[...], sc2_ref[...], sh2_ref[...],
                    rwt_ref, rb_ref, tri_ref, x1_ref, h2_ref, ri_ref, rw_ref,
                    cnt_ref, carry)

    pl.when(i < n_p_tiles)(lambda: run(xp_ref))
    pl.when(i >= n_p_tiles)(lambda: run(xs_ref))


def _route_out_shapes(n):
    return [jax.ShapeDtypeStruct((n, D_MODEL), F32),
            jax.ShapeDtypeStruct((n * SLAB, LANES), F32),
            jax.ShapeDtypeStruct((SUBLANES, n), I32),
            jax.ShapeDtypeStruct((SUBLANES, n), F32),
            jax.ShapeDtypeStruct((N_CLASS_ROWS, LANES), F32)]


def _ssm_out(u, y, d_skip, w_glu_bf, xp, xs, modpat, norm_ffn, rwt, rb, tri):
    n = u.shape[0]
    tm = TOKEN_TILE
    n_p_tiles = xp.shape[0] * xp.shape[1] // tm
    p = modpat.shape[1]
    pat = lambda i: jnp.where(i < n_p_tiles, 0, 1)
    mod = lambda k: pl.BlockSpec((None, p, D_MODEL), lambda i: (pat(i), 0, k))
    full = lambda shape: pl.BlockSpec(shape, lambda i: (0,) * len(shape))
    rowblk = pl.BlockSpec((tm, D_MODEL), lambda i: (i, 0))
    return pl.pallas_call(
        functools.partial(_ssm_out_kernel, n_p_tiles=n_p_tiles),
        out_shape=_route_out_shapes(n),
        grid=(n // tm,),
        in_specs=[
            rowblk,
            pl.BlockSpec((None, tm, D_MODEL), lambda i: (0, i, 0)),
            pl.BlockSpec((None, tm, D_MODEL), lambda i: (1, i, 0)),
            full((1, D_MODEL)),
            full((D_MODEL, 2 * D_MODEL)),
        ] + _stream_specs(xp, xs, n_p_tiles) + [
            mod(2), full((1, D_MODEL)), mod(4), mod(3),
            full((N_EXPERTS, D_MODEL)), full((N_EXPERTS, 1)), full((tm, tm)),
        ],
        out_specs=[rowblk,
                   pl.BlockSpec((tm * SLAB, LANES), lambda i: (i, 0)),
                   pl.BlockSpec((SUBLANES, tm), lambda i: (0, i)),
                   pl.BlockSpec((SUBLANES, tm), lambda i: (0, i)),
                   full((N_CLASS_ROWS, LANES))],
        scratch_shapes=[pltpu.VMEM((LANE_TILES, tm, LANES), F32),
                        pltpu.VMEM((N_CLASS_ROWS, LANES), F32)],
        compiler_params=_cparams(("arbitrary",)),
        name="ssm_out",
    )(u, y, y, d_skip, w_glu_bf, xp, xs, modpat, norm_ffn, modpat, modpat,
      rwt, rb, tri)


def _fnet_kernel(xp_ref, xpp_ref, xs_ref, nm_ref, sc1_ref, sh1_ref, cs_ref,
                 dftp_ref, dfts_ref, wout_ref, g1_ref, nf_ref, sc2_ref, sh2_ref,
                 rwt_ref, rb_ref, tri_ref, x1_ref, h2_ref, ri_ref, rw_ref, cnt_ref,
                 ucs, ybuf, carry, *, bp, tiles_s):
    s = pl.program_id(0)
    n_groups = D_MODEL // FNET_GROUP
    tr = FNET_TILE
    per = tiles_s + 1
    wslot = s % 2

    @pl.when(s == 0)
    def _():
        carry[...] = jnp.zeros(carry.shape, F32)
        ybuf[...] = jnp.zeros(ybuf.shape, F32)

    def stage1(x_ref, seq):
        def body(r, _):
            rows = pl.ds(pl.multiple_of(r * tr, tr), tr)
            h = _norm_mod(x_ref[rows, :], nm_ref[...], sc1_ref[...],
                          sh1_ref[...]).astype(BF16)
            for k in range(n_groups):
                cols = slice(FNET_GROUP * k, FNET_GROUP * (k + 1))
                t = jnp.dot(h[:, cols], cs_ref[...], preferred_element_type=F32)
                ucs[rows, cols] = t[:, :FNET_GROUP].astype(BF16)
                ucs[pl.ds(pl.multiple_of(seq + r * tr, tr), tr), cols] = (
                    t[:, FNET_GROUP:].astype(BF16))
            return 0
        lax.fori_loop(0, seq // tr, body, 0)

    def step(dft_ref, seq, xrow, valid):
        scale = float((seq * FNET_GROUP) ** -0.5)
        m = jnp.dot(ybuf[1 - wslot].astype(BF16), wout_ref[...],
                    preferred_element_type=F32)
        _post_mixer(m, xrow, g1_ref[...], nf_ref[...], sc2_ref[...],
                    sh2_ref[...], rwt_ref, rb_ref, tri_ref, x1_ref, h2_ref,
                    ri_ref, rw_ref, cnt_ref, carry, valid)
        ybuf[wslot] = jnp.dot(dft_ref[...], ucs[0:2 * seq, :],
                              preferred_element_type=F32) * scale

    @pl.when(s <= bp)
    def _():
        stage1(xp_ref, xp_ref.shape[0])
        step(dftp_ref, xp_ref.shape[0], xpp_ref[...], s >= 1)

    @pl.when(s > bp)
    def _():
        j = (s - bp - 1) % per
        pl.when(j == 0)(lambda: stage1(xs_ref, xs_ref.shape[0]))
        prev = jnp.maximum(j - 1, 0)
        xrow = xs_ref[pl.ds(pl.multiple_of(prev * tr, tr), tr), :]
        step(dfts_ref, xs_ref.shape[0], xrow, j >= 1)


def _dft_table(seq):
    inner = 64
    k = jnp.arange(seq, dtype=I32)[:, None]
    t1 = jnp.arange(seq // inner, dtype=I32)[None, :] * inner
    t2 = jnp.arange(inner, dtype=I32)[None, :]
    ang = lambda t: ((k * t) % seq).astype(F32) * (2.0 * jnp.pi / seq)
    ca, sa = jnp.cos(ang(t1))[:, :, None], jnp.sin(ang(t1))[:, :, None]
    cb, sb = jnp.cos(ang(t2))[:, None, :], jnp.sin(ang(t2))[:, None, :]
    cos = (ca * cb - sa * sb).reshape(seq, seq)
    sin = (sa * cb + ca * sb).reshape(seq, seq)
    return jnp.concatenate([cos, -sin], axis=1).astype(BF16)


def _fnet(xp, xs, norm_mix, modpat, w_out_bf, norm_ffn, rwt, rb):
    bp, lp, _ = xp.shape
    bs, ls, _ = xs.shape
    tr = FNET_TILE
    assert lp == tr
    tiles_s = ls // tr
    per = tiles_s + 1
    n = bp * lp + bs * ls
    p = modpat.shape[1]
    kc = jnp.arange(FNET_GROUP, dtype=I32)
    angc = ((kc[:, None] * kc[None, :]) % FNET_GROUP).astype(F32) * (
        2.0 * jnp.pi / FNET_GROUP)
    cs = jnp.concatenate([jnp.cos(angc), jnp.sin(angc)], axis=1).astype(BF16)
    tri = jnp.triu(jnp.ones((tr, tr), BF16), k=1)

    q = lambda s: jnp.maximum(s - bp - 1, 0)
    sb = lambda s: jnp.minimum(q(s) // per, bs - 1)
    sj = lambda s: q(s) % per
    pat = lambda s: jnp.where(s <= bp, 0, 1 + sb(s))
    blk = lambda s: jnp.where(s <= bp, jnp.maximum(s - 1, 0),
                              bp + sb(s) * tiles_s + jnp.maximum(sj(s) - 1, 0))
    mod = lambda kk: pl.BlockSpec((None, p, D_MODEL), lambda s: (pat(s), 0, kk))
    full = lambda shape: pl.BlockSpec(shape, lambda s: (0,) * len(shape))
    kern = functools.partial(_fnet_kernel, bp=bp, tiles_s=tiles_s)
    return pl.pallas_call(
        kern,
        out_shape=_route_out_shapes(n),
        grid=(bp + 1 + bs * per,),
        in_specs=[
            pl.BlockSpec((None, lp, D_MODEL),
                         lambda s: (jnp.minimum(s, bp - 1), 0, 0)),
            pl.BlockSpec((None, lp, D_MODEL),
                         lambda s: (jnp.clip(s - 1, 0, bp - 1), 0, 0)),
            pl.BlockSpec((None, ls, D_MODEL), lambda s: (sb(s), 0, 0)),
            full((1, D_MODEL)), mod(1), mod(0),
            full((FNET_GROUP, 2 * FNET_GROUP)),
            full((tr, 2 * lp)),
            pl.BlockSpec((tr, 2 * ls),
                         lambda s: (jnp.minimum(sj(s), tiles_s - 1), 0)),
            full((D_MODEL, D_MODEL)),
            mod(2), full((1, D_MODEL)), mod(4), mod(3),
            full((N_EXPERTS, D_MODEL)), full((N_EXPERTS, 1)), full((tr, tr)),
        ],
        out_specs=[pl.BlockSpec((tr, D_MODEL), lambda s: (blk(s), 0)),
                   pl.BlockSpec((tr * SLAB, LANES), lambda s: (blk(s), 0)),
                   pl.BlockSpec((SUBLANES, tr), lambda s: (0, blk(s))),
                   pl.BlockSpec((SUBLANES, tr), lambda s: (0, blk(s))),
                   full((N_CLASS_ROWS, LANES))],
        scratch_shapes=[pltpu.VMEM((2 * ls, D_MODEL), BF16),
                        pltpu.VMEM((2, tr, D_MODEL), F32),
                        pltpu.VMEM((N_CLASS_ROWS, LANES), F32)],
        compiler_params=_cparams(("arbitrary",)),
        name="fnet",
    )(xp, xp, xs, norm_mix, modpat, modpat, cs, _dft_table(lp), _dft_table(ls),
      w_out_bf, modpat, norm_ffn, modpat, modpat, rwt, rb, tri)


def _invert_kernel(slot_ref, ends_ref, gsrc_ref, sdst_ref):
    n = slot_ref.shape[0]
    n_slots = gsrc_ref.shape[0]
    t = FFN_TILE
    unroll = 8

    def pad_tile(base):
        def pad(j, _):
            for k in range(unroll):
                r = j * unroll + k
                gsrc_ref[base + r] = 0
                sdst_ref[base + r] = n + r
            return 0
        lax.fori_loop(0, t // unroll, pad, 0)

    for q in range(N_PAIRS):
        pad_tile(jnp.maximum(ends_ref[q] - t, 0))

    def unused(b, _):
        pad_tile(b * t)
        return 0
    lax.fori_loop(ends_ref[N_PAIRS - 1] // t, n_slots // t, unused, 0)

    def body(j, _):
        for k in range(unroll):
            t = j * unroll + k
            s = slot_ref[t]
            gsrc_ref[s] = t
            sdst_ref[s] = t
        return 0
    lax.fori_loop(0, n // unroll, body, 0)


def _invert(slot, ends, n_slots):
    smem = pl.BlockSpec(memory_space=pltpu.SMEM)
    return pl.pallas_call(
        _invert_kernel,
        out_shape=[jax.ShapeDtypeStruct((n_slots,), I32)] * 2,
        in_specs=[smem, smem],
        out_specs=[smem, smem],
        name="moe_invert",
    )(slot, ends)


def _ffn_kernel(tea_ref, teb_ref, nu_ref, gsrc_ref, sdst_ref, wa_ref, wb_ref,
                h_hbm, wga, wua, wda, wgb, wub, wdb, y_hbm,
                xbuf0, xbuf1, obuf0, obuf1, wcola, wcolb, gsem, ssem):
    del tea_ref, teb_ref
    i = pl.program_id(0)
    nu = nu_ref[0]
    t = FFN_TILE

    def slab(ref, r):
        return ref.at[pl.ds(pl.multiple_of(r * SLAB, SLAB), SLAB), :]

    def gather_start(tile, xbuf, sem):
        base = tile * t
        for r in range(t):
            pltpu.make_async_copy(slab(h_hbm, gsrc_ref[base + r]), slab(xbuf, r),
                                  sem).start(priority=r % 2)

    def gather_wait(xbuf, sem):
        pltpu.make_async_copy(h_hbm.at[pl.ds(0, t * SLAB), :], xbuf, sem).wait()

    def scatter_start(tile, obuf, sem):
        base = tile * t
        for r in range(t):
            pltpu.make_async_copy(slab(obuf, r), slab(y_hbm, sdst_ref[base + r]),
                                  sem).start(priority=r % 2)

    def scatter_wait(obuf, sem):
        pltpu.make_async_copy(obuf, y_hbm.at[pl.ds(0, t * SLAB), :], sem).wait()

    def expert(xb, wg, wu, wd):
        g = jnp.dot(xb, wg[...], preferred_element_type=F32)
        u = jnp.dot(xb, wu[...], preferred_element_type=F32)
        a = (g * _sigmoid(g)) * u
        return jnp.dot(a.astype(BF16), wd[...], preferred_element_type=F32)

    def step(xc, xn, oc, op, gc, gn, sc, sp):
        pl.when(i == 0)(lambda: gather_start(0, xc, gc))
        pl.when(i >= 2)(lambda: scatter_wait(oc, sc))
        gather_wait(xc, gc)
        pl.when(i + 1 < nu)(lambda: gather_start(i + 1, xn, gn))
        pl.when(i >= 1)(lambda: scatter_start(i - 1, op, sp))
        for r in range(t):
            tok = gsrc_ref[i * t + r]
            wcola[r:r + 1, :] = jnp.full((1, LANES), wa_ref[tok], F32)
            wcolb[r:r + 1, :] = jnp.full((1, LANES), wb_ref[tok], F32)
        xb = _rows_from_slabs(xc, t).astype(BF16)
        ya = expert(xb, wga, wua, wda)
        yb = expert(xb, wgb, wub, wdb)
        y = wcola[:, 0:1] * ya + wcolb[:, 0:1] * yb
        for k, piece in enumerate(_lane_tiles(y)):
            oc[pl.ds(k, t, stride=SLAB), :] = piece

    def drain(oc, op, sc, sp):
        pl.when(i >= 2)(lambda: scatter_wait(oc, sc))
        scatter_start(i - 1, op, sp)
        scatter_wait(op, sp)
        spare_row = y_hbm.shape[0] - t * SLAB
        oc[...] = jnp.zeros(oc.shape, F32)
        spare = pltpu.make_async_copy(
            oc, y_hbm.at[pl.ds(spare_row, t * SLAB), :], sc)
        spare.start()
        spare.wait()

    even = (i % 2) == 0
    odd = jnp.logical_not(even)
    g0, g1, s0, s1 = gsem.at[0], gsem.at[1], ssem.at[0], ssem.at[1]
    pl.when(jnp.logical_and(i < nu, even))(
        lambda: step(xbuf0, xbuf1, obuf0, obuf1, g0, g1, s0, s1))
    pl.when(jnp.logical_and(i < nu, odd))(
        lambda: step(xbuf1, xbuf0, obuf1, obuf0, g1, g0, s1, s0))
    pl.when(jnp.logical_and(i == nu, even))(
        lambda: drain(obuf0, obuf1, s0, s1))
    pl.when(jnp.logical_and(i == nu, odd))(
        lambda: drain(obuf1, obuf0, s1, s0))


def _expert_ffn(tea, teb, n_used, gsrc, sdst, wa_tok, wb_tok, h2, wg, wu, wd,
                layer):
    n_tok = h2.shape[0] // SLAB
    max_tiles = gsrc.shape[0] // FFN_TILE
    wa = lambda a, b: pl.BlockSpec(
        (None, None, a, b), lambda i, ta, tb, *_: (layer, ta[i], 0, 0))
    wb = lambda a, b: pl.BlockSpec(
        (None, None, a, b), lambda i, ta, tb, *_: (layer, tb[i], 0, 0))
    anyspec = pl.BlockSpec(memory_space=pl.ANY)
    return pl.pallas_call(
        _ffn_kernel,
        out_shape=jax.ShapeDtypeStruct(((n_tok + FFN_TILE) * SLAB, LANES), F32),
        grid_spec=pltpu.PrefetchScalarGridSpec(
            num_scalar_prefetch=7,
            grid=(max_tiles + 1,),
            in_specs=[anyspec,
                      wa(D_MODEL, D_EXPERT), wa(D_MODEL, D_EXPERT),
                      wa(D_EXPERT, D_MODEL),
                      wb(D_MODEL, D_EXPERT), wb(D_MODEL, D_EXPERT),
                      wb(D_EXPERT, D_MODEL)],
            out_specs=anyspec,
            scratch_shapes=[pltpu.VMEM((FFN_TILE * SLAB, LANES), F32)] * 4 + [
                            pltpu.VMEM((FFN_TILE, LANES), F32),
                            pltpu.VMEM((FFN_TILE, LANES), F32),
                            pltpu.SemaphoreType.DMA((2,)),
                            pltpu.SemaphoreType.DMA((2,))],
        ),
        compiler_params=_cparams(("arbitrary",)),
        name="moe_ffn",
    )(tea, teb, n_used, gsrc, sdst, wa_tok, wb_tok, h2, wg, wu, wd, wg, wu, wd)


def _moe(h2, ri, rw, cnt, wg_bf, wu_bf, wd_bf, layer):
    n = h2.shape[0] // SLAB
    max_tiles = n // FFN_TILE + N_PAIRS
    n_slots = max_tiles * FFN_TILE
    counts = cnt[:N_PAIRS, 0].astype(I32)
    padded = ((counts + FFN_TILE - 1) // FFN_TILE) * FFN_TILE
    ends = jnp.cumsum(padded)
    offs = ends - padded
    q, rank = ri[0], ri[1]
    cls = jnp.arange(N_PAIRS, dtype=I32)
    slot = rank + jnp.sum(jnp.where(q[None, :] == cls[:, None], offs[:, None], 0),
                          axis=0)
    n_used = ends[-1] // FFN_TILE
    tile = jnp.arange(max_tiles + 1, dtype=I32)
    tq = jnp.sum((tile[:, None] * FFN_TILE >= ends[None, :]).astype(I32), axis=1)
    tq_last = jnp.sum(((n_used - 1) * FFN_TILE >= ends).astype(I32))
    tq = jnp.where(tile < n_used, tq, tq_last)
    pa = jnp.array([a for a, _ in PAIRS], I32)
    pb = jnp.array([b for _, b in PAIRS], I32)
    grp, pidx = tq // len(PAIRS), tq % len(PAIRS)
    tea = EXPERTS_PER_GROUP * grp + jnp.take(pa, pidx)
    teb = EXPERTS_PER_GROUP * grp + jnp.take(pb, pidx)
    gsrc, sdst = _invert(slot, ends, n_slots)
    return _expert_ffn(tea, teb, n_used.reshape(1), gsrc, sdst, rw[0], rw[1], h2,
                       wg_bf, wu_bf, wd_bf, layer)


def _moe_out_tm_kernel(x1_ref, y_ref, g2_ref, op_ref, os_ref, scr, *, n_p_tiles):
    i = pl.program_id(0)
    y = _rows_from_slabs(y_ref, x1_ref.shape[0])
    x2 = x1_ref[...] + _per_row(y, g2_ref[...], lambda a, b: a * b)
    pl.when(i < n_p_tiles)(
        lambda: _from_time_major(x2, op_ref, scr, op_ref.shape[0]))
    pl.when(i >= n_p_tiles)(
        lambda: _from_time_major(x2, os_ref, scr, os_ref.shape[0]))


def _moe_out_tm(x1, y, modpat, shape_p, shape_s):
    n = x1.shape[0]
    tm = TOKEN_TILE
    n_p_tiles = shape_p[0] * shape_p[1] // tm
    p = modpat.shape[1]
    rowblk = pl.BlockSpec((tm, D_MODEL), lambda i: (i, 0))
    stream = _stream_specs(jax.ShapeDtypeStruct(shape_p, F32),
                           jax.ShapeDtypeStruct(shape_s, F32), n_p_tiles)
    return pl.pallas_call(
        functools.partial(_moe_out_tm_kernel, n_p_tiles=n_p_tiles),
        out_shape=[jax.ShapeDtypeStruct(shape_p, F32),
                   jax.ShapeDtypeStruct(shape_s, F32)],
        grid=(n // tm,),
        in_specs=[rowblk,
                  pl.BlockSpec((tm * SLAB, LANES), lambda i: (i, 0)),
                  pl.BlockSpec((None, p, D_MODEL),
                               lambda i: (jnp.where(i < n_p_tiles, 0, 1), 0, 5))],
        out_specs=stream,
        scratch_shapes=[pltpu.VMEM((LANE_TILES, tm, LANES), F32)],
        compiler_params=_cparams(("arbitrary",)),
        name="moe_out_tm",
    )(x1, y, modpat)


def _moe_out_final_kernel(x1_ref, y_ref, g2_ref, nfin_ref, op_ref, os_ref,
                          *, n_p_tiles):
    i = pl.program_id(0)
    y = _rows_from_slabs(y_ref, x1_ref.shape[0])
    x2 = x1_ref[...] + _per_row(y, g2_ref[...], lambda a, b: a * b)
    out = _rms(x2) * nfin_ref[...]

    @pl.when(i < n_p_tiles)
    def _():
        op_ref[...] = out

    @pl.when(i >= n_p_tiles)
    def _():
        os_ref[...] = out


def _moe_out_final(x1, y, modpat, norm_final, n_p, rows_per_request):
    n = x1.shape[0]
    tm = TOKEN_TILE
    n_p_tiles = n_p // tm
    p = modpat.shape[1]
    rowblk = pl.BlockSpec((tm, D_MODEL), lambda i: (i, 0))
    pat = lambda i: jnp.where(
        i < n_p_tiles, 0, 1 + (i - n_p_tiles) // (rows_per_request // tm))
    return pl.pallas_call(
        functools.partial(_moe_out_final_kernel, n_p_tiles=n_p_tiles),
        out_shape=[jax.ShapeDtypeStruct((n_p, D_MODEL), F32),
                   jax.ShapeDtypeStruct((n - n_p, D_MODEL), F32)],
        grid=(n // tm,),
        in_specs=[rowblk,
                  pl.BlockSpec((tm * SLAB, LANES), lambda i: (i, 0)),
                  pl.BlockSpec((None, p, D_MODEL), lambda i: (pat(i), 0, 5)),
                  pl.BlockSpec((1, D_MODEL), lambda i: (0, 0))],
        out_specs=[
            pl.BlockSpec((tm, D_MODEL),
                         lambda i: (jnp.minimum(i, n_p_tiles - 1), 0)),
            pl.BlockSpec((tm, D_MODEL),
                         lambda i: (jnp.maximum(i - n_p_tiles, 0), 0))],
        compiler_params=_cparams(("arbitrary",)),
        name="moe_out_final",
    )(x1, y, modpat, norm_final)


def _block_diag_weights(bb_re, bb_im, c_re, c_im):
    lt = LANE_TILES
    gl = SSM_GROUPS // lt
    eye = jnp.eye(gl, dtype=F32)

    def in_map(bb):
        b5 = bb.reshape(2, lt, gl, STATE_DIM, SSM_GROUP)
        t = jnp.einsum("dinph,kn->dikhnp", b5, eye)
        return t.reshape(2, lt, gl * SSM_GROUP, gl * STATE_DIM)

    def out_map(cc):
        c5 = cc.reshape(2, lt, gl, SSM_GROUP, STATE_DIM)
        t = jnp.einsum("dikhp,kn->dikpnh", c5, eye)
        return t.reshape(2, lt, gl * STATE_DIM, gl * SSM_GROUP)

    bt = jnp.concatenate([in_map(bb_re), in_map(bb_im)], axis=-1).astype(BF16)
    return bt, out_map(c_re).astype(BF16), out_map(c_im).astype(BF16)


def kernel(x_prompt, x_sample, c, state_ssm_re, state_ssm_im, c_ctx, norm_mix, norm_ffn, w_ada, b_ada, ssm_w_in, ssm_lam_re, ssm_lam_im, ssm_log_dt, ssm_b_re, ssm_b_im, ssm_c_re, ssm_c_im, ssm_d, ssm_w_glu, fnet_w_out, router_w, router_b, moe_w_gate, moe_w_up, moe_w_down, norm_final):
    bp, lp, _ = x_prompt.shape
    bs, ls, _ = x_sample.shape
    n_p = bp * lp
    n_s = bs * ls
    n = n_p + n_s
    tm = TOKEN_TILE

    cond = jnp.zeros((N_COND, D_MODEL), F32).at[0].set(c_ctx).at[1:1 + bs].set(c)
    modtab = _ada_table(cond, w_ada, b_ada).reshape(DEPTH, N_COND, N_MOD, D_MODEL)

    rwt = router_w.T
    rb = router_b.reshape(N_EXPERTS, 1)
    row = lambda v: v.reshape(1, D_MODEL)

    period = max(bp, bs)
    pat_tm = jnp.stack([
        jnp.broadcast_to(modtab[0, 0], (period, N_MOD, D_MODEL)),
        jnp.tile(modtab[0, 1:1 + bs], (period // bs, 1, 1))])
    pat_tm = pat_tm.reshape(2, period, N_MOD * D_MODEL)

    a_re, a_im, bb_re, bb_im = _zoh(ssm_lam_re[0], ssm_lam_im[0], ssm_log_dt[0],
                                    ssm_b_re[0], ssm_b_im[0])
    bt, ct_re, ct_im = _block_diag_weights(bb_re, bb_im, ssm_c_re[0], ssm_c_im[0])
    a_re = a_re.reshape(2, 1, N_STATE)
    a_im = a_im.reshape(2, 1, N_STATE)

    u = _ssm_in(x_prompt, x_sample, row(norm_mix[0]), pat_tm,
                ssm_w_in[0].astype(BF16))
    h0s_re = state_ssm_re[:, 0].reshape(bs, 2, N_STATE).transpose(1, 0, 2)
    h0s_im = state_ssm_im[:, 0].reshape(bs, 2, N_STATE).transpose(1, 0, 2)
    y_scan, fin_re, fin_im = _ssm_scan(u, bt, ct_re, ct_im, a_re, a_im,
                                       h0s_re, h0s_im, bp=bp, lp=lp, bs=bs, ls=ls)
    tri = jnp.triu(jnp.ones((tm, tm), BF16), k=1)
    x1, h2, ri, rw, cnt = _ssm_out(u, y_scan, row(ssm_d[0]),
                               ssm_w_glu[0].astype(BF16), x_prompt, x_sample,
                               pat_tm, row(norm_ffn[0]), rwt, rb, tri)
    wg_bf, wu_bf, wd_bf = (w.astype(BF16) for w in (moe_w_gate, moe_w_up,
                                                    moe_w_down))
    y_moe = _moe(h2, ri, rw, cnt, wg_bf, wu_bf, wd_bf, 0)
    x2_p, x2_s = _moe_out_tm(x1, y_moe, pat_tm, x_prompt.shape, x_sample.shape)

    pat_bm = jnp.broadcast_to(modtab[1][:1 + bs, None],
                              (1 + bs, SUBLANES, N_MOD, D_MODEL))
    pat_bm = pat_bm.reshape(1 + bs, SUBLANES, N_MOD * D_MODEL)
    x3, h2, ri, rw, cnt = _fnet(x2_p, x2_s, row(norm_mix[1]), pat_bm,
                            fnet_w_out[0].astype(BF16), row(norm_ffn[1]), rwt, rb)
    y_moe = _moe(h2, ri, rw, cnt, wg_bf, wu_bf, wd_bf, 1)
    y_p, y_s = _moe_out_final(x3, y_moe, pat_bm, row(norm_final), n_p, ls)

    st = lambda f: f.transpose(1, 0, 2).reshape(bp, 1, 2, SSM_GROUPS, STATE_DIM)
    return (y_p.reshape(bp, lp, D_MODEL), y_s.reshape(bs, ls, D_MODEL),
            st(fin_re), st(fin_im))
```

```python
import functools

import jax
import jax.numpy as jnp
from jax import lax
from jax.experimental import pallas as pl
from jax.experimental.pallas import tpu as pltpu

F32 = jnp.float32
BF16 = jnp.bfloat16
I32 = jnp.int32
HIGHEST = lax.Precision.HIGHEST

D_MODEL = 1024
DEPTH = 2
SSM_GROUP = 16
SSM_GROUPS = 64
STATE_DIM = 64
N_STATE = SSM_GROUPS * STATE_DIM
FNET_GROUP = 128
N_EXPERTS = 16
N_EXPERT_GROUPS = 4
EXPERTS_PER_GROUP = 4
D_EXPERT = 1024
N_MOD = 6
EPS = 1e-6

LANES = 128
SUBLANES = 8
LANE_TILES = D_MODEL // LANES
VMEM_LIMIT = 56 * 1024 * 1024

TOKEN_TILE = 512
FNET_TILE = 256
FFN_TILE = 256
N_COND = 16

PAIRS = ((0, 1), (0, 2), (0, 3), (1, 3), (1, 2), (3, 2))
N_PAIRS = N_EXPERT_GROUPS * len(PAIRS)
N_CLASS_ROWS = 32
SLAB = SUBLANES


def _cparams(sem, vmem=VMEM_LIMIT):
    return pltpu.CompilerParams(dimension_semantics=sem, vmem_limit_bytes=vmem)


def _sigmoid(x):
    return 1.0 / (1.0 + jnp.exp(-x))


def _gelu_tanh(x):
    c = 0.7978845608028654
    return x * (0.5 * (1.0 + jnp.tanh(c * (x + 0.044715 * (x * x * x)))))


def _per_row(v, pat, fn):
    tm, d = v.shape
    p = pat.shape[0]
    return fn(v.reshape(tm // p, p, d), pat[None]).reshape(tm, d)


def _rms(x):
    ms = jnp.mean(x * x, axis=-1, keepdims=True)
    return x * lax.rsqrt(ms + EPS)


def _norm_mod(x, g, sc, sh):
    y = _rms(x) * g
    y = _per_row(y, sc, lambda a, b: a * (1.0 + b))
    return _per_row(y, sh, lambda a, b: a + b)


def _lane_tiles(v):
    return [v[:, LANES * k:LANES * (k + 1)] for k in range(v.shape[1] // LANES)]


def _to_time_major(x_ref, scr, batch):
    tt = x_ref.shape[1]
    for b in range(batch):
        for k, piece in enumerate(_lane_tiles(x_ref[b])):
            scr[k, pl.ds(b, tt, stride=batch), :] = piece
    return jnp.concatenate([scr[k] for k in range(LANE_TILES)], axis=1)


def _from_time_major(v, o_ref, scr, batch):
    tt = v.shape[0] // batch
    for k, piece in enumerate(_lane_tiles(v)):
        scr[k] = piece
    for b in range(batch):
        o_ref[b] = jnp.concatenate(
            [scr[k, pl.ds(b, tt, stride=batch), :] for k in range(LANE_TILES)],
            axis=1)


def _slab_rows(ref, k, n):
    return ref[pl.ds(k, n, stride=SLAB), :]


def _store_slabs(v, ref):
    n = v.shape[0]
    for k, piece in enumerate(_lane_tiles(v)):
        ref[pl.ds(k, n, stride=SLAB), :] = piece


def _rows_from_slabs(y_ref, n):
    return jnp.concatenate([_slab_rows(y_ref, k, n) for k in range(SLAB)], axis=1)


def _route(logits_t, rb, tri, carry):
    ne, tm = logits_t.shape
    s = _sigmoid(logits_t)
    bz = s + rb
    row = lambda a, r: a[r:r + 1, :]
    gs = []
    for g in range(N_EXPERT_GROUPS):
        v0, v1, v2, v3 = (row(bz, EXPERTS_PER_GROUP * g + j) for j in range(4))
        hi1, lo1 = jnp.maximum(v0, v1), jnp.minimum(v0, v1)
        hi2, lo2 = jnp.maximum(v2, v3), jnp.minimum(v2, v3)
        top1 = jnp.maximum(hi1, hi2)
        top2 = jnp.maximum(jnp.minimum(hi1, hi2), jnp.maximum(lo1, lo2))
        gs.append(top1 + top2)
    bg = jnp.zeros((1, tm), I32)
    bv = gs[0]
    for g in range(1, N_EXPERT_GROUPS):
        upd = gs[g] > bv
        bg = jnp.where(upd, g, bg)
        bv = jnp.where(upd, gs[g], bv)
    cb, cs = [], []
    for j in range(EXPERTS_PER_GROUP):
        vb, vs = row(bz, j), row(s, j)
        for g in range(1, N_EXPERT_GROUPS):
            sel = bg == g
            vb = jnp.where(sel, row(bz, EXPERTS_PER_GROUP * g + j), vb)
            vs = jnp.where(sel, row(s, EXPERTS_PER_GROUP * g + j), vs)
        cb.append(vb)
        cs.append(vs)
    i1 = jnp.zeros((1, tm), I32)
    b1, s1 = cb[0], cs[0]
    for j in range(1, EXPERTS_PER_GROUP):
        upd = cb[j] > b1
        i1 = jnp.where(upd, j, i1)
        b1 = jnp.where(upd, cb[j], b1)
        s1 = jnp.where(upd, cs[j], s1)
    i2 = jnp.zeros((1, tm), I32)
    b2 = jnp.full((1, tm), -jnp.inf, F32)
    s2 = jnp.zeros((1, tm), F32)
    for j in range(EXPERTS_PER_GROUP):
        cand = jnp.where(i1 == j, -jnp.inf, cb[j])
        upd = cand > b2
        i2 = jnp.where(upd, j, i2)
        b2 = jnp.where(upd, cand, b2)
        s2 = jnp.where(upd, cs[j], s2)
    den = s1 + s2
    w1 = s1 / den
    w2 = s2 / den
    lo = jnp.minimum(i1, i2)
    hi = jnp.maximum(i1, i2)
    pidx = jnp.where(lo == 0, hi - 1,
                     jnp.where(lo == 1, jnp.where(hi == 3, 3, 4), 5))
    first = jnp.where(pidx < 3, 0, jnp.where(pidx < 5, 1, 3))
    wa = jnp.where(i1 == first, w1, w2)
    wb = jnp.where(i1 == first, w2, w1)
    q = len(PAIRS) * bg + pidx
    qio = lax.broadcasted_iota(I32, (N_CLASS_ROWS, tm), 0)
    oh = qio == q
    ohf = jnp.where(oh, 1.0, 0.0)
    cum = jnp.dot(ohf.astype(BF16), tri, preferred_element_type=F32) + carry
    rank = jnp.sum(jnp.where(oh, cum, 0.0), axis=0, keepdims=True)
    new_carry = carry + jnp.sum(ohf, axis=1, keepdims=True)
    return q, rank.astype(I32), wa, wb, new_carry


def _post_mixer(m, x, g1, nf, sc2, sh2, rwt_ref, rb_ref, tri_ref,
                x1_ref, h2_ref, ri_ref, rw_ref, cnt_ref, carry, valid=None):
    tm = x.shape[0]
    x1 = x + _per_row(m, g1, lambda a, b: a * b)
    x1_ref[...] = x1
    h2 = _norm_mod(x1, nf, sc2, sh2)
    h_hi = h2.astype(BF16)
    h_lo = (h2 - h_hi.astype(F32)).astype(BF16)
    nt = (((1,), (1,)), ((), ()))
    la = lax.dot_general(rwt_ref[...], h_hi, nt, preferred_element_type=F32)
    lb = lax.dot_general(rwt_ref[0:N_EXPERTS, :], h_lo, nt,
                         preferred_element_type=F32)
    logits_t = la[:N_EXPERTS] + la[N_EXPERTS:] + lb
    q, rank, wa, wb, nc = _route(logits_t, rb_ref[...], tri_ref[...],
                                 carry[:, 0:1])
    _store_slabs(h2, h2_ref)
    rw_ref[0:1, :] = wa
    rw_ref[1:2, :] = wb
    rw_ref[2:8, :] = jnp.zeros((6, tm), F32)
    ri_ref[0:1, :] = q
    ri_ref[1:2, :] = rank
    ri_ref[2:8, :] = jnp.zeros((6, tm), I32)
    if valid is not None:
        nc = jnp.where(valid, nc, carry[:, 0:1])
    carry[...] = jnp.broadcast_to(nc, carry.shape)
    cnt_ref[...] = carry[...]


def _ada_kernel(cond_ref, w_ref, b_ref, o_ref):
    c = cond_ref[...]
    s = c * _sigmoid(c)
    o_ref[...] = jnp.dot(s, w_ref[...], precision=HIGHEST,
                         preferred_element_type=F32) + b_ref[...]


def _ada_table(cond, w_ada, b_ada):
    tn = 1536
    n_out = N_MOD * D_MODEL
    return pl.pallas_call(
        _ada_kernel,
        out_shape=jax.ShapeDtypeStruct((DEPTH, N_COND, n_out), F32),
        grid=(DEPTH, n_out // tn),
        in_specs=[
            pl.BlockSpec((N_COND, D_MODEL), lambda l, j: (0, 0)),
            pl.BlockSpec((None, D_MODEL, tn), lambda l, j: (l, 0, j)),
            pl.BlockSpec((None, 1, tn), lambda l, j: (l, 0, j)),
        ],
        out_specs=pl.BlockSpec((None, N_COND, tn), lambda l, j: (l, 0, j)),
        compiler_params=_cparams(("arbitrary", "arbitrary")),
        name="ada_table",
    )(cond, w_ada, b_ada.reshape(DEPTH, 1, n_out))


def _zoh_kernel(lr_ref, li_ref, ldt_ref, br_ref, bi_ref,
                are_ref, aim_ref, bbre_ref, bbim_ref):
    lr = lr_ref[...]
    li = li_ref[...]
    dt = jnp.exp(ldt_ref[...])
    mag = jnp.exp(lr * dt)
    a_re = mag * jnp.cos(li * dt)
    a_im = mag * jnp.sin(li * dt)
    den = lr * lr + li * li
    nr = a_re - 1.0
    f_re = (nr * lr + a_im * li) / den
    f_im = (a_im * lr - nr * li) / den
    br = br_ref[...]
    bi = bi_ref[...]
    are_ref[...] = a_re
    aim_ref[...] = a_im
    bbre_ref[...] = f_re * br - f_im * bi
    bbim_ref[...] = f_re * bi + f_im * br


def _zoh(lam_re, lam_im, log_dt, b_re, b_im):
    shape = b_re.shape
    flat = (shape[0] * shape[1] * shape[2] * shape[3] // LANES, LANES)
    bc = lambda a: jnp.broadcast_to(a, shape).reshape(flat)
    args = (bc(lam_re[..., None]), bc(lam_im[..., None]),
            bc(log_dt[:, :, None, None]), b_re.reshape(flat), b_im.reshape(flat))
    outs = pl.pallas_call(
        _zoh_kernel,
        out_shape=[jax.ShapeDtypeStruct(flat, F32)] * 4,
        name="zoh_discretize",
    )(*args)
    a_re, a_im, bb_re, bb_im = (o.reshape(shape) for o in outs)
    return a_re[..., 0], a_im[..., 0], bb_re, bb_im


def _stream_specs(xp, xs, n_p_tiles):
    bp, bs = xp.shape[0], xs.shape[0]
    return [
        pl.BlockSpec((bp, TOKEN_TILE // bp, D_MODEL),
                     lambda i: (0, jnp.minimum(i, n_p_tiles - 1), 0)),
        pl.BlockSpec((bs, TOKEN_TILE // bs, D_MODEL),
                     lambda i: (0, jnp.maximum(i - n_p_tiles, 0), 0)),
    ]


def _ssm_in_kernel(xp_ref, xs_ref, g_ref, sc_ref, sh_ref, w_ref, u_ref, scr,
                   *, n_p_tiles):
    def run(x_ref):
        x = _to_time_major(x_ref, scr, x_ref.shape[0])
        h = _norm_mod(x, g_ref[...], sc_ref[...], sh_ref[...])
        u_ref[...] = jnp.dot(h.astype(BF16), w_ref[...],
                             preferred_element_type=F32)

    i = pl.program_id(0)
    pl.when(i < n_p_tiles)(lambda: run(xp_ref))
    pl.when(i >= n_p_tiles)(lambda: run(xs_ref))


def _ssm_in(xp, xs, g, modpat, w_in_bf):
    n = (xp.shape[0] * xp.shape[1] + xs.shape[0] * xs.shape[1])
    tm = TOKEN_TILE
    n_p_tiles = xp.shape[0] * xp.shape[1] // tm
    p = modpat.shape[1]
    pat = lambda i: jnp.where(i < n_p_tiles, 0, 1)
    mod = lambda k: pl.BlockSpec((None, p, D_MODEL), lambda i: (pat(i), 0, k))
    return pl.pallas_call(
        functools.partial(_ssm_in_kernel, n_p_tiles=n_p_tiles),
        out_shape=jax.ShapeDtypeStruct((n, D_MODEL), F32),
        grid=(n // tm,),
        in_specs=_stream_specs(xp, xs, n_p_tiles) + [
            pl.BlockSpec((1, D_MODEL), lambda i: (0, 0)),
            mod(1), mod(0),
            pl.BlockSpec((D_MODEL, D_MODEL), lambda i: (0, 0)),
        ],
        out_specs=pl.BlockSpec((tm, D_MODEL), lambda i: (i, 0)),
        scratch_shapes=[pltpu.VMEM((LANE_TILES, tm, LANES), F32)],
        compiler_params=_cparams(("arbitrary",)),
        name="ssm_in",
    )(xp, xs, g, modpat, modpat, w_in_bf)


def _scan_kernel(u_ref, bt_ref, cre_ref, cim_ref, are_ref, aim_ref,
                 h0re_ref, h0im_ref, y_ref, fre_ref, fim_ref,
                 *scratch, n_p_chunks, bp, bs):
    xre = scratch[:LANE_TILES]
    xim = scratch[LANE_TILES:2 * LANE_TILES]
    hre, him = scratch[2 * LANE_TILES:]
    d = pl.program_id(0)
    c = pl.program_id(1)
    cols = N_STATE // LANE_TILES
    rows = u_ref.shape[0]

    def chunk(batch, reverse):
        steps = rows // batch

        def b_proj(s):
            u = u_ref[:, LANES * s:LANES * (s + 1)].astype(BF16)
            xt = jnp.dot(u, bt_ref[s], preferred_element_type=F32)
            xre[s][...] = xt[:, :cols]
            xim[s][...] = xt[:, cols:]

        def scan(s):
            sl = slice(cols * s, cols * (s + 1))
            ar = jnp.broadcast_to(are_ref[:, sl], (batch, cols))
            ai = jnp.broadcast_to(aim_ref[:, sl], (batch, cols))
            hr = hre[0:batch, sl]
            hi = him[0:batch, sl]
            for t in range(steps):
                tt = steps - 1 - t if reverse else t
                r = slice(tt * batch, (tt + 1) * batch)
                nr = ar * hr - ai * hi + xre[s][r, :]
                ni = ar * hi + ai * hr + xim[s][r, :]
                xre[s][r, :] = nr
                xim[s][r, :] = ni
                hr, hi = nr, ni
            hre[0:batch, sl] = hr
            him[0:batch, sl] = hi

        def c_proj(s):
            y_ref[:, LANES * s:LANES * (s + 1)] = (
                jnp.dot(xre[s][...].astype(BF16), cre_ref[s],
                        preferred_element_type=F32)
                - jnp.dot(xim[s][...].astype(BF16), cim_ref[s],
                          preferred_element_type=F32))

        for stage in range(LANE_TILES + 2):
            if stage < LANE_TILES:
                b_proj(stage)
            if 0 <= stage - 1 < LANE_TILES:
                scan(stage - 1)
            if 0 <= stage - 2 < LANE_TILES:
                c_proj(stage - 2)

    @pl.when(c == 0)
    def _():
        hre[...] = jnp.zeros(hre.shape, F32)
        him[...] = jnp.zeros(him.shape, F32)

    @pl.when(c == n_p_chunks)
    def _():
        hre[0:bs, :] = h0re_ref[...]
        him[0:bs, :] = h0im_ref[...]

    in_p = c < n_p_chunks
    in_s = jnp.logical_not(in_p)
    fwd = d == 0
    bwd = jnp.logical_not(fwd)
    pl.when(jnp.logical_and(in_p, fwd))(lambda: chunk(bp, False))
    pl.when(jnp.logical_and(in_p, bwd))(lambda: chunk(bp, True))
    pl.when(jnp.logical_and(in_s, fwd))(lambda: chunk(bs, False))
    pl.when(jnp.logical_and(in_s, bwd))(lambda: chunk(bs, True))

    @pl.when(c == n_p_chunks - 1)
    def _():
        fre_ref[...] = hre[0:bp, :]
        fim_ref[...] = him[0:bp, :]


def _ssm_scan(u, bt, ct_re, ct_im, a_re, a_im, h0_re, h0_im, *, bp, lp, bs, ls):
    n = u.shape[0]
    rows = TOKEN_TILE
    n_p_chunks = bp * lp // rows
    n_s_chunks = bs * ls // rows

    def chunk(d, c):
        in_p = jnp.where(d == 0, c, n_p_chunks - 1 - c)
        cs = c - n_p_chunks
        in_s = n_p_chunks + jnp.where(d == 0, cs, n_s_chunks - 1 - cs)
        return jnp.where(c < n_p_chunks, in_p, in_s)

    cols = N_STATE // LANE_TILES
    dspec = lambda shape: pl.BlockSpec((None,) + shape,
                                       lambda d, c: (d,) + (0,) * len(shape))
    kern = functools.partial(_scan_kernel, n_p_chunks=n_p_chunks, bp=bp, bs=bs)
    return pl.pallas_call(
        kern,
        out_shape=[jax.ShapeDtypeStruct((2, n, D_MODEL), F32),
                   jax.ShapeDtypeStruct((2, bp, N_STATE), F32),
                   jax.ShapeDtypeStruct((2, bp, N_STATE), F32)],
        grid=(2, n_p_chunks + n_s_chunks),
        in_specs=[
            pl.BlockSpec((rows, D_MODEL), lambda d, c: (chunk(d, c), 0)),
            dspec((LANE_TILES, LANES, 2 * cols)),
            dspec((LANE_TILES, cols, LANES)),
            dspec((LANE_TILES, cols, LANES)),
            dspec((1, N_STATE)),
            dspec((1, N_STATE)),
            dspec((bs, N_STATE)),
            dspec((bs, N_STATE)),
        ],
        out_specs=[
            pl.BlockSpec((None, rows, D_MODEL), lambda d, c: (d, chunk(d, c), 0)),
            dspec((bp, N_STATE)),
            dspec((bp, N_STATE)),
        ],
        scratch_shapes=(
            [pltpu.VMEM((rows, cols), F32)] * (2 * LANE_TILES)
            + [pltpu.VMEM((max(bp, bs), N_STATE), F32)] * 2),
        compiler_params=_cparams(("arbitrary", "arbitrary")),
        name="ssm_scan",
    )(u, bt, ct_re, ct_im, a_re, a_im, h0_re, h0_im)


def _ssm_out_kernel(u_ref, y0_ref, y1_ref, dsk_ref, wglu_ref, xp_ref, xs_ref,
                    g1_ref, nf_ref, sc2_ref, sh2_ref, rwt_ref, rb_ref, tri_ref,
                    x1_ref, h2_ref, ri_ref, rw_ref, cnt_ref, scr, carry,
                    *, n_p_tiles):
    i = pl.program_id(0)

    @pl.when(i == 0)
    def _():
        carry[...] = jnp.zeros(carry.shape, F32)

    def run(x_ref):
        x = _to_time_major(x_ref, scr, x_ref.shape[0])
        y = u_ref[...] * dsk_ref[...] + y0_ref[...] + y1_ref[...]
        ge = _gelu_tanh(y).astype(BF16)
        vg = jnp.dot(ge, wglu_ref[...], preferred_element_type=F32)
        m = vg[:, :D_MODEL] * _sigmoid(vg[:, D_MODEL:])
        _post_mixer(m, x, g1_ref[...], nf_ref[...], sc2_ref[...], sh2_ref[...],
                    rwt_ref, rb_ref, tri_ref, x1_ref, h2_ref, ri_ref, rw_ref,
                    cnt_ref, carry)

    pl.when(i < n_p_tiles)(lambda: run(xp_ref))
    pl.when(i >= n_p_tiles)(lambda: run(xs_ref))


def _route_out_shapes(n):
    return [jax.ShapeDtypeStruct((n, D_MODEL), F32),
            jax.ShapeDtypeStruct((n * SLAB, LANES), F32),
            jax.ShapeDtypeStruct((SUBLANES, n), I32),
            jax.ShapeDtypeStruct((SUBLANES, n), F32),
            jax.ShapeDtypeStruct((N_CLASS_ROWS, LANES), F32)]


def _ssm_out(u, y, d_skip, w_glu_bf, xp, xs, modpat, norm_ffn, rwt, rb, tri):
    n = u.shape[0]
    tm = TOKEN_TILE
    n_p_tiles = xp.shape[0] * xp.shape[1] // tm
    p = modpat.shape[1]
    pat = lambda i: jnp.where(i < n_p_tiles, 0, 1)
    mod = lambda k: pl.BlockSpec((None, p, D_MODEL), lambda i: (pat(i), 0, k))
    full = lambda shape: pl.BlockSpec(shape, lambda i: (0,) * len(shape))
    rowblk = pl.BlockSpec((tm, D_MODEL), lambda i: (i, 0))
    return pl.pallas_call(
        functools.partial(_ssm_out_kernel, n_p_tiles=n_p_tiles),
        out_shape=_route_out_shapes(n),
        grid=(n // tm,),
        in_specs=[
            rowblk,
            pl.BlockSpec((None, tm, D_MODEL), lambda i: (0, i, 0)),
            pl.BlockSpec((None, tm, D_MODEL), lambda i: (1, i, 0)),
            full((1, D_MODEL)),
            full((D_MODEL, 2 * D_MODEL)),
        ] + _stream_specs(xp, xs, n_p_tiles) + [
            mod(2), full((1, D_MODEL)), mod(4), mod(3),
            full((2 * N_EXPERTS, D_MODEL)), full((N_EXPERTS, 1)), full((tm, tm)),
        ],
        out_specs=[rowblk,
                   pl.BlockSpec((tm * SLAB, LANES), lambda i: (i, 0)),
                   pl.BlockSpec((SUBLANES, tm), lambda i: (0, i)),
                   pl.BlockSpec((SUBLANES, tm), lambda i: (0, i)),
                   full((N_CLASS_ROWS, LANES))],
        scratch_shapes=[pltpu.VMEM((LANE_TILES, tm, LANES), F32),
                        pltpu.VMEM((N_CLASS_ROWS, LANES), F32)],
        compiler_params=_cparams(("arbitrary",)),
        name="ssm_out",
    )(u, y, y, d_skip, w_glu_bf, xp, xs, modpat, norm_ffn, modpat, modpat,
      rwt, rb, tri)


def _fnet_kernel(xp_ref, xpp_ref, xs_ref, nm_ref, sc1_ref, sh1_ref, cs_ref,
                 dftp_ref, dfts_ref, wout_ref, g1_ref, nf_ref, sc2_ref, sh2_ref,
                 rwt_ref, rb_ref, tri_ref, x1_ref, h2_ref, ri_ref, rw_ref, cnt_ref,
                 ucs, ybuf, carry, *, bp, tiles_s):
    s = pl.program_id(0)
    n_groups = D_MODEL // FNET_GROUP
    tr = FNET_TILE
    per = tiles_s + 1
    wslot = s % 2

    @pl.when(s == 0)
    def _():
        carry[...] = jnp.zeros(carry.shape, F32)
        ybuf[...] = jnp.zeros(ybuf.shape, F32)

    def stage1(x_ref, seq):
        def body(r, _):
            rows = pl.ds(pl.multiple_of(r * tr, tr), tr)
            h = _norm_mod(x_ref[rows, :], nm_ref[...], sc1_ref[...],
                          sh1_ref[...]).astype(BF16)
            for k in range(n_groups):
                cols = slice(FNET_GROUP * k, FNET_GROUP * (k + 1))
                t = jnp.dot(h[:, cols], cs_ref[...], preferred_element_type=F32)
                ucs[rows, cols] = t[:, :FNET_GROUP].astype(BF16)
                ucs[pl.ds(pl.multiple_of(seq + r * tr, tr), tr), cols] = (
                    t[:, FNET_GROUP:].astype(BF16))
            return 0
        lax.fori_loop(0, seq // tr, body, 0)

    def step(dft_ref, seq, xrow, valid):
        scale = float((seq * FNET_GROUP) ** -0.5)
        m = jnp.dot(ybuf[1 - wslot].astype(BF16), wout_ref[...],
                    preferred_element_type=F32)
        _post_mixer(m, xrow, g1_ref[...], nf_ref[...], sc2_ref[...],
                    sh2_ref[...], rwt_ref, rb_ref, tri_ref, x1_ref, h2_ref,
                    ri_ref, rw_ref, cnt_ref, carry, valid)
        ybuf[wslot] = jnp.dot(dft_ref[...], ucs[0:2 * seq, :],
                              preferred_element_type=F32) * scale

    @pl.when(s <= bp)
    def _():
        stage1(xp_ref, xp_ref.shape[0])
        step(dftp_ref, xp_ref.shape[0], xpp_ref[...], s >= 1)

    @pl.when(s > bp)
    def _():
        j = (s - bp - 1) % per
        pl.when(j == 0)(lambda: stage1(xs_ref, xs_ref.shape[0]))
        prev = jnp.maximum(j - 1, 0)
        xrow = xs_ref[pl.ds(pl.multiple_of(prev * tr, tr), tr), :]
        step(dfts_ref, xs_ref.shape[0], xrow, j >= 1)


def _dft_table(seq):
    inner = 64
    k = jnp.arange(seq, dtype=I32)[:, None]
    t1 = jnp.arange(seq // inner, dtype=I32)[None, :] * inner
    t2 = jnp.arange(inner, dtype=I32)[None, :]
    ang = lambda t: ((k * t) % seq).astype(F32) * (2.0 * jnp.pi / seq)
    ca, sa = jnp.cos(ang(t1))[:, :, None], jnp.sin(ang(t1))[:, :, None]
    cb, sb = jnp.cos(ang(t2))[:, None, :], jnp.sin(ang(t2))[:, None, :]
    cos = (ca * cb - sa * sb).reshape(seq, seq)
    sin = (sa * cb + ca * sb).reshape(seq, seq)
    return jnp.concatenate([cos, -sin], axis=1).astype(BF16)


def _fnet(xp, xs, norm_mix, modpat, w_out_bf, norm_ffn, rwt, rb):
    bp, lp, _ = xp.shape
    bs, ls, _ = xs.shape
    tr = FNET_TILE
    assert lp == tr
    tiles_s = ls // tr
    per = tiles_s + 1
    n = bp * lp + bs * ls
    p = modpat.shape[1]
    kc = jnp.arange(FNET_GROUP, dtype=I32)
    angc = ((kc[:, None] * kc[None, :]) % FNET_GROUP).astype(F32) * (
        2.0 * jnp.pi / FNET_GROUP)
    cs = jnp.concatenate([jnp.cos(angc), jnp.sin(angc)], axis=1).astype(BF16)
    tri = jnp.triu(jnp.ones((tr, tr), BF16), k=1)

    q = lambda s: jnp.maximum(s - bp - 1, 0)
    sb = lambda s: jnp.minimum(q(s) // per, bs - 1)
    sj = lambda s: q(s) % per
    pat = lambda s: jnp.where(s <= bp, 0, 1 + sb(s))
    blk = lambda s: jnp.where(s <= bp, jnp.maximum(s - 1, 0),
                              bp + sb(s) * tiles_s + jnp.maximum(sj(s) - 1, 0))
    mod = lambda kk: pl.BlockSpec((None, p, D_MODEL), lambda s: (pat(s), 0, kk))
    full = lambda shape: pl.BlockSpec(shape, lambda s: (0,) * len(shape))
    kern = functools.partial(_fnet_kernel, bp=bp, tiles_s=tiles_s)
    return pl.pallas_call(
        kern,
        out_shape=_route_out_shapes(n),
        grid=(bp + 1 + bs * per,),
        in_specs=[
            pl.BlockSpec((None, lp, D_MODEL),
                         lambda s: (jnp.minimum(s, bp - 1), 0, 0)),
            pl.BlockSpec((None, lp, D_MODEL),
                         lambda s: (jnp.clip(s - 1, 0, bp - 1), 0, 0)),
            pl.BlockSpec((None, ls, D_MODEL), lambda s: (sb(s), 0, 0)),
            full((1, D_MODEL)), mod(1), mod(0),
            full((FNET_GROUP, 2 * FNET_GROUP)),
            full((tr, 2 * lp)),
            pl.BlockSpec((tr, 2 * ls),
                         lambda s: (jnp.minimum(sj(s), tiles_s - 1), 0)),
            full((D_MODEL, D_MODEL)),
            mod(2), full((1, D_MODEL)), mod(4), mod(3),
            full((2 * N_EXPERTS, D_MODEL)), full((N_EXPERTS, 1)), full((tr, tr)),
        ],
        out_specs=[pl.BlockSpec((tr, D_MODEL), lambda s: (blk(s), 0)),
                   pl.BlockSpec((tr * SLAB, LANES), lambda s: (blk(s), 0)),
                   pl.BlockSpec((SUBLANES, tr), lambda s: (0, blk(s))),
                   pl.BlockSpec((SUBLANES, tr), lambda s: (0, blk(s))),
                   full((N_CLASS_ROWS, LANES))],
        scratch_shapes=[pltpu.VMEM((2 * ls, D_MODEL), BF16),
                        pltpu.VMEM((2, tr, D_MODEL), F32),
                        pltpu.VMEM((N_CLASS_ROWS, LANES), F32)],
        compiler_params=_cparams(("arbitrary",)),
        name="fnet",
    )(xp, xp, xs, norm_mix, modpat, modpat, cs, _dft_table(lp), _dft_table(ls),
      w_out_bf, modpat, norm_ffn, modpat, modpat, rwt, rb, tri)


def _invert_kernel(slot_ref, ends_ref, gsrc_ref, sdst_ref):
    n = slot_ref.shape[0]
    n_slots = gsrc_ref.shape[0]
    t = FFN_TILE
    unroll = 8

    def pad_tile(base):
        def pad(j, _):
            for k in range(unroll):
                r = j * unroll + k
                gsrc_ref[base + r] = 0
                sdst_ref[base + r] = n + r
            return 0
        lax.fori_loop(0, t // unroll, pad, 0)

    for q in range(N_PAIRS):
        pad_tile(jnp.maximum(ends_ref[q] - t, 0))

    def unused(b, _):
        pad_tile(b * t)
        return 0
    lax.fori_loop(ends_ref[N_PAIRS - 1] // t, n_slots // t, unused, 0)

    def body(j, _):
        for k in range(unroll):
            t = j * unroll + k
            s = slot_ref[t]
            gsrc_ref[s] = t
            sdst_ref[s] = t
        return 0
    lax.fori_loop(0, n // unroll, body, 0)


def _invert(slot, ends, n_slots):
    smem = pl.BlockSpec(memory_space=pltpu.SMEM)
    return pl.pallas_call(
        _invert_kernel,
        out_shape=[jax.ShapeDtypeStruct((n_slots,), I32)] * 2,
        in_specs=[smem, smem],
        out_specs=[smem, smem],
        name="moe_invert",
    )(slot, ends)


def _ffn_kernel(tea_ref, teb_ref, nu_ref, gsrc_ref, sdst_ref, wa_ref, wb_ref,
                h_hbm, wga, wua, wda, wgb, wub, wdb, y_hbm,
                xbuf0, xbuf1, obuf0, obuf1, wcola, wcolb, gsem, ssem):
    del tea_ref, teb_ref
    i = pl.program_id(0)
    nu = nu_ref[0]
    t = FFN_TILE

    def slab(ref, r):
        return ref.at[pl.ds(pl.multiple_of(r * SLAB, SLAB), SLAB), :]

    def gather_start(tile, xbuf, sem):
        base = tile * t
        for r in range(t):
            pltpu.make_async_copy(slab(h_hbm, gsrc_ref[base + r]), slab(xbuf, r),
                                  sem).start(priority=r % 2)

    def gather_wait(xbuf, sem):
        pltpu.make_async_copy(h_hbm.at[pl.ds(0, t * SLAB), :], xbuf, sem).wait()

    def scatter_start(tile, obuf, sem):
        base = tile * t
        for r in range(t):
            pltpu.make_async_copy(slab(obuf, r), slab(y_hbm, sdst_ref[base + r]),
                                  sem).start(priority=r % 2)

    def scatter_wait(obuf, sem):
        pltpu.make_async_copy(obuf, y_hbm.at[pl.ds(0, t * SLAB), :], sem).wait()

    def expert(xb, wg, wu, wd):
        g = jnp.dot(xb, wg[...], preferred_element_type=F32)
        u = jnp.dot(xb, wu[...], preferred_element_type=F32)
        a = (g * _sigmoid(g)) * u
        return jnp.dot(a.astype(BF16), wd[...], preferred_element_type=F32)

    def step(xc, xn, oc, op, gc, gn, sc, sp):
        pl.when(i == 0)(lambda: gather_start(0, xc, gc))
        pl.when(i >= 2)(lambda: scatter_wait(oc, sc))
        gather_wait(xc, gc)
        pl.when(i + 1 < nu)(lambda: gather_start(i + 1, xn, gn))
        pl.when(i >= 1)(lambda: scatter_start(i - 1, op, sp))
        for r in range(t):
            tok = gsrc_ref[i * t + r]
            wcola[r:r + 1, :] = jnp.full((1, LANES), wa_ref[tok], F32)
            wcolb[r:r + 1, :] = jnp.full((1, LANES), wb_ref[tok], F32)
        xb = _rows_from_slabs(xc, t).astype(BF16)
        ya = expert(xb, wga, wua, wda)
        yb = expert(xb, wgb, wub, wdb)
        y = wcola[:, 0:1] * ya + wcolb[:, 0:1] * yb
        for k, piece in enumerate(_lane_tiles(y)):
            oc[pl.ds(k, t, stride=SLAB), :] = piece

    def drain(oc, op, sc, sp):
        pl.when(i >= 2)(lambda: scatter_wait(oc, sc))
        scatter_start(i - 1, op, sp)
        scatter_wait(op, sp)
        spare_row = y_hbm.shape[0] - t * SLAB
        oc[...] = jnp.zeros(oc.shape, F32)
        spare = pltpu.make_async_copy(
            oc, y_hbm.at[pl.ds(spare_row, t * SLAB), :], sc)
        spare.start()
        spare.wait()

    even = (i % 2) == 0
    odd = jnp.logical_not(even)
    g0, g1, s0, s1 = gsem.at[0], gsem.at[1], ssem.at[0], ssem.at[1]
    pl.when(jnp.logical_and(i < nu, even))(
        lambda: step(xbuf0, xbuf1, obuf0, obuf1, g0, g1, s0, s1))
    pl.when(jnp.logical_and(i < nu, odd))(
        lambda: step(xbuf1, xbuf0, obuf1, obuf0, g1, g0, s1, s0))
    pl.when(jnp.logical_and(i == nu, even))(
        lambda: drain(obuf0, obuf1, s0, s1))
    pl.when(jnp.logical_and(i == nu, odd))(
        lambda: drain(obuf1, obuf0, s1, s0))


def _expert_ffn(tea, teb, n_used, gsrc, sdst, wa_tok, wb_tok, h2, wg, wu, wd,
                layer):
    n_tok = h2.shape[0] // SLAB
    max_tiles = gsrc.shape[0] // FFN_TILE
    wa = lambda a, b: pl.BlockSpec(
        (None, None, a, b), lambda i, ta, tb, *_: (layer, ta[i], 0, 0))
    wb = lambda a, b: pl.BlockSpec(
        (None, None, a, b), lambda i, ta, tb, *_: (layer, tb[i], 0, 0))
    anyspec = pl.BlockSpec(memory_space=pl.ANY)
    return pl.pallas_call(
        _ffn_kernel,
        out_shape=jax.ShapeDtypeStruct(((n_tok + FFN_TILE) * SLAB, LANES), F32),
        grid_spec=pltpu.PrefetchScalarGridSpec(
            num_scalar_prefetch=7,
            grid=(max_tiles + 1,),
            in_specs=[anyspec,
                      wa(D_MODEL, D_EXPERT), wa(D_MODEL, D_EXPERT),
                      wa(D_EXPERT, D_MODEL),
                      wb(D_MODEL, D_EXPERT), wb(D_MODEL, D_EXPERT),
                      wb(D_EXPERT, D_MODEL)],
            out_specs=anyspec,
            scratch_shapes=[pltpu.VMEM((FFN_TILE * SLAB, LANES), F32)] * 4 + [
                            pltpu.VMEM((FFN_TILE, LANES), F32),
                            pltpu.VMEM((FFN_TILE, LANES), F32),
                            pltpu.SemaphoreType.DMA((2,)),
                            pltpu.SemaphoreType.DMA((2,))],
        ),
        compiler_params=_cparams(("arbitrary",)),
        name="moe_ffn",
    )(tea, teb, n_used, gsrc, sdst, wa_tok, wb_tok, h2, wg, wu, wd, wg, wu, wd)


def _moe(h2, ri, rw, cnt, wg_bf, wu_bf, wd_bf, layer):
    n = h2.shape[0] // SLAB
    max_tiles = n // FFN_TILE + N_PAIRS
    n_slots = max_tiles * FFN_TILE
    counts = cnt[:N_PAIRS, 0].astype(I32)
    padded = ((counts + FFN_TILE - 1) // FFN_TILE) * FFN_TILE
    ends = jnp.cumsum(padded)
    offs = ends - padded
    q, rank = ri[0], ri[1]
    cls = jnp.arange(N_PAIRS, dtype=I32)
    slot = rank + jnp.sum(jnp.where(q[None, :] == cls[:, None], offs[:, None], 0),
                          axis=0)
    n_used = ends[-1] // FFN_TILE
    tile = jnp.arange(max_tiles + 1, dtype=I32)
    tq = jnp.sum((tile[:, None] * FFN_TILE >= ends[None, :]).astype(I32), axis=1)
    tq_last = jnp.sum(((n_used - 1) * FFN_TILE >= ends).astype(I32))
    tq = jnp.where(tile < n_used, tq, tq_last)
    pa = jnp.array([a for a, _ in PAIRS], I32)
    pb = jnp.array([b for _, b in PAIRS], I32)
    grp, pidx = tq // len(PAIRS), tq % len(PAIRS)
    tea = EXPERTS_PER_GROUP * grp + jnp.take(pa, pidx)
    teb = EXPERTS_PER_GROUP * grp + jnp.take(pb, pidx)
    gsrc, sdst = _invert(slot, ends, n_slots)
    return _expert_ffn(tea, teb, n_used.reshape(1), gsrc, sdst, rw[0], rw[1], h2,
                       wg_bf, wu_bf, wd_bf, layer)


def _moe_out_tm_kernel(x1_ref, y_ref, g2_ref, op_ref, os_ref, scr, *, n_p_tiles):
    i = pl.program_id(0)
    y = _rows_from_slabs(y_ref, x1_ref.shape[0])
    x2 = x1_ref[...] + _per_row(y, g2_ref[...], lambda a, b: a * b)
    pl.when(i < n_p_tiles)(
        lambda: _from_time_major(x2, op_ref, scr, op_ref.shape[0]))
    pl.when(i >= n_p_tiles)(
        lambda: _from_time_major(x2, os_ref, scr, os_ref.shape[0]))


def _moe_out_tm(x1, y, modpat, shape_p, shape_s):
    n = x1.shape[0]
    tm = TOKEN_TILE
    n_p_tiles = shape_p[0] * shape_p[1] // tm
    p = modpat.shape[1]
    rowblk = pl.BlockSpec((tm, D_MODEL), lambda i: (i, 0))
    stream = _stream_specs(jax.ShapeDtypeStruct(shape_p, F32),
                           jax.ShapeDtypeStruct(shape_s, F32), n_p_tiles)
    return pl.pallas_call(
        functools.partial(_moe_out_tm_kernel, n_p_tiles=n_p_tiles),
        out_shape=[jax.ShapeDtypeStruct(shape_p, F32),
                   jax.ShapeDtypeStruct(shape_s, F32)],
        grid=(n // tm,),
        in_specs=[rowblk,
                  pl.BlockSpec((tm * SLAB, LANES), lambda i: (i, 0)),
                  pl.BlockSpec((None, p, D_MODEL),
                               lambda i: (jnp.where(i < n_p_tiles, 0, 1), 0, 5))],
        out_specs=stream,
        scratch_shapes=[pltpu.VMEM((LANE_TILES, tm, LANES), F32)],
        compiler_params=_cparams(("arbitrary",)),
        name="moe_out_tm",
    )(x1, y, modpat)


def _moe_out_final_kernel(x1_ref, y_ref, g2_ref, nfin_ref, op_ref, os_ref,
                          *, n_p_tiles):
    i = pl.program_id(0)
    y = _rows_from_slabs(y_ref, x1_ref.shape[0])
    x2 = x1_ref[...] + _per_row(y, g2_ref[...], lambda a, b: a * b)
    out = _rms(x2) * nfin_ref[...]

    @pl.when(i < n_p_tiles)
    def _():
        op_ref[...] = out

    @pl.when(i >= n_p_tiles)
    def _():
        os_ref[...] = out


def _moe_out_final(x1, y, modpat, norm_final, n_p, rows_per_request):
    n = x1.shape[0]
    tm = TOKEN_TILE
    n_p_tiles = n_p // tm
    p = modpat.shape[1]
    rowblk = pl.BlockSpec((tm, D_MODEL), lambda i: (i, 0))
    pat = lambda i: jnp.where(
        i < n_p_tiles, 0, 1 + (i - n_p_tiles) // (rows_per_request // tm))
    return pl.pallas_call(
        functools.partial(_moe_out_final_kernel, n_p_tiles=n_p_tiles),
        out_shape=[jax.ShapeDtypeStruct((n_p, D_MODEL), F32),
                   jax.ShapeDtypeStruct((n - n_p, D_MODEL), F32)],
        grid=(n // tm,),
        in_specs=[rowblk,
                  pl.BlockSpec((tm * SLAB, LANES), lambda i: (i, 0)),
                  pl.BlockSpec((None, p, D_MODEL), lambda i: (pat(i), 0, 5)),
                  pl.BlockSpec((1, D_MODEL), lambda i: (0, 0))],
        out_specs=[
            pl.BlockSpec((tm, D_MODEL),
                         lambda i: (jnp.minimum(i, n_p_tiles - 1), 0)),
            pl.BlockSpec((tm, D_MODEL),
                         lambda i: (jnp.maximum(i - n_p_tiles, 0), 0))],
        compiler_params=_cparams(("arbitrary",)),
        name="moe_out_final",
    )(x1, y, modpat, norm_final)


def _block_diag_weights(bb_re, bb_im, c_re, c_im):
    lt = LANE_TILES
    gl = SSM_GROUPS // lt
    eye = jnp.eye(gl, dtype=F32)

    def in_map(bb):
        b5 = bb.reshape(2, lt, gl, STATE_DIM, SSM_GROUP)
        t = jnp.einsum("dinph,kn->dikhnp", b5, eye)
        return t.reshape(2, lt, gl * SSM_GROUP, gl * STATE_DIM)

    def out_map(cc):
        c5 = cc.reshape(2, lt, gl, SSM_GROUP, STATE_DIM)
        t = jnp.einsum("dikhp,kn->dikpnh", c5, eye)
        return t.reshape(2, lt, gl * STATE_DIM, gl * SSM_GROUP)

    bt = jnp.concatenate([in_map(bb_re), in_map(bb_im)], axis=-1).astype(BF16)
    return bt, out_map(c_re).astype(BF16), out_map(c_im).astype(BF16)


def kernel(x_prompt, x_sample, c, state_ssm_re, state_ssm_im, c_ctx, norm_mix, norm_ffn, w_ada, b_ada, ssm_w_in, ssm_lam_re, ssm_lam_im, ssm_log_dt, ssm_b_re, ssm_b_im, ssm_c_re, ssm_c_im, ssm_d, ssm_w_glu, fnet_w_out, router_w, router_b, moe_w_gate, moe_w_up, moe_w_down, norm_final):
    bp, lp, _ = x_prompt.shape
    bs, ls, _ = x_sample.shape
    n_p = bp * lp
    n_s = bs * ls
    n = n_p + n_s
    tm = TOKEN_TILE

    cond = jnp.zeros((N_COND, D_MODEL), F32).at[0].set(c_ctx).at[1:1 + bs].set(c)
    modtab = _ada_table(cond, w_ada, b_ada).reshape(DEPTH, N_COND, N_MOD, D_MODEL)

    rwt_hi = router_w.T.astype(BF16)
    rwt_lo = (router_w.T - rwt_hi.astype(F32)).astype(BF16)
    rwt = jnp.concatenate([rwt_hi, rwt_lo])
    rb = router_b.reshape(N_EXPERTS, 1)
    row = lambda v: v.reshape(1, D_MODEL)

    period = max(bp, bs)
    pat_tm = jnp.stack([
        jnp.broadcast_to(modtab[0, 0], (period, N_MOD, D_MODEL)),
        jnp.tile(modtab[0, 1:1 + bs], (period // bs, 1, 1))])
    pat_tm = pat_tm.reshape(2, period, N_MOD * D_MODEL)

    a_re, a_im, bb_re, bb_im = _zoh(ssm_lam_re[0], ssm_lam_im[0], ssm_log_dt[0],
                                    ssm_b_re[0], ssm_b_im[0])
    bt, ct_re, ct_im = _block_diag_weights(bb_re, bb_im, ssm_c_re[0], ssm_c_im[0])
    a_re = a_re.reshape(2, 1, N_STATE)
    a_im = a_im.reshape(2, 1, N_STATE)

    u = _ssm_in(x_prompt, x_sample, row(norm_mix[0]), pat_tm,
                ssm_w_in[0].astype(BF16))
    h0s_re = state_ssm_re[:, 0].reshape(bs, 2, N_STATE).transpose(1, 0, 2)
    h0s_im = state_ssm_im[:, 0].reshape(bs, 2, N_STATE).transpose(1, 0, 2)
    y_scan, fin_re, fin_im = _ssm_scan(u, bt, ct_re, ct_im, a_re, a_im,
                                       h0s_re, h0s_im, bp=bp, lp=lp, bs=bs, ls=ls)
    tri = jnp.triu(jnp.ones((tm, tm), BF16), k=1)
    x1, h2, ri, rw, cnt = _ssm_out(u, y_scan, row(ssm_d[0]),
                               ssm_w_glu[0].astype(BF16), x_prompt, x_sample,
                               pat_tm, row(norm_ffn[0]), rwt, rb, tri)
    wg_bf, wu_bf, wd_bf = (w.astype(BF16) for w in (moe_w_gate, moe_w_up,
                                                    moe_w_down))
    y_moe = _moe(h2, ri, rw, cnt, wg_bf, wu_bf, wd_bf, 0)
    x2_p, x2_s = _moe_out_tm(x1, y_moe, pat_tm, x_prompt.shape, x_sample.shape)

    pat_bm = jnp.broadcast_to(modtab[1][:1 + bs, None],
                              (1 + bs, SUBLANES, N_MOD, D_MODEL))
    pat_bm = pat_bm.reshape(1 + bs, SUBLANES, N_MOD * D_MODEL)
    x3, h2, ri, rw, cnt = _fnet(x2_p, x2_s, row(norm_mix[1]), pat_bm,
                            fnet_w_out[0].astype(BF16), row(norm_ffn[1]), rwt, rb)
    y_moe = _moe(h2, ri, rw, cnt, wg_bf, wu_bf, wd_bf, 1)
    y_p, y_s = _moe_out_final(x3, y_moe, pat_bm, row(norm_final), n_p, ls)

    st = lambda f: f.transpose(1, 0, 2).reshape(bp, 1, 2, SSM_GROUPS, STATE_DIM)
    return (y_p.reshape(bp, lp, D_MODEL), y_s.reshape(bs, ls, D_MODEL),
            st(fin_re), st(fin_im))
```

```python
import functools

import jax
import jax.numpy as jnp
from jax import lax
from jax.experimental import pallas as pl
from jax.experimental.pallas import tpu as pltpu

F32 = jnp.float32
BF16 = jnp.bfloat16
I32 = jnp.int32

D_MODEL = 1024
DEPTH = 2
SSM_GROUP = 16
SSM_GROUPS = 64
STATE_DIM = 64
N_STATE = SSM_GROUPS * STATE_DIM
FNET_GROUP = 128
N_EXPERTS = 16
N_EXPERT_GROUPS = 4
EXPERTS_PER_GROUP = 4
D_EXPERT = 1024
N_MOD = 6
EPS = 1e-6

LANES = 128
SUBLANES = 8
LANE_TILES = D_MODEL // LANES
VMEM_LIMIT = 56 * 1024 * 1024

TOKEN_TILE = 512
FNET_TILE = 256
FFN_TILE = 256
N_COND = 16

PAIRS = ((0, 1), (0, 2), (0, 3), (1, 3), (1, 2), (3, 2))
N_PAIRS = N_EXPERT_GROUPS * len(PAIRS)
N_CLASS_ROWS = 32
SLAB = SUBLANES


def _cparams(sem, vmem=VMEM_LIMIT):
    return pltpu.CompilerParams(dimension_semantics=sem, vmem_limit_bytes=vmem)


def _sigmoid(x):
    return 1.0 / (1.0 + jnp.exp(-x))


def _gelu_tanh(x):
    c = 0.7978845608028654
    return x * (0.5 * (1.0 + jnp.tanh(c * (x + 0.044715 * (x * x * x)))))


def _per_row(v, pat, fn):
    tm, d = v.shape
    p = pat.shape[0]
    return fn(v.reshape(tm // p, p, d), pat[None]).reshape(tm, d)


def _rms(x):
    ms = jnp.mean(x * x, axis=-1, keepdims=True)
    return x * lax.rsqrt(ms + EPS)


def _norm_mod(x, g, sc, sh):
    y = _rms(x) * g
    y = _per_row(y, sc, lambda a, b: a * (1.0 + b))
    return _per_row(y, sh, lambda a, b: a + b)


def _lane_tiles(v):
    return [v[:, LANES * k:LANES * (k + 1)] for k in range(v.shape[1] // LANES)]


def _to_time_major(x_ref, scr, batch):
    tt = x_ref.shape[1]
    for b in range(batch):
        for k, piece in enumerate(_lane_tiles(x_ref[b])):
            scr[k, pl.ds(b, tt, stride=batch), :] = piece
    return jnp.concatenate([scr[k] for k in range(LANE_TILES)], axis=1)


def _from_time_major(v, o_ref, scr, batch):
    tt = v.shape[0] // batch
    for k, piece in enumerate(_lane_tiles(v)):
        scr[k] = piece
    for b in range(batch):
        o_ref[b] = jnp.concatenate(
            [scr[k, pl.ds(b, tt, stride=batch), :] for k in range(LANE_TILES)],
            axis=1)


def _slab_rows(ref, k, n):
    return ref[pl.ds(k, n, stride=SLAB), :]


def _store_slabs(v, ref):
    n = v.shape[0]
    for k, piece in enumerate(_lane_tiles(v)):
        ref[pl.ds(k, n, stride=SLAB), :] = piece


def _rows_from_slabs(y_ref, n):
    return jnp.concatenate([_slab_rows(y_ref, k, n) for k in range(SLAB)], axis=1)


def _route(logits_t, rb, tri, carry):
    ne, tm = logits_t.shape
    s = _sigmoid(logits_t)
    bz = s + rb
    row = lambda a, r: a[r:r + 1, :]
    gs = []
    for g in range(N_EXPERT_GROUPS):
        v0, v1, v2, v3 = (row(bz, EXPERTS_PER_GROUP * g + j) for j in range(4))
        hi1, lo1 = jnp.maximum(v0, v1), jnp.minimum(v0, v1)
        hi2, lo2 = jnp.maximum(v2, v3), jnp.minimum(v2, v3)
        top1 = jnp.maximum(hi1, hi2)
        top2 = jnp.maximum(jnp.minimum(hi1, hi2), jnp.maximum(lo1, lo2))
        gs.append(top1 + top2)
    bg = jnp.zeros((1, tm), I32)
    bv = gs[0]
    for g in range(1, N_EXPERT_GROUPS):
        upd = gs[g] > bv
        bg = jnp.where(upd, g, bg)
        bv = jnp.where(upd, gs[g], bv)
    cb, cs = [], []
    for j in range(EXPERTS_PER_GROUP):
        vb, vs = row(bz, j), row(s, j)
        for g in range(1, N_EXPERT_GROUPS):
            sel = bg == g
            vb = jnp.where(sel, row(bz, EXPERTS_PER_GROUP * g + j), vb)
            vs = jnp.where(sel, row(s, EXPERTS_PER_GROUP * g + j), vs)
        cb.append(vb)
        cs.append(vs)
    i1 = jnp.zeros((1, tm), I32)
    b1, s1 = cb[0], cs[0]
    for j in range(1, EXPERTS_PER_GROUP):
        upd = cb[j] > b1
        i1 = jnp.where(upd, j, i1)
        b1 = jnp.where(upd, cb[j], b1)
        s1 = jnp.where(upd, cs[j], s1)
    i2 = jnp.zeros((1, tm), I32)
    b2 = jnp.full((1, tm), -jnp.inf, F32)
    s2 = jnp.zeros((1, tm), F32)
    for j in range(EXPERTS_PER_GROUP):
        cand = jnp.where(i1 == j, -jnp.inf, cb[j])
        upd = cand > b2
        i2 = jnp.where(upd, j, i2)
        b2 = jnp.where(upd, cand, b2)
        s2 = jnp.where(upd, cs[j], s2)
    den = s1 + s2
    w1 = s1 / den
    w2 = s2 / den
    lo = jnp.minimum(i1, i2)
    hi = jnp.maximum(i1, i2)
    pidx = jnp.where(lo == 0, hi - 1,
                     jnp.where(lo == 1, jnp.where(hi == 3, 3, 4), 5))
    first = jnp.where(pidx < 3, 0, jnp.where(pidx < 5, 1, 3))
    wa = jnp.where(i1 == first, w1, w2)
    wb = jnp.where(i1 == first, w2, w1)
    q = len(PAIRS) * bg + pidx
    qio = lax.broadcasted_iota(I32, (N_CLASS_ROWS, tm), 0)
    oh = qio == q
    ohf = jnp.where(oh, 1.0, 0.0)
    cum = jnp.dot(ohf.astype(BF16), tri, preferred_element_type=F32) + carry
    rank = jnp.sum(jnp.where(oh, cum, 0.0), axis=0, keepdims=True)
    new_carry = carry + jnp.sum(ohf, axis=1, keepdims=True)
    return q, rank.astype(I32), wa, wb, new_carry


def _post_mixer(m, x, g1, nf, sc2, sh2, rwt_ref, rb_ref, tri_ref,
                x1_ref, h2_ref, ri_ref, rw_ref, cnt_ref, carry, valid=None):
    tm = x.shape[0]
    x1 = x + _per_row(m, g1, lambda a, b: a * b)
    x1_ref[...] = x1
    h2 = _norm_mod(x1, nf, sc2, sh2)
    h_hi = h2.astype(BF16)
    h_lo = (h2 - h_hi.astype(F32)).astype(BF16)
    nt = (((1,), (1,)), ((), ()))
    la = lax.dot_general(rwt_ref[...], h_hi, nt, preferred_element_type=F32)
    lb = lax.dot_general(rwt_ref[0:N_EXPERTS, :], h_lo, nt,
                         preferred_element_type=F32)
    logits_t = la[:N_EXPERTS] + la[N_EXPERTS:] + lb
    q, rank, wa, wb, nc = _route(logits_t, rb_ref[...], tri_ref[...],
                                 carry[:, 0:1])
    _store_slabs(h2, h2_ref)
    rw_ref[0:1, :] = wa
    rw_ref[1:2, :] = wb
    rw_ref[2:8, :] = jnp.zeros((6, tm), F32)
    ri_ref[0:1, :] = q
    ri_ref[1:2, :] = rank
    ri_ref[2:8, :] = jnp.zeros((6, tm), I32)
    if valid is not None:
        nc = jnp.where(valid, nc, carry[:, 0:1])
    carry[...] = jnp.broadcast_to(nc, carry.shape)
    cnt_ref[...] = carry[...]


def _ada_kernel(cond_ref, w_ref, b_ref, o_ref):
    c = cond_ref[...]
    s = c * _sigmoid(c)
    s_hi = s.astype(BF16)
    s_lo = (s - s_hi.astype(F32)).astype(BF16)
    w = w_ref[...]
    w_hi = w.astype(BF16)
    w_lo = (w - w_hi.astype(F32)).astype(BF16)
    dot = functools.partial(jnp.dot, preferred_element_type=F32)
    o_ref[...] = dot(s_hi, w_hi) + dot(s_lo, w_hi) + dot(s_hi, w_lo) + b_ref[...]


def _ada_table(cond, w_ada, b_ada):
    tn = 1536
    n_out = N_MOD * D_MODEL
    return pl.pallas_call(
        _ada_kernel,
        out_shape=jax.ShapeDtypeStruct((DEPTH, N_COND, n_out), F32),
        grid=(DEPTH, n_out // tn),
        in_specs=[
            pl.BlockSpec((N_COND, D_MODEL), lambda l, j: (0, 0)),
            pl.BlockSpec((None, D_MODEL, tn), lambda l, j: (l, 0, j)),
            pl.BlockSpec((None, 1, tn), lambda l, j: (l, 0, j)),
        ],
        out_specs=pl.BlockSpec((None, N_COND, tn), lambda l, j: (l, 0, j)),
        compiler_params=_cparams(("arbitrary", "arbitrary")),
        name="ada_table",
    )(cond, w_ada, b_ada.reshape(DEPTH, 1, n_out))


def _zoh_kernel(lr_ref, li_ref, ldt_ref, br_ref, bi_ref,
                are_ref, aim_ref, bbre_ref, bbim_ref):
    lr = lr_ref[...]
    li = li_ref[...]
    dt = jnp.exp(ldt_ref[...])
    mag = jnp.exp(lr * dt)
    a_re = mag * jnp.cos(li * dt)
    a_im = mag * jnp.sin(li * dt)
    den = lr * lr + li * li
    nr = a_re - 1.0
    f_re = (nr * lr + a_im * li) / den
    f_im = (a_im * lr - nr * li) / den
    br = br_ref[...]
    bi = bi_ref[...]
    are_ref[...] = a_re
    aim_ref[...] = a_im
    bbre_ref[...] = f_re * br - f_im * bi
    bbim_ref[...] = f_re * bi + f_im * br


def _zoh(lam_re, lam_im, log_dt, b_re, b_im):
    shape = b_re.shape
    flat = (shape[0] * shape[1] * shape[2] * shape[3] // LANES, LANES)
    bc = lambda a: jnp.broadcast_to(a, shape).reshape(flat)
    args = (bc(lam_re[..., None]), bc(lam_im[..., None]),
            bc(log_dt[:, :, None, None]), b_re.reshape(flat), b_im.reshape(flat))
    outs = pl.pallas_call(
        _zoh_kernel,
        out_shape=[jax.ShapeDtypeStruct(flat, F32)] * 4,
        name="zoh_discretize",
    )(*args)
    a_re, a_im, bb_re, bb_im = (o.reshape(shape) for o in outs)
    return a_re[..., 0], a_im[..., 0], bb_re, bb_im


def _stream_specs(xp, xs, n_p_tiles):
    bp, bs = xp.shape[0], xs.shape[0]
    return [
        pl.BlockSpec((bp, TOKEN_TILE // bp, D_MODEL),
                     lambda i: (0, jnp.minimum(i, n_p_tiles - 1), 0)),
        pl.BlockSpec((bs, TOKEN_TILE // bs, D_MODEL),
                     lambda i: (0, jnp.maximum(i - n_p_tiles, 0), 0)),
    ]


def _ssm_in_kernel(xp_ref, xs_ref, g_ref, sc_ref, sh_ref, w_ref, u_ref, scr,
                   hbuf, *, n_p_tiles):
    i = pl.program_id(0)
    wslot = i % 2

    @pl.when(i == 0)
    def _():
        hbuf[...] = jnp.zeros(hbuf.shape, BF16)

    def run(x_ref):
        u_ref[...] = jnp.dot(hbuf[1 - wslot], w_ref[...],
                             preferred_element_type=F32)
        x = _to_time_major(x_ref, scr, x_ref.shape[0])
        hbuf[wslot] = _norm_mod(x, g_ref[...], sc_ref[...],
                                sh_ref[...]).astype(BF16)

    pl.when(i < n_p_tiles)(lambda: run(xp_ref))
    pl.when(i >= n_p_tiles)(lambda: run(xs_ref))


def _ssm_in(xp, xs, g, modpat, w_in_bf):
    bp, bs = xp.shape[0], xs.shape[0]
    n = bp * xp.shape[1] + bs * xs.shape[1]
    tm = TOKEN_TILE
    n_tiles = n // tm
    n_p_tiles = bp * xp.shape[1] // tm
    p = modpat.shape[1]
    pat = lambda i: jnp.where(i < n_p_tiles, 0, 1)
    mod = lambda k: pl.BlockSpec((None, p, D_MODEL), lambda i: (pat(i), 0, k))
    return pl.pallas_call(
        functools.partial(_ssm_in_kernel, n_p_tiles=n_p_tiles),
        out_shape=jax.ShapeDtypeStruct((n, D_MODEL), F32),
        grid=(n_tiles + 1,),
        in_specs=[
            pl.BlockSpec((bp, tm // bp, D_MODEL),
                         lambda i: (0, jnp.minimum(i, n_p_tiles - 1), 0)),
            pl.BlockSpec((bs, tm // bs, D_MODEL),
                         lambda i: (0, jnp.clip(i - n_p_tiles, 0,
                                                n_tiles - n_p_tiles - 1), 0)),
            pl.BlockSpec((1, D_MODEL), lambda i: (0, 0)),
            mod(1), mod(0),
            pl.BlockSpec((D_MODEL, D_MODEL), lambda i: (0, 0)),
        ],
        out_specs=pl.BlockSpec((tm, D_MODEL), lambda i: (jnp.maximum(i - 1, 0), 0)),
        scratch_shapes=[pltpu.VMEM((LANE_TILES, tm, LANES), F32),
                        pltpu.VMEM((2, tm, D_MODEL), BF16)],
        compiler_params=_cparams(("arbitrary",)),
        name="ssm_in",
    )(xp, xs, g, modpat, modpat, w_in_bf)


def _scan_kernel(u_ref, bt_ref, cre_ref, cim_ref, are_ref, aim_ref,
                 h0re_ref, h0im_ref, y_ref, fre_ref, fim_ref,
                 *scratch, n_p_chunks, bp, bs):
    xre = scratch[:LANE_TILES]
    xim = scratch[LANE_TILES:2 * LANE_TILES]
    hre, him = scratch[2 * LANE_TILES:]
    d = pl.program_id(0)
    c = pl.program_id(1)
    cols = N_STATE // LANE_TILES
    rows = u_ref.shape[0]

    def chunk(batch, reverse):
        steps = rows // batch

        def b_proj(s):
            u = u_ref[:, LANES * s:LANES * (s + 1)].astype(BF16)
            xt = jnp.dot(u, bt_ref[s], preferred_element_type=F32)
            xre[s][...] = xt[:, :cols]
            xim[s][...] = xt[:, cols:]

        def scan(s):
            sl = slice(cols * s, cols * (s + 1))
            ar = jnp.broadcast_to(are_ref[:, sl], (batch, cols))
            ai = jnp.broadcast_to(aim_ref[:, sl], (batch, cols))
            hr = hre[0:batch, sl]
            hi = him[0:batch, sl]
            for t in range(steps):
                tt = steps - 1 - t if reverse else t
                r = slice(tt * batch, (tt + 1) * batch)
                nr = ar * hr - ai * hi + xre[s][r, :]
                ni = ar * hi + ai * hr + xim[s][r, :]
                xre[s][r, :] = nr
                xim[s][r, :] = ni
                hr, hi = nr, ni
            hre[0:batch, sl] = hr
            him[0:batch, sl] = hi

        def c_proj(s):
            y_ref[:, LANES * s:LANES * (s + 1)] = (
                jnp.dot(xre[s][...].astype(BF16), cre_ref[s],
                        preferred_element_type=F32)
                - jnp.dot(xim[s][...].astype(BF16), cim_ref[s],
                          preferred_element_type=F32))

        for stage in range(LANE_TILES + 2):
            if stage < LANE_TILES:
                b_proj(stage)
            if 0 <= stage - 1 < LANE_TILES:
                scan(stage - 1)
            if 0 <= stage - 2 < LANE_TILES:
                c_proj(stage - 2)

    @pl.when(c == 0)
    def _():
        hre[...] = jnp.zeros(hre.shape, F32)
        him[...] = jnp.zeros(him.shape, F32)

    @pl.when(c == n_p_chunks)
    def _():
        hre[0:bs, :] = h0re_ref[...]
        him[0:bs, :] = h0im_ref[...]

    in_p = c < n_p_chunks
    in_s = jnp.logical_not(in_p)
    fwd = d == 0
    bwd = jnp.logical_not(fwd)
    pl.when(jnp.logical_and(in_p, fwd))(lambda: chunk(bp, False))
    pl.when(jnp.logical_and(in_p, bwd))(lambda: chunk(bp, True))
    pl.when(jnp.logical_and(in_s, fwd))(lambda: chunk(bs, False))
    pl.when(jnp.logical_and(in_s, bwd))(lambda: chunk(bs, True))

    @pl.when(c == n_p_chunks - 1)
    def _():
        fre_ref[...] = hre[0:bp, :]
        fim_ref[...] = him[0:bp, :]


def _ssm_scan(u, bt, ct_re, ct_im, a_re, a_im, h0_re, h0_im, *, bp, lp, bs, ls):
    n = u.shape[0]
    rows = TOKEN_TILE
    n_p_chunks = bp * lp // rows
    n_s_chunks = bs * ls // rows

    def chunk(d, c):
        in_p = jnp.where(d == 0, c, n_p_chunks - 1 - c)
        cs = c - n_p_chunks
        in_s = n_p_chunks + jnp.where(d == 0, cs, n_s_chunks - 1 - cs)
        return jnp.where(c < n_p_chunks, in_p, in_s)

    cols = N_STATE // LANE_TILES
    dspec = lambda shape: pl.BlockSpec((None,) + shape,
                                       lambda d, c: (d,) + (0,) * len(shape))
    kern = functools.partial(_scan_kernel, n_p_chunks=n_p_chunks, bp=bp, bs=bs)
    return pl.pallas_call(
        kern,
        out_shape=[jax.ShapeDtypeStruct((2, n, D_MODEL), F32),
                   jax.ShapeDtypeStruct((2, bp, N_STATE), F32),
                   jax.ShapeDtypeStruct((2, bp, N_STATE), F32)],
        grid=(2, n_p_chunks + n_s_chunks),
        in_specs=[
            pl.BlockSpec((rows, D_MODEL), lambda d, c: (chunk(d, c), 0)),
            dspec((LANE_TILES, LANES, 2 * cols)),
            dspec((LANE_TILES, cols, LANES)),
            dspec((LANE_TILES, cols, LANES)),
            dspec((1, N_STATE)),
            dspec((1, N_STATE)),
            dspec((bs, N_STATE)),
            dspec((bs, N_STATE)),
        ],
        out_specs=[
            pl.BlockSpec((None, rows, D_MODEL), lambda d, c: (d, chunk(d, c), 0)),
            dspec((bp, N_STATE)),
            dspec((bp, N_STATE)),
        ],
        scratch_shapes=(
            [pltpu.VMEM((rows, cols), F32)] * (2 * LANE_TILES)
            + [pltpu.VMEM((max(bp, bs), N_STATE), F32)] * 2),
        compiler_params=_cparams(("arbitrary", "arbitrary")),
        name="ssm_scan",
    )(u, bt, ct_re, ct_im, a_re, a_im, h0_re, h0_im)


def _ssm_out_kernel(u_ref, y0_ref, y1_ref, dsk_ref, wglu_ref, xp_ref, xs_ref,
                    g1_ref, nf_ref, sc2_ref, sh2_ref, rwt_ref, rb_ref, tri_ref,
                    x1_ref, h2_ref, ri_ref, rw_ref, cnt_ref, scr, carry,
                    *, n_p_tiles):
    i = pl.program_id(0)

    @pl.when(i == 0)
    def _():
        carry[...] = jnp.zeros(carry.shape, F32)

    def run(x_ref):
        x = _to_time_major(x_ref, scr, x_ref.shape[0])
        y = u_ref[...] * dsk_ref[...] + y0_ref[...] + y1_ref[...]
        ge = _gelu_tanh(y).astype(BF16)
        vg = jnp.dot(ge, wglu_ref[...], preferred_element_type=F32)
        m = vg[:, :D_MODEL] * _sigmoid(vg[:, D_MODEL:])
        _post_mixer(m, x, g1_ref[...], nf_ref[...], sc2_ref[...], sh2_ref[...],
                    rwt_ref, rb_ref, tri_ref, x1_ref, h2_ref, ri_ref, rw_ref,
                    cnt_ref, carry)

    pl.when(i < n_p_tiles)(lambda: run(xp_ref))
    pl.when(i >= n_p_tiles)(lambda: run(xs_ref))


def _route_out_shapes(n):
    return [jax.ShapeDtypeStruct((n, D_MODEL), F32),
            jax.ShapeDtypeStruct((n * SLAB, LANES), F32),
            jax.ShapeDtypeStruct((SUBLANES, n), I32),
            jax.ShapeDtypeStruct((SUBLANES, n), F32),
            jax.ShapeDtypeStruct((N_CLASS_ROWS, LANES), F32)]


def _ssm_out(u, y, d_skip, w_glu_bf, xp, xs, modpat, norm_ffn, rwt, rb, tri):
    n = u.shape[0]
    tm = TOKEN_TILE
    n_p_tiles = xp.shape[0] * xp.shape[1] // tm
    p = modpat.shape[1]
    pat = lambda i: jnp.where(i < n_p_tiles, 0, 1)
    mod = lambda k: pl.BlockSpec((None, p, D_MODEL), lambda i: (pat(i), 0, k))
    full = lambda shape: pl.BlockSpec(shape, lambda i: (0,) * len(shape))
    rowblk = pl.BlockSpec((tm, D_MODEL), lambda i: (i, 0))
    return pl.pallas_call(
        functools.partial(_ssm_out_kernel, n_p_tiles=n_p_tiles),
        out_shape=_route_out_shapes(n),
        grid=(n // tm,),
        in_specs=[
            rowblk,
            pl.BlockSpec((None, tm, D_MODEL), lambda i: (0, i, 0)),
            pl.BlockSpec((None, tm, D_MODEL), lambda i: (1, i, 0)),
            full((1, D_MODEL)),
            full((D_MODEL, 2 * D_MODEL)),
        ] + _stream_specs(xp, xs, n_p_tiles) + [
            mod(2), full((1, D_MODEL)), mod(4), mod(3),
            full((2 * N_EXPERTS, D_MODEL)), full((N_EXPERTS, 1)), full((tm, tm)),
        ],
        out_specs=[rowblk,
                   pl.BlockSpec((tm * SLAB, LANES), lambda i: (i, 0)),
                   pl.BlockSpec((SUBLANES, tm), lambda i: (0, i)),
                   pl.BlockSpec((SUBLANES, tm), lambda i: (0, i)),
                   full((N_CLASS_ROWS, LANES))],
        scratch_shapes=[pltpu.VMEM((LANE_TILES, tm, LANES), F32),
                        pltpu.VMEM((N_CLASS_ROWS, LANES), F32)],
        compiler_params=_cparams(("arbitrary",)),
        name="ssm_out",
    )(u, y, y, d_skip, w_glu_bf, xp, xs, modpat, norm_ffn, modpat, modpat,
      rwt, rb, tri)


def _fnet_kernel(xp_ref, xpp_ref, xs_ref, nm_ref, sc1_ref, sh1_ref, cs_ref,
                 dftp_ref, dfts_ref, wout_ref, g1_ref, nf_ref, sc2_ref, sh2_ref,
                 rwt_ref, rb_ref, tri_ref, x1_ref, h2_ref, ri_ref, rw_ref, cnt_ref,
                 ucs, ybuf, carry, *, bp, tiles_s):
    s = pl.program_id(0)
    n_groups = D_MODEL // FNET_GROUP
    tr = FNET_TILE
    per = tiles_s + 1
    wslot = s % 2

    @pl.when(s == 0)
    def _():
        carry[...] = jnp.zeros(carry.shape, F32)
        ybuf[...] = jnp.zeros(ybuf.shape, F32)

    def stage1(x_ref, seq):
        def body(r, _):
            rows = pl.ds(pl.multiple_of(r * tr, tr), tr)
            h = _norm_mod(x_ref[rows, :], nm_ref[...], sc1_ref[...],
                          sh1_ref[...]).astype(BF16)
            for k in range(n_groups):
                cols = slice(FNET_GROUP * k, FNET_GROUP * (k + 1))
                t = jnp.dot(h[:, cols], cs_ref[...], preferred_element_type=F32)
                ucs[rows, cols] = t[:, :FNET_GROUP].astype(BF16)
                ucs[pl.ds(pl.multiple_of(seq + r * tr, tr), tr), cols] = (
                    t[:, FNET_GROUP:].astype(BF16))
            return 0
        lax.fori_loop(0, seq // tr, body, 0)

    def step(dft_ref, seq, xrow, valid):
        scale = float((seq * FNET_GROUP) ** -0.5)
        m = jnp.dot(ybuf[1 - wslot].astype(BF16), wout_ref[...],
                    preferred_element_type=F32)
        _post_mixer(m, xrow, g1_ref[...], nf_ref[...], sc2_ref[...],
                    sh2_ref[...], rwt_ref, rb_ref, tri_ref, x1_ref, h2_ref,
                    ri_ref, rw_ref, cnt_ref, carry, valid)
        ybuf[wslot] = jnp.dot(dft_ref[...], ucs[0:2 * seq, :],
                              preferred_element_type=F32) * scale

    @pl.when(s <= bp)
    def _():
        stage1(xp_ref, xp_ref.shape[0])
        step(dftp_ref, xp_ref.shape[0], xpp_ref[...], s >= 1)

    @pl.when(s > bp)
    def _():
        j = (s - bp - 1) % per
        pl.when(j == 0)(lambda: stage1(xs_ref, xs_ref.shape[0]))
        prev = jnp.maximum(j - 1, 0)
        xrow = xs_ref[pl.ds(pl.multiple_of(prev * tr, tr), tr), :]
        step(dfts_ref, xs_ref.shape[0], xrow, j >= 1)


def _dft_table(seq):
    inner = 64
    k = jnp.arange(seq, dtype=I32)[:, None]
    t1 = jnp.arange(seq // inner, dtype=I32)[None, :] * inner
    t2 = jnp.arange(inner, dtype=I32)[None, :]
    ang = lambda t: ((k * t) % seq).astype(F32) * (2.0 * jnp.pi / seq)
    ca, sa = jnp.cos(ang(t1))[:, :, None], jnp.sin(ang(t1))[:, :, None]
    cb, sb = jnp.cos(ang(t2))[:, None, :], jnp.sin(ang(t2))[:, None, :]
    cos = (ca * cb - sa * sb).reshape(seq, seq)
    sin = (sa * cb + ca * sb).reshape(seq, seq)
    return jnp.concatenate([cos, -sin], axis=1).astype(BF16)


def _fnet(xp, xs, norm_mix, modpat, w_out_bf, norm_ffn, rwt, rb):
    bp, lp, _ = xp.shape
    bs, ls, _ = xs.shape
    tr = FNET_TILE
    assert lp == tr
    tiles_s = ls // tr
    per = tiles_s + 1
    n = bp * lp + bs * ls
    p = modpat.shape[1]
    kc = jnp.arange(FNET_GROUP, dtype=I32)
    angc = ((kc[:, None] * kc[None, :]) % FNET_GROUP).astype(F32) * (
        2.0 * jnp.pi / FNET_GROUP)
    cs = jnp.concatenate([jnp.cos(angc), jnp.sin(angc)], axis=1).astype(BF16)
    tri = jnp.triu(jnp.ones((tr, tr), BF16), k=1)

    q = lambda s: jnp.maximum(s - bp - 1, 0)
    sb = lambda s: jnp.minimum(q(s) // per, bs - 1)
    sj = lambda s: q(s) % per
    pat = lambda s: jnp.where(s <= bp, 0, 1 + sb(s))
    blk = lambda s: jnp.where(s <= bp, jnp.maximum(s - 1, 0),
                              bp + sb(s) * tiles_s + jnp.maximum(sj(s) - 1, 0))
    mod = lambda kk: pl.BlockSpec((None, p, D_MODEL), lambda s: (pat(s), 0, kk))
    full = lambda shape: pl.BlockSpec(shape, lambda s: (0,) * len(shape))
    kern = functools.partial(_fnet_kernel, bp=bp, tiles_s=tiles_s)
    return pl.pallas_call(
        kern,
        out_shape=_route_out_shapes(n),
        grid=(bp + 1 + bs * per,),
        in_specs=[
            pl.BlockSpec((None, lp, D_MODEL),
                         lambda s: (jnp.minimum(s, bp - 1), 0, 0)),
            pl.BlockSpec((None, lp, D_MODEL),
                         lambda s: (jnp.clip(s - 1, 0, bp - 1), 0, 0)),
            pl.BlockSpec((None, ls, D_MODEL), lambda s: (sb(s), 0, 0)),
            full((1, D_MODEL)), mod(1), mod(0),
            full((FNET_GROUP, 2 * FNET_GROUP)),
            full((tr, 2 * lp)),
            pl.BlockSpec((tr, 2 * ls),
                         lambda s: (jnp.minimum(sj(s), tiles_s - 1), 0)),
            full((D_MODEL, D_MODEL)),
            mod(2), full((1, D_MODEL)), mod(4), mod(3),
            full((2 * N_EXPERTS, D_MODEL)), full((N_EXPERTS, 1)), full((tr, tr)),
        ],
        out_specs=[pl.BlockSpec((tr, D_MODEL), lambda s: (blk(s), 0)),
                   pl.BlockSpec((tr * SLAB, LANES), lambda s: (blk(s), 0)),
                   pl.BlockSpec((SUBLANES, tr), lambda s: (0, blk(s))),
                   pl.BlockSpec((SUBLANES, tr), lambda s: (0, blk(s))),
                   full((N_CLASS_ROWS, LANES))],
        scratch_shapes=[pltpu.VMEM((2 * ls, D_MODEL), BF16),
                        pltpu.VMEM((2, tr, D_MODEL), F32),
                        pltpu.VMEM((N_CLASS_ROWS, LANES), F32)],
        compiler_params=_cparams(("arbitrary",)),
        name="fnet",
    )(xp, xp, xs, norm_mix, modpat, modpat, cs, _dft_table(lp), _dft_table(ls),
      w_out_bf, modpat, norm_ffn, modpat, modpat, rwt, rb, tri)


def _invert_kernel(slot_ref, ends_ref, gsrc_ref, sdst_ref):
    n = slot_ref.shape[0]
    n_slots = gsrc_ref.shape[0]
    t = FFN_TILE
    unroll = 8

    def pad_tile(base):
        def pad(j, _):
            for k in range(unroll):
                r = j * unroll + k
                gsrc_ref[base + r] = 0
                sdst_ref[base + r] = n + r
            return 0
        lax.fori_loop(0, t // unroll, pad, 0)

    for q in range(N_PAIRS):
        pad_tile(jnp.maximum(ends_ref[q] - t, 0))

    def unused(b, _):
        pad_tile(b * t)
        return 0
    lax.fori_loop(ends_ref[N_PAIRS - 1] // t, n_slots // t, unused, 0)

    def body(j, _):
        for k in range(unroll):
            t = j * unroll + k
            s = slot_ref[t]
            gsrc_ref[s] = t
            sdst_ref[s] = t
        return 0
    lax.fori_loop(0, n // unroll, body, 0)


def _invert(slot, ends, n_slots):
    smem = pl.BlockSpec(memory_space=pltpu.SMEM)
    return pl.pallas_call(
        _invert_kernel,
        out_shape=[jax.ShapeDtypeStruct((n_slots,), I32)] * 2,
        in_specs=[smem, smem],
        out_specs=[smem, smem],
        name="moe_invert",
    )(slot, ends)


def _ffn_kernel(tea_ref, teb_ref, nu_ref, gsrc_ref, sdst_ref, wa_ref, wb_ref,
                h_hbm, wga, wua, wda, wgb, wub, wdb, y_hbm,
                xbuf0, xbuf1, obuf0, obuf1, wcola, wcolb, gsem, ssem):
    del tea_ref, teb_ref
    i = pl.program_id(0)
    nu = nu_ref[0]
    t = FFN_TILE

    def slab(ref, r):
        return ref.at[pl.ds(pl.multiple_of(r * SLAB, SLAB), SLAB), :]

    def gather_start(tile, xbuf, sem):
        base = tile * t
        for r in range(t):
            pltpu.make_async_copy(slab(h_hbm, gsrc_ref[base + r]), slab(xbuf, r),
                                  sem).start(priority=r % 2)

    def gather_wait(xbuf, sem):
        pltpu.make_async_copy(h_hbm.at[pl.ds(0, t * SLAB), :], xbuf, sem).wait()

    def scatter_start(tile, obuf, sem):
        base = tile * t
        for r in range(t):
            pltpu.make_async_copy(slab(obuf, r), slab(y_hbm, sdst_ref[base + r]),
                                  sem).start(priority=r % 2)

    def scatter_wait(obuf, sem):
        pltpu.make_async_copy(obuf, y_hbm.at[pl.ds(0, t * SLAB), :], sem).wait()

    def expert(xb, wg, wu, wd):
        g = jnp.dot(xb, wg[...], preferred_element_type=F32)
        u = jnp.dot(xb, wu[...], preferred_element_type=F32)
        a = (g * _sigmoid(g)) * u
        return jnp.dot(a.astype(BF16), wd[...], preferred_element_type=F32)

    def step(xc, xn, oc, op, gc, gn, sc, sp):
        pl.when(i == 0)(lambda: gather_start(0, xc, gc))
        pl.when(i >= 2)(lambda: scatter_wait(oc, sc))
        gather_wait(xc, gc)
        pl.when(i + 1 < nu)(lambda: gather_start(i + 1, xn, gn))
        pl.when(i >= 1)(lambda: scatter_start(i - 1, op, sp))
        for r in range(t):
            tok = gsrc_ref[i * t + r]
            wcola[r:r + 1, :] = jnp.full((1, LANES), wa_ref[tok], F32)
            wcolb[r:r + 1, :] = jnp.full((1, LANES), wb_ref[tok], F32)
        xb = _rows_from_slabs(xc, t).astype(BF16)
        ya = expert(xb, wga, wua, wda)
        yb = expert(xb, wgb, wub, wdb)
        y = wcola[:, 0:1] * ya + wcolb[:, 0:1] * yb
        for k, piece in enumerate(_lane_tiles(y)):
            oc[pl.ds(k, t, stride=SLAB), :] = piece

    def drain(oc, op, sc, sp):
        pl.when(i >= 2)(lambda: scatter_wait(oc, sc))
        scatter_start(i - 1, op, sp)
        scatter_wait(op, sp)
        spare_row = y_hbm.shape[0] - t * SLAB
        oc[...] = jnp.zeros(oc.shape, F32)
        spare = pltpu.make_async_copy(
            oc, y_hbm.at[pl.ds(spare_row, t * SLAB), :], sc)
        spare.start()
        spare.wait()

    even = (i % 2) == 0
    odd = jnp.logical_not(even)
    g0, g1, s0, s1 = gsem.at[0], gsem.at[1], ssem.at[0], ssem.at[1]
    pl.when(jnp.logical_and(i < nu, even))(
        lambda: step(xbuf0, xbuf1, obuf0, obuf1, g0, g1, s0, s1))
    pl.when(jnp.logical_and(i < nu, odd))(
        lambda: step(xbuf1, xbuf0, obuf1, obuf0, g1, g0, s1, s0))
    pl.when(jnp.logical_and(i == nu, even))(
        lambda: drain(obuf0, obuf1, s0, s1))
    pl.when(jnp.logical_and(i == nu, odd))(
        lambda: drain(obuf1, obuf0, s1, s0))


def _expert_ffn(tea, teb, n_used, gsrc, sdst, wa_tok, wb_tok, h2, wg, wu, wd,
                layer):
    n_tok = h2.shape[0] // SLAB
    max_tiles = gsrc.shape[0] // FFN_TILE
    wa = lambda a, b: pl.BlockSpec(
        (None, None, a, b), lambda i, ta, tb, *_: (layer, ta[i], 0, 0))
    wb = lambda a, b: pl.BlockSpec(
        (None, None, a, b), lambda i, ta, tb, *_: (layer, tb[i], 0, 0))
    anyspec = pl.BlockSpec(memory_space=pl.ANY)
    return pl.pallas_call(
        _ffn_kernel,
        out_shape=jax.ShapeDtypeStruct(((n_tok + FFN_TILE) * SLAB, LANES), F32),
        grid_spec=pltpu.PrefetchScalarGridSpec(
            num_scalar_prefetch=7,
            grid=(max_tiles + 1,),
            in_specs=[anyspec,
                      wa(D_MODEL, D_EXPERT), wa(D_MODEL, D_EXPERT),
                      wa(D_EXPERT, D_MODEL),
                      wb(D_MODEL, D_EXPERT), wb(D_MODEL, D_EXPERT),
                      wb(D_EXPERT, D_MODEL)],
            out_specs=anyspec,
            scratch_shapes=[pltpu.VMEM((FFN_TILE * SLAB, LANES), F32)] * 4 + [
                            pltpu.VMEM((FFN_TILE, LANES), F32),
                            pltpu.VMEM((FFN_TILE, LANES), F32),
                            pltpu.SemaphoreType.DMA((2,)),
                            pltpu.SemaphoreType.DMA((2,))],
        ),
        compiler_params=_cparams(("arbitrary",)),
        name="moe_ffn",
    )(tea, teb, n_used, gsrc, sdst, wa_tok, wb_tok, h2, wg, wu, wd, wg, wu, wd)


def _moe(h2, ri, rw, cnt, wg_bf, wu_bf, wd_bf, layer):
    n = h2.shape[0] // SLAB
    max_tiles = n // FFN_TILE + N_PAIRS
    n_slots = max_tiles * FFN_TILE
    counts = cnt[:N_PAIRS, 0].astype(I32)
    padded = ((counts + FFN_TILE - 1) // FFN_TILE) * FFN_TILE
    ends = jnp.cumsum(padded)
    offs = ends - padded
    q, rank = ri[0], ri[1]
    cls = jnp.arange(N_PAIRS, dtype=I32)
    slot = rank + jnp.sum(jnp.where(q[None, :] == cls[:, None], offs[:, None], 0),
                          axis=0)
    n_used = ends[-1] // FFN_TILE
    tile = jnp.arange(max_tiles + 1, dtype=I32)
    tq = jnp.sum((tile[:, None] * FFN_TILE >= ends[None, :]).astype(I32), axis=1)
    tq_last = jnp.sum(((n_used - 1) * FFN_TILE >= ends).astype(I32))
    tq = jnp.where(tile < n_used, tq, tq_last)
    pa = jnp.array([a for a, _ in PAIRS], I32)
    pb = jnp.array([b for _, b in PAIRS], I32)
    grp, pidx = tq // len(PAIRS), tq % len(PAIRS)
    tea = EXPERTS_PER_GROUP * grp + jnp.take(pa, pidx)
    teb = EXPERTS_PER_GROUP * grp + jnp.take(pb, pidx)
    gsrc, sdst = _invert(slot, ends, n_slots)
    return _expert_ffn(tea, teb, n_used.reshape(1), gsrc, sdst, rw[0], rw[1], h2,
                       wg_bf, wu_bf, wd_bf, layer)


def _moe_out_tm_kernel(x1_ref, y_ref, g2_ref, op_ref, os_ref, scr, *, n_p_tiles):
    i = pl.program_id(0)
    y = _rows_from_slabs(y_ref, x1_ref.shape[0])
    x2 = x1_ref[...] + _per_row(y, g2_ref[...], lambda a, b: a * b)
    pl.when(i < n_p_tiles)(
        lambda: _from_time_major(x2, op_ref, scr, op_ref.shape[0]))
    pl.when(i >= n_p_tiles)(
        lambda: _from_time_major(x2, os_ref, scr, os_ref.shape[0]))


def _moe_out_tm(x1, y, modpat, shape_p, shape_s):
    n = x1.shape[0]
    tm = TOKEN_TILE
    n_p_tiles = shape_p[0] * shape_p[1] // tm
    p = modpat.shape[1]
    rowblk = pl.BlockSpec((tm, D_MODEL), lambda i: (i, 0))
    stream = _stream_specs(jax.ShapeDtypeStruct(shape_p, F32),
                           jax.ShapeDtypeStruct(shape_s, F32), n_p_tiles)
    return pl.pallas_call(
        functools.partial(_moe_out_tm_kernel, n_p_tiles=n_p_tiles),
        out_shape=[jax.ShapeDtypeStruct(shape_p, F32),
                   jax.ShapeDtypeStruct(shape_s, F32)],
        grid=(n // tm,),
        in_specs=[rowblk,
                  pl.BlockSpec((tm * SLAB, LANES), lambda i: (i, 0)),
                  pl.BlockSpec((None, p, D_MODEL),
                               lambda i: (jnp.where(i < n_p_tiles, 0, 1), 0, 5))],
        out_specs=stream,
        scratch_shapes=[pltpu.VMEM((LANE_TILES, tm, LANES), F32)],
        compiler_params=_cparams(("arbitrary",)),
        name="moe_out_tm",
    )(x1, y, modpat)


def _moe_out_final_kernel(x1_ref, y_ref, g2_ref, nfin_ref, op_ref, os_ref,
                          *, n_p_tiles):
    i = pl.program_id(0)
    y = _rows_from_slabs(y_ref, x1_ref.shape[0])
    x2 = x1_ref[...] + _per_row(y, g2_ref[...], lambda a, b: a * b)
    out = _rms(x2) * nfin_ref[...]

    @pl.when(i < n_p_tiles)
    def _():
        op_ref[...] = out

    @pl.when(i >= n_p_tiles)
    def _():
        os_ref[...] = out


def _moe_out_final(x1, y, modpat, norm_final, n_p, rows_per_request):
    n = x1.shape[0]
    tm = TOKEN_TILE
    n_p_tiles = n_p // tm
    p = modpat.shape[1]
    rowblk = pl.BlockSpec((tm, D_MODEL), lambda i: (i, 0))
    pat = lambda i: jnp.where(
        i < n_p_tiles, 0, 1 + (i - n_p_tiles) // (rows_per_request // tm))
    return pl.pallas_call(
        functools.partial(_moe_out_final_kernel, n_p_tiles=n_p_tiles),
        out_shape=[jax.ShapeDtypeStruct((n_p, D_MODEL), F32),
                   jax.ShapeDtypeStruct((n - n_p, D_MODEL), F32)],
        grid=(n // tm,),
        in_specs=[rowblk,
                  pl.BlockSpec((tm * SLAB, LANES), lambda i: (i, 0)),
                  pl.BlockSpec((None, p, D_MODEL), lambda i: (pat(i), 0, 5)),
                  pl.BlockSpec((1, D_MODEL), lambda i: (0, 0))],
        out_specs=[
            pl.BlockSpec((tm, D_MODEL),
                         lambda i: (jnp.minimum(i, n_p_tiles - 1), 0)),
            pl.BlockSpec((tm, D_MODEL),
                         lambda i: (jnp.maximum(i - n_p_tiles, 0), 0))],
        compiler_params=_cparams(("arbitrary",)),
        name="moe_out_final",
    )(x1, y, modpat, norm_final)


def _block_diag_weights(bb_re, bb_im, c_re, c_im):
    lt = LANE_TILES
    gl = SSM_GROUPS // lt
    eye = jnp.eye(gl, dtype=F32)

    def in_map(bb):
        b5 = bb.reshape(2, lt, gl, STATE_DIM, SSM_GROUP)
        t = jnp.einsum("dinph,kn->dikhnp", b5, eye)
        return t.reshape(2, lt, gl * SSM_GROUP, gl * STATE_DIM)

    def out_map(cc):
        c5 = cc.reshape(2, lt, gl, SSM_GROUP, STATE_DIM)
        t = jnp.einsum("dikhp,kn->dikpnh", c5, eye)
        return t.reshape(2, lt, gl * STATE_DIM, gl * SSM_GROUP)

    bt = jnp.concatenate([in_map(bb_re), in_map(bb_im)], axis=-1).astype(BF16)
    return bt, out_map(c_re).astype(BF16), out_map(c_im).astype(BF16)


def kernel(x_prompt, x_sample, c, state_ssm_re, state_ssm_im, c_ctx, norm_mix, norm_ffn, w_ada, b_ada, ssm_w_in, ssm_lam_re, ssm_lam_im, ssm_log_dt, ssm_b_re, ssm_b_im, ssm_c_re, ssm_c_im, ssm_d, ssm_w_glu, fnet_w_out, router_w, router_b, moe_w_gate, moe_w_up, moe_w_down, norm_final):
    bp, lp, _ = x_prompt.shape
    bs, ls, _ = x_sample.shape
    n_p = bp * lp
    n_s = bs * ls
    n = n_p + n_s
    tm = TOKEN_TILE

    cond = jnp.zeros((N_COND, D_MODEL), F32).at[0].set(c_ctx).at[1:1 + bs].set(c)
    modtab = _ada_table(cond, w_ada, b_ada).reshape(DEPTH, N_COND, N_MOD, D_MODEL)

    rwt_hi = router_w.T.astype(BF16)
    rwt_lo = (router_w.T - rwt_hi.astype(F32)).astype(BF16)
    rwt = jnp.concatenate([rwt_hi, rwt_lo])
    rb = router_b.reshape(N_EXPERTS, 1)
    row = lambda v: v.reshape(1, D_MODEL)

    period = max(bp, bs)
    pat_tm = jnp.stack([
        jnp.broadcast_to(modtab[0, 0], (period, N_MOD, D_MODEL)),
        jnp.tile(modtab[0, 1:1 + bs], (period // bs, 1, 1))])
    pat_tm = pat_tm.reshape(2, period, N_MOD * D_MODEL)

    a_re, a_im, bb_re, bb_im = _zoh(ssm_lam_re[0], ssm_lam_im[0], ssm_log_dt[0],
                                    ssm_b_re[0], ssm_b_im[0])
    bt, ct_re, ct_im = _block_diag_weights(bb_re, bb_im, ssm_c_re[0], ssm_c_im[0])
    a_re = a_re.reshape(2, 1, N_STATE)
    a_im = a_im.reshape(2, 1, N_STATE)

    u = _ssm_in(x_prompt, x_sample, row(norm_mix[0]), pat_tm,
                ssm_w_in[0].astype(BF16))
    h0s_re = state_ssm_re[:, 0].reshape(bs, 2, N_STATE).transpose(1, 0, 2)
    h0s_im = state_ssm_im[:, 0].reshape(bs, 2, N_STATE).transpose(1, 0, 2)
    y_scan, fin_re, fin_im = _ssm_scan(u, bt, ct_re, ct_im, a_re, a_im,
                                       h0s_re, h0s_im, bp=bp, lp=lp, bs=bs, ls=ls)
    tri = jnp.triu(jnp.ones((tm, tm), BF16), k=1)
    x1, h2, ri, rw, cnt = _ssm_out(u, y_scan, row(ssm_d[0]),
                               ssm_w_glu[0].astype(BF16), x_prompt, x_sample,
                               pat_tm, row(norm_ffn[0]), rwt, rb, tri)
    wg_bf, wu_bf, wd_bf = (w.astype(BF16) for w in (moe_w_gate, moe_w_up,
                                                    moe_w_down))
    y_moe = _moe(h2, ri, rw, cnt, wg_bf, wu_bf, wd_bf, 0)
    x2_p, x2_s = _moe_out_tm(x1, y_moe, pat_tm, x_prompt.shape, x_sample.shape)

    pat_bm = jnp.broadcast_to(modtab[1][:1 + bs, None],
                              (1 + bs, SUBLANES, N_MOD, D_MODEL))
    pat_bm = pat_bm.reshape(1 + bs, SUBLANES, N_MOD * D_MODEL)
    x3, h2, ri, rw, cnt = _fnet(x2_p, x2_s, row(norm_mix[1]), pat_bm,
                            fnet_w_out[0].astype(BF16), row(norm_ffn[1]), rwt, rb)
    y_moe = _moe(h2, ri, rw, cnt, wg_bf, wu_bf, wd_bf, 1)
    y_p, y_s = _moe_out_final(x3, y_moe, pat_bm, row(norm_final), n_p, ls)

    st = lambda f: f.transpose(1, 0, 2).reshape(bp, 1, 2, SSM_GROUPS, STATE_DIM)
    return (y_p.reshape(bp, lp, D_MODEL), y_s.reshape(bs, ls, D_MODEL),
            st(fin_re), st(fin_im))
```

```python
import functools

import jax
import jax.numpy as jnp
from jax import lax
from jax.experimental import pallas as pl
from jax.experimental.pallas import tpu as pltpu

F32 = jnp.float32
BF16 = jnp.bfloat16
I32 = jnp.int32

D_MODEL = 1024
DEPTH = 2
SSM_GROUP = 16
SSM_GROUPS = 64
STATE_DIM = 64
N_STATE = SSM_GROUPS * STATE_DIM
FNET_GROUP = 128
N_EXPERTS = 16
N_EXPERT_GROUPS = 4
EXPERTS_PER_GROUP = 4
D_EXPERT = 1024
N_MOD = 6
EPS = 1e-6

LANES = 128
SUBLANES = 8
LANE_TILES = D_MODEL // LANES
VMEM_LIMIT = 56 * 1024 * 1024

TOKEN_TILE = 512
FNET_TILE = 256
FFN_TILE = 256
N_COND = 16

PAIRS = ((0, 1), (0, 2), (0, 3), (1, 3), (1, 2), (3, 2))
N_PAIRS = N_EXPERT_GROUPS * len(PAIRS)
N_CLASS_ROWS = 32
SLAB = SUBLANES


def _cparams(sem, vmem=VMEM_LIMIT):
    return pltpu.CompilerParams(dimension_semantics=sem, vmem_limit_bytes=vmem)


def _sigmoid(x):
    return 1.0 / (1.0 + jnp.exp(-x))


def _gelu_tanh(x):
    c = 0.7978845608028654
    return x * (0.5 * (1.0 + jnp.tanh(c * (x + 0.044715 * (x * x * x)))))


def _per_row(v, pat, fn):
    tm, d = v.shape
    p = pat.shape[0]
    return fn(v.reshape(tm // p, p, d), pat[None]).reshape(tm, d)


def _rms(x):
    ms = jnp.mean(x * x, axis=-1, keepdims=True)
    return x * lax.rsqrt(ms + EPS)


def _norm_mod(x, g, sc, sh):
    y = _rms(x) * g
    y = _per_row(y, sc, lambda a, b: a * (1.0 + b))
    return _per_row(y, sh, lambda a, b: a + b)


def _lane_tiles(v):
    return [v[:, LANES * k:LANES * (k + 1)] for k in range(v.shape[1] // LANES)]


def _to_time_major(x_ref, scr, batch):
    tt = x_ref.shape[1]
    for b in range(batch):
        for k, piece in enumerate(_lane_tiles(x_ref[b])):
            scr[k, pl.ds(b, tt, stride=batch), :] = piece
    return jnp.concatenate([scr[k] for k in range(LANE_TILES)], axis=1)


def _from_time_major(v, o_ref, scr, batch):
    tt = v.shape[0] // batch
    for k, piece in enumerate(_lane_tiles(v)):
        scr[k] = piece
    for b in range(batch):
        o_ref[b] = jnp.concatenate(
            [scr[k, pl.ds(b, tt, stride=batch), :] for k in range(LANE_TILES)],
            axis=1)


def _slab_rows(ref, k, n):
    return ref[pl.ds(k, n, stride=SLAB), :]


def _store_slabs(v, ref):
    n = v.shape[0]
    for k, piece in enumerate(_lane_tiles(v)):
        ref[pl.ds(k, n, stride=SLAB), :] = piece


def _rows_from_slabs(y_ref, n):
    return jnp.concatenate([_slab_rows(y_ref, k, n) for k in range(SLAB)], axis=1)


def _route(logits_t, rb, tri, carry):
    ne, tm = logits_t.shape
    s = _sigmoid(logits_t)
    bz = s + rb
    row = lambda a, r: a[r:r + 1, :]
    gs = []
    for g in range(N_EXPERT_GROUPS):
        v0, v1, v2, v3 = (row(bz, EXPERTS_PER_GROUP * g + j) for j in range(4))
        hi1, lo1 = jnp.maximum(v0, v1), jnp.minimum(v0, v1)
        hi2, lo2 = jnp.maximum(v2, v3), jnp.minimum(v2, v3)
        top1 = jnp.maximum(hi1, hi2)
        top2 = jnp.maximum(jnp.minimum(hi1, hi2), jnp.maximum(lo1, lo2))
        gs.append(top1 + top2)
    bg = jnp.zeros((1, tm), I32)
    bv = gs[0]
    for g in range(1, N_EXPERT_GROUPS):
        upd = gs[g] > bv
        bg = jnp.where(upd, g, bg)
        bv = jnp.where(upd, gs[g], bv)
    cb, cs = [], []
    for j in range(EXPERTS_PER_GROUP):
        vb, vs = row(bz, j), row(s, j)
        for g in range(1, N_EXPERT_GROUPS):
            sel = bg == g
            vb = jnp.where(sel, row(bz, EXPERTS_PER_GROUP * g + j), vb)
            vs = jnp.where(sel, row(s, EXPERTS_PER_GROUP * g + j), vs)
        cb.append(vb)
        cs.append(vs)
    i1 = jnp.zeros((1, tm), I32)
    b1, s1 = cb[0], cs[0]
    for j in range(1, EXPERTS_PER_GROUP):
        upd = cb[j] > b1
        i1 = jnp.where(upd, j, i1)
        b1 = jnp.where(upd, cb[j], b1)
        s1 = jnp.where(upd, cs[j], s1)
    i2 = jnp.zeros((1, tm), I32)
    b2 = jnp.full((1, tm), -jnp.inf, F32)
    s2 = jnp.zeros((1, tm), F32)
    for j in range(EXPERTS_PER_GROUP):
        cand = jnp.where(i1 == j, -jnp.inf, cb[j])
        upd = cand > b2
        i2 = jnp.where(upd, j, i2)
        b2 = jnp.where(upd, cand, b2)
        s2 = jnp.where(upd, cs[j], s2)
    den = s1 + s2
    w1 = s1 / den
    w2 = s2 / den
    lo = jnp.minimum(i1, i2)
    hi = jnp.maximum(i1, i2)
    pidx = jnp.where(lo == 0, hi - 1,
                     jnp.where(lo == 1, jnp.where(hi == 3, 3, 4), 5))
    first = jnp.where(pidx < 3, 0, jnp.where(pidx < 5, 1, 3))
    wa = jnp.where(i1 == first, w1, w2)
    wb = jnp.where(i1 == first, w2, w1)
    q = len(PAIRS) * bg + pidx
    qio = lax.broadcasted_iota(I32, (N_CLASS_ROWS, tm), 0)
    oh = qio == q
    ohf = jnp.where(oh, 1.0, 0.0)
    cum = jnp.dot(ohf.astype(BF16), tri, preferred_element_type=F32) + carry
    rank = jnp.sum(jnp.where(oh, cum, 0.0), axis=0, keepdims=True)
    new_carry = carry + jnp.sum(ohf, axis=1, keepdims=True)
    return q, rank.astype(I32), wa, wb, new_carry


def _post_mixer(m, x, g1, nf, sc2, sh2, rwt_ref, rb_ref, tri_ref,
                x1_ref, h2_ref, ri_ref, rw_ref, cnt_ref, carry, valid=None):
    tm = x.shape[0]
    x1 = x + _per_row(m, g1, lambda a, b: a * b)
    x1_ref[...] = x1
    h2 = _norm_mod(x1, nf, sc2, sh2)
    h_hi = h2.astype(BF16)
    h_lo = (h2 - h_hi.astype(F32)).astype(BF16)
    nt = (((1,), (1,)), ((), ()))
    la = lax.dot_general(rwt_ref[...], h_hi, nt, preferred_element_type=F32)
    lb = lax.dot_general(rwt_ref[0:N_EXPERTS, :], h_lo, nt,
                         preferred_element_type=F32)
    logits_t = la[:N_EXPERTS] + la[N_EXPERTS:] + lb
    q, rank, wa, wb, nc = _route(logits_t, rb_ref[...], tri_ref[...],
                                 carry[:, 0:1])
    _store_slabs(h2, h2_ref)
    rw_ref[0:1, :] = wa
    rw_ref[1:2, :] = wb
    rw_ref[2:8, :] = jnp.zeros((6, tm), F32)
    ri_ref[0:1, :] = q
    ri_ref[1:2, :] = rank
    ri_ref[2:8, :] = jnp.zeros((6, tm), I32)
    if valid is not None:
        nc = jnp.where(valid, nc, carry[:, 0:1])
    carry[...] = jnp.broadcast_to(nc, carry.shape)
    cnt_ref[...] = carry[...]


def _ada_kernel(cond_ref, w_ref, b_ref, o_ref):
    c = cond_ref[...]
    s = c * _sigmoid(c)
    s_hi = s.astype(BF16)
    s_lo = (s - s_hi.astype(F32)).astype(BF16)
    w = w_ref[...]
    w_hi = w.astype(BF16)
    w_lo = (w - w_hi.astype(F32)).astype(BF16)
    dot = functools.partial(jnp.dot, preferred_element_type=F32)
    o_ref[...] = dot(s_hi, w_hi) + dot(s_lo, w_hi) + dot(s_hi, w_lo) + b_ref[...]


def _ada_table(cond, w_ada, b_ada):
    tn = 1536
    n_out = N_MOD * D_MODEL
    return pl.pallas_call(
        _ada_kernel,
        out_shape=jax.ShapeDtypeStruct((DEPTH, N_COND, n_out), F32),
        grid=(DEPTH, n_out // tn),
        in_specs=[
            pl.BlockSpec((N_COND, D_MODEL), lambda l, j: (0, 0)),
            pl.BlockSpec((None, D_MODEL, tn), lambda l, j: (l, 0, j)),
            pl.BlockSpec((None, 1, tn), lambda l, j: (l, 0, j)),
        ],
        out_specs=pl.BlockSpec((None, N_COND, tn), lambda l, j: (l, 0, j)),
        compiler_params=_cparams(("arbitrary", "arbitrary")),
        name="ada_table",
    )(cond, w_ada, b_ada.reshape(DEPTH, 1, n_out))


def _zoh_kernel(lr_ref, li_ref, ldt_ref, br_ref, bi_ref,
                are_ref, aim_ref, bbre_ref, bbim_ref):
    lr = lr_ref[...]
    li = li_ref[...]
    dt = jnp.exp(ldt_ref[...])
    mag = jnp.exp(lr * dt)
    a_re = mag * jnp.cos(li * dt)
    a_im = mag * jnp.sin(li * dt)
    den = lr * lr + li * li
    nr = a_re - 1.0
    f_re = (nr * lr + a_im * li) / den
    f_im = (a_im * lr - nr * li) / den
    br = br_ref[...]
    bi = bi_ref[...]
    are_ref[...] = a_re
    aim_ref[...] = a_im
    bbre_ref[...] = f_re * br - f_im * bi
    bbim_ref[...] = f_re * bi + f_im * br


def _zoh(lam_re, lam_im, log_dt, b_re, b_im):
    shape = b_re.shape
    flat = (shape[0] * shape[1] * shape[2] * shape[3] // LANES, LANES)
    bc = lambda a: jnp.broadcast_to(a, shape).reshape(flat)
    args = (bc(lam_re[..., None]), bc(lam_im[..., None]),
            bc(log_dt[:, :, None, None]), b_re.reshape(flat), b_im.reshape(flat))
    outs = pl.pallas_call(
        _zoh_kernel,
        out_shape=[jax.ShapeDtypeStruct(flat, F32)] * 4,
        name="zoh_discretize",
    )(*args)
    a_re, a_im, bb_re, bb_im = (o.reshape(shape) for o in outs)
    return a_re[..., 0], a_im[..., 0], bb_re, bb_im


def _stream_specs(xp, xs, n_p_tiles):
    bp, bs = xp.shape[0], xs.shape[0]
    return [
        pl.BlockSpec((bp, TOKEN_TILE // bp, D_MODEL),
                     lambda i: (0, jnp.minimum(i, n_p_tiles - 1), 0)),
        pl.BlockSpec((bs, TOKEN_TILE // bs, D_MODEL),
                     lambda i: (0, jnp.maximum(i - n_p_tiles, 0), 0)),
    ]


def _ssm_in_kernel(xp_ref, xs_ref, g_ref, sc_ref, sh_ref, w_ref, u_ref, scr,
                   *, n_p_tiles):
    def run(x_ref):
        x = _to_time_major(x_ref, scr, x_ref.shape[0])
        h = _norm_mod(x, g_ref[...], sc_ref[...], sh_ref[...])
        u_ref[...] = jnp.dot(h.astype(BF16), w_ref[...],
                             preferred_element_type=F32)

    i = pl.program_id(0)
    pl.when(i < n_p_tiles)(lambda: run(xp_ref))
    pl.when(i >= n_p_tiles)(lambda: run(xs_ref))


def _ssm_in(xp, xs, g, modpat, w_in_bf):
    n = (xp.shape[0] * xp.shape[1] + xs.shape[0] * xs.shape[1])
    tm = TOKEN_TILE
    n_p_tiles = xp.shape[0] * xp.shape[1] // tm
    p = modpat.shape[1]
    pat = lambda i: jnp.where(i < n_p_tiles, 0, 1)
    mod = lambda k: pl.BlockSpec((None, p, D_MODEL), lambda i: (pat(i), 0, k))
    return pl.pallas_call(
        functools.partial(_ssm_in_kernel, n_p_tiles=n_p_tiles),
        out_shape=jax.ShapeDtypeStruct((n, D_MODEL), F32),
        grid=(n // tm,),
        in_specs=_stream_specs(xp, xs, n_p_tiles) + [
            pl.BlockSpec((1, D_MODEL), lambda i: (0, 0)),
            mod(1), mod(0),
            pl.BlockSpec((D_MODEL, D_MODEL), lambda i: (0, 0)),
        ],
        out_specs=pl.BlockSpec((tm, D_MODEL), lambda i: (i, 0)),
        scratch_shapes=[pltpu.VMEM((LANE_TILES, tm, LANES), F32)],
        compiler_params=_cparams(("arbitrary",)),
        name="ssm_in",
    )(xp, xs, g, modpat, modpat, w_in_bf)


def _scan_kernel(u_ref, bt_ref, cre_ref, cim_ref, are_ref, aim_ref,
                 h0re_ref, h0im_ref, y_ref, fre_ref, fim_ref,
                 *scratch, n_p_chunks, bp, bs):
    xre = scratch[:LANE_TILES]
    xim = scratch[LANE_TILES:2 * LANE_TILES]
    hre, him = scratch[2 * LANE_TILES:]
    d = pl.program_id(0)
    c = pl.program_id(1)
    cols = N_STATE // LANE_TILES
    rows = u_ref.shape[0]

    def chunk(batch, reverse):
        steps = rows // batch

        def b_proj(s):
            u = u_ref[:, LANES * s:LANES * (s + 1)].astype(BF16)
            xt = jnp.dot(u, bt_ref[s], preferred_element_type=F32)
            xre[s][...] = xt[:, :cols]
            xim[s][...] = xt[:, cols:]

        def scan(s):
            sl = slice(cols * s, cols * (s + 1))
            ar = jnp.broadcast_to(are_ref[:, sl], (batch, cols))
            ai = jnp.broadcast_to(aim_ref[:, sl], (batch, cols))
            hr = hre[0:batch, sl]
            hi = him[0:batch, sl]
            for t in range(steps):
                tt = steps - 1 - t if reverse else t
                r = slice(tt * batch, (tt + 1) * batch)
                nr = ar * hr - ai * hi + xre[s][r, :]
                ni = ar * hi + ai * hr + xim[s][r, :]
                xre[s][r, :] = nr
                xim[s][r, :] = ni
                hr, hi = nr, ni
            hre[0:batch, sl] = hr
            him[0:batch, sl] = hi

        def c_proj(s):
            y_ref[:, LANES * s:LANES * (s + 1)] = (
                jnp.dot(xre[s][...].astype(BF16), cre_ref[s],
                        preferred_element_type=F32)
                - jnp.dot(xim[s][...].astype(BF16), cim_ref[s],
                          preferred_element_type=F32))

        for stage in range(LANE_TILES + 2):
            if stage < LANE_TILES:
                b_proj(stage)
            if 0 <= stage - 1 < LANE_TILES:
                scan(stage - 1)
            if 0 <= stage - 2 < LANE_TILES:
                c_proj(stage - 2)

    @pl.when(c == 0)
    def _():
        hre[...] = jnp.zeros(hre.shape, F32)
        him[...] = jnp.zeros(him.shape, F32)

    @pl.when(c == n_p_chunks)
    def _():
        hre[0:bs, :] = h0re_ref[...]
        him[0:bs, :] = h0im_ref[...]

    in_p = c < n_p_chunks
    in_s = jnp.logical_not(in_p)
    fwd = d == 0
    bwd = jnp.logical_not(fwd)
    pl.when(jnp.logical_and(in_p, fwd))(lambda: chunk(bp, False))
    pl.when(jnp.logical_and(in_p, bwd))(lambda: chunk(bp, True))
    pl.when(jnp.logical_and(in_s, fwd))(lambda: chunk(bs, False))
    pl.when(jnp.logical_and(in_s, bwd))(lambda: chunk(bs, True))

    @pl.when(c == n_p_chunks - 1)
    def _():
        fre_ref[...] = hre[0:bp, :]
        fim_ref[...] = him[0:bp, :]


def _ssm_scan(u, bt, ct_re, ct_im, a_re, a_im, h0_re, h0_im, *, bp, lp, bs, ls):
    n = u.shape[0]
    rows = TOKEN_TILE
    n_p_chunks = bp * lp // rows
    n_s_chunks = bs * ls // rows

    def chunk(d, c):
        in_p = jnp.where(d == 0, c, n_p_chunks - 1 - c)
        cs = c - n_p_chunks
        in_s = n_p_chunks + jnp.where(d == 0, cs, n_s_chunks - 1 - cs)
        return jnp.where(c < n_p_chunks, in_p, in_s)

    cols = N_STATE // LANE_TILES
    dspec = lambda shape: pl.BlockSpec((None,) + shape,
                                       lambda d, c: (d,) + (0,) * len(shape))
    kern = functools.partial(_scan_kernel, n_p_chunks=n_p_chunks, bp=bp, bs=bs)
    return pl.pallas_call(
        kern,
        out_shape=[jax.ShapeDtypeStruct((2, n, D_MODEL), F32),
                   jax.ShapeDtypeStruct((2, bp, N_STATE), F32),
                   jax.ShapeDtypeStruct((2, bp, N_STATE), F32)],
        grid=(2, n_p_chunks + n_s_chunks),
        in_specs=[
            pl.BlockSpec((rows, D_MODEL), lambda d, c: (chunk(d, c), 0)),
            dspec((LANE_TILES, LANES, 2 * cols)),
            dspec((LANE_TILES, cols, LANES)),
            dspec((LANE_TILES, cols, LANES)),
            dspec((1, N_STATE)),
            dspec((1, N_STATE)),
            dspec((bs, N_STATE)),
            dspec((bs, N_STATE)),
        ],
        out_specs=[
            pl.BlockSpec((None, rows, D_MODEL), lambda d, c: (d, chunk(d, c), 0)),
            dspec((bp, N_STATE)),
            dspec((bp, N_STATE)),
        ],
        scratch_shapes=(
            [pltpu.VMEM((rows, cols), F32)] * (2 * LANE_TILES)
            + [pltpu.VMEM((max(bp, bs), N_STATE), F32)] * 2),
        compiler_params=_cparams(("arbitrary", "arbitrary")),
        name="ssm_scan",
    )(u, bt, ct_re, ct_im, a_re, a_im, h0_re, h0_im)


def _ssm_out_kernel(u_ref, y0_ref, y1_ref, dsk_ref, wglu_ref, xp_ref, xs_ref,
                    g1_ref, nf_ref, sc2_ref, sh2_ref, rwt_ref, rb_ref, tri_ref,
                    x1_ref, h2_ref, ri_ref, rw_ref, cnt_ref, scr, carry,
                    *, n_p_tiles):
    i = pl.program_id(0)

    @pl.when(i == 0)
    def _():
        carry[...] = jnp.zeros(carry.shape, F32)

    def run(x_ref):
        x = _to_time_major(x_ref, scr, x_ref.shape[0])
        y = u_ref[...] * dsk_ref[...] + y0_ref[...] + y1_ref[...]
        ge = _gelu_tanh(y).astype(BF16)
        vg = jnp.dot(ge, wglu_ref[...], preferred_element_type=F32)
        m = vg[:, :D_MODEL] * _sigmoid(vg[:, D_MODEL:])
        _post_mixer(m, x, g1_ref[...], nf_ref[...], sc2_ref[...], sh2_ref[...],
                    rwt_ref, rb_ref, tri_ref, x1_ref, h2_ref, ri_ref, rw_ref,
                    cnt_ref, carry)

    pl.when(i < n_p_tiles)(lambda: run(xp_ref))
    pl.when(i >= n_p_tiles)(lambda: run(xs_ref))


def _route_out_shapes(n):
    return [jax.ShapeDtypeStruct((n, D_MODEL), F32),
            jax.ShapeDtypeStruct((n * SLAB, LANES), F32),
            jax.ShapeDtypeStruct((SUBLANES, n), I32),
            jax.ShapeDtypeStruct((SUBLANES, n), F32),
            jax.ShapeDtypeStruct((N_CLASS_ROWS, LANES), F32)]


def _ssm_out(u, y, d_skip, w_glu_bf, xp, xs, modpat, norm_ffn, rwt, rb, tri):
    n = u.shape[0]
    tm = TOKEN_TILE
    n_p_tiles = xp.shape[0] * xp.shape[1] // tm
    p = modpat.shape[1]
    pat = lambda i: jnp.where(i < n_p_tiles, 0, 1)
    mod = lambda k: pl.BlockSpec((None, p, D_MODEL), lambda i: (pat(i), 0, k))
    full = lambda shape: pl.BlockSpec(shape, lambda i: (0,) * len(shape))
    rowblk = pl.BlockSpec((tm, D_MODEL), lambda i: (i, 0))
    return pl.pallas_call(
        functools.partial(_ssm_out_kernel, n_p_tiles=n_p_tiles),
        out_shape=_route_out_shapes(n),
        grid=(n // tm,),
        in_specs=[
            rowblk,
            pl.BlockSpec((None, tm, D_MODEL), lambda i: (0, i, 0)),
            pl.BlockSpec((None, tm, D_MODEL), lambda i: (1, i, 0)),
            full((1, D_MODEL)),
            full((D_MODEL, 2 * D_MODEL)),
        ] + _stream_specs(xp, xs, n_p_tiles) + [
            mod(2), full((1, D_MODEL)), mod(4), mod(3),
            full((2 * N_EXPERTS, D_MODEL)), full((N_EXPERTS, 1)), full((tm, tm)),
        ],
        out_specs=[rowblk,
                   pl.BlockSpec((tm * SLAB, LANES), lambda i: (i, 0)),
                   pl.BlockSpec((SUBLANES, tm), lambda i: (0, i)),
                   pl.BlockSpec((SUBLANES, tm), lambda i: (0, i)),
                   full((N_CLASS_ROWS, LANES))],
        scratch_shapes=[pltpu.VMEM((LANE_TILES, tm, LANES), F32),
                        pltpu.VMEM((N_CLASS_ROWS, LANES), F32)],
        compiler_params=_cparams(("arbitrary",)),
        name="ssm_out",
    )(u, y, y, d_skip, w_glu_bf, xp, xs, modpat, norm_ffn, modpat, modpat,
      rwt, rb, tri)


def _fnet_kernel(xp_ref, xpp_ref, xs_ref, nm_ref, sc1_ref, sh1_ref, cs_ref,
                 dftp_ref, dfts_ref, wout_ref, g1_ref, nf_ref, sc2_ref, sh2_ref,
                 rwt_ref, rb_ref, tri_ref, x1_ref, h2_ref, ri_ref, rw_ref, cnt_ref,
                 ucs, ybuf, carry, *, bp, tiles_s):
    s = pl.program_id(0)
    n_groups = D_MODEL // FNET_GROUP
    tr = FNET_TILE
    per = tiles_s + 1
    wslot = s % 2

    @pl.when(s == 0)
    def _():
        carry[...] = jnp.zeros(carry.shape, F32)
        ybuf[...] = jnp.zeros(ybuf.shape, F32)

    def stage1(x_ref, seq):
        def body(r, _):
            rows = pl.ds(pl.multiple_of(r * tr, tr), tr)
            h = _norm_mod(x_ref[rows, :], nm_ref[...], sc1_ref[...],
                          sh1_ref[...]).astype(BF16)
            for k in range(n_groups):
                cols = slice(FNET_GROUP * k, FNET_GROUP * (k + 1))
                t = jnp.dot(h[:, cols], cs_ref[...], preferred_element_type=F32)
                ucs[rows, cols] = t[:, :FNET_GROUP].astype(BF16)
                ucs[pl.ds(pl.multiple_of(seq + r * tr, tr), tr), cols] = (
                    t[:, FNET_GROUP:].astype(BF16))
            return 0
        lax.fori_loop(0, seq // tr, body, 0)

    def step(dft_ref, seq, xrow, valid):
        scale = float((seq * FNET_GROUP) ** -0.5)
        m = jnp.dot(ybuf[1 - wslot].astype(BF16), wout_ref[...],
                    preferred_element_type=F32)
        _post_mixer(m, xrow, g1_ref[...], nf_ref[...], sc2_ref[...],
                    sh2_ref[...], rwt_ref, rb_ref, tri_ref, x1_ref, h2_ref,
                    ri_ref, rw_ref, cnt_ref, carry, valid)
        ybuf[wslot] = jnp.dot(dft_ref[...], ucs[0:2 * seq, :],
                              preferred_element_type=F32) * scale

    @pl.when(s <= bp)
    def _():
        stage1(xp_ref, xp_ref.shape[0])
        step(dftp_ref, xp_ref.shape[0], xpp_ref[...], s >= 1)

    @pl.when(s > bp)
    def _():
        j = (s - bp - 1) % per
        pl.when(j == 0)(lambda: stage1(xs_ref, xs_ref.shape[0]))
        prev = jnp.maximum(j - 1, 0)
        xrow = xs_ref[pl.ds(pl.multiple_of(prev * tr, tr), tr), :]
        step(dfts_ref, xs_ref.shape[0], xrow, j >= 1)


def _dft_table(seq):
    inner = 64
    k = jnp.arange(seq, dtype=I32)[:, None]
    t1 = jnp.arange(seq // inner, dtype=I32)[None, :] * inner
    t2 = jnp.arange(inner, dtype=I32)[None, :]
    ang = lambda t: ((k * t) % seq).astype(F32) * (2.0 * jnp.pi / seq)
    ca, sa = jnp.cos(ang(t1))[:, :, None], jnp.sin(ang(t1))[:, :, None]
    cb, sb = jnp.cos(ang(t2))[:, None, :], jnp.sin(ang(t2))[:, None, :]
    cos = (ca * cb - sa * sb).reshape(seq, seq)
    sin = (sa * cb + ca * sb).reshape(seq, seq)
    return jnp.concatenate([cos, -sin], axis=1).astype(BF16)


def _fnet(xp, xs, norm_mix, modpat, w_out_bf, norm_ffn, rwt, rb):
    bp, lp, _ = xp.shape
    bs, ls, _ = xs.shape
    tr = FNET_TILE
    assert lp == tr
    tiles_s = ls // tr
    per = tiles_s + 1
    n = bp * lp + bs * ls
    p = modpat.shape[1]
    kc = jnp.arange(FNET_GROUP, dtype=I32)
    angc = ((kc[:, None] * kc[None, :]) % FNET_GROUP).astype(F32) * (
        2.0 * jnp.pi / FNET_GROUP)
    cs = jnp.concatenate([jnp.cos(angc), jnp.sin(angc)], axis=1).astype(BF16)
    tri = jnp.triu(jnp.ones((tr, tr), BF16), k=1)

    q = lambda s: jnp.maximum(s - bp - 1, 0)
    sb = lambda s: jnp.minimum(q(s) // per, bs - 1)
    sj = lambda s: q(s) % per
    pat = lambda s: jnp.where(s <= bp, 0, 1 + sb(s))
    blk = lambda s: jnp.where(s <= bp, jnp.maximum(s - 1, 0),
                              bp + sb(s) * tiles_s + jnp.maximum(sj(s) - 1, 0))
    mod = lambda kk: pl.BlockSpec((None, p, D_MODEL), lambda s: (pat(s), 0, kk))
    full = lambda shape: pl.BlockSpec(shape, lambda s: (0,) * len(shape))
    kern = functools.partial(_fnet_kernel, bp=bp, tiles_s=tiles_s)
    return pl.pallas_call(
        kern,
        out_shape=_route_out_shapes(n),
        grid=(bp + 1 + bs * per,),
        in_specs=[
            pl.BlockSpec((None, lp, D_MODEL),
                         lambda s: (jnp.minimum(s, bp - 1), 0, 0)),
            pl.BlockSpec((None, lp, D_MODEL),
                         lambda s: (jnp.clip(s - 1, 0, bp - 1), 0, 0)),
            pl.BlockSpec((None, ls, D_MODEL), lambda s: (sb(s), 0, 0)),
            full((1, D_MODEL)), mod(1), mod(0),
            full((FNET_GROUP, 2 * FNET_GROUP)),
            full((tr, 2 * lp)),
            pl.BlockSpec((tr, 2 * ls),
                         lambda s: (jnp.minimum(sj(s), tiles_s - 1), 0)),
            full((D_MODEL, D_MODEL)),
            mod(2), full((1, D_MODEL)), mod(4), mod(3),
            full((2 * N_EXPERTS, D_MODEL)), full((N_EXPERTS, 1)), full((tr, tr)),
        ],
        out_specs=[pl.BlockSpec((tr, D_MODEL), lambda s: (blk(s), 0)),
                   pl.BlockSpec((tr * SLAB, LANES), lambda s: (blk(s), 0)),
                   pl.BlockSpec((SUBLANES, tr), lambda s: (0, blk(s))),
                   pl.BlockSpec((SUBLANES, tr), lambda s: (0, blk(s))),
                   full((N_CLASS_ROWS, LANES))],
        scratch_shapes=[pltpu.VMEM((2 * ls, D_MODEL), BF16),
                        pltpu.VMEM((2, tr, D_MODEL), F32),
                        pltpu.VMEM((N_CLASS_ROWS, LANES), F32)],
        compiler_params=_cparams(("arbitrary",)),
        name="fnet",
    )(xp, xp, xs, norm_mix, modpat, modpat, cs, _dft_table(lp), _dft_table(ls),
      w_out_bf, modpat, norm_ffn, modpat, modpat, rwt, rb, tri)


def _invert_kernel(slot_ref, ends_ref, gsrc_ref, sdst_ref):
    n = slot_ref.shape[0]
    n_slots = gsrc_ref.shape[0]
    t = FFN_TILE
    unroll = 8

    def pad_tile(base):
        def pad(j, _):
            for k in range(unroll):
                r = j * unroll + k
                gsrc_ref[base + r] = 0
                sdst_ref[base + r] = n + r
            return 0
        lax.fori_loop(0, t // unroll, pad, 0)

    for q in range(N_PAIRS):
        pad_tile(jnp.maximum(ends_ref[q] - t, 0))

    def unused(b, _):
        pad_tile(b * t)
        return 0
    lax.fori_loop(ends_ref[N_PAIRS - 1] // t, n_slots // t, unused, 0)

    def body(j, _):
        for k in range(unroll):
            t = j * unroll + k
            s = slot_ref[t]
            gsrc_ref[s] = t
            sdst_ref[s] = t
        return 0
    lax.fori_loop(0, n // unroll, body, 0)


def _invert(slot, ends, n_slots):
    smem = pl.BlockSpec(memory_space=pltpu.SMEM)
    return pl.pallas_call(
        _invert_kernel,
        out_shape=[jax.ShapeDtypeStruct((n_slots,), I32)] * 2,
        in_specs=[smem, smem],
        out_specs=[smem, smem],
        name="moe_invert",
    )(slot, ends)


def _ffn_kernel(tea_ref, teb_ref, nu_ref, gsrc_ref, sdst_ref, wa_ref, wb_ref,
                h_hbm, wga, wua, wda, wgb, wub, wdb, y_hbm,
                xbuf0, xbuf1, obuf0, obuf1, wcola, wcolb, gsem, ssem):
    del tea_ref, teb_ref
    i = pl.program_id(0)
    nu = nu_ref[0]
    t = FFN_TILE

    def slab(ref, r):
        return ref.at[pl.ds(pl.multiple_of(r * SLAB, SLAB), SLAB), :]

    def gather_start(tile, xbuf, sem):
        base = tile * t
        for r in range(t):
            pltpu.make_async_copy(slab(h_hbm, gsrc_ref[base + r]), slab(xbuf, r),
                                  sem).start(priority=0)

    def gather_wait(xbuf, sem):
        pltpu.make_async_copy(h_hbm.at[pl.ds(0, t * SLAB), :], xbuf, sem).wait()

    def scatter_start(tile, obuf, sem):
        base = tile * t
        for r in range(t):
            pltpu.make_async_copy(slab(obuf, r), slab(y_hbm, sdst_ref[base + r]),
                                  sem).start(priority=1)

    def scatter_wait(obuf, sem):
        pltpu.make_async_copy(obuf, y_hbm.at[pl.ds(0, t * SLAB), :], sem).wait()

    def expert(xb, wg, wu, wd):
        g = jnp.dot(xb, wg[...], preferred_element_type=F32)
        u = jnp.dot(xb, wu[...], preferred_element_type=F32)
        a = (g * _sigmoid(g)) * u
        return jnp.dot(a.astype(BF16), wd[...], preferred_element_type=F32)

    def step(xc, xn, oc, op, gc, gn, sc, sp):
        pl.when(i == 0)(lambda: gather_start(0, xc, gc))
        pl.when(i >= 2)(lambda: scatter_wait(oc, sc))
        gather_wait(xc, gc)
        pl.when(i + 1 < nu)(lambda: gather_start(i + 1, xn, gn))
        pl.when(i >= 1)(lambda: scatter_start(i - 1, op, sp))
        for r in range(t):
            tok = gsrc_ref[i * t + r]
            wcola[r:r + 1, :] = jnp.full((1, LANES), wa_ref[tok], F32)
            wcolb[r:r + 1, :] = jnp.full((1, LANES), wb_ref[tok], F32)
        xb = _rows_from_slabs(xc, t).astype(BF16)
        ya = expert(xb, wga, wua, wda)
        yb = expert(xb, wgb, wub, wdb)
        y = wcola[:, 0:1] * ya + wcolb[:, 0:1] * yb
        for k, piece in enumerate(_lane_tiles(y)):
            oc[pl.ds(k, t, stride=SLAB), :] = piece

    def drain(oc, op, sc, sp):
        pl.when(i >= 2)(lambda: scatter_wait(oc, sc))
        scatter_start(i - 1, op, sp)
        scatter_wait(op, sp)
        spare_row = y_hbm.shape[0] - t * SLAB
        oc[...] = jnp.zeros(oc.shape, F32)
        spare = pltpu.make_async_copy(
            oc, y_hbm.at[pl.ds(spare_row, t * SLAB), :], sc)
        spare.start()
        spare.wait()

    even = (i % 2) == 0
    odd = jnp.logical_not(even)
    g0, g1, s0, s1 = gsem.at[0], gsem.at[1], ssem.at[0], ssem.at[1]
    pl.when(jnp.logical_and(i < nu, even))(
        lambda: step(xbuf0, xbuf1, obuf0, obuf1, g0, g1, s0, s1))
    pl.when(jnp.logical_and(i < nu, odd))(
        lambda: step(xbuf1, xbuf0, obuf1, obuf0, g1, g0, s1, s0))
    pl.when(jnp.logical_and(i == nu, even))(
        lambda: drain(obuf0, obuf1, s0, s1))
    pl.when(jnp.logical_and(i == nu, odd))(
        lambda: drain(obuf1, obuf0, s1, s0))


def _expert_ffn(tea, teb, n_used, gsrc, sdst, wa_tok, wb_tok, h2, wg, wu, wd,
                layer):
    n_tok = h2.shape[0] // SLAB
    max_tiles = gsrc.shape[0] // FFN_TILE
    wa = lambda a, b: pl.BlockSpec(
        (None, None, a, b), lambda i, ta, tb, *_: (layer, ta[i], 0, 0))
    wb = lambda a, b: pl.BlockSpec(
        (None, None, a, b), lambda i, ta, tb, *_: (layer, tb[i], 0, 0))
    anyspec = pl.BlockSpec(memory_space=pl.ANY)
    return pl.pallas_call(
        _ffn_kernel,
        out_shape=jax.ShapeDtypeStruct(((n_tok + FFN_TILE) * SLAB, LANES), F32),
        grid_spec=pltpu.PrefetchScalarGridSpec(
            num_scalar_prefetch=7,
            grid=(max_tiles + 1,),
            in_specs=[anyspec,
                      wa(D_MODEL, D_EXPERT), wa(D_MODEL, D_EXPERT),
                      wa(D_EXPERT, D_MODEL),
                      wb(D_MODEL, D_EXPERT), wb(D_MODEL, D_EXPERT),
                      wb(D_EXPERT, D_MODEL)],
            out_specs=anyspec,
            scratch_shapes=[pltpu.VMEM((FFN_TILE * SLAB, LANES), F32)] * 4 + [
                            pltpu.VMEM((FFN_TILE, LANES), F32),
                            pltpu.VMEM((FFN_TILE, LANES), F32),
                            pltpu.SemaphoreType.DMA((2,)),
                            pltpu.SemaphoreType.DMA((2,))],
        ),
        compiler_params=_cparams(("arbitrary",)),
        name="moe_ffn",
    )(tea, teb, n_used, gsrc, sdst, wa_tok, wb_tok, h2, wg, wu, wd, wg, wu, wd)


def _moe(h2, ri, rw, cnt, wg_bf, wu_bf, wd_bf, layer):
    n = h2.shape[0] // SLAB
    max_tiles = n // FFN_TILE + N_PAIRS
    n_slots = max_tiles * FFN_TILE
    counts = cnt[:N_PAIRS, 0].astype(I32)
    padded = ((counts + FFN_TILE - 1) // FFN_TILE) * FFN_TILE
    ends = jnp.cumsum(padded)
    offs = ends - padded
    q, rank = ri[0], ri[1]
    cls = jnp.arange(N_PAIRS, dtype=I32)
    slot = rank + jnp.sum(jnp.where(q[None, :] == cls[:, None], offs[:, None], 0),
                          axis=0)
    n_used = ends[-1] // FFN_TILE
    tile = jnp.arange(max_tiles + 1, dtype=I32)
    tq = jnp.sum((tile[:, None] * FFN_TILE >= ends[None, :]).astype(I32), axis=1)
    tq_last = jnp.sum(((n_used - 1) * FFN_TILE >= ends).astype(I32))
    tq = jnp.where(tile < n_used, tq, tq_last)
    pa = jnp.array([a for a, _ in PAIRS], I32)
    pb = jnp.array([b for _, b in PAIRS], I32)
    grp, pidx = tq // len(PAIRS), tq % len(PAIRS)
    tea = EXPERTS_PER_GROUP * grp + jnp.take(pa, pidx)
    teb = EXPERTS_PER_GROUP * grp + jnp.take(pb, pidx)
    gsrc, sdst = _invert(slot, ends, n_slots)
    return _expert_ffn(tea, teb, n_used.reshape(1), gsrc, sdst, rw[0], rw[1], h2,
                       wg_bf, wu_bf, wd_bf, layer)


def _moe_out_tm_kernel(x1_ref, y_ref, g2_ref, op_ref, os_ref, scr, *, n_p_tiles):
    i = pl.program_id(0)
    y = _rows_from_slabs(y_ref, x1_ref.shape[0])
    x2 = x1_ref[...] + _per_row(y, g2_ref[...], lambda a, b: a * b)
    pl.when(i < n_p_tiles)(
        lambda: _from_time_major(x2, op_ref, scr, op_ref.shape[0]))
    pl.when(i >= n_p_tiles)(
        lambda: _from_time_major(x2, os_ref, scr, os_ref.shape[0]))


def _moe_out_tm(x1, y, modpat, shape_p, shape_s):
    n = x1.shape[0]
    tm = TOKEN_TILE
    n_p_tiles = shape_p[0] * shape_p[1] // tm
    p = modpat.shape[1]
    rowblk = pl.BlockSpec((tm, D_MODEL), lambda i: (i, 0))
    stream = _stream_specs(jax.ShapeDtypeStruct(shape_p, F32),
                           jax.ShapeDtypeStruct(shape_s, F32), n_p_tiles)
    return pl.pallas_call(
        functools.partial(_moe_out_tm_kernel, n_p_tiles=n_p_tiles),
        out_shape=[jax.ShapeDtypeStruct(shape_p, F32),
                   jax.ShapeDtypeStruct(shape_s, F32)],
        grid=(n // tm,),
        in_specs=[rowblk,
                  pl.BlockSpec((tm * SLAB, LANES), lambda i: (i, 0)),
                  pl.BlockSpec((None, p, D_MODEL),
                               lambda i: (jnp.where(i < n_p_tiles, 0, 1), 0, 5))],
        out_specs=stream,
        scratch_shapes=[pltpu.VMEM((LANE_TILES, tm, LANES), F32)],
        compiler_params=_cparams(("arbitrary",)),
        name="moe_out_tm",
    )(x1, y, modpat)


def _moe_out_final_kernel(x1_ref, y_ref, g2_ref, nfin_ref, op_ref, os_ref,
                          *, n_p_tiles):
    i = pl.program_id(0)
    y = _rows_from_slabs(y_ref, x1_ref.shape[0])
    x2 = x1_ref[...] + _per_row(y, g2_ref[...], lambda a, b: a * b)
    out = _rms(x2) * nfin_ref[...]

    @pl.when(i < n_p_tiles)
    def _():
        op_ref[...] = out

    @pl.when(i >= n_p_tiles)
    def _():
        os_ref[...] = out


def _moe_out_final(x1, y, modpat, norm_final, n_p, rows_per_request):
    n = x1.shape[0]
    tm = TOKEN_TILE
    n_p_tiles = n_p // tm
    p = modpat.shape[1]
    rowblk = pl.BlockSpec((tm, D_MODEL), lambda i: (i, 0))
    pat = lambda i: jnp.where(
        i < n_p_tiles, 0, 1 + (i - n_p_tiles) // (rows_per_request // tm))
    return pl.pallas_call(
        functools.partial(_moe_out_final_kernel, n_p_tiles=n_p_tiles),
        out_shape=[jax.ShapeDtypeStruct((n_p, D_MODEL), F32),
                   jax.ShapeDtypeStruct((n - n_p, D_MODEL), F32)],
        grid=(n // tm,),
        in_specs=[rowblk,
                  pl.BlockSpec((tm * SLAB, LANES), lambda i: (i, 0)),
                  pl.BlockSpec((None, p, D_MODEL), lambda i: (pat(i), 0, 5)),
                  pl.BlockSpec((1, D_MODEL), lambda i: (0, 0))],
        out_specs=[
            pl.BlockSpec((tm, D_MODEL),
                         lambda i: (jnp.minimum(i, n_p_tiles - 1), 0)),
            pl.BlockSpec((tm, D_MODEL),
                         lambda i: (jnp.maximum(i - n_p_tiles, 0), 0))],
        compiler_params=_cparams(("arbitrary",)),
        name="moe_out_final",
    )(x1, y, modpat, norm_final)


def _block_diag_weights(bb_re, bb_im, c_re, c_im):
    lt = LANE_TILES
    gl = SSM_GROUPS // lt
    eye = jnp.eye(gl, dtype=F32)

    def in_map(bb):
        b5 = bb.reshape(2, lt, gl, STATE_DIM, SSM_GROUP)
        t = jnp.einsum("dinph,kn->dikhnp", b5, eye)
        return t.reshape(2, lt, gl * SSM_GROUP, gl * STATE_DIM)

    def out_map(cc):
        c5 = cc.reshape(2, lt, gl, SSM_GROUP, STATE_DIM)
        t = jnp.einsum("dikhp,kn->dikpnh", c5, eye)
        return t.reshape(2, lt, gl * STATE_DIM, gl * SSM_GROUP)

    bt = jnp.concatenate([in_map(bb_re), in_map(bb_im)], axis=-1).astype(BF16)
    return bt, out_map(c_re).astype(BF16), out_map(c_im).astype(BF16)


def kernel(x_prompt, x_sample, c, state_ssm_re, state_ssm_im, c_ctx, norm_mix, norm_ffn, w_ada, b_ada, ssm_w_in, ssm_lam_re, ssm_lam_im, ssm_log_dt, ssm_b_re, ssm_b_im, ssm_c_re, ssm_c_im, ssm_d, ssm_w_glu, fnet_w_out, router_w, router_b, moe_w_gate, moe_w_up, moe_w_down, norm_final):
    bp, lp, _ = x_prompt.shape
    bs, ls, _ = x_sample.shape
    n_p = bp * lp
    n_s = bs * ls
    n = n_p + n_s
    tm = TOKEN_TILE

    cond = jnp.zeros((N_COND, D_MODEL), F32).at[0].set(c_ctx).at[1:1 + bs].set(c)
    modtab = _ada_table(cond, w_ada, b_ada).reshape(DEPTH, N_COND, N_MOD, D_MODEL)

    rwt_hi = router_w.T.astype(BF16)
    rwt_lo = (router_w.T - rwt_hi.astype(F32)).astype(BF16)
    rwt = jnp.concatenate([rwt_hi, rwt_lo])
    rb = router_b.reshape(N_EXPERTS, 1)
    row = lambda v: v.reshape(1, D_MODEL)

    period = max(bp, bs)
    pat_tm = jnp.stack([
        jnp.broadcast_to(modtab[0, 0], (period, N_MOD, D_MODEL)),
        jnp.tile(modtab[0, 1:1 + bs], (period // bs, 1, 1))])
    pat_tm = pat_tm.reshape(2, period, N_MOD * D_MODEL)

    a_re, a_im, bb_re, bb_im = _zoh(ssm_lam_re[0], ssm_lam_im[0], ssm_log_dt[0],
                                    ssm_b_re[0], ssm_b_im[0])
    bt, ct_re, ct_im = _block_diag_weights(bb_re, bb_im, ssm_c_re[0], ssm_c_im[0])
    a_re = a_re.reshape(2, 1, N_STATE)
    a_im = a_im.reshape(2, 1, N_STATE)

    u = _ssm_in(x_prompt, x_sample, row(norm_mix[0]), pat_tm,
                ssm_w_in[0].astype(BF16))
    h0s_re = state_ssm_re[:, 0].reshape(bs, 2, N_STATE).transpose(1, 0, 2)
    h0s_im = state_ssm_im[:, 0].reshape(bs, 2, N_STATE).transpose(1, 0, 2)
    y_scan, fin_re, fin_im = _ssm_scan(u, bt, ct_re, ct_im, a_re, a_im,
                                       h0s_re, h0s_im, bp=bp, lp=lp, bs=bs, ls=ls)
    tri = jnp.triu(jnp.ones((tm, tm), BF16), k=1)
    x1, h2, ri, rw, cnt = _ssm_out(u, y_scan, row(ssm_d[0]),
                               ssm_w_glu[0].astype(BF16), x_prompt, x_sample,
                               pat_tm, row(norm_ffn[0]), rwt, rb, tri)
    wg_bf, wu_bf, wd_bf = (w.astype(BF16) for w in (moe_w_gate, moe_w_up,
                                                    moe_w_down))
    y_moe = _moe(h2, ri, rw, cnt, wg_bf, wu_bf, wd_bf, 0)
    x2_p, x2_s = _moe_out_tm(x1, y_moe, pat_tm, x_prompt.shape, x_sample.shape)

    pat_bm = jnp.broadcast_to(modtab[1][:1 + bs, None],
                              (1 + bs, SUBLANES, N_MOD, D_MODEL))
    pat_bm = pat_bm.reshape(1 + bs, SUBLANES, N_MOD * D_MODEL)
    x3, h2, ri, rw, cnt = _fnet(x2_p, x2_s, row(norm_mix[1]), pat_bm,
                            fnet_w_out[0].astype(BF16), row(norm_ffn[1]), rwt, rb)
    y_moe = _moe(h2, ri, rw, cnt, wg_bf, wu_bf, wd_bf, 1)
    y_p, y_s = _moe_out_final(x3, y_moe, pat_bm, row(norm_final), n_p, ls)

    st = lambda f: f.transpose(1, 0, 2).reshape(bp, 1, 2, SSM_GROUPS, STATE_DIM)
    return (y_p.reshape(bp, lp, D_MODEL), y_s.reshape(bs, ls, D_MODEL),
            st(fin_re), st(fin_im))
```

```python
import functools

import jax
import jax.numpy as jnp
from jax import lax
from jax.experimental import pallas as pl
from jax.experimental.pallas import tpu as pltpu

F32 = jnp.float32
BF16 = jnp.bfloat16
I32 = jnp.int32

D_MODEL = 1024
DEPTH = 2
SSM_GROUP = 16
SSM_GROUPS = 64
STATE_DIM = 64
N_STATE = SSM_GROUPS * STATE_DIM
FNET_GROUP = 128
N_EXPERTS = 16
N_EXPERT_GROUPS = 4
EXPERTS_PER_GROUP = 4
D_EXPERT = 1024
N_MOD = 6
EPS = 1e-6

LANES = 128
SUBLANES = 8
LANE_TILES = D_MODEL // LANES
VMEM_LIMIT = 56 * 1024 * 1024

TOKEN_TILE = 512
FNET_TILE = 256
FFN_TILE = 256
N_COND = 16

PAIRS = ((0, 1), (0, 2), (0, 3), (1, 3), (1, 2), (3, 2))
N_PAIRS = N_EXPERT_GROUPS * len(PAIRS)
N_CLASS_ROWS = 32
SLAB = SUBLANES


def _cparams(sem, vmem=VMEM_LIMIT):
    return pltpu.CompilerParams(dimension_semantics=sem, vmem_limit_bytes=vmem)


def _sigmoid(x):
    return 1.0 / (1.0 + jnp.exp(-x))


def _gelu_tanh(x):
    c = 0.7978845608028654
    return x * (0.5 * (1.0 + jnp.tanh(c * (x + 0.044715 * (x * x * x)))))


def _per_row(v, pat, fn):
    tm, d = v.shape
    p = pat.shape[0]
    return fn(v.reshape(tm // p, p, d), pat[None]).reshape(tm, d)


def _rms(x):
    ms = jnp.mean(x * x, axis=-1, keepdims=True)
    return x * lax.rsqrt(ms + EPS)


def _norm_mod(x, g, sc, sh):
    y = _rms(x) * g
    y = _per_row(y, sc, lambda a, b: a * (1.0 + b))
    return _per_row(y, sh, lambda a, b: a + b)


def _lane_tiles(v):
    return [v[:, LANES * k:LANES * (k + 1)] for k in range(v.shape[1] // LANES)]


def _to_time_major(x_ref, scr, batch):
    tt = x_ref.shape[1]
    for b in range(batch):
        for k, piece in enumerate(_lane_tiles(x_ref[b])):
            scr[k, pl.ds(b, tt, stride=batch), :] = piece
    return jnp.concatenate([scr[k] for k in range(LANE_TILES)], axis=1)


def _from_time_major(v, o_ref, scr, batch):
    tt = v.shape[0] // batch
    for k, piece in enumerate(_lane_tiles(v)):
        scr[k] = piece
    for b in range(batch):
        o_ref[b] = jnp.concatenate(
            [scr[k, pl.ds(b, tt, stride=batch), :] for k in range(LANE_TILES)],
            axis=1)


def _slab_rows(ref, k, n):
    return ref[pl.ds(k, n, stride=SLAB), :]


def _store_slabs(v, ref):
    n = v.shape[0]
    for k, piece in enumerate(_lane_tiles(v)):
        ref[pl.ds(k, n, stride=SLAB), :] = piece


def _rows_from_slabs(y_ref, n):
    return jnp.concatenate([_slab_rows(y_ref, k, n) for k in range(SLAB)], axis=1)


def _route(logits_t, rb, tri, carry):
    ne, tm = logits_t.shape
    s = _sigmoid(logits_t)
    bz = s + rb
    row = lambda a, r: a[r:r + 1, :]
    gs = []
    for g in range(N_EXPERT_GROUPS):
        v0, v1, v2, v3 = (row(bz, EXPERTS_PER_GROUP * g + j) for j in range(4))
        hi1, lo1 = jnp.maximum(v0, v1), jnp.minimum(v0, v1)
        hi2, lo2 = jnp.maximum(v2, v3), jnp.minimum(v2, v3)
        top1 = jnp.maximum(hi1, hi2)
        top2 = jnp.maximum(jnp.minimum(hi1, hi2), jnp.maximum(lo1, lo2))
        gs.append(top1 + top2)
    bg = jnp.zeros((1, tm), I32)
    bv = gs[0]
    for g in range(1, N_EXPERT_GROUPS):
        upd = gs[g] > bv
        bg = jnp.where(upd, g, bg)
        bv = jnp.where(upd, gs[g], bv)
    cb, cs = [], []
    for j in range(EXPERTS_PER_GROUP):
        vb, vs = row(bz, j), row(s, j)
        for g in range(1, N_EXPERT_GROUPS):
            sel = bg == g
            vb = jnp.where(sel, row(bz, EXPERTS_PER_GROUP * g + j), vb)
            vs = jnp.where(sel, row(s, EXPERTS_PER_GROUP * g + j), vs)
        cb.append(vb)
        cs.append(vs)
    i1 = jnp.zeros((1, tm), I32)
    b1, s1 = cb[0], cs[0]
    for j in range(1, EXPERTS_PER_GROUP):
        upd = cb[j] > b1
        i1 = jnp.where(upd, j, i1)
        b1 = jnp.where(upd, cb[j], b1)
        s1 = jnp.where(upd, cs[j], s1)
    i2 = jnp.zeros((1, tm), I32)
    b2 = jnp.full((1, tm), -jnp.inf, F32)
    s2 = jnp.zeros((1, tm), F32)
    for j in range(EXPERTS_PER_GROUP):
        cand = jnp.where(i1 == j, -jnp.inf, cb[j])
        upd = cand > b2
        i2 = jnp.where(upd, j, i2)
        b2 = jnp.where(upd, cand, b2)
        s2 = jnp.where(upd, cs[j], s2)
    den = s1 + s2
    w1 = s1 / den
    w2 = s2 / den
    lo = jnp.minimum(i1, i2)
    hi = jnp.maximum(i1, i2)
    pidx = jnp.where(lo == 0, hi - 1,
                     jnp.where(lo == 1, jnp.where(hi == 3, 3, 4), 5))
    first = jnp.where(pidx < 3, 0, jnp.where(pidx < 5, 1, 3))
    wa = jnp.where(i1 == first, w1, w2)
    wb = jnp.where(i1 == first, w2, w1)
    q = len(PAIRS) * bg + pidx
    qio = lax.broadcasted_iota(I32, (N_CLASS_ROWS, tm), 0)
    oh = qio == q
    ohf = jnp.where(oh, 1.0, 0.0)
    cum = jnp.dot(ohf.astype(BF16), tri, preferred_element_type=F32) + carry
    rank = jnp.sum(jnp.where(oh, cum, 0.0), axis=0, keepdims=True)
    new_carry = carry + jnp.sum(ohf, axis=1, keepdims=True)
    return q, rank.astype(I32), wa, wb, new_carry


def _post_mixer(m, x, g1, nf, sc2, sh2, rwt_ref, rb_ref, tri_ref,
                x1_ref, h2_ref, ri_ref, rw_ref, cnt_ref, carry, valid=None):
    tm = x.shape[0]
    x1 = x + _per_row(m, g1, lambda a, b: a * b)
    x1_ref[...] = x1
    h2 = _norm_mod(x1, nf, sc2, sh2)
    h_hi = h2.astype(BF16)
    h_lo = (h2 - h_hi.astype(F32)).astype(BF16)
    nt = (((1,), (1,)), ((), ()))
    la = lax.dot_general(rwt_ref[...], h_hi, nt, preferred_element_type=F32)
    lb = lax.dot_general(rwt_ref[0:N_EXPERTS, :], h_lo, nt,
                         preferred_element_type=F32)
    logits_t = la[:N_EXPERTS] + la[N_EXPERTS:] + lb
    q, rank, wa, wb, nc = _route(logits_t, rb_ref[...], tri_ref[...],
                                 carry[:, 0:1])
    _store_slabs(h2, h2_ref)
    rw_ref[0:1, :] = wa
    rw_ref[1:2, :] = wb
    rw_ref[2:8, :] = jnp.zeros((6, tm), F32)
    ri_ref[0:1, :] = q
    ri_ref[1:2, :] = rank
    ri_ref[2:8, :] = jnp.zeros((6, tm), I32)
    if valid is not None:
        nc = jnp.where(valid, nc, carry[:, 0:1])
    carry[...] = jnp.broadcast_to(nc, carry.shape)
    cnt_ref[...] = carry[...]


def _ada_kernel(cond_ref, w_ref, b_ref, o_ref):
    c = cond_ref[...]
    s = c * _sigmoid(c)
    s_hi = s.astype(BF16)
    s_lo = (s - s_hi.astype(F32)).astype(BF16)
    w = w_ref[...]
    w_hi = w.astype(BF16)
    w_lo = (w - w_hi.astype(F32)).astype(BF16)
    dot = functools.partial(jnp.dot, preferred_element_type=F32)
    o_ref[...] = dot(s_hi, w_hi) + dot(s_lo, w_hi) + dot(s_hi, w_lo) + b_ref[...]


def _ada_table(cond, w_ada, b_ada):
    tn = 1536
    n_out = N_MOD * D_MODEL
    return pl.pallas_call(
        _ada_kernel,
        out_shape=jax.ShapeDtypeStruct((DEPTH, N_COND, n_out), F32),
        grid=(DEPTH, n_out // tn),
        in_specs=[
            pl.BlockSpec((N_COND, D_MODEL), lambda l, j: (0, 0)),
            pl.BlockSpec((None, D_MODEL, tn), lambda l, j: (l, 0, j)),
            pl.BlockSpec((None, 1, tn), lambda l, j: (l, 0, j)),
        ],
        out_specs=pl.BlockSpec((None, N_COND, tn), lambda l, j: (l, 0, j)),
        compiler_params=_cparams(("arbitrary", "arbitrary")),
        name="ada_table",
    )(cond, w_ada, b_ada.reshape(DEPTH, 1, n_out))


def _zoh_kernel(lr_ref, li_ref, ldt_ref, br_ref, bi_ref,
                are_ref, aim_ref, bbre_ref, bbim_ref):
    lr = lr_ref[...]
    li = li_ref[...]
    dt = jnp.exp(ldt_ref[...])
    mag = jnp.exp(lr * dt)
    a_re = mag * jnp.cos(li * dt)
    a_im = mag * jnp.sin(li * dt)
    den = lr * lr + li * li
    nr = a_re - 1.0
    f_re = (nr * lr + a_im * li) / den
    f_im = (a_im * lr - nr * li) / den
    br = br_ref[...]
    bi = bi_ref[...]
    are_ref[...] = a_re
    aim_ref[...] = a_im
    bbre_ref[...] = f_re * br - f_im * bi
    bbim_ref[...] = f_re * bi + f_im * br


def _zoh(lam_re, lam_im, log_dt, b_re, b_im):
    shape = b_re.shape
    flat = (shape[0] * shape[1] * shape[2] * shape[3] // LANES, LANES)
    bc = lambda a: jnp.broadcast_to(a, shape).reshape(flat)
    args = (bc(lam_re[..., None]), bc(lam_im[..., None]),
            bc(log_dt[:, :, None, None]), b_re.reshape(flat), b_im.reshape(flat))
    outs = pl.pallas_call(
        _zoh_kernel,
        out_shape=[jax.ShapeDtypeStruct(flat, F32)] * 4,
        name="zoh_discretize",
    )(*args)
    a_re, a_im, bb_re, bb_im = (o.reshape(shape) for o in outs)
    return a_re[..., 0], a_im[..., 0], bb_re, bb_im


def _stream_specs(xp, xs, n_p_tiles):
    bp, bs = xp.shape[0], xs.shape[0]
    return [
        pl.BlockSpec((bp, TOKEN_TILE // bp, D_MODEL),
                     lambda i: (0, jnp.minimum(i, n_p_tiles - 1), 0)),
        pl.BlockSpec((bs, TOKEN_TILE // bs, D_MODEL),
                     lambda i: (0, jnp.maximum(i - n_p_tiles, 0), 0)),
    ]


def _ssm_in_kernel(xp_ref, xs_ref, g_ref, sc_ref, sh_ref, w_ref, u_ref, scr,
                   *, n_p_tiles):
    def run(x_ref):
        x = _to_time_major(x_ref, scr, x_ref.shape[0])
        h = _norm_mod(x, g_ref[...], sc_ref[...], sh_ref[...])
        u_ref[...] = jnp.dot(h.astype(BF16), w_ref[...],
                             preferred_element_type=F32)

    i = pl.program_id(0)
    pl.when(i < n_p_tiles)(lambda: run(xp_ref))
    pl.when(i >= n_p_tiles)(lambda: run(xs_ref))


def _ssm_in(xp, xs, g, modpat, w_in_bf):
    n = (xp.shape[0] * xp.shape[1] + xs.shape[0] * xs.shape[1])
    tm = TOKEN_TILE
    n_p_tiles = xp.shape[0] * xp.shape[1] // tm
    p = modpat.shape[1]
    pat = lambda i: jnp.where(i < n_p_tiles, 0, 1)
    mod = lambda k: pl.BlockSpec((None, p, D_MODEL), lambda i: (pat(i), 0, k))
    return pl.pallas_call(
        functools.partial(_ssm_in_kernel, n_p_tiles=n_p_tiles),
        out_shape=jax.ShapeDtypeStruct((n, D_MODEL), F32),
        grid=(n // tm,),
        in_specs=_stream_specs(xp, xs, n_p_tiles) + [
            pl.BlockSpec((1, D_MODEL), lambda i: (0, 0)),
            mod(1), mod(0),
            pl.BlockSpec((D_MODEL, D_MODEL), lambda i: (0, 0)),
        ],
        out_specs=pl.BlockSpec((tm, D_MODEL), lambda i: (i, 0)),
        scratch_shapes=[pltpu.VMEM((LANE_TILES, tm, LANES), F32)],
        compiler_params=_cparams(("arbitrary",)),
        name="ssm_in",
    )(xp, xs, g, modpat, modpat, w_in_bf)


def _scan_kernel(u_ref, bt_ref, cre_ref, cim_ref, are_ref, aim_ref,
                 h0re_ref, h0im_ref, y_ref, fre_ref, fim_ref,
                 *scratch, n_p_chunks, bp, bs):
    xre = scratch[:LANE_TILES]
    xim = scratch[LANE_TILES:2 * LANE_TILES]
    hre, him = scratch[2 * LANE_TILES:]
    d = pl.program_id(0)
    c = pl.program_id(1)
    cols = N_STATE // LANE_TILES
    rows = u_ref.shape[0]

    def chunk(batch, reverse):
        steps = rows // batch

        def b_proj(s):
            u = u_ref[:, LANES * s:LANES * (s + 1)].astype(BF16)
            xt = jnp.dot(u, bt_ref[s], preferred_element_type=F32)
            xre[s][...] = xt[:, :cols]
            xim[s][...] = xt[:, cols:]

        def scan(s):
            sl = slice(cols * s, cols * (s + 1))
            ar = jnp.broadcast_to(are_ref[:, sl], (batch, cols))
            ai = jnp.broadcast_to(aim_ref[:, sl], (batch, cols))
            hr = hre[0:batch, sl]
            hi = him[0:batch, sl]
            for t in range(steps):
                tt = steps - 1 - t if reverse else t
                r = slice(tt * batch, (tt + 1) * batch)
                nr = ar * hr - ai * hi + xre[s][r, :]
                ni = ar * hi + ai * hr + xim[s][r, :]
                xre[s][r, :] = nr
                xim[s][r, :] = ni
                hr, hi = nr, ni
            hre[0:batch, sl] = hr
            him[0:batch, sl] = hi

        def c_proj(s):
            y_ref[:, LANES * s:LANES * (s + 1)] = (
                jnp.dot(xre[s][...].astype(BF16), cre_ref[s],
                        preferred_element_type=F32)
                - jnp.dot(xim[s][...].astype(BF16), cim_ref[s],
                          preferred_element_type=F32))

        for stage in range(LANE_TILES + 2):
            if stage < LANE_TILES:
                b_proj(stage)
            if 0 <= stage - 1 < LANE_TILES:
                scan(stage - 1)
            if 0 <= stage - 2 < LANE_TILES:
                c_proj(stage - 2)

    @pl.when(c == 0)
    def _():
        hre[...] = jnp.zeros(hre.shape, F32)
        him[...] = jnp.zeros(him.shape, F32)

    @pl.when(c == n_p_chunks)
    def _():
        hre[0:bs, :] = h0re_ref[...]
        him[0:bs, :] = h0im_ref[...]

    in_p = c < n_p_chunks
    in_s = jnp.logical_not(in_p)
    fwd = d == 0
    bwd = jnp.logical_not(fwd)
    pl.when(jnp.logical_and(in_p, fwd))(lambda: chunk(bp, False))
    pl.when(jnp.logical_and(in_p, bwd))(lambda: chunk(bp, True))
    pl.when(jnp.logical_and(in_s, fwd))(lambda: chunk(bs, False))
    pl.when(jnp.logical_and(in_s, bwd))(lambda: chunk(bs, True))

    @pl.when(c == n_p_chunks - 1)
    def _():
        fre_ref[...] = hre[0:bp, :]
        fim_ref[...] = him[0:bp, :]


def _ssm_scan(u, bt, ct_re, ct_im, a_re, a_im, h0_re, h0_im, *, bp, lp, bs, ls):
    n = u.shape[0]
    rows = TOKEN_TILE
    n_p_chunks = bp * lp // rows
    n_s_chunks = bs * ls // rows

    def chunk(d, c):
        in_p = jnp.where(d == 0, c, n_p_chunks - 1 - c)
        cs = c - n_p_chunks
        in_s = n_p_chunks + jnp.where(d == 0, cs, n_s_chunks - 1 - cs)
        return jnp.where(c < n_p_chunks, in_p, in_s)

    cols = N_STATE // LANE_TILES
    dspec = lambda shape: pl.BlockSpec((None,) + shape,
                                       lambda d, c: (d,) + (0,) * len(shape))
    kern = functools.partial(_scan_kernel, n_p_chunks=n_p_chunks, bp=bp, bs=bs)
    return pl.pallas_call(
        kern,
        out_shape=[jax.ShapeDtypeStruct((2, n, D_MODEL), F32),
                   jax.ShapeDtypeStruct((2, bp, N_STATE), F32),
                   jax.ShapeDtypeStruct((2, bp, N_STATE), F32)],
        grid=(2, n_p_chunks + n_s_chunks),
        in_specs=[
            pl.BlockSpec((rows, D_MODEL), lambda d, c: (chunk(d, c), 0)),
            dspec((LANE_TILES, LANES, 2 * cols)),
            dspec((LANE_TILES, cols, LANES)),
            dspec((LANE_TILES, cols, LANES)),
            dspec((1, N_STATE)),
            dspec((1, N_STATE)),
            dspec((bs, N_STATE)),
            dspec((bs, N_STATE)),
        ],
        out_specs=[
            pl.BlockSpec((None, rows, D_MODEL), lambda d, c: (d, chunk(d, c), 0)),
            dspec((bp, N_STATE)),
            dspec((bp, N_STATE)),
        ],
        scratch_shapes=(
            [pltpu.VMEM((rows, cols), F32)] * (2 * LANE_TILES)
            + [pltpu.VMEM((max(bp, bs), N_STATE), F32)] * 2),
        compiler_params=_cparams(("arbitrary", "arbitrary")),
        name="ssm_scan",
    )(u, bt, ct_re, ct_im, a_re, a_im, h0_re, h0_im)


def _ssm_out_kernel(u_ref, y0_ref, y1_ref, dsk_ref, wglu_ref, xp_ref, xs_ref,
                    g1_ref, nf_ref, sc2_ref, sh2_ref, rwt_ref, rb_ref, tri_ref,
                    x1_ref, h2_ref, ri_ref, rw_ref, cnt_ref, scr, carry,
                    *, n_p_tiles):
    i = pl.program_id(0)

    @pl.when(i == 0)
    def _():
        carry[...] = jnp.zeros(carry.shape, F32)

    def run(x_ref):
        x = _to_time_major(x_ref, scr, x_ref.shape[0])
        y = u_ref[...] * dsk_ref[...] + y0_ref[...] + y1_ref[...]
        ge = _gelu_tanh(y).astype(BF16)
        vg = jnp.dot(ge, wglu_ref[...], preferred_element_type=F32)
        m = vg[:, :D_MODEL] * _sigmoid(vg[:, D_MODEL:])
        _post_mixer(m, x, g1_ref[...], nf_ref[...], sc2_ref[...], sh2_ref[...],
                    rwt_ref, rb_ref, tri_ref, x1_ref, h2_ref, ri_ref, rw_ref,
                    cnt_ref, carry)

    pl.when(i < n_p_tiles)(lambda: run(xp_ref))
    pl.when(i >= n_p_tiles)(lambda: run(xs_ref))


def _route_out_shapes(n):
    return [jax.ShapeDtypeStruct((n, D_MODEL), F32),
            jax.ShapeDtypeStruct((n * SLAB, LANES), F32),
            jax.ShapeDtypeStruct((SUBLANES, n), I32),
            jax.ShapeDtypeStruct((SUBLANES, n), F32),
            jax.ShapeDtypeStruct((N_CLASS_ROWS, LANES), F32)]


def _ssm_out(u, y, d_skip, w_glu_bf, xp, xs, modpat, norm_ffn, rwt, rb, tri):
    n = u.shape[0]
    tm = TOKEN_TILE
    n_p_tiles = xp.shape[0] * xp.shape[1] // tm
    p = modpat.shape[1]
    pat = lambda i: jnp.where(i < n_p_tiles, 0, 1)
    mod = lambda k: pl.BlockSpec((None, p, D_MODEL), lambda i: (pat(i), 0, k))
    full = lambda shape: pl.BlockSpec(shape, lambda i: (0,) * len(shape))
    rowblk = pl.BlockSpec((tm, D_MODEL), lambda i: (i, 0))
    return pl.pallas_call(
        functools.partial(_ssm_out_kernel, n_p_tiles=n_p_tiles),
        out_shape=_route_out_shapes(n),
        grid=(n // tm,),
        in_specs=[
            rowblk,
            pl.BlockSpec((None, tm, D_MODEL), lambda i: (0, i, 0)),
            pl.BlockSpec((None, tm, D_MODEL), lambda i: (1, i, 0)),
            full((1, D_MODEL)),
            full((D_MODEL, 2 * D_MODEL)),
        ] + _stream_specs(xp, xs, n_p_tiles) + [
            mod(2), full((1, D_MODEL)), mod(4), mod(3),
            full((2 * N_EXPERTS, D_MODEL)), full((N_EXPERTS, 1)), full((tm, tm)),
        ],
        out_specs=[rowblk,
                   pl.BlockSpec((tm * SLAB, LANES), lambda i: (i, 0)),
                   pl.BlockSpec((SUBLANES, tm), lambda i: (0, i)),
                   pl.BlockSpec((SUBLANES, tm), lambda i: (0, i)),
                   full((N_CLASS_ROWS, LANES))],
        scratch_shapes=[pltpu.VMEM((LANE_TILES, tm, LANES), F32),
                        pltpu.VMEM((N_CLASS_ROWS, LANES), F32)],
        compiler_params=_cparams(("arbitrary",)),
        name="ssm_out",
    )(u, y, y, d_skip, w_glu_bf, xp, xs, modpat, norm_ffn, modpat, modpat,
      rwt, rb, tri)


def _fnet_kernel(xp_ref, xpp_ref, xs_ref, nm_ref, sc1_ref, sh1_ref, cs_ref,
                 dftp_ref, dfts_ref, wout_ref, g1_ref, nf_ref, sc2_ref, sh2_ref,
                 rwt_ref, rb_ref, tri_ref, x1_ref, h2_ref, ri_ref, rw_ref, cnt_ref,
                 ucs, ybuf, carry, *, bp, tiles_s):
    s = pl.program_id(0)
    n_groups = D_MODEL // FNET_GROUP
    tr = FNET_TILE
    per = tiles_s + 1
    wslot = s % 2

    @pl.when(s == 0)
    def _():
        carry[...] = jnp.zeros(carry.shape, F32)
        ybuf[...] = jnp.zeros(ybuf.shape, F32)

    def stage1(x_ref, seq):
        def body(r, _):
            rows = pl.ds(pl.multiple_of(r * tr, tr), tr)
            h = _norm_mod(x_ref[rows, :], nm_ref[...], sc1_ref[...],
                          sh1_ref[...]).astype(BF16)
            for k in range(n_groups):
                cols = slice(FNET_GROUP * k, FNET_GROUP * (k + 1))
                t = jnp.dot(h[:, cols], cs_ref[...], preferred_element_type=F32)
                ucs[rows, cols] = t[:, :FNET_GROUP].astype(BF16)
                ucs[pl.ds(pl.multiple_of(seq + r * tr, tr), tr), cols] = (
                    t[:, FNET_GROUP:].astype(BF16))
            return 0
        lax.fori_loop(0, seq // tr, body, 0)

    def step(dft_ref, seq, xrow, valid):
        scale = float((seq * FNET_GROUP) ** -0.5)
        m = jnp.dot(ybuf[1 - wslot].astype(BF16), wout_ref[...],
                    preferred_element_type=F32)
        _post_mixer(m, xrow, g1_ref[...], nf_ref[...], sc2_ref[...],
                    sh2_ref[...], rwt_ref, rb_ref, tri_ref, x1_ref, h2_ref,
                    ri_ref, rw_ref, cnt_ref, carry, valid)
        ybuf[wslot] = jnp.dot(dft_ref[...], ucs[0:2 * seq, :],
                              preferred_element_type=F32) * scale

    @pl.when(s <= bp)
    def _():
        stage1(xp_ref, xp_ref.shape[0])
        step(dftp_ref, xp_ref.shape[0], xpp_ref[...], s >= 1)

    @pl.when(s > bp)
    def _():
        j = (s - bp - 1) % per
        pl.when(j == 0)(lambda: stage1(xs_ref, xs_ref.shape[0]))
        prev = jnp.maximum(j - 1, 0)
        xrow = xs_ref[pl.ds(pl.multiple_of(prev * tr, tr), tr), :]
        step(dfts_ref, xs_ref.shape[0], xrow, j >= 1)


def _dft_table(seq):
    inner = 64
    k = jnp.arange(seq, dtype=I32)[:, None]
    t1 = jnp.arange(seq // inner, dtype=I32)[None, :] * inner
    t2 = jnp.arange(inner, dtype=I32)[None, :]
    ang = lambda t: ((k * t) % seq).astype(F32) * (2.0 * jnp.pi / seq)
    ca, sa = jnp.cos(ang(t1))[:, :, None], jnp.sin(ang(t1))[:, :, None]
    cb, sb = jnp.cos(ang(t2))[:, None, :], jnp.sin(ang(t2))[:, None, :]
    cos = (ca * cb - sa * sb).reshape(seq, seq)
    sin = (sa * cb + ca * sb).reshape(seq, seq)
    return jnp.concatenate([cos, -sin], axis=1).astype(BF16)


def _fnet(xp, xs, norm_mix, modpat, w_out_bf, norm_ffn, rwt, rb):
    bp, lp, _ = xp.shape
    bs, ls, _ = xs.shape
    tr = FNET_TILE
    assert lp == tr
    tiles_s = ls // tr
    per = tiles_s + 1
    n = bp * lp + bs * ls
    p = modpat.shape[1]
    kc = jnp.arange(FNET_GROUP, dtype=I32)
    angc = ((kc[:, None] * kc[None, :]) % FNET_GROUP).astype(F32) * (
        2.0 * jnp.pi / FNET_GROUP)
    cs = jnp.concatenate([jnp.cos(angc), jnp.sin(angc)], axis=1).astype(BF16)
    tri = jnp.triu(jnp.ones((tr, tr), BF16), k=1)

    q = lambda s: jnp.maximum(s - bp - 1, 0)
    sb = lambda s: jnp.minimum(q(s) // per, bs - 1)
    sj = lambda s: q(s) % per
    pat = lambda s: jnp.where(s <= bp, 0, 1 + sb(s))
    blk = lambda s: jnp.where(s <= bp, jnp.maximum(s - 1, 0),
                              bp + sb(s) * tiles_s + jnp.maximum(sj(s) - 1, 0))
    mod = lambda kk: pl.BlockSpec((None, p, D_MODEL), lambda s: (pat(s), 0, kk))
    full = lambda shape: pl.BlockSpec(shape, lambda s: (0,) * len(shape))
    kern = functools.partial(_fnet_kernel, bp=bp, tiles_s=tiles_s)
    return pl.pallas_call(
        kern,
        out_shape=_route_out_shapes(n),
        grid=(bp + 1 + bs * per,),
        in_specs=[
            pl.BlockSpec((None, lp, D_MODEL),
                         lambda s: (jnp.minimum(s, bp - 1), 0, 0)),
            pl.BlockSpec((None, lp, D_MODEL),
                         lambda s: (jnp.clip(s - 1, 0, bp - 1), 0, 0)),
            pl.BlockSpec((None, ls, D_MODEL), lambda s: (sb(s), 0, 0)),
            full((1, D_MODEL)), mod(1), mod(0),
            full((FNET_GROUP, 2 * FNET_GROUP)),
            full((tr, 2 * lp)),
            pl.BlockSpec((tr, 2 * ls),
                         lambda s: (jnp.minimum(sj(s), tiles_s - 1), 0)),
            full((D_MODEL, D_MODEL)),
            mod(2), full((1, D_MODEL)), mod(4), mod(3),
            full((2 * N_EXPERTS, D_MODEL)), full((N_EXPERTS, 1)), full((tr, tr)),
        ],
        out_specs=[pl.BlockSpec((tr, D_MODEL), lambda s: (blk(s), 0)),
                   pl.BlockSpec((tr * SLAB, LANES), lambda s: (blk(s), 0)),
                   pl.BlockSpec((SUBLANES, tr), lambda s: (0, blk(s))),
                   pl.BlockSpec((SUBLANES, tr), lambda s: (0, blk(s))),
                   full((N_CLASS_ROWS, LANES))],
        scratch_shapes=[pltpu.VMEM((2 * ls, D_MODEL), BF16),
                        pltpu.VMEM((2, tr, D_MODEL), F32),
                        pltpu.VMEM((N_CLASS_ROWS, LANES), F32)],
        compiler_params=_cparams(("arbitrary",)),
        name="fnet",
    )(xp, xp, xs, norm_mix, modpat, modpat, cs, _dft_table(lp), _dft_table(ls),
      w_out_bf, modpat, norm_ffn, modpat, modpat, rwt, rb, tri)


def _invert_kernel(slot_ref, ends_ref, gsrc_ref, sdst_ref):
    n = slot_ref.shape[0]
    n_slots = gsrc_ref.shape[0]
    t = FFN_TILE
    unroll = 8

    def pad_tile(base):
        def pad(j, _):
            for k in range(unroll):
                r = j * unroll + k
                gsrc_ref[base + r] = 0
                sdst_ref[base + r] = n + r
            return 0
        lax.fori_loop(0, t // unroll, pad, 0)

    for q in range(N_PAIRS):
        pad_tile(jnp.maximum(ends_ref[q] - t, 0))

    def unused(b, _):
        pad_tile(b * t)
        return 0
    lax.fori_loop(ends_ref[N_PAIRS - 1] // t, n_slots // t, unused, 0)

    def body(j, _):
        for k in range(unroll):
            t = j * unroll + k
            s = slot_ref[t]
            gsrc_ref[s] = t
            sdst_ref[s] = t
        return 0
    lax.fori_loop(0, n // unroll, body, 0)


def _invert(slot, ends, n_slots):
    smem = pl.BlockSpec(memory_space=pltpu.SMEM)
    return pl.pallas_call(
        _invert_kernel,
        out_shape=[jax.ShapeDtypeStruct((n_slots,), I32)] * 2,
        in_specs=[smem, smem],
        out_specs=[smem, smem],
        name="moe_invert",
    )(slot, ends)


def _ffn_kernel(tea_ref, teb_ref, nu_ref, gsrc_ref, sdst_ref, wa_ref, wb_ref,
                h_hbm, wga, wua, wda, wgb, wub, wdb, y_hbm,
                xbuf0, xbuf1, obuf0, obuf1, wcola, wcolb, gsem, ssem):
    del tea_ref, teb_ref
    i = pl.program_id(0)
    nu = nu_ref[0]
    t = FFN_TILE

    def slab(ref, r):
        return ref.at[pl.ds(pl.multiple_of(r * SLAB, SLAB), SLAB), :]

    def gather_start(tile, xbuf, sem):
        base = tile * t
        for r in range(t):
            pltpu.make_async_copy(slab(h_hbm, gsrc_ref[base + r]), slab(xbuf, r),
                                  sem).start(priority=r % 2)

    def gather_wait(xbuf, sem):
        pltpu.make_async_copy(h_hbm.at[pl.ds(0, t * SLAB), :], xbuf, sem).wait()

    def scatter_start(tile, obuf, sem):
        base = tile * t
        for r in range(t):
            pltpu.make_async_copy(slab(obuf, r), slab(y_hbm, sdst_ref[base + r]),
                                  sem).start(priority=r % 2)

    def scatter_wait(obuf, sem):
        pltpu.make_async_copy(obuf, y_hbm.at[pl.ds(0, t * SLAB), :], sem).wait()

    def expert(xb, wg, wu, wd):
        g = jnp.dot(xb, wg[...], preferred_element_type=F32)
        u = jnp.dot(xb, wu[...], preferred_element_type=F32)
        a = (g * _sigmoid(g)) * u
        return jnp.dot(a.astype(BF16), wd[...], preferred_element_type=F32)

    def step(xc, xn, oc, op, gc, gn, sc, sp):
        pl.when(i == 0)(lambda: gather_start(0, xc, gc))
        pl.when(i >= 2)(lambda: scatter_wait(oc, sc))
        gather_wait(xc, gc)
        pl.when(i + 1 < nu)(lambda: gather_start(i + 1, xn, gn))
        pl.when(i >= 1)(lambda: scatter_start(i - 1, op, sp))
        for r in range(t):
            tok = gsrc_ref[i * t + r]
            wcola[r:r + 1, :] = jnp.full((1, LANES), wa_ref[tok], F32)
            wcolb[r:r + 1, :] = jnp.full((1, LANES), wb_ref[tok], F32)
        xb = _rows_from_slabs(xc, t).astype(BF16)
        ya = expert(xb, wga, wua, wda)
        yb = expert(xb, wgb, wub, wdb)
        y = wcola[:, 0:1] * ya + wcolb[:, 0:1] * yb
        for k, piece in enumerate(_lane_tiles(y)):
            oc[pl.ds(k, t, stride=SLAB), :] = piece

    def drain(oc, op, sc, sp):
        pl.when(i >= 2)(lambda: scatter_wait(oc, sc))
        scatter_start(i - 1, op, sp)
        scatter_wait(op, sp)
        spare_row = y_hbm.shape[0] - t * SLAB
        oc[...] = jnp.zeros(oc.shape, F32)
        spare = pltpu.make_async_copy(
            oc, y_hbm.at[pl.ds(spare_row, t * SLAB), :], sc)
        spare.start()
        spare.wait()

    even = (i % 2) == 0
    odd = jnp.logical_not(even)
    g0, g1, s0, s1 = gsem.at[0], gsem.at[1], ssem.at[0], ssem.at[1]
    pl.when(jnp.logical_and(i < nu, even))(
        lambda: step(xbuf0, xbuf1, obuf0, obuf1, g0, g1, s0, s1))
    pl.when(jnp.logical_and(i < nu, odd))(
        lambda: step(xbuf1, xbuf0, obuf1, obuf0, g1, g0, s1, s0))
    pl.when(jnp.logical_and(i == nu, even))(
        lambda: drain(obuf0, obuf1, s0, s1))
    pl.when(jnp.logical_and(i == nu, odd))(
        lambda: drain(obuf1, obuf0, s1, s0))


def _expert_ffn(tea, teb, n_used, gsrc, sdst, wa_tok, wb_tok, h2, wg, wu, wd,
                layer):
    n_tok = h2.shape[0] // SLAB
    max_tiles = gsrc.shape[0] // FFN_TILE
    wa = lambda a, b: pl.BlockSpec(
        (None, None, a, b), lambda i, ta, tb, *_: (layer, ta[i], 0, 0))
    wb = lambda a, b: pl.BlockSpec(
        (None, None, a, b), lambda i, ta, tb, *_: (layer, tb[i], 0, 0))
    anyspec = pl.BlockSpec(memory_space=pl.ANY)
    return pl.pallas_call(
        _ffn_kernel,
        out_shape=jax.ShapeDtypeStruct(((n_tok + FFN_TILE) * SLAB, LANES), F32),
        grid_spec=pltpu.PrefetchScalarGridSpec(
            num_scalar_prefetch=7,
            grid=(max_tiles + 1,),
            in_specs=[anyspec,
                      wa(D_MODEL, D_EXPERT), wa(D_MODEL, D_EXPERT),
                      wa(D_EXPERT, D_MODEL),
                      wb(D_MODEL, D_EXPERT), wb(D_MODEL, D_EXPERT),
                      wb(D_EXPERT, D_MODEL)],
            out_specs=anyspec,
            scratch_shapes=[pltpu.VMEM((FFN_TILE * SLAB, LANES), F32)] * 4 + [
                            pltpu.VMEM((FFN_TILE, LANES), F32),
                            pltpu.VMEM((FFN_TILE, LANES), F32),
                            pltpu.SemaphoreType.DMA((2,)),
                            pltpu.SemaphoreType.DMA((2,))],
        ),
        compiler_params=_cparams(("arbitrary",)),
        name="moe_ffn",
    )(tea, teb, n_used, gsrc, sdst, wa_tok, wb_tok, h2, wg, wu, wd, wg, wu, wd)


def _moe(h2, ri, rw, cnt, wg_bf, wu_bf, wd_bf, layer):
    n = h2.shape[0] // SLAB
    max_tiles = n // FFN_TILE + N_PAIRS
    n_slots = max_tiles * FFN_TILE
    counts = cnt[:N_PAIRS, 0].astype(I32)
    padded = ((counts + FFN_TILE - 1) // FFN_TILE) * FFN_TILE
    ends = jnp.cumsum(padded)
    offs = ends - padded
    q, rank = ri[0], ri[1]
    cls = jnp.arange(N_PAIRS, dtype=I32)
    slot = rank + jnp.sum(jnp.where(q[None, :] == cls[:, None], offs[:, None], 0),
                          axis=0)
    n_used = ends[-1] // FFN_TILE
    tile = jnp.arange(max_tiles + 1, dtype=I32)
    tq = jnp.sum((tile[:, None] * FFN_TILE >= ends[None, :]).astype(I32), axis=1)
    tq_last = jnp.sum(((n_used - 1) * FFN_TILE >= ends).astype(I32))
    tq = jnp.where(tile < n_used, tq, tq_last)
    pa = jnp.array([a for a, _ in PAIRS], I32)
    pb = jnp.array([b for _, b in PAIRS], I32)
    grp, pidx = tq // len(PAIRS), tq % len(PAIRS)
    tea = EXPERTS_PER_GROUP * grp + jnp.take(pa, pidx)
    teb = EXPERTS_PER_GROUP * grp + jnp.take(pb, pidx)
    gsrc, sdst = _invert(slot, ends, n_slots)
    return _expert_ffn(tea, teb, n_used.reshape(1), gsrc, sdst, rw[0], rw[1], h2,
                       wg_bf, wu_bf, wd_bf, layer)


def _moe_out_tm_kernel(x1_ref, y_ref, g2_ref, op_ref, os_ref, scr, *, n_p_tiles):
    i = pl.program_id(0)
    y = _rows_from_slabs(y_ref, x1_ref.shape[0])
    x2 = x1_ref[...] + _per_row(y, g2_ref[...], lambda a, b: a * b)
    pl.when(i < n_p_tiles)(
        lambda: _from_time_major(x2, op_ref, scr, op_ref.shape[0]))
    pl.when(i >= n_p_tiles)(
        lambda: _from_time_major(x2, os_ref, scr, os_ref.shape[0]))


def _moe_out_tm(x1, y, modpat, shape_p, shape_s):
    n = x1.shape[0]
    tm = TOKEN_TILE
    n_p_tiles = shape_p[0] * shape_p[1] // tm
    p = modpat.shape[1]
    rowblk = pl.BlockSpec((tm, D_MODEL), lambda i: (i, 0))
    stream = _stream_specs(jax.ShapeDtypeStruct(shape_p, F32),
                           jax.ShapeDtypeStruct(shape_s, F32), n_p_tiles)
    return pl.pallas_call(
        functools.partial(_moe_out_tm_kernel, n_p_tiles=n_p_tiles),
        out_shape=[jax.ShapeDtypeStruct(shape_p, F32),
                   jax.ShapeDtypeStruct(shape_s, F32)],
        grid=(n // tm,),
        in_specs=[rowblk,
                  pl.BlockSpec((tm * SLAB, LANES), lambda i: (i, 0)),
                  pl.BlockSpec((None, p, D_MODEL),
                               lambda i: (jnp.where(i < n_p_tiles, 0, 1), 0, 5))],
        out_specs=stream,
        scratch_shapes=[pltpu.VMEM((LANE_TILES, tm, LANES), F32)],
        compiler_params=_cparams(("arbitrary",)),
        name="moe_out_tm",
    )(x1, y, modpat)


def _moe_out_final_kernel(x1_ref, y_ref, g2_ref, nfin_ref, op_ref, os_ref,
                          *, n_p_tiles):
    i = pl.program_id(0)
    y = _rows_from_slabs(y_ref, x1_ref.shape[0])
    x2 = x1_ref[...] + _per_row(y, g2_ref[...], lambda a, b: a * b)
    out = _rms(x2) * nfin_ref[...]

    @pl.when(i < n_p_tiles)
    def _():
        op_ref[...] = out

    @pl.when(i >= n_p_tiles)
    def _():
        os_ref[...] = out


def _moe_out_final(x1, y, modpat, norm_final, n_p, rows_per_request):
    n = x1.shape[0]
    tm = TOKEN_TILE
    n_p_tiles = n_p // tm
    p = modpat.shape[1]
    rowblk = pl.BlockSpec((tm, D_MODEL), lambda i: (i, 0))
    pat = lambda i: jnp.where(
        i < n_p_tiles, 0, 1 + (i - n_p_tiles) // (rows_per_request // tm))
    return pl.pallas_call(
        functools.partial(_moe_out_final_kernel, n_p_tiles=n_p_tiles),
        out_shape=[jax.ShapeDtypeStruct((n_p, D_MODEL), F32),
                   jax.ShapeDtypeStruct((n - n_p, D_MODEL), F32)],
        grid=(n // tm,),
        in_specs=[rowblk,
                  pl.BlockSpec((tm * SLAB, LANES), lambda i: (i, 0)),
                  pl.BlockSpec((None, p, D_MODEL), lambda i: (pat(i), 0, 5)),
                  pl.BlockSpec((1, D_MODEL), lambda i: (0, 0))],
        out_specs=[
            pl.BlockSpec((tm, D_MODEL),
                         lambda i: (jnp.minimum(i, n_p_tiles - 1), 0)),
            pl.BlockSpec((tm, D_MODEL),
                         lambda i: (jnp.maximum(i - n_p_tiles, 0), 0))],
        compiler_params=_cparams(("arbitrary",)),
        name="moe_out_final",
    )(x1, y, modpat, norm_final)


def _block_diag_weights(bb_re, bb_im, c_re, c_im):
    lt = LANE_TILES
    gl = SSM_GROUPS // lt
    eye = jnp.eye(gl, dtype=F32)

    def in_map(bb):
        b5 = bb.reshape(2, lt, gl, STATE_DIM, SSM_GROUP)
        t = jnp.einsum("dinph,kn->dikhnp", b5, eye)
        return t.reshape(2, lt, gl * SSM_GROUP, gl * STATE_DIM)

    def out_map(cc):
        c5 = cc.reshape(2, lt, gl, SSM_GROUP, STATE_DIM)
        t = jnp.einsum("dikhp,kn->dikpnh", c5, eye)
        return t.reshape(2, lt, gl * STATE_DIM, gl * SSM_GROUP)

    bt = jnp.concatenate([in_map(bb_re), in_map(bb_im)], axis=-1).astype(BF16)
    return bt, out_map(c_re).astype(BF16), out_map(c_im).astype(BF16)


def kernel(x_prompt, x_sample, c, state_ssm_re, state_ssm_im, c_ctx, norm_mix, norm_ffn, w_ada, b_ada, ssm_w_in, ssm_lam_re, ssm_lam_im, ssm_log_dt, ssm_b_re, ssm_b_im, ssm_c_re, ssm_c_im, ssm_d, ssm_w_glu, fnet_w_out, router_w, router_b, moe_w_gate, moe_w_up, moe_w_down, norm_final):
    bp, lp, _ = x_prompt.shape
    bs, ls, _ = x_sample.shape
    n_p = bp * lp
    n_s = bs * ls
    n = n_p + n_s
    tm = TOKEN_TILE

    cond = jnp.zeros((N_COND, D_MODEL), F32).at[0].set(c_ctx).at[1:1 + bs].set(c)
    modtab = _ada_table(cond, w_ada, b_ada).reshape(DEPTH, N_COND, N_MOD, D_MODEL)

    rwt_hi = router_w.T.astype(BF16)
    rwt_lo = (router_w.T - rwt_hi.astype(F32)).astype(BF16)
    rwt = jnp.concatenate([rwt_hi, rwt_lo])
    rb = router_b.reshape(N_EXPERTS, 1)
    row = lambda v: v.reshape(1, D_MODEL)

    period = max(bp, bs)
    pat_tm = jnp.stack([
        jnp.broadcast_to(modtab[0, 0], (period, N_MOD, D_MODEL)),
        jnp.tile(modtab[0, 1:1 + bs], (period // bs, 1, 1))])
    pat_tm = pat_tm.reshape(2, period, N_MOD * D_MODEL)

    a_re, a_im, bb_re, bb_im = _zoh(ssm_lam_re[0], ssm_lam_im[0], ssm_log_dt[0],
                                    ssm_b_re[0], ssm_b_im[0])
    bt, ct_re, ct_im = _block_diag_weights(bb_re, bb_im, ssm_c_re[0], ssm_c_im[0])
    a_re = a_re.reshape(2, 1, N_STATE)
    a_im = a_im.reshape(2, 1, N_STATE)

    u = _ssm_in(x_prompt, x_sample, row(norm_mix[0]), pat_tm,
                ssm_w_in[0].astype(BF16))
    h0s_re = state_ssm_re[:, 0].reshape(bs, 2, N_STATE).transpose(1, 0, 2)
    h0s_im = state_ssm_im[:, 0].reshape(bs, 2, N_STATE).transpose(1, 0, 2)
    y_scan, fin_re, fin_im = _ssm_scan(u, bt, ct_re, ct_im, a_re, a_im,
                                       h0s_re, h0s_im, bp=bp, lp=lp, bs=bs, ls=ls)
    tri = jnp.triu(jnp.ones((tm, tm), BF16), k=1)
    x1, h2, ri, rw, cnt = _ssm_out(u, y_scan, row(ssm_d[0]),
                               ssm_w_glu[0].astype(BF16), x_prompt, x_sample,
                               pat_tm, row(norm_ffn[0]), rwt, rb, tri)
    wg_bf, wu_bf, wd_bf = (w.astype(BF16) for w in (moe_w_gate, moe_w_up,
                                                    moe_w_down))
    y_moe = _moe(h2, ri, rw, cnt, wg_bf, wu_bf, wd_bf, 0)
    x2_p, x2_s = _moe_out_tm(x1, y_moe, pat_tm, x_prompt.shape, x_sample.shape)

    pat_bm = jnp.broadcast_to(modtab[1][:1 + bs, None],
                              (1 + bs, SUBLANES, N_MOD, D_MODEL))
    pat_bm = pat_bm.reshape(1 + bs, SUBLANES, N_MOD * D_MODEL)
    x3, h2, ri, rw, cnt = _fnet(x2_p, x2_s, row(norm_mix[1]), pat_bm,
                            fnet_w_out[0].astype(BF16), row(norm_ffn[1]), rwt, rb)
    y_moe = _moe(h2, ri, rw, cnt, wg_bf, wu_bf, wd_bf, 1)
    y_p, y_s = _moe_out_final(x3, y_moe, pat_bm, row(norm_final), n_p, ls)

    st = lambda f: f.transpose(1, 0, 2).reshape(bp, 1, 2, SSM_GROUPS, STATE_DIM)
    return (y_p.reshape(bp, lp, D_MODEL), y_s.reshape(bs, ls, D_MODEL),
            st(fin_re), st(fin_im))
```

```python
import functools

import jax
import jax.numpy as jnp
from jax import lax
from jax.experimental import pallas as pl
from jax.experimental.pallas import tpu as pltpu

F32 = jnp.float32
BF16 = jnp.bfloat16
I32 = jnp.int32

D_MODEL = 1024
DEPTH = 2
SSM_GROUP = 16
SSM_GROUPS = 64
STATE_DIM = 64
N_STATE = SSM_GROUPS * STATE_DIM
FNET_GROUP = 128
N_EXPERTS = 16
N_EXPERT_GROUPS = 4
EXPERTS_PER_GROUP = 4
D_EXPERT = 1024
N_MOD = 6
EPS = 1e-6

LANES = 128
SUBLANES = 8
LANE_TILES = D_MODEL // LANES
VMEM_LIMIT = 56 * 1024 * 1024

TOKEN_TILE = 512
FNET_TILE = 256
FFN_TILE = 256
N_COND = 16

PAIRS = ((0, 1), (0, 2), (0, 3), (1, 3), (1, 2), (3, 2))
N_PAIRS = N_EXPERT_GROUPS * len(PAIRS)
N_CLASS_ROWS = 32
SLAB = SUBLANES


def _cparams(sem, vmem=VMEM_LIMIT):
    return pltpu.CompilerParams(dimension_semantics=sem, vmem_limit_bytes=vmem)


def _sigmoid(x):
    return 1.0 / (1.0 + jnp.exp(-x))


def _gelu_tanh(x):
    c = 0.7978845608028654
    return x * (0.5 * (1.0 + jnp.tanh(c * (x + 0.044715 * (x * x * x)))))


def _per_row(v, pat, fn):
    tm, d = v.shape
    p = pat.shape[0]
    return fn(v.reshape(tm // p, p, d), pat[None]).reshape(tm, d)


def _rms(x):
    ms = jnp.mean(x * x, axis=-1, keepdims=True)
    return x * lax.rsqrt(ms + EPS)


def _norm_mod(x, g, sc, sh):
    y = _rms(x) * g
    y = _per_row(y, sc, lambda a, b: a * (1.0 + b))
    return _per_row(y, sh, lambda a, b: a + b)


def _lane_tiles(v):
    return [v[:, LANES * k:LANES * (k + 1)] for k in range(v.shape[1] // LANES)]


def _to_time_major(x_ref, scr, batch):
    tt = x_ref.shape[1]
    for b in range(batch):
        for k, piece in enumerate(_lane_tiles(x_ref[b])):
            scr[k, pl.ds(b, tt, stride=batch), :] = piece
    return jnp.concatenate([scr[k] for k in range(LANE_TILES)], axis=1)


def _from_time_major(v, o_ref, scr, batch):
    tt = v.shape[0] // batch
    for k, piece in enumerate(_lane_tiles(v)):
        scr[k] = piece
    for b in range(batch):
        o_ref[b] = jnp.concatenate(
            [scr[k, pl.ds(b, tt, stride=batch), :] for k in range(LANE_TILES)],
            axis=1)


def _slab_rows(ref, k, n):
    return ref[pl.ds(k, n, stride=SLAB), :]


def _store_slabs(v, ref):
    n = v.shape[0]
    for k, piece in enumerate(_lane_tiles(v)):
        ref[pl.ds(k, n, stride=SLAB), :] = piece


def _rows_from_slabs(y_ref, n):
    return jnp.concatenate([_slab_rows(y_ref, k, n) for k in range(SLAB)], axis=1)


def _route(logits_t, rb, tri, carry):
    ne, tm = logits_t.shape
    s = _sigmoid(logits_t)
    bz = s + rb
    row = lambda a, r: a[r:r + 1, :]
    gs = []
    for g in range(N_EXPERT_GROUPS):
        v0, v1, v2, v3 = (row(bz, EXPERTS_PER_GROUP * g + j) for j in range(4))
        hi1, lo1 = jnp.maximum(v0, v1), jnp.minimum(v0, v1)
        hi2, lo2 = jnp.maximum(v2, v3), jnp.minimum(v2, v3)
        top1 = jnp.maximum(hi1, hi2)
        top2 = jnp.maximum(jnp.minimum(hi1, hi2), jnp.maximum(lo1, lo2))
        gs.append(top1 + top2)
    bg = jnp.zeros((1, tm), I32)
    bv = gs[0]
    for g in range(1, N_EXPERT_GROUPS):
        upd = gs[g] > bv
        bg = jnp.where(upd, g, bg)
        bv = jnp.where(upd, gs[g], bv)
    cb, cs = [], []
    for j in range(EXPERTS_PER_GROUP):
        vb, vs = row(bz, j), row(s, j)
        for g in range(1, N_EXPERT_GROUPS):
            sel = bg == g
            vb = jnp.where(sel, row(bz, EXPERTS_PER_GROUP * g + j), vb)
            vs = jnp.where(sel, row(s, EXPERTS_PER_GROUP * g + j), vs)
        cb.append(vb)
        cs.append(vs)
    i1 = jnp.zeros((1, tm), I32)
    b1, s1 = cb[0], cs[0]
    for j in range(1, EXPERTS_PER_GROUP):
        upd = cb[j] > b1
        i1 = jnp.where(upd, j, i1)
        b1 = jnp.where(upd, cb[j], b1)
        s1 = jnp.where(upd, cs[j], s1)
    i2 = jnp.zeros((1, tm), I32)
    b2 = jnp.full((1, tm), -jnp.inf, F32)
    s2 = jnp.zeros((1, tm), F32)
    for j in range(EXPERTS_PER_GROUP):
        cand = jnp.where(i1 == j, -jnp.inf, cb[j])
        upd = cand > b2
        i2 = jnp.where(upd, j, i2)
        b2 = jnp.where(upd, cand, b2)
        s2 = jnp.where(upd, cs[j], s2)
    den = s1 + s2
    w1 = s1 / den
    w2 = s2 / den
    lo = jnp.minimum(i1, i2)
    hi = jnp.maximum(i1, i2)
    pidx = jnp.where(lo == 0, hi - 1,
                     jnp.where(lo == 1, jnp.where(hi == 3, 3, 4), 5))
    first = jnp.where(pidx < 3, 0, jnp.where(pidx < 5, 1, 3))
    wa = jnp.where(i1 == first, w1, w2)
    wb = jnp.where(i1 == first, w2, w1)
    q = len(PAIRS) * bg + pidx
    qio = lax.broadcasted_iota(I32, (N_CLASS_ROWS, tm), 0)
    oh = qio == q
    ohf = jnp.where(oh, 1.0, 0.0)
    cum = jnp.dot(ohf.astype(BF16), tri, preferred_element_type=F32) + carry
    rank = jnp.sum(jnp.where(oh, cum, 0.0), axis=0, keepdims=True)
    new_carry = carry + jnp.sum(ohf, axis=1, keepdims=True)
    return q, rank.astype(I32), wa, wb, new_carry


def _post_mixer(m, x, g1, nf, sc2, sh2, rwt_ref, rb_ref, tri_ref,
                x1_ref, h2_ref, ri_ref, rw_ref, cnt_ref, carry, valid=None):
    tm = x.shape[0]
    x1 = x + _per_row(m, g1, lambda a, b: a * b)
    x1_ref[...] = x1
    h2 = _norm_mod(x1, nf, sc2, sh2)
    h_hi = h2.astype(BF16)
    h_lo = (h2 - h_hi.astype(F32)).astype(BF16)
    nt = (((1,), (1,)), ((), ()))
    la = lax.dot_general(rwt_ref[...], h_hi, nt, preferred_element_type=F32)
    lb = lax.dot_general(rwt_ref[0:N_EXPERTS, :], h_lo, nt,
                         preferred_element_type=F32)
    logits_t = la[:N_EXPERTS] + la[N_EXPERTS:] + lb
    q, rank, wa, wb, nc = _route(logits_t, rb_ref[...], tri_ref[...],
                                 carry[:, 0:1])
    _store_slabs(h2, h2_ref)
    rw_ref[0:1, :] = wa
    rw_ref[1:2, :] = wb
    rw_ref[2:8, :] = jnp.zeros((6, tm), F32)
    ri_ref[0:1, :] = q
    ri_ref[1:2, :] = rank
    ri_ref[2:8, :] = jnp.zeros((6, tm), I32)
    if valid is not None:
        nc = jnp.where(valid, nc, carry[:, 0:1])
    carry[...] = jnp.broadcast_to(nc, carry.shape)
    cnt_ref[...] = carry[...]


def _ada_kernel(cond_ref, w_ref, b_ref, o_ref):
    c = cond_ref[...]
    s = c * _sigmoid(c)
    s_hi = s.astype(BF16)
    s_lo = (s - s_hi.astype(F32)).astype(BF16)
    w = w_ref[...]
    w_hi = w.astype(BF16)
    w_lo = (w - w_hi.astype(F32)).astype(BF16)
    dot = functools.partial(jnp.dot, preferred_element_type=F32)
    o_ref[...] = dot(s_hi, w_hi) + dot(s_lo, w_hi) + dot(s_hi, w_lo) + b_ref[...]


def _ada_table(cond, w_ada, b_ada):
    tn = 1536
    n_out = N_MOD * D_MODEL
    return pl.pallas_call(
        _ada_kernel,
        out_shape=jax.ShapeDtypeStruct((DEPTH, N_COND, n_out), F32),
        grid=(DEPTH, n_out // tn),
        in_specs=[
            pl.BlockSpec((N_COND, D_MODEL), lambda l, j: (0, 0)),
            pl.BlockSpec((None, D_MODEL, tn), lambda l, j: (l, 0, j)),
            pl.BlockSpec((None, 1, tn), lambda l, j: (l, 0, j)),
        ],
        out_specs=pl.BlockSpec((None, N_COND, tn), lambda l, j: (l, 0, j)),
        compiler_params=_cparams(("arbitrary", "arbitrary")),
        name="ada_table",
    )(cond, w_ada, b_ada.reshape(DEPTH, 1, n_out))


def _zoh_kernel(lr_ref, li_ref, ldt_ref, br_ref, bi_ref,
                are_ref, aim_ref, bbre_ref, bbim_ref):
    lr = lr_ref[...]
    li = li_ref[...]
    dt = jnp.exp(ldt_ref[...])
    mag = jnp.exp(lr * dt)
    a_re = mag * jnp.cos(li * dt)
    a_im = mag * jnp.sin(li * dt)
    den = lr * lr + li * li
    nr = a_re - 1.0
    f_re = (nr * lr + a_im * li) / den
    f_im = (a_im * lr - nr * li) / den
    br = br_ref[...]
    bi = bi_ref[...]
    are_ref[...] = a_re
    aim_ref[...] = a_im
    bbre_ref[...] = f_re * br - f_im * bi
    bbim_ref[...] = f_re * bi + f_im * br


def _zoh(lam_re, lam_im, log_dt, b_re, b_im):
    shape = b_re.shape
    flat = (shape[0] * shape[1] * shape[2] * shape[3] // LANES, LANES)
    bc = lambda a: jnp.broadcast_to(a, shape).reshape(flat)
    args = (bc(lam_re[..., None]), bc(lam_im[..., None]),
            bc(log_dt[:, :, None, None]), b_re.reshape(flat), b_im.reshape(flat))
    outs = pl.pallas_call(
        _zoh_kernel,
        out_shape=[jax.ShapeDtypeStruct(flat, F32)] * 4,
        name="zoh_discretize",
    )(*args)
    a_re, a_im, bb_re, bb_im = (o.reshape(shape) for o in outs)
    return a_re[..., 0], a_im[..., 0], bb_re, bb_im


def _stream_specs(xp, xs, n_p_tiles):
    bp, bs = xp.shape[0], xs.shape[0]
    return [
        pl.BlockSpec((bp, TOKEN_TILE // bp, D_MODEL),
                     lambda i: (0, jnp.minimum(i, n_p_tiles - 1), 0)),
        pl.BlockSpec((bs, TOKEN_TILE // bs, D_MODEL),
                     lambda i: (0, jnp.maximum(i - n_p_tiles, 0), 0)),
    ]


def _ssm_in_kernel(xp_ref, xs_ref, g_ref, sc_ref, sh_ref, w_ref, u_ref, scr,
                   *, n_p_tiles):
    def run(x_ref):
        x = _to_time_major(x_ref, scr, x_ref.shape[0])
        h = _norm_mod(x, g_ref[...], sc_ref[...], sh_ref[...])
        u_ref[...] = jnp.dot(h.astype(BF16), w_ref[...],
                             preferred_element_type=F32)

    i = pl.program_id(0)
    pl.when(i < n_p_tiles)(lambda: run(xp_ref))
    pl.when(i >= n_p_tiles)(lambda: run(xs_ref))


def _ssm_in(xp, xs, g, modpat, w_in_bf):
    n = (xp.shape[0] * xp.shape[1] + xs.shape[0] * xs.shape[1])
    tm = TOKEN_TILE
    n_p_tiles = xp.shape[0] * xp.shape[1] // tm
    p = modpat.shape[1]
    pat = lambda i: jnp.where(i < n_p_tiles, 0, 1)
    mod = lambda k: pl.BlockSpec((None, p, D_MODEL), lambda i: (pat(i), 0, k))
    return pl.pallas_call(
        functools.partial(_ssm_in_kernel, n_p_tiles=n_p_tiles),
        out_shape=jax.ShapeDtypeStruct((n, D_MODEL), F32),
        grid=(n // tm,),
        in_specs=_stream_specs(xp, xs, n_p_tiles) + [
            pl.BlockSpec((1, D_MODEL), lambda i: (0, 0)),
            mod(1), mod(0),
            pl.BlockSpec((D_MODEL, D_MODEL), lambda i: (0, 0)),
        ],
        out_specs=pl.BlockSpec((tm, D_MODEL), lambda i: (i, 0)),
        scratch_shapes=[pltpu.VMEM((LANE_TILES, tm, LANES), F32)],
        compiler_params=_cparams(("arbitrary",)),
        name="ssm_in",
    )(xp, xs, g, modpat, modpat, w_in_bf)


def _scan_kernel(u_ref, bt_ref, cre_ref, cim_ref, are_ref, aim_ref,
                 h0re_ref, h0im_ref, y_ref, fre_ref, fim_ref,
                 *scratch, n_p_chunks, bp, bs):
    xre = scratch[:LANE_TILES]
    xim = scratch[LANE_TILES:2 * LANE_TILES]
    hre, him = scratch[2 * LANE_TILES:]
    d = pl.program_id(0)
    c = pl.program_id(1)
    cols = N_STATE // LANE_TILES
    rows = u_ref.shape[0]

    def chunk(batch, reverse):
        steps = rows // batch

        def b_proj(s):
            u = u_ref[:, LANES * s:LANES * (s + 1)].astype(BF16)
            xt = jnp.dot(u, bt_ref[s], preferred_element_type=F32)
            xre[s][...] = xt[:, :cols]
            xim[s][...] = xt[:, cols:]

        def scan(s):
            sl = slice(cols * s, cols * (s + 1))
            ar = jnp.broadcast_to(are_ref[:, sl], (batch, cols))
            ai = jnp.broadcast_to(aim_ref[:, sl], (batch, cols))
            hr = hre[0:batch, sl]
            hi = him[0:batch, sl]
            for t in range(steps):
                tt = steps - 1 - t if reverse else t
                r = slice(tt * batch, (tt + 1) * batch)
                nr = ar * hr - ai * hi + xre[s][r, :]
                ni = ar * hi + ai * hr + xim[s][r, :]
                xre[s][r, :] = nr
                xim[s][r, :] = ni
                hr, hi = nr, ni
            hre[0:batch, sl] = hr
            him[0:batch, sl] = hi

        def c_proj(s):
            y_ref[:, LANES * s:LANES * (s + 1)] = (
                jnp.dot(xre[s][...].astype(BF16), cre_ref[s],
                        preferred_element_type=F32)
                - jnp.dot(xim[s][...].astype(BF16), cim_ref[s],
                          preferred_element_type=F32))

        for stage in range(LANE_TILES + 2):
            if stage < LANE_TILES:
                b_proj(stage)
            if 0 <= stage - 1 < LANE_TILES:
                scan(stage - 1)
            if 0 <= stage - 2 < LANE_TILES:
                c_proj(stage - 2)

    @pl.when(c == 0)
    def _():
        hre[...] = jnp.zeros(hre.shape, F32)
        him[...] = jnp.zeros(him.shape, F32)

    @pl.when(c == n_p_chunks)
    def _():
        hre[0:bs, :] = h0re_ref[...]
        him[0:bs, :] = h0im_ref[...]

    in_p = c < n_p_chunks
    in_s = jnp.logical_not(in_p)
    fwd = d == 0
    bwd = jnp.logical_not(fwd)
    pl.when(jnp.logical_and(in_p, fwd))(lambda: chunk(bp, False))
    pl.when(jnp.logical_and(in_p, bwd))(lambda: chunk(bp, True))
    pl.when(jnp.logical_and(in_s, fwd))(lambda: chunk(bs, False))
    pl.when(jnp.logical_and(in_s, bwd))(lambda: chunk(bs, True))

    @pl.when(c == n_p_chunks - 1)
    def _():
        fre_ref[...] = hre[0:bp, :]
        fim_ref[...] = him[0:bp, :]


def _ssm_scan(u, bt, ct_re, ct_im, a_re, a_im, h0_re, h0_im, *, bp, lp, bs, ls):
    n = u.shape[0]
    rows = TOKEN_TILE
    n_p_chunks = bp * lp // rows
    n_s_chunks = bs * ls // rows

    def chunk(d, c):
        in_p = jnp.where(d == 0, c, n_p_chunks - 1 - c)
        cs = c - n_p_chunks
        in_s = n_p_chunks + jnp.where(d == 0, cs, n_s_chunks - 1 - cs)
        return jnp.where(c < n_p_chunks, in_p, in_s)

    cols = N_STATE // LANE_TILES
    dspec = lambda shape: pl.BlockSpec((None,) + shape,
                                       lambda d, c: (d,) + (0,) * len(shape))
    kern = functools.partial(_scan_kernel, n_p_chunks=n_p_chunks, bp=bp, bs=bs)
    return pl.pallas_call(
        kern,
        out_shape=[jax.ShapeDtypeStruct((2, n, D_MODEL), F32),
                   jax.ShapeDtypeStruct((2, bp, N_STATE), F32),
                   jax.ShapeDtypeStruct((2, bp, N_STATE), F32)],
        grid=(2, n_p_chunks + n_s_chunks),
        in_specs=[
            pl.BlockSpec((rows, D_MODEL), lambda d, c: (chunk(d, c), 0)),
            dspec((LANE_TILES, LANES, 2 * cols)),
            dspec((LANE_TILES, cols, LANES)),
            dspec((LANE_TILES, cols, LANES)),
            dspec((1, N_STATE)),
            dspec((1, N_STATE)),
            dspec((bs, N_STATE)),
            dspec((bs, N_STATE)),
        ],
        out_specs=[
            pl.BlockSpec((None, rows, D_MODEL), lambda d, c: (d, chunk(d, c), 0)),
            dspec((bp, N_STATE)),
            dspec((bp, N_STATE)),
        ],
        scratch_shapes=(
            [pltpu.VMEM((rows, cols), F32)] * (2 * LANE_TILES)
            + [pltpu.VMEM((max(bp, bs), N_STATE), F32)] * 2),
        compiler_params=_cparams(("arbitrary", "arbitrary")),
        name="ssm_scan",
    )(u, bt, ct_re, ct_im, a_re, a_im, h0_re, h0_im)


def _ssm_out_kernel(u_ref, y0_ref, y1_ref, dsk_ref, wglu_ref, xp_ref, xs_ref,
                    g1_ref, nf_ref, sc2_ref, sh2_ref, rwt_ref, rb_ref, tri_ref,
                    x1_ref, h2_ref, ri_ref, rw_ref, cnt_ref, scr, carry,
                    *, n_p_tiles):
    i = pl.program_id(0)

    @pl.when(i == 0)
    def _():
        carry[...] = jnp.zeros(carry.shape, F32)

    def run(x_ref):
        x = _to_time_major(x_ref, scr, x_ref.shape[0])
        y = u_ref[...] * dsk_ref[...] + y0_ref[...] + y1_ref[...]
        ge = _gelu_tanh(y).astype(BF16)
        vg = jnp.dot(ge, wglu_ref[...], preferred_element_type=F32)
        m = vg[:, :D_MODEL] * _sigmoid(vg[:, D_MODEL:])
        _post_mixer(m, x, g1_ref[...], nf_ref[...], sc2_ref[...], sh2_ref[...],
                    rwt_ref, rb_ref, tri_ref, x1_ref, h2_ref, ri_ref, rw_ref,
                    cnt_ref, carry)

    pl.when(i < n_p_tiles)(lambda: run(xp_ref))
    pl.when(i >= n_p_tiles)(lambda: run(xs_ref))


def _route_out_shapes(n):
    return [jax.ShapeDtypeStruct((n, D_MODEL), F32),
            jax.ShapeDtypeStruct((n * SLAB, LANES), F32),
            jax.ShapeDtypeStruct((SUBLANES, n), I32),
            jax.ShapeDtypeStruct((SUBLANES, n), F32),
            jax.ShapeDtypeStruct((N_CLASS_ROWS, LANES), F32)]


def _ssm_out(u, y, d_skip, w_glu_bf, xp, xs, modpat, norm_ffn, rwt, rb, tri):
    n = u.shape[0]
    tm = TOKEN_TILE
    n_p_tiles = xp.shape[0] * xp.shape[1] // tm
    p = modpat.shape[1]
    pat = lambda i: jnp.where(i < n_p_tiles, 0, 1)
    mod = lambda k: pl.BlockSpec((None, p, D_MODEL), lambda i: (pat(i), 0, k))
    full = lambda shape: pl.BlockSpec(shape, lambda i: (0,) * len(shape))
    rowblk = pl.BlockSpec((tm, D_MODEL), lambda i: (i, 0))
    return pl.pallas_call(
        functools.partial(_ssm_out_kernel, n_p_tiles=n_p_tiles),
        out_shape=_route_out_shapes(n),
        grid=(n // tm,),
        in_specs=[
            rowblk,
            pl.BlockSpec((None, tm, D_MODEL), lambda i: (0, i, 0)),
            pl.BlockSpec((None, tm, D_MODEL), lambda i: (1, i, 0)),
            full((1, D_MODEL)),
            full((D_MODEL, 2 * D_MODEL)),
        ] + _stream_specs(xp, xs, n_p_tiles) + [
            mod(2), full((1, D_MODEL)), mod(4), mod(3),
            full((2 * N_EXPERTS, D_MODEL)), full((N_EXPERTS, 1)), full((tm, tm)),
        ],
        out_specs=[rowblk,
                   pl.BlockSpec((tm * SLAB, LANES), lambda i: (i, 0)),
                   pl.BlockSpec((SUBLANES, tm), lambda i: (0, i)),
                   pl.BlockSpec((SUBLANES, tm), lambda i: (0, i)),
                   full((N_CLASS_ROWS, LANES))],
        scratch_shapes=[pltpu.VMEM((LANE_TILES, tm, LANES), F32),
                        pltpu.VMEM((N_CLASS_ROWS, LANES), F32)],
        compiler_params=_cparams(("arbitrary",)),
        name="ssm_out",
    )(u, y, y, d_skip, w_glu_bf, xp, xs, modpat, norm_ffn, modpat, modpat,
      rwt, rb, tri)


def _fnet_kernel(xp_ref, xpp_ref, xs_ref, nm_ref, sc1_ref, sh1_ref, cs_ref,
                 dftp_ref, dfts_ref, wout_ref, g1_ref, nf_ref, sc2_ref, sh2_ref,
                 rwt_ref, rb_ref, tri_ref, x1_ref, h2_ref, ri_ref, rw_ref, cnt_ref,
                 ucs, ybuf, carry, *, bp, tiles_s):
    s = pl.program_id(0)
    n_groups = D_MODEL // FNET_GROUP
    tr = FNET_TILE
    per = tiles_s + 1
    wslot = s % 2

    @pl.when(s == 0)
    def _():
        carry[...] = jnp.zeros(carry.shape, F32)
        ybuf[...] = jnp.zeros(ybuf.shape, F32)

    def stage1(x_ref, seq):
        def body(r, _):
            rows = pl.ds(pl.multiple_of(r * tr, tr), tr)
            h = _norm_mod(x_ref[rows, :], nm_ref[...], sc1_ref[...],
                          sh1_ref[...]).astype(BF16)
            for k in range(n_groups):
                cols = slice(FNET_GROUP * k, FNET_GROUP * (k + 1))
                t = jnp.dot(h[:, cols], cs_ref[...], preferred_element_type=F32)
                ucs[rows, cols] = t[:, :FNET_GROUP].astype(BF16)
                ucs[pl.ds(pl.multiple_of(seq + r * tr, tr), tr), cols] = (
                    t[:, FNET_GROUP:].astype(BF16))
            return 0
        lax.fori_loop(0, seq // tr, body, 0)

    def step(dft_ref, seq, xrow, valid):
        scale = float((seq * FNET_GROUP) ** -0.5)
        m = jnp.dot(ybuf[1 - wslot].astype(BF16), wout_ref[...],
                    preferred_element_type=F32)
        _post_mixer(m, xrow, g1_ref[...], nf_ref[...], sc2_ref[...],
                    sh2_ref[...], rwt_ref, rb_ref, tri_ref, x1_ref, h2_ref,
                    ri_ref, rw_ref, cnt_ref, carry, valid)
        ybuf[wslot] = jnp.dot(dft_ref[...], ucs[0:2 * seq, :],
                              preferred_element_type=F32) * scale

    @pl.when(s <= bp)
    def _():
        stage1(xp_ref, xp_ref.shape[0])
        step(dftp_ref, xp_ref.shape[0], xpp_ref[...], s >= 1)

    @pl.when(s > bp)
    def _():
        j = (s - bp - 1) % per
        pl.when(j == 0)(lambda: stage1(xs_ref, xs_ref.shape[0]))
        prev = jnp.maximum(j - 1, 0)
        xrow = xs_ref[pl.ds(pl.multiple_of(prev * tr, tr), tr), :]
        step(dfts_ref, xs_ref.shape[0], xrow, j >= 1)


def _dft_table(seq):
    inner = 64
    k = jnp.arange(seq, dtype=I32)[:, None]
    t1 = jnp.arange(seq // inner, dtype=I32)[None, :] * inner
    t2 = jnp.arange(inner, dtype=I32)[None, :]
    ang = lambda t: ((k * t) % seq).astype(F32) * (2.0 * jnp.pi / seq)
    ca, sa = jnp.cos(ang(t1))[:, :, None], jnp.sin(ang(t1))[:, :, None]
    cb, sb = jnp.cos(ang(t2))[:, None, :], jnp.sin(ang(t2))[:, None, :]
    cos = (ca * cb - sa * sb).reshape(seq, seq)
    sin = (sa * cb + ca * sb).reshape(seq, seq)
    return jnp.concatenate([cos, -sin], axis=1).astype(BF16)


def _fnet(xp, xs, norm_mix, modpat, w_out_bf, norm_ffn, rwt, rb):
    bp, lp, _ = xp.shape
    bs, ls, _ = xs.shape
    tr = FNET_TILE
    assert lp == tr
    tiles_s = ls // tr
    per = tiles_s + 1
    n = bp * lp + bs * ls
    p = modpat.shape[1]
    kc = jnp.arange(FNET_GROUP, dtype=I32)
    angc = ((kc[:, None] * kc[None, :]) % FNET_GROUP).astype(F32) * (
        2.0 * jnp.pi / FNET_GROUP)
    cs = jnp.concatenate([jnp.cos(angc), jnp.sin(angc)], axis=1).astype(BF16)
    tri = jnp.triu(jnp.ones((tr, tr), BF16), k=1)

    q = lambda s: jnp.maximum(s - bp - 1, 0)
    sb = lambda s: jnp.minimum(q(s) // per, bs - 1)
    sj = lambda s: q(s) % per
    pat = lambda s: jnp.where(s <= bp, 0, 1 + sb(s))
    blk = lambda s: jnp.where(s <= bp, jnp.maximum(s - 1, 0),
                              bp + sb(s) * tiles_s + jnp.maximum(sj(s) - 1, 0))
    mod = lambda kk: pl.BlockSpec((None, p, D_MODEL), lambda s: (pat(s), 0, kk))
    full = lambda shape: pl.BlockSpec(shape, lambda s: (0,) * len(shape))
    kern = functools.partial(_fnet_kernel, bp=bp, tiles_s=tiles_s)
    return pl.pallas_call(
        kern,
        out_shape=_route_out_shapes(n),
        grid=(bp + 1 + bs * per,),
        in_specs=[
            pl.BlockSpec((None, lp, D_MODEL),
                         lambda s: (jnp.minimum(s, bp - 1), 0, 0)),
            pl.BlockSpec((None, lp, D_MODEL),
                         lambda s: (jnp.clip(s - 1, 0, bp - 1), 0, 0)),
            pl.BlockSpec((None, ls, D_MODEL), lambda s: (sb(s), 0, 0)),
            full((1, D_MODEL)), mod(1), mod(0),
            full((FNET_GROUP, 2 * FNET_GROUP)),
            full((tr, 2 * lp)),
            pl.BlockSpec((tr, 2 * ls),
                         lambda s: (jnp.minimum(sj(s), tiles_s - 1), 0)),
            full((D_MODEL, D_MODEL)),
            mod(2), full((1, D_MODEL)), mod(4), mod(3),
            full((2 * N_EXPERTS, D_MODEL)), full((N_EXPERTS, 1)), full((tr, tr)),
        ],
        out_specs=[pl.BlockSpec((tr, D_MODEL), lambda s: (blk(s), 0)),
                   pl.BlockSpec((tr * SLAB, LANES), lambda s: (blk(s), 0)),
                   pl.BlockSpec((SUBLANES, tr), lambda s: (0, blk(s))),
                   pl.BlockSpec((SUBLANES, tr), lambda s: (0, blk(s))),
                   full((N_CLASS_ROWS, LANES))],
        scratch_shapes=[pltpu.VMEM((2 * ls, D_MODEL), BF16),
                        pltpu.VMEM((2, tr, D_MODEL), F32),
                        pltpu.VMEM((N_CLASS_ROWS, LANES), F32)],
        compiler_params=_cparams(("arbitrary",)),
        name="fnet",
    )(xp, xp, xs, norm_mix, modpat, modpat, cs, _dft_table(lp), _dft_table(ls),
      w_out_bf, modpat, norm_ffn, modpat, modpat, rwt, rb, tri)


def _invert_kernel(slot_ref, ends_ref, gsrc_ref, sdst_ref):
    n = slot_ref.shape[0]
    n_slots = gsrc_ref.shape[0]
    t = FFN_TILE
    unroll = 8

    def pad_tile(base):
        def pad(j, _):
            for k in range(unroll):
                r = j * unroll + k
                gsrc_ref[base + r] = 0
                sdst_ref[base + r] = n + r
            return 0
        lax.fori_loop(0, t // unroll, pad, 0)

    for q in range(N_PAIRS):
        pad_tile(jnp.maximum(ends_ref[q] - t, 0))

    def unused(b, _):
        pad_tile(b * t)
        return 0
    lax.fori_loop(ends_ref[N_PAIRS - 1] // t, n_slots // t, unused, 0)

    def body(j, _):
        for k in range(unroll):
            t = j * unroll + k
            s = slot_ref[t]
            gsrc_ref[s] = t
            sdst_ref[s] = t
        return 0
    lax.fori_loop(0, n // unroll, body, 0)


def _invert(slot, ends, n_slots):
    smem = pl.BlockSpec(memory_space=pltpu.SMEM)
    return pl.pallas_call(
        _invert_kernel,
        out_shape=[jax.ShapeDtypeStruct((n_slots,), I32)] * 2,
        in_specs=[smem, smem],
        out_specs=[smem, smem],
        name="moe_invert",
    )(slot, ends)


def _ffn_kernel(tea_ref, teb_ref, nu_ref, gsrc_ref, sdst_ref, wa_ref, wb_ref,
                h_hbm, wga, wua, wda, wgb, wub, wdb, y_hbm,
                xbuf0, xbuf1, obuf0, obuf1, wcola, wcolb, gsem, ssem):
    del tea_ref, teb_ref
    i = pl.program_id(0)
    nu = nu_ref[0]
    t = FFN_TILE

    def slab(ref, r):
        return ref.at[pl.ds(pl.multiple_of(r * SLAB, SLAB), SLAB), :]

    def gather_start(tile, xbuf, sem):
        base = tile * t
        for r in range(t):
            pltpu.make_async_copy(slab(h_hbm, gsrc_ref[base + r]), slab(xbuf, r),
                                  sem).start(priority=r % 2)

    def gather_wait(xbuf, sem):
        pltpu.make_async_copy(h_hbm.at[pl.ds(0, t * SLAB), :], xbuf, sem).wait()

    def scatter_start(tile, obuf, sem):
        base = tile * t
        for r in range(t):
            pltpu.make_async_copy(slab(obuf, r), slab(y_hbm, sdst_ref[base + r]),
                                  sem).start(priority=r % 2)

    def scatter_wait(obuf, sem):
        pltpu.make_async_copy(obuf, y_hbm.at[pl.ds(0, t * SLAB), :], sem).wait()

    def expert(xb, wg, wu, wd):
        g = jnp.dot(xb, wg[...], preferred_element_type=F32)
        u = jnp.dot(xb, wu[...], preferred_element_type=F32)
        a = (g * _sigmoid(g)) * u
        return jnp.dot(a.astype(BF16), wd[...], preferred_element_type=F32)

    def step(xc, xn, oc, op, gc, gn, sc, sp):
        pl.when(i == 0)(lambda: gather_start(0, xc, gc))
        pl.when(i >= 2)(lambda: scatter_wait(oc, sc))
        gather_wait(xc, gc)
        pl.when(i + 1 < nu)(lambda: gather_start(i + 1, xn, gn))
        pl.when(i >= 1)(lambda: scatter_start(i - 1, op, sp))
        for r in range(t):
            tok = gsrc_ref[i * t + r]
            wcola[r:r + 1, :] = jnp.full((1, LANES), wa_ref[tok], F32)
            wcolb[r:r + 1, :] = jnp.full((1, LANES), wb_ref[tok], F32)
        xb = _rows_from_slabs(xc, t).astype(BF16)
        ya = expert(xb, wga, wua, wda)
        yb = expert(xb, wgb, wub, wdb)
        y = wcola[:, 0:1] * ya + wcolb[:, 0:1] * yb
        for k, piece in enumerate(_lane_tiles(y)):
            oc[pl.ds(k, t, stride=SLAB), :] = piece

    def drain(oc, op, sc, sp):
        pl.when(i >= 2)(lambda: scatter_wait(oc, sc))
        scatter_start(i - 1, op, sp)
        scatter_wait(op, sp)
        spare_row = y_hbm.shape[0] - t * SLAB
        oc[...] = jnp.zeros(oc.shape, F32)
        spare = pltpu.make_async_copy(
            oc, y_hbm.at[pl.ds(spare_row, t * SLAB), :], sc)
        spare.start()
        spare.wait()

    even = (i % 2) == 0
    odd = jnp.logical_not(even)
    g0, g1, s0, s1 = gsem.at[0], gsem.at[1], ssem.at[0], ssem.at[1]
    pl.when(jnp.logical_and(i < nu, even))(
        lambda: step(xbuf0, xbuf1, obuf0, obuf1, g0, g1, s0, s1))
    pl.when(jnp.logical_and(i < nu, odd))(
        lambda: step(xbuf1, xbuf0, obuf1, obuf0, g1, g0, s1, s0))
    pl.when(jnp.logical_and(i == nu, even))(
        lambda: drain(obuf0, obuf1, s0, s1))
    pl.when(jnp.logical_and(i == nu, odd))(
        lambda: drain(obuf1, obuf0, s1, s0))


def _expert_ffn(tea, teb, n_used, gsrc, sdst, wa_tok, wb_tok, h2, wg, wu, wd,
                layer):
    n_tok = h2.shape[0] // SLAB
    max_tiles = gsrc.shape[0] // FFN_TILE
    wa = lambda a, b: pl.BlockSpec(
        (None, None, a, b), lambda i, ta, tb, *_: (layer, ta[i], 0, 0))
    wb = lambda a, b: pl.BlockSpec(
        (None, None, a, b), lambda i, ta, tb, *_: (layer, tb[i], 0, 0))
    anyspec = pl.BlockSpec(memory_space=pl.ANY)
    return pl.pallas_call(
        _ffn_kernel,
        out_shape=jax.ShapeDtypeStruct(((n_tok + FFN_TILE) * SLAB, LANES), F32),
        grid_spec=pltpu.PrefetchScalarGridSpec(
            num_scalar_prefetch=7,
            grid=(max_tiles + 1,),
            in_specs=[anyspec,
                      wa(D_MODEL, D_EXPERT), wa(D_MODEL, D_EXPERT),
                      wa(D_EXPERT, D_MODEL),
                      wb(D_MODEL, D_EXPERT), wb(D_MODEL, D_EXPERT),
                      wb(D_EXPERT, D_MODEL)],
            out_specs=anyspec,
            scratch_shapes=[pltpu.VMEM((FFN_TILE * SLAB, LANES), F32)] * 4 + [
                            pltpu.VMEM((FFN_TILE, LANES), F32),
                            pltpu.VMEM((FFN_TILE, LANES), F32),
                            pltpu.SemaphoreType.DMA((2,)),
                            pltpu.SemaphoreType.DMA((2,))],
        ),
        compiler_params=_cparams(("arbitrary",)),
        name="moe_ffn",
    )(tea, teb, n_used, gsrc, sdst, wa_tok, wb_tok, h2, wg, wu, wd, wg, wu, wd)


def _moe(h2, ri, rw, cnt, wg_bf, wu_bf, wd_bf, layer):
    n = h2.shape[0] // SLAB
    max_tiles = n // FFN_TILE + N_PAIRS
    n_slots = max_tiles * FFN_TILE
    counts = cnt[:N_PAIRS, 0].astype(I32)
    padded = ((counts + FFN_TILE - 1) // FFN_TILE) * FFN_TILE
    ends = jnp.cumsum(padded)
    offs = ends - padded
    q, rank = ri[0], ri[1]
    cls = jnp.arange(N_PAIRS, dtype=I32)
    slot = rank + jnp.sum(jnp.where(q[None, :] == cls[:, None], offs[:, None], 0),
                          axis=0)
    n_used = ends[-1] // FFN_TILE
    tile = jnp.arange(max_tiles + 1, dtype=I32)
    tq = jnp.sum((tile[:, None] * FFN_TILE >= ends[None, :]).astype(I32), axis=1)
    tq_last = jnp.sum(((n_used - 1) * FFN_TILE >= ends).astype(I32))
    tq = jnp.where(tile < n_used, tq, tq_last)
    pa = jnp.array([a for a, _ in PAIRS], I32)
    pb = jnp.array([b for _, b in PAIRS], I32)
    grp, pidx = tq // len(PAIRS), tq % len(PAIRS)
    tea = EXPERTS_PER_GROUP * grp + jnp.take(pa, pidx)
    teb = EXPERTS_PER_GROUP * grp + jnp.take(pb, pidx)
    gsrc, sdst = _invert(slot, ends, n_slots)
    return _expert_ffn(tea, teb, n_used.reshape(1), gsrc, sdst, rw[0], rw[1], h2,
                       wg_bf, wu_bf, wd_bf, layer)


def _moe_out_tm_kernel(x1_ref, y_ref, g2_ref, op_ref, os_ref, scr, *, n_p_tiles):
    i = pl.program_id(0)
    y = _rows_from_slabs(y_ref, x1_ref.shape[0])
    x2 = x1_ref[...] + _per_row(y, g2_ref[...], lambda a, b: a * b)
    pl.when(i < n_p_tiles)(
        lambda: _from_time_major(x2, op_ref, scr, op_ref.shape[0]))
    pl.when(i >= n_p_tiles)(
        lambda: _from_time_major(x2, os_ref, scr, os_ref.shape[0]))


def _moe_out_tm(x1, y, modpat, shape_p, shape_s):
    n = x1.shape[0]
    tm = TOKEN_TILE
    n_p_tiles = shape_p[0] * shape_p[1] // tm
    p = modpat.shape[1]
    rowblk = pl.BlockSpec((tm, D_MODEL), lambda i: (i, 0))
    stream = _stream_specs(jax.ShapeDtypeStruct(shape_p, F32),
                           jax.ShapeDtypeStruct(shape_s, F32), n_p_tiles)
    return pl.pallas_call(
        functools.partial(_moe_out_tm_kernel, n_p_tiles=n_p_tiles),
        out_shape=[jax.ShapeDtypeStruct(shape_p, F32),
                   jax.ShapeDtypeStruct(shape_s, F32)],
        grid=(n // tm,),
        in_specs=[rowblk,
                  pl.BlockSpec((tm * SLAB, LANES), lambda i: (i, 0)),
                  pl.BlockSpec((None, p, D_MODEL),
                               lambda i: (jnp.where(i < n_p_tiles, 0, 1), 0, 5))],
        out_specs=stream,
        scratch_shapes=[pltpu.VMEM((LANE_TILES, tm, LANES), F32)],
        compiler_params=_cparams(("arbitrary",)),
        name="moe_out_tm",
    )(x1, y, modpat)


def _moe_out_final_kernel(x1_ref, y_ref, g2_ref, nfin_ref, op_ref, os_ref,
                          *, n_p_tiles):
    i = pl.program_id(0)
    y = _rows_from_slabs(y_ref, x1_ref.shape[0])
    x2 = x1_ref[...] + _per_row(y, g2_ref[...], lambda a, b: a * b)
    out = _rms(x2) * nfin_ref[...]

    @pl.when(i < n_p_tiles)
    def _():
        op_ref[...] = out

    @pl.when(i >= n_p_tiles)
    def _():
        os_ref[...] = out


def _moe_out_final(x1, y, modpat, norm_final, n_p, rows_per_request):
    n = x1.shape[0]
    tm = TOKEN_TILE
    n_p_tiles = n_p // tm
    p = modpat.shape[1]
    rowblk = pl.BlockSpec((tm, D_MODEL), lambda i: (i, 0))
    pat = lambda i: jnp.where(
        i < n_p_tiles, 0, 1 + (i - n_p_tiles) // (rows_per_request // tm))
    return pl.pallas_call(
        functools.partial(_moe_out_final_kernel, n_p_tiles=n_p_tiles),
        out_shape=[jax.ShapeDtypeStruct((n_p, D_MODEL), F32),
                   jax.ShapeDtypeStruct((n - n_p, D_MODEL), F32)],
        grid=(n // tm,),
        in_specs=[rowblk,
                  pl.BlockSpec((tm * SLAB, LANES), lambda i: (i, 0)),
                  pl.BlockSpec((None, p, D_MODEL), lambda i: (pat(i), 0, 5)),
                  pl.BlockSpec((1, D_MODEL), lambda i: (0, 0))],
        out_specs=[
            pl.BlockSpec((tm, D_MODEL),
                         lambda i: (jnp.minimum(i, n_p_tiles - 1), 0)),
            pl.BlockSpec((tm, D_MODEL),
                         lambda i: (jnp.maximum(i - n_p_tiles, 0), 0))],
        compiler_params=_cparams(("arbitrary",)),
        name="moe_out_final",
    )(x1, y, modpat, norm_final)


def _block_diag_weights(bb_re, bb_im, c_re, c_im):
    lt = LANE_TILES
    gl = SSM_GROUPS // lt
    same_group = jnp.eye(gl, dtype=jnp.bool_)[None, None, :, None, :, None]

    def in_map(bb):
        b5 = bb.reshape(2, lt, gl, STATE_DIM, SSM_GROUP)
        t = jnp.transpose(b5, (0, 1, 4, 2, 3))[:, :, None]
        t = jnp.where(same_group, t, 0.0)
        return t.reshape(2, lt, gl * SSM_GROUP, gl * STATE_DIM)

    def out_map(cc):
        c5 = cc.reshape(2, lt, gl, SSM_GROUP, STATE_DIM)
        t = jnp.transpose(c5, (0, 1, 2, 4, 3))[:, :, :, :, None]
        t = jnp.where(same_group, t, 0.0)
        return t.reshape(2, lt, gl * STATE_DIM, gl * SSM_GROUP)

    bt = jnp.concatenate([in_map(bb_re), in_map(bb_im)], axis=-1).astype(BF16)
    return bt, out_map(c_re).astype(BF16), out_map(c_im).astype(BF16)


def kernel(x_prompt, x_sample, c, state_ssm_re, state_ssm_im, c_ctx, norm_mix, norm_ffn, w_ada, b_ada, ssm_w_in, ssm_lam_re, ssm_lam_im, ssm_log_dt, ssm_b_re, ssm_b_im, ssm_c_re, ssm_c_im, ssm_d, ssm_w_glu, fnet_w_out, router_w, router_b, moe_w_gate, moe_w_up, moe_w_down, norm_final):
    bp, lp, _ = x_prompt.shape
    bs, ls, _ = x_sample.shape
    n_p = bp * lp
    n_s = bs * ls
    n = n_p + n_s
    tm = TOKEN_TILE

    cond = jnp.zeros((N_COND, D_MODEL), F32).at[0].set(c_ctx).at[1:1 + bs].set(c)
    modtab = _ada_table(cond, w_ada, b_ada).reshape(DEPTH, N_COND, N_MOD, D_MODEL)

    rwt_hi = router_w.T.astype(BF16)
    rwt_lo = (router_w.T - rwt_hi.astype(F32)).astype(BF16)
    rwt = jnp.concatenate([rwt_hi, rwt_lo])
    rb = router_b.reshape(N_EXPERTS, 1)
    row = lambda v: v.reshape(1, D_MODEL)

    period = max(bp, bs)
    pat_tm = jnp.stack([
        jnp.broadcast_to(modtab[0, 0], (period, N_MOD, D_MODEL)),
        jnp.tile(modtab[0, 1:1 + bs], (period // bs, 1, 1))])
    pat_tm = pat_tm.reshape(2, period, N_MOD * D_MODEL)

    a_re, a_im, bb_re, bb_im = _zoh(ssm_lam_re[0], ssm_lam_im[0], ssm_log_dt[0],
                                    ssm_b_re[0], ssm_b_im[0])
    bt, ct_re, ct_im = _block_diag_weights(bb_re, bb_im, ssm_c_re[0], ssm_c_im[0])
    a_re = a_re.reshape(2, 1, N_STATE)
    a_im = a_im.reshape(2, 1, N_STATE)

    u = _ssm_in(x_prompt, x_sample, row(norm_mix[0]), pat_tm,
                ssm_w_in[0].astype(BF16))
    h0s_re = state_ssm_re[:, 0].reshape(bs, 2, N_STATE).transpose(1, 0, 2)
    h0s_im = state_ssm_im[:, 0].reshape(bs, 2, N_STATE).transpose(1, 0, 2)
    y_scan, fin_re, fin_im = _ssm_scan(u, bt, ct_re, ct_im, a_re, a_im,
                                       h0s_re, h0s_im, bp=bp, lp=lp, bs=bs, ls=ls)
    tri = jnp.triu(jnp.ones((tm, tm), BF16), k=1)
    x1, h2, ri, rw, cnt = _ssm_out(u, y_scan, row(ssm_d[0]),
                               ssm_w_glu[0].astype(BF16), x_prompt, x_sample,
                               pat_tm, row(norm_ffn[0]), rwt, rb, tri)
    wg_bf, wu_bf, wd_bf = (w.astype(BF16) for w in (moe_w_gate, moe_w_up,
                                                    moe_w_down))
    y_moe = _moe(h2, ri, rw, cnt, wg_bf, wu_bf, wd_bf, 0)
    x2_p, x2_s = _moe_out_tm(x1, y_moe, pat_tm, x_prompt.shape, x_sample.shape)

    pat_bm = jnp.broadcast_to(modtab[1][:1 + bs, None],
                              (1 + bs, SUBLANES, N_MOD, D_MODEL))
    pat_bm = pat_bm.reshape(1 + bs, SUBLANES, N_MOD * D_MODEL)
    x3, h2, ri, rw, cnt = _fnet(x2_p, x2_s, row(norm_mix[1]), pat_bm,
                            fnet_w_out[0].astype(BF16), row(norm_ffn[1]), rwt, rb)
    y_moe = _moe(h2, ri, rw, cnt, wg_bf, wu_bf, wd_bf, 1)
    y_p, y_s = _moe_out_final(x3, y_moe, pat_bm, row(norm_final), n_p, ls)

    st = lambda f: f.transpose(1, 0, 2).reshape(bp, 1, 2, SSM_GROUPS, STATE_DIM)
    return (y_p.reshape(bp, lp, D_MODEL), y_s.reshape(bs, ls, D_MODEL),
            st(fin_re), st(fin_im))
```
